```python
import math
import jax, jax.numpy as jnp
from jax import lax
import numpy as np

D_MODEL = 2048
BATCH = 32
SEQ = 256
DEPTH = 2
DEC_BATCH = 2
DEC_SEQ = 2048
PAST_LEN = 512

GRID_W = 64
HEAD_DIM = 128
N_HEADS = D_MODEL // HEAD_DIM
HA = N_HEADS // 2
DIFF_HD = HEAD_DIM // 2
DIFF_VD = HEAD_DIM
HB = N_HEADS // 2
GLA_DK = HEAD_DIM // 2
GLA_DV = HEAD_DIM
GLA_GATE_RANK = 16
GLA_TAU = 16.0
GLA_CHUNK = 64
KV_HEADS = N_HEADS // 4
Q_PER_KV = N_HEADS // KV_HEADS
WINDOW = 128
Q_BLOCK = 128
D_FF = -(-8 * D_MODEL // (3 * 256)) * 256
N_EVEN = (DEPTH + 1) // 2
N_ODD = DEPTH // 2
ROPE_BASE = 10000.0
EPS = 1e-6
NEG_INF = -1e30
EVEN_WIDTHS = (HA * 2 * DIFF_HD, HA * 2 * DIFF_HD, HA * DIFF_VD, HB * GLA_DK, HB * GLA_DK, HB * GLA_DV, HB * GLA_DV)
EVEN_SPLITS = tuple(int(s) for s in np.cumsum(EVEN_WIDTHS)[:-1])
EVEN_IN = sum(EVEN_WIDTHS)
EVEN_MIX = HA * DIFF_VD + HB * GLA_DV
ODD_IN = (N_HEADS + 2 * KV_HEADS) * HEAD_DIM
ODD_SPLITS = (N_HEADS * HEAD_DIM, (N_HEADS + KV_HEADS) * HEAD_DIM)
F32 = jnp.float32

kernel_name = 'hybrid_diffusion_prefix_step'


def rms_norm(x, g):
    xf = x.astype(F32)
    y = xf * lax.rsqrt(jnp.mean(xf * xf, axis=-1, keepdims=True) + EPS)
    return (y * g.astype(F32)).astype(x.dtype)


def modulate(h, shift, scale):
    return h * (1 + scale) + shift


def adaln_params(cvec, w, b):
    mod = jax.nn.silu(cvec) @ w + b
    return jnp.split(mod[:, None, :], 6, axis=-1)


def swiglu(h, w_in, w_out):
    gate, up = jnp.split(h @ w_in, 2, axis=-1)
    return (jax.nn.silu(gate) * up) @ w_out


def grid_angles(n_tok, rot_dim):
    rows = n_tok // GRID_W
    t = jnp.arange(rows * GRID_W)
    row = (t // GRID_W).astype(F32)
    col = (t % GRID_W).astype(F32)
    half = rot_dim // 2
    inv = ROPE_BASE ** (-jnp.arange(0, half, 2, dtype=F32) / half)
    return row[:, None] * inv[None], col[:, None] * inv[None]


def rotate(x, ang):
    shape = (1, ang.shape[0]) + (1,) * (x.ndim - 3) + (ang.shape[1],)
    cos = jnp.cos(ang).reshape(shape)
    sin = jnp.sin(ang).reshape(shape)
    x1, x2 = jnp.split(x, 2, axis=-1)
    return jnp.concatenate([x1 * cos - x2 * sin, x2 * cos + x1 * sin], axis=-1)


def axial_rope(x, ang):
    ang_row, ang_col = ang
    xr, xc = jnp.split(x.astype(F32), 2, axis=-1)
    return jnp.concatenate([rotate(xr, ang_row), rotate(xc, ang_col)], axis=-1).astype(x.dtype)


def softmax_with_sink(s, sink):
    sink = jnp.broadcast_to(sink.astype(F32), s.shape[:-1] + (1,))
    return jax.nn.softmax(jnp.concatenate([sink, s], axis=-1), axis=-1)[..., 1:]


def diff_lambda(lq1, lk1, lq2, lk2, lam_init):
    return (jnp.exp(jnp.sum(lq1.astype(F32) * lk1.astype(F32)))
            - jnp.exp(jnp.sum(lq2.astype(F32) * lk2.astype(F32))) + lam_init)


def diff_attention(q, k, v, lam):
    B, Lq = q.shape[0], q.shape[1]
    nb = Lq // Q_BLOCK
    qb = jnp.moveaxis(q.reshape((B, nb, Q_BLOCK) + q.shape[2:]), 1, 0)
    scale = DIFF_HD ** -0.5

    def one(qblk):
        s = jnp.einsum('bqhmd,bkhmd->bhmqk', qblk, k, preferred_element_type=F32) * scale
        p = jax.nn.softmax(s, axis=-1)
        w = p[:, :, 0] - lam * p[:, :, 1]
        return jnp.einsum('bhqk,bkhd->bqhd', w.astype(v.dtype), v)

    o = lax.map(one, qb)
    return jnp.moveaxis(o, 0, 1).reshape(B, Lq, HA, DIFF_VD)


def gla_chunked(q, k, v, g, s0):
    B, L, H, _ = q.shape
    DV = v.shape[-1]
    n = L // GLA_CHUNK

    def chunks(t):
        return jnp.swapaxes(t.reshape(B, n, GLA_CHUNK, H, t.shape[-1]), 2, 3)

    qc, kc, vc, gc = chunks(q), chunks(k), chunks(v), chunks(g)
    b = jnp.cumsum(gc, axis=3)
    b_last = b[:, :, :, -1:, :]
    q_in = qc * jnp.exp(b)
    k_in = kc * jnp.exp(-b)
    k_end = kc * jnp.exp(b_last - b)
    lower = jnp.tril(jnp.ones((GLA_CHUNK, GLA_CHUNK), dtype=bool))
    a = jnp.where(lower, jnp.einsum('bnhid,bnhjd->bnhij', q_in, k_in), 0.0)
    o_intra = jnp.einsum('bnhij,bnhjv->bnhiv', a, vc)
    kv = jnp.einsum('bnhjd,bnhjv->bnhdv', k_end, vc)
    decay = jnp.exp(b_last[:, :, :, 0, :])

    def step(s, inp):
        dec, kv_n = inp
        return dec[..., None] * s + kv_n, s

    s_fin, s_prev = lax.scan(step, s0, (jnp.moveaxis(decay, 1, 0), jnp.moveaxis(kv, 1, 0)))
    o_inter = jnp.einsum('bnhid,nbhdv->bnhiv', q_in, s_prev)
    o = jnp.swapaxes(o_intra + o_inter, 2, 3).reshape(B, L, H, DV)
    return o, s_fin


def bidir_gla(q, k, v, g_f, g_b, s0_f, s0_b):
    o_f, s_f = gla_chunked(q, k, v, g_f, s0_f)
    o_b, s_b = gla_chunked(jnp.flip(q, 1), jnp.flip(k, 1), jnp.flip(v, 1), jnp.flip(g_b, 1), s0_b)
    return o_f + jnp.flip(o_b, 1), s_f, s_b


def even_projections(h, w_in, wg1, wg2, bg):
    B, L, _ = h.shape
    aq, ak, av, bq, bk, bv, br = jnp.split(h @ w_in, EVEN_SPLITS, axis=-1)
    aq = aq.reshape(B, L, HA, 2, DIFF_HD)
    ak = ak.reshape(B, L, HA, 2, DIFF_HD)
    av = av.reshape(B, L, HA, DIFF_VD)
    bq = bq.reshape(B, L, HB, GLA_DK).astype(F32) * GLA_DK ** -0.5
    bk = bk.reshape(B, L, HB, GLA_DK).astype(F32)
    bv = bv.reshape(B, L, HB, GLA_DV).astype(F32)
    low = jnp.einsum('bld,edr->eblr', h, wg1)
    logit = jnp.einsum('eblr,erk->eblk', low, wg2) + bg[:, None, None, :]
    g = (jax.nn.log_sigmoid(logit.astype(F32)) / GLA_TAU).reshape(2, B, L, HB, GLA_DK)
    return aq, ak, av, bq, bk, bv, br, g


def even_output(o_a, o_b, br, subln_g, gla_g, w_out, lam_init, dtype):
    B, L = o_a.shape[0], o_a.shape[1]
    ya = rms_norm(o_a.astype(F32), subln_g) * (1.0 - lam_init)
    yb = rms_norm(o_b, gla_g) * jax.nn.silu(br.astype(F32).reshape(B, L, HB, GLA_DV))
    y = jnp.concatenate([ya.reshape(B, L, -1), yb.reshape(B, L, -1)], axis=-1).astype(dtype)
    return y @ w_out


def even_context(h, lam, lam_init, w_in, subln_g, wg1, wg2, bg, gla_g, w_out):
    B, L, _ = h.shape
    aq, ak, av, bq, bk, bv, br, g = even_projections(h, w_in, wg1, wg2, bg)
    o_a = diff_attention(aq, ak, av, lam)
    s0 = jnp.zeros((B, HB, GLA_DK, GLA_DV), F32)
    o_b, s_f, s_b = bidir_gla(bq, bk, bv, g[0], g[1], s0, s0)
    out = even_output(o_a, o_b, br, subln_g, gla_g, w_out, lam_init, h.dtype)
    return (out, ak.reshape(B, L, HA, 2 * DIFF_HD), av,
            jnp.stack([s_f, s_b], axis=1).astype(h.dtype))


def even_latent(h, ak_ctx, av_ctx, s_ctx, ang, lam, lam_init, w_in, subln_g, wg1, wg2, bg, gla_g, w_out):
    B, L, _ = h.shape
    P = ak_ctx.shape[1]
    aq, ak, av, bq, bk, bv, br, g = even_projections(h, w_in, wg1, wg2, bg)
    aq = axial_rope(aq, ang)
    ak = axial_rope(ak, ang)
    k_all = jnp.concatenate([ak_ctx.reshape(B, P, HA, 2, DIFF_HD).astype(ak.dtype), ak], axis=1)
    v_all = jnp.concatenate([av_ctx.astype(av.dtype), av], axis=1)
    o_a = diff_attention(aq, k_all, v_all, lam)
    sf = s_ctx.astype(F32)
    o_b, _, _ = bidir_gla(bq, bk, bv, g[0], g[1], sf[:, 0], sf[:, 1])
    return even_output(o_a, o_b, br, subln_g, gla_g, w_out, lam_init, h.dtype)


def odd_projections(h, w_in):
    B, L, _ = h.shape
    q, k, v = jnp.split(h @ w_in, ODD_SPLITS, axis=-1)
    return (q.reshape(B, L, N_HEADS, HEAD_DIM), k.reshape(B, L, KV_HEADS, HEAD_DIM),
            v.reshape(B, L, KV_HEADS, HEAD_DIM))


def gqa_sink_dense(q, k, v, sink):
    B, Lq = q.shape[0], q.shape[1]
    nb = Lq // Q_BLOCK
    qb = jnp.moveaxis(q.reshape(B, nb, Q_BLOCK, KV_HEADS, Q_PER_KV, HEAD_DIM), 1, 0)
    sink_l = sink.reshape(KV_HEADS, Q_PER_KV)[None, :, :, None, None]
    scale = HEAD_DIM ** -0.5

    def one(qblk):
        s = jnp.einsum('bqkgd,bskd->bkgqs', qblk, k, preferred_element_type=F32) * scale
        p = softmax_with_sink(s, sink_l)
        return jnp.einsum('bkgqs,bskd->bqkgd', p.astype(v.dtype), v)

    o = lax.map(one, qb)
    return jnp.moveaxis(o, 0, 1).reshape(B, Lq, N_HEADS, HEAD_DIM)


def window_gqa_sink(q, k, v, kc, vc, sink):
    B, L = q.shape[0], q.shape[1]
    P = kc.shape[1]
    nb = L // WINDOW
    qg = q.reshape(B, nb, WINDOW, KV_HEADS, Q_PER_KV, HEAD_DIM)
    pad = ((0, 0), (WINDOW, WINDOW), (0, 0), (0, 0))
    kp = jnp.pad(k, pad)
    vp = jnp.pad(v, pad)
    idx = jnp.arange(nb)[:, None] * WINDOW + jnp.arange(3 * WINDOW)[None, :]
    kb = kp[:, idx]
    vb = vp[:, idx]
    qpos = jnp.arange(nb)[:, None] * WINDOW + jnp.arange(WINDOW)[None, :]
    kpos = idx - WINDOW
    valid = ((jnp.abs(qpos[:, :, None] - kpos[:, None, :]) <= WINDOW)
             & (kpos[:, None, :] >= 0) & (kpos[:, None, :] < L))
    scale = HEAD_DIM ** -0.5
    s_w = jnp.einsum('bnqkgd,bnskd->bnkgqs', qg, kb, preferred_element_type=F32) * scale
    s_w = jnp.where(valid[None, :, None, None], s_w, NEG_INF)
    s_c = jnp.einsum('bnqkgd,bpkd->bnkgqp', qg, kc.astype(q.dtype), preferred_element_type=F32) * scale
    p = softmax_with_sink(jnp.concatenate([s_c, s_w], axis=-1),
                          sink.reshape(KV_HEADS, Q_PER_KV)[None, None, :, :, None, None])
    p_c = p[..., :P].astype(v.dtype)
    p_w = p[..., P:].astype(v.dtype)
    o = (jnp.einsum('bnkgqp,bpkd->bnqkgd', p_c, vc.astype(v.dtype))
         + jnp.einsum('bnkgqs,bnskd->bnqkgd', p_w, vb))
    return o.reshape(B, L, N_HEADS, HEAD_DIM)


def odd_context(h, w_in, sink, w_out):
    B, L, _ = h.shape
    q, k, v = odd_projections(h, w_in)
    o = gqa_sink_dense(q, k, v, sink)
    return o.reshape(B, L, N_HEADS * HEAD_DIM) @ w_out, k, v


def odd_latent(h, kc, vc, ang, w_in, sink, w_out):
    B, L, _ = h.shape
    q, k, v = odd_projections(h, w_in)
    q = axial_rope(q, ang)
    k = axial_rope(k, ang)
    o = window_gqa_sink(q, k, v, kc, vc, sink)
    return o.reshape(B, L, N_HEADS * HEAD_DIM) @ w_out


def setup_inputs(seed: int = 0) -> dict:
    key = jax.random.key(seed)
    ks = iter(jax.random.split(key, 32))

    def nrm(shape, scale=1.0):
        return jax.random.normal(next(ks), shape, F32) * scale

    def gain(shape):
        return 1.0 + 0.02 * jax.random.normal(next(ks), shape, F32)

    return {
        'x_prompt': nrm((BATCH, SEQ, D_MODEL)),
        'x_sample': nrm((DEC_BATCH, DEC_SEQ, D_MODEL)),
        'c': nrm((DEC_BATCH, D_MODEL)),
        'cache_a_k': nrm((DEC_BATCH, N_EVEN, PAST_LEN, HA, 2 * DIFF_HD)),
        'cache_a_v': nrm((DEC_BATCH, N_EVEN, PAST_LEN, HA, DIFF_VD)),
        'state_b': nrm((DEC_BATCH, N_EVEN, 2, HB, GLA_DK, GLA_DV)),
        'cache_c_k': nrm((DEC_BATCH, N_ODD, PAST_LEN, KV_HEADS, HEAD_DIM)),
        'cache_c_v': nrm((DEC_BATCH, N_ODD, PAST_LEN, KV_HEADS, HEAD_DIM)),
        'c_ctx': nrm((D_MODEL,)),
        'w_ada': nrm((DEPTH, D_MODEL, 6 * D_MODEL), 0.5 * D_MODEL ** -0.5),
        'b_ada': nrm((DEPTH, 6 * D_MODEL), 0.01),
        'norm_g': gain((DEPTH, 2, D_MODEL)),
        'w_in_even': nrm((N_EVEN, D_MODEL, EVEN_IN), D_MODEL ** -0.5),
        'lam_q1': nrm((N_EVEN, DIFF_HD), 0.1),
        'lam_k1': nrm((N_EVEN, DIFF_HD), 0.1),
        'lam_q2': nrm((N_EVEN, DIFF_HD), 0.1),
        'lam_k2': nrm((N_EVEN, DIFF_HD), 0.1),
        'subln_g': gain((N_EVEN, DIFF_VD)),
        'w_gate1': nrm((N_EVEN, 2, D_MODEL, GLA_GATE_RANK), D_MODEL ** -0.5),
        'w_gate2': nrm((N_EVEN, 2, GLA_GATE_RANK, HB * GLA_DK), GLA_GATE_RANK ** -0.5),
        'b_gate': nrm((N_EVEN, 2, HB * GLA_DK), 0.1),
        'gla_norm_g': gain((N_EVEN, GLA_DV)),
        'w_out_even': nrm((N_EVEN, EVEN_MIX, D_MODEL), EVEN_MIX ** -0.5),
        'w_in_odd': nrm((N_ODD, D_MODEL, ODD_IN), D_MODEL ** -0.5),
        'sinks': nrm((N_ODD, N_HEADS)),
        'w_out_odd': nrm((N_ODD, N_HEADS * HEAD_DIM, D_MODEL), (N_HEADS * HEAD_DIM) ** -0.5),
        'w_ffn_in': nrm((DEPTH, D_MODEL, 2 * D_FF), D_MODEL ** -0.5),
        'w_ffn_out': nrm((DEPTH, D_FF, D_MODEL), D_FF ** -0.5),
        'final_norm_g': gain((D_MODEL,)),
    }


def reference(x_prompt, x_sample, c, cache_a_k, cache_a_v, state_b, cache_c_k, cache_c_v, c_ctx,
              w_ada, b_ada, norm_g, w_in_even, lam_q1, lam_k1, lam_q2, lam_k2, subln_g,
              w_gate1, w_gate2, b_gate, gla_norm_g, w_out_even, w_in_odd, sinks, w_out_odd,
              w_ffn_in, w_ffn_out, final_norm_g):
    xp, xs = x_prompt, x_sample
    n_lat = xs.shape[1]
    ang_d = grid_angles(n_lat, DIFF_HD)
    ang_h = grid_angles(n_lat, HEAD_DIM)
    new_a_k, new_a_v, new_s_b, new_c_k, new_c_v = [], [], [], [], []
    for layer in range(DEPTH):
        j = layer // 2
        pm = adaln_params(c_ctx[None, :], w_ada[layer], b_ada[layer])
        sm = adaln_params(c, w_ada[layer], b_ada[layer])
        hp = modulate(rms_norm(xp, norm_g[layer, 0]), pm[0], pm[1])
        hs = modulate(rms_norm(xs, norm_g[layer, 0]), sm[0], sm[1])
        if layer % 2 == 0:
            lam_init = 0.8 - 0.6 * math.exp(-0.3 * layer)
            lam = diff_lambda(lam_q1[j], lam_k1[j], lam_q2[j], lam_k2[j], lam_init)
            mp, ak, av, sb = even_context(hp, lam, lam_init, w_in_even[j], subln_g[j], w_gate1[j],
                                          w_gate2[j], b_gate[j], gla_norm_g[j], w_out_even[j])
            ms = even_latent(hs, cache_a_k[:, j], cache_a_v[:, j], state_b[:, j], ang_d, lam, lam_init,
                             w_in_even[j], subln_g[j], w_gate1[j], w_gate2[j], b_gate[j],
                             gla_norm_g[j], w_out_even[j])
            new_a_k.append(ak)
            new_a_v.append(av)
            new_s_b.append(sb)
        else:
            mp, ck, cv = odd_context(hp, w_in_odd[j], sinks[j], w_out_odd[j])
            ms = odd_latent(hs, cache_c_k[:, j], cache_c_v[:, j], ang_h, w_in_odd[j], sinks[j], w_out_odd[j])
            new_c_k.append(ck)
            new_c_v.append(cv)
        xp = xp + pm[2] * mp
        xs = xs + sm[2] * ms
        xp = xp + pm[5] * swiglu(modulate(rms_norm(xp, norm_g[layer, 1]), pm[3], pm[4]),
                                 w_ffn_in[layer], w_ffn_out[layer])
        xs = xs + sm[5] * swiglu(modulate(rms_norm(xs, norm_g[layer, 1]), sm[3], sm[4]),
                                 w_ffn_in[layer], w_ffn_out[layer])
    y_prompt = rms_norm(xp, final_norm_g)
    y_sample = rms_norm(xs, final_norm_g)
    return (y_prompt, y_sample, jnp.stack(new_a_k, axis=1), jnp.stack(new_a_v, axis=1),
            jnp.stack(new_s_b, axis=1), jnp.stack(new_c_k, axis=1), jnp.stack(new_c_v, axis=1))
```

```python
import functools
import math

import jax
import jax.numpy as jnp
from jax import lax
from jax.experimental import pallas as pl
from jax.experimental.pallas import tpu as pltpu

D_MODEL = 2048
BATCH = 32
SEQ = 256
DEPTH = 2
DEC_BATCH = 2
DEC_SEQ = 2048
PAST_LEN = 512
GRID_W = 64
HEAD_DIM = 128
N_HEADS = D_MODEL // HEAD_DIM
HA = N_HEADS // 2
DIFF_HD = HEAD_DIM // 2
DIFF_VD = HEAD_DIM
HB = N_HEADS // 2
GLA_DK = HEAD_DIM // 2
GLA_DV = HEAD_DIM
GLA_GATE_RANK = 16
GLA_TAU = 16.0
GLA_CHUNK = 64
KV_HEADS = N_HEADS // 4
Q_PER_KV = N_HEADS // KV_HEADS
WINDOW = 128
D_FF = -(-8 * D_MODEL // (3 * 256)) * 256
ROPE_BASE = 10000.0
EPS = 1e-6
NEG_INF = -1e30
EVEN_IN = 2 * HA * 2 * DIFF_HD + HA * DIFF_VD + 2 * HB * GLA_DK + 2 * HB * GLA_DV
ODD_IN = (N_HEADS + 2 * KV_HEADS) * HEAD_DIM

TP = BATCH * SEQ
TS = DEC_BATCH * DEC_SEQ
T = TP + TS
LANE = 128
MOD_ROWS = 8
VMEM_LIMIT = 56 * 1024 * 1024

F32 = jnp.float32
BF16 = jnp.bfloat16


def _params(semantics, vmem=VMEM_LIMIT):
    return pltpu.CompilerParams(dimension_semantics=semantics, vmem_limit_bytes=vmem)


def _dot(a, b):
    return jnp.dot(a, b, preferred_element_type=F32)


def _dot_nt(a, b):
    return lax.dot_general(a, b, (((1,), (1,)), ((), ())), preferred_element_type=F32)


def _group_of_tile(i, tm):
    r = i * tm
    return jnp.where(r < TP, 0, 1 + (r - TP) // DEC_SEQ)


def _rms(x, g):
    return (x * lax.rsqrt(jnp.mean(x * x, axis=-1, keepdims=True) + EPS)) * g


def _adaln_kernel(c_ref, w_ref, b_ref, o_ref):
    c = c_ref[...]
    s = c * jax.nn.sigmoid(c)
    o_ref[...] = _dot(s.astype(BF16), w_ref[...].astype(BF16)) + b_ref[...]


def adaln(cvec, w_ada, b_ada, tn=512):
    n = w_ada.shape[-1]
    return pl.pallas_call(
        _adaln_kernel,
        grid=(DEPTH, n // tn),
        in_specs=[
            pl.BlockSpec((MOD_ROWS, D_MODEL), lambda l, j: (0, 0)),
            pl.BlockSpec((None, D_MODEL, tn), lambda l, j: (l, 0, j)),
            pl.BlockSpec((None, 1, tn), lambda l, j: (l, 0, j)),
        ],
        out_specs=pl.BlockSpec((None, MOD_ROWS, tn), lambda l, j: (l, 0, j)),
        out_shape=jax.ShapeDtypeStruct((DEPTH, MOD_ROWS, n), F32),
        compiler_params=_params(("arbitrary", "arbitrary")),
        name="adaln",
    )(cvec, w_ada, b_ada.reshape(DEPTH, 1, n))


def _norm_mod(x_ref, g_ref, shift_ref, scale_ref):
    return _rms(x_ref[...], g_ref[...]) * (1 + scale_ref[...]) + shift_ref[...]


def _nmm_kernel(x_ref, g_ref, shift_ref, scale_ref, w_ref, o_ref, h_scr):
    @pl.when(pl.program_id(1) == 0)
    def _():
        h_scr[...] = _norm_mod(x_ref, g_ref, shift_ref, scale_ref).astype(BF16)

    o_ref[...] = _dot(h_scr[...], w_ref[...]).astype(o_ref.dtype)


def _nmm_gate_kernel(x_ref, g_ref, shift_ref, scale_ref, w_ref, wg1_ref, wg2_ref, bg_ref,
                     o_ref, gate_ref, h_scr):
    @pl.when(pl.program_id(1) == 0)
    def _():
        h = _norm_mod(x_ref, g_ref, shift_ref, scale_ref).astype(BF16)
        h_scr[...] = h
        low = _dot(h, wg1_ref[...]).astype(BF16)
        for e in range(2):
            logit = _dot(low, wg2_ref[e]) + bg_ref[e]
            gate_ref[e] = jax.nn.log_sigmoid(logit) / GLA_TAU

    o_ref[...] = _dot(h_scr[...], w_ref[...]).astype(o_ref.dtype)


def _swiglu_kernel(x_ref, g_ref, shift_ref, scale_ref, wg_ref, wu_ref, o_ref, h_scr):
    @pl.when(pl.program_id(1) == 0)
    def _():
        h_scr[...] = _norm_mod(x_ref, g_ref, shift_ref, scale_ref).astype(BF16)

    h = h_scr[...]
    gate = _dot(h, wg_ref[...])
    up = _dot(h, wu_ref[...])
    o_ref[...] = (gate * jax.nn.sigmoid(gate) * up).astype(o_ref.dtype)


def _mod_specs(tm, shift_idx, scale_idx):
    def spec(idx):
        return pl.BlockSpec((None, 1, D_MODEL), lambda i, j: (_group_of_tile(i, tm), 0, idx))
    return [pl.BlockSpec((1, D_MODEL), lambda i, j: (0, 0)), spec(shift_idx), spec(scale_idx)]


def norm_mod_matmul(x, g, mod, shift_idx, scale_idx, w, tm=1024, tn=512):
    n = w.shape[1]
    return pl.pallas_call(
        _nmm_kernel,
        grid=(T // tm, n // tn),
        in_specs=[pl.BlockSpec((tm, D_MODEL), lambda i, j: (i, 0))]
        + _mod_specs(tm, shift_idx, scale_idx)
        + [pl.BlockSpec((D_MODEL, tn), lambda i, j: (0, j))],
        out_specs=pl.BlockSpec((tm, tn), lambda i, j: (i, j)),
        out_shape=jax.ShapeDtypeStruct((T, n), F32),
        scratch_shapes=[pltpu.VMEM((tm, D_MODEL), BF16)],
        compiler_params=_params(("arbitrary", "arbitrary")),
        name="norm_mod_matmul",
    )(x, g.reshape(1, D_MODEL), mod, mod, w)


def norm_mod_matmul_gate(x, g, mod, shift_idx, scale_idx, w, wg1, wg2, bg, tm=1024, tn=512):
    n = w.shape[1]
    ng = HB * GLA_DK
    return pl.pallas_call(
        _nmm_gate_kernel,
        grid=(T // tm, n // tn),
        in_specs=[pl.BlockSpec((tm, D_MODEL), lambda i, j: (i, 0))]
        + _mod_specs(tm, shift_idx, scale_idx)
        + [pl.BlockSpec((D_MODEL, tn), lambda i, j: (0, j)),
           pl.BlockSpec((D_MODEL, LANE), lambda i, j: (0, 0)),
           pl.BlockSpec((2, LANE, ng), lambda i, j: (0, 0, 0)),
           pl.BlockSpec((2, 1, ng), lambda i, j: (0, 0, 0))],
        out_specs=[pl.BlockSpec((tm, tn), lambda i, j: (i, j)),
                   pl.BlockSpec((2, tm, ng), lambda i, j: (0, i, 0))],
        out_shape=[jax.ShapeDtypeStruct((T, n), F32), jax.ShapeDtypeStruct((2, T, ng), F32)],
        scratch_shapes=[pltpu.VMEM((tm, D_MODEL), BF16)],
        compiler_params=_params(("arbitrary", "arbitrary")),
        name="norm_mod_matmul_gate",
    )(x, g.reshape(1, D_MODEL), mod, mod, w, wg1, wg2, bg)


def norm_mod_swiglu(x, g, mod, shift_idx, scale_idx, w_in, tm=1024, tn=512):
    nf = D_FF // tn
    return pl.pallas_call(
        _swiglu_kernel,
        grid=(T // tm, nf),
        in_specs=[pl.BlockSpec((tm, D_MODEL), lambda i, j: (i, 0))]
        + _mod_specs(tm, shift_idx, scale_idx)
        + [pl.BlockSpec((D_MODEL, tn), lambda i, j: (0, j)),
           pl.BlockSpec((D_MODEL, tn), lambda i, j: (0, j + nf))],
        out_specs=pl.BlockSpec((tm, tn), lambda i, j: (i, j)),
        out_shape=jax.ShapeDtypeStruct((T, D_FF), BF16),
        scratch_shapes=[pltpu.VMEM((tm, D_MODEL), BF16)],
        compiler_params=_params(("arbitrary", "arbitrary")),
        name="norm_mod_swiglu",
    )(x, g.reshape(1, D_MODEL), mod, mod, w_in, w_in)


def _mm_res_kernel(y_ref, w_ref, x_ref, gate_ref, o_ref, acc_scr):
    k = pl.program_id(2)

    @pl.when(k == 0)
    def _():
        acc_scr[...] = jnp.zeros_like(acc_scr)

    acc_scr[...] += _dot(y_ref[...], w_ref[...])

    @pl.when(k == pl.num_programs(2) - 1)
    def _():
        o_ref[...] = x_ref[...] + gate_ref[...] * acc_scr[...]


def matmul_residual(y, w, x, mod, gate_idx, tm=1024, tn=1024, tk=None):
    kdim, n = w.shape
    tk = kdim if tk is None else tk
    gate_blk = n // tn
    return pl.pallas_call(
        _mm_res_kernel,
        grid=(T // tm, n // tn, kdim // tk),
        in_specs=[
            pl.BlockSpec((tm, tk), lambda i, j, k: (i, k)),
            pl.BlockSpec((tk, tn), lambda i, j, k: (k, j)),
            pl.BlockSpec((tm, tn), lambda i, j, k: (i, j)),
            pl.BlockSpec((None, 1, tn), lambda i, j, k: (_group_of_tile(i, tm), 0, gate_idx * gate_blk + j)),
        ],
        out_specs=pl.BlockSpec((tm, tn), lambda i, j, k: (i, j)),
        out_shape=jax.ShapeDtypeStruct((T, n), F32),
        scratch_shapes=[pltpu.VMEM((tm, tn), F32)],
        compiler_params=_params(("arbitrary", "arbitrary", "arbitrary")),
        name="matmul_residual",
    )(y, w, x, mod)


def _final_norm_kernel(x_ref, g_ref, o_ref):
    o_ref[...] = _rms(x_ref[...], g_ref[...])


def final_norm(x, g, tm=1024):
    return pl.pallas_call(
        _final_norm_kernel,
        grid=(T // tm,),
        in_specs=[pl.BlockSpec((tm, D_MODEL), lambda i: (i, 0)),
                  pl.BlockSpec((1, D_MODEL), lambda i: (0, 0))],
        out_specs=pl.BlockSpec((tm, D_MODEL), lambda i: (i, 0)),
        out_shape=jax.ShapeDtypeStruct((T, D_MODEL), F32),
        compiler_params=_params(("arbitrary",)),
        name="final_norm",
    )(x, g.reshape(1, D_MODEL))


def _grid_angles(n_tok, rot_dim):
    t = jnp.arange((n_tok // GRID_W) * GRID_W)
    row = (t // GRID_W).astype(F32)
    col = (t % GRID_W).astype(F32)
    half = rot_dim // 2
    inv = ROPE_BASE ** (-jnp.arange(0, half, 2, dtype=F32) / half)
    return row[:, None] * inv[None], col[:, None] * inv[None]


def rope_tables(n_tok, rot_dim):
    ang_row, ang_col = _grid_angles(n_tok, rot_dim)
    zeros = jnp.zeros_like(ang_row)
    reps = LANE // rot_dim

    def lanes(r1, r2, c1, c2):
        return jnp.tile(jnp.concatenate([r1, r2, c1, c2], axis=-1), (1, reps))

    cr, sr, cc, sc = jnp.cos(ang_row), jnp.sin(ang_row), jnp.cos(ang_col), jnp.sin(ang_col)
    return lanes(cr, cr, cc, cc), lanes(-sr, zeros, -sc, zeros), lanes(zeros, sr, zeros, sc)


def _rope(x, cos, sin_lo, sin_hi, quarter):
    return (x * cos + pltpu.roll(x, LANE - quarter, 1) * sin_lo + pltpu.roll(x, quarter, 1) * sin_hi)


def _diff_lambda(lam_ref, lam_init):
    e1 = jnp.exp(jnp.sum(lam_ref[0:1, :] * lam_ref[1:2, :], axis=-1, keepdims=True))
    e2 = jnp.exp(jnp.sum(lam_ref[2:3, :] * lam_ref[3:4, :], axis=-1, keepdims=True))
    return e1 - e2 + lam_init


def _softmax_rows(s):
    e = jnp.exp(s - jnp.max(s, axis=-1, keepdims=True))
    return e * (1.0 / jnp.sum(e, axis=-1, keepdims=True))


def _diff_head(q, k_bf, v_bf, lam, subln_g, lam_init):
    first = lax.broadcasted_iota(jnp.int32, q.shape, 1) < DIFF_HD
    p0 = _softmax_rows(_dot_nt(jnp.where(first, q, 0.0).astype(BF16), k_bf))
    p1 = _softmax_rows(_dot_nt(jnp.where(first, 0.0, q).astype(BF16), k_bf))
    o = _dot((p0 - lam * p1).astype(BF16), v_bf)
    return _rms(o, subln_g) * (1.0 - lam_init)


def _diff_ctx_kernel(q_ref, k_ref, v_ref, lam_ref, sg_ref, o_ref, *, lam_init):
    lam = _diff_lambda(lam_ref, lam_init)
    scale = DIFF_HD ** -0.5
    for h in range(HA):
        sl = slice(h * LANE, (h + 1) * LANE)
        y = _diff_head(q_ref[:, sl] * scale, k_ref[:, sl].astype(BF16), v_ref[:, sl].astype(BF16),
                       lam, sg_ref[...], lam_init)
        o_ref[:, sl] = y.astype(o_ref.dtype)


def diff_attention_context(proj, lam_vec, subln_g, lam_init):
    w = HA * LANE
    return pl.pallas_call(
        functools.partial(_diff_ctx_kernel, lam_init=lam_init),
        grid=(BATCH,),
        in_specs=[pl.BlockSpec((SEQ, w), lambda b: (b, 0)),
                  pl.BlockSpec((SEQ, w), lambda b: (b, 1)),
                  pl.BlockSpec((SEQ, w), lambda b: (b, 2)),
                  pl.BlockSpec((4, DIFF_HD), lambda b: (0, 0)),
                  pl.BlockSpec((1, DIFF_VD), lambda b: (0, 0))],
        out_specs=pl.BlockSpec((SEQ, w), lambda b: (b, 0)),
        out_shape=jax.ShapeDtypeStruct((TP, w), BF16),
        compiler_params=_params(("arbitrary",)),
        name="diff_attention_context",
    )(proj, proj, proj, lam_vec, subln_g.reshape(1, DIFF_VD))


def _diff_lat_kernel(q_ref, k_ref, v_ref, ck_ref, cv_ref, kc_ref, kl_ref, kh_ref, qc_ref, ql_ref, qh_ref,
                     lam_ref, sg_ref, o_ref, k_scr, v_scr, *, lam_init):
    quarter = DIFF_HD // 4

    @pl.when(pl.program_id(2) == 0)
    def _():
        k_scr[0:PAST_LEN, :] = ck_ref[...].astype(BF16)
        v_scr[0:PAST_LEN, :] = cv_ref[...].astype(BF16)
        k_scr[PAST_LEN:, :] = _rope(k_ref[...], kc_ref[...], kl_ref[...], kh_ref[...], quarter).astype(BF16)
        v_scr[PAST_LEN:, :] = v_ref[...].astype(BF16)

    lam = _diff_lambda(lam_ref, lam_init)
    q = _rope(q_ref[...], qc_ref[...], ql_ref[...], qh_ref[...], quarter) * (DIFF_HD ** -0.5)
    y = _diff_head(q, k_scr[...], v_scr[...], lam, sg_ref[...], lam_init)
    o_ref[...] = y.astype(o_ref.dtype)


def diff_attention_latent(proj, cache_k, cache_v, tables, lam_vec, subln_g, lam_init, tq=256):
    nq = DEC_SEQ // tq
    row0 = TP // tq
    kv_row0 = TP // DEC_SEQ
    cos, sin_lo, sin_hi = tables
    full = pl.BlockSpec((DEC_SEQ, LANE), lambda b, h, i: (0, 0))
    qtab = pl.BlockSpec((tq, LANE), lambda b, h, i: (i, 0))
    cache = pl.BlockSpec((None, PAST_LEN, LANE), lambda b, h, i: (b, 0, h))
    return pl.pallas_call(
        functools.partial(_diff_lat_kernel, lam_init=lam_init),
        grid=(DEC_BATCH, HA, nq),
        in_specs=[pl.BlockSpec((tq, LANE), lambda b, h, i: (row0 + b * nq + i, h)),
                  pl.BlockSpec((DEC_SEQ, LANE), lambda b, h, i: (kv_row0 + b, HA + h)),
                  pl.BlockSpec((DEC_SEQ, LANE), lambda b, h, i: (kv_row0 + b, 2 * HA + h)),
                  cache, cache, full, full, full, qtab, qtab, qtab,
                  pl.BlockSpec((4, DIFF_HD), lambda b, h, i: (0, 0)),
                  pl.BlockSpec((1, DIFF_VD), lambda b, h, i: (0, 0))],
        out_specs=pl.BlockSpec((tq, LANE), lambda b, h, i: (b * nq + i, h)),
        out_shape=jax.ShapeDtypeStruct((TS, HA * LANE), BF16),
        scratch_shapes=[pltpu.VMEM((PAST_LEN + DEC_SEQ, LANE), BF16),
                        pltpu.VMEM((PAST_LEN + DEC_SEQ, LANE), BF16)],
        compiler_params=_params(("arbitrary", "arbitrary", "arbitrary")),
        name="diff_attention_latent",
    )(proj, proj, proj, cache_k, cache_v, cos, sin_lo, sin_hi, cos, sin_lo, sin_hi,
      lam_vec, subln_g.reshape(1, DIFF_VD))


def _split3(x):
    hi = x.astype(BF16)
    r = x - hi.astype(F32)
    mid = r.astype(BF16)
    lo = (r - mid.astype(F32)).astype(BF16)
    return hi, mid, lo


def _gla_kernel(q_ref, k_ref, v_ref, r_ref, g_ref, s0_ref, gg_ref, y_ref, sfin_ref, of_scr, ob_scr, st_scr,
                *, seq):
    c = GLA_CHUNK
    n_chunks = seq // c
    rows = lax.broadcasted_iota(jnp.int32, (c, c), 0)
    cols = lax.broadcasted_iota(jnp.int32, (c, c), 1)
    keep = (rows >= cols, cols >= rows)
    tri = tuple(jnp.where(m, 1.0, 0.0).astype(BF16) for m in keep)
    lane = lax.broadcasted_iota(jnp.int32, (c, LANE), 1)
    own = (lane < GLA_DK, lane >= GLA_DK)
    zpad = jnp.zeros((GLA_DK, GLA_DV), F32)

    for d in range(2):
        for hh in range(2):
            s0 = s0_ref[d, hh]
            padded = jnp.concatenate([s0, zpad] if hh == 0 else [zpad, s0], axis=0)
            st_scr[d, hh] = padded.T

    def chunk(n, carry):
        for d in range(2):
            r0 = pl.multiple_of((n if d == 0 else n_chunks - 1 - n) * c, c)
            rs = pl.ds(r0, c)
            hi, mid, lo = _split3(g_ref[d, rs, :])
            b = _dot(tri[d], hi) + _dot(tri[d], mid) + _dot(tri[d], lo)
            b_last = b[c - 1:c, :] if d == 0 else b[0:1, :]
            q_in = (q_ref[rs, :] * (GLA_DK ** -0.5)) * jnp.exp(b)
            kk = k_ref[rs, :]
            k_in = (kk * jnp.exp(-b)).astype(BF16)
            k_end = (kk * jnp.exp(b_last - b)).astype(BF16)
            decay = jnp.exp(b_last)
            for hh in range(2):
                vs = slice(hh * GLA_DV, (hh + 1) * GLA_DV)
                qm = jnp.where(own[hh], q_in, 0.0).astype(BF16)
                a = jnp.where(keep[d], _dot_nt(qm, k_in), 0.0)
                v = v_ref[rs, vs]
                st = st_scr[d, hh]
                o = _dot(a.astype(BF16), v.astype(BF16)) + _dot_nt(qm, st.astype(BF16))
                st_scr[d, hh] = st * decay + _dot(v.T.astype(BF16), k_end)
                if d == 0:
                    of_scr[rs, vs] = o
                else:
                    ob_scr[rs, vs] = o
        return carry

    lax.fori_loop(0, n_chunks, chunk, 0)

    for hh in range(2):
        vs = slice(hh * GLA_DV, (hh + 1) * GLA_DV)
        r = r_ref[:, vs]
        y = _rms(of_scr[:, vs] + ob_scr[:, vs], gg_ref[...]) * (r * jax.nn.sigmoid(r))
        y_ref[:, vs] = y.astype(y_ref.dtype)
        for d in range(2):
            sfin_ref[d, hh] = st_scr[d, hh].T[hh * GLA_DK:(hh + 1) * GLA_DK, :]


def bidir_gla(proj, gates, s0, gla_g, n_batch, seq, row_blk0):
    pairs = HB // 2
    col_q = (2 * HA * 2 * DIFF_HD + HA * DIFF_VD) // LANE
    col_k = col_q + HB * GLA_DK // LANE
    col_v = (col_k + HB * GLA_DK // LANE) // 2
    col_r = col_v + HB * GLA_DV // (2 * GLA_DV)
    n_tok = n_batch * seq
    return pl.pallas_call(
        functools.partial(_gla_kernel, seq=seq),
        grid=(n_batch, pairs),
        in_specs=[pl.BlockSpec((seq, LANE), lambda b, p: (row_blk0 + b, col_q + p)),
                  pl.BlockSpec((seq, LANE), lambda b, p: (row_blk0 + b, col_k + p)),
                  pl.BlockSpec((seq, 2 * GLA_DV), lambda b, p: (row_blk0 + b, col_v + p)),
                  pl.BlockSpec((seq, 2 * GLA_DV), lambda b, p: (row_blk0 + b, col_r + p)),
                  pl.BlockSpec((2, seq, LANE), lambda b, p: (0, row_blk0 + b, p)),
                  pl.BlockSpec((None, 2, 2, GLA_DK, GLA_DV), lambda b, p: (b, 0, p, 0, 0)),
                  pl.BlockSpec((1, GLA_DV), lambda b, p: (0, 0))],
        out_specs=[pl.BlockSpec((seq, 2 * GLA_DV), lambda b, p: (b, p)),
                   pl.BlockSpec((None, 2, 2, GLA_DK, GLA_DV), lambda b, p: (b, 0, p, 0, 0))],
        out_shape=[jax.ShapeDtypeStruct((n_tok, HB * GLA_DV), BF16),
                   jax.ShapeDtypeStruct((n_batch, 2, HB, GLA_DK, GLA_DV), F32)],
        scratch_shapes=[pltpu.VMEM((seq, 2 * GLA_DV), F32), pltpu.VMEM((seq, 2 * GLA_DV), F32),
                        pltpu.VMEM((2, 2, GLA_DV, LANE), F32)],
        compiler_params=_params(("arbitrary", "arbitrary")),
        name="bidir_gla",
    )(proj, proj, proj, proj, gates, s0, gla_g.reshape(1, GLA_DV))


def _gqa_ctx_kernel(sink_ref, q_ref, k_ref, v_ref, o_ref):
    scale = HEAD_DIM ** -0.5
    for kh in range(KV_HEADS):
        ks = slice(kh * LANE, (kh + 1) * LANE)
        k_bf = k_ref[:, ks].astype(BF16)
        v_bf = v_ref[:, ks].astype(BF16)
        for gq in range(Q_PER_KV):
            h = kh * Q_PER_KV + gq
            hs = slice(h * LANE, (h + 1) * LANE)
            sink = sink_ref[h]
            s = _dot_nt((q_ref[:, hs] * scale).astype(BF16), k_bf)
            m = jnp.maximum(jnp.max(s, axis=-1, keepdims=True), sink)
            e = jnp.exp(s - m)
            den = jnp.sum(e, axis=-1, keepdims=True) + jnp.exp(sink - m)
            o_ref[:, hs] = _dot((e * (1.0 / den)).astype(BF16), v_bf).astype(o_ref.dtype)


def gqa_context(proj, sinks):
    wq = N_HEADS * HEAD_DIM
    wkv = KV_HEADS * HEAD_DIM
    return pl.pallas_call(
        _gqa_ctx_kernel,
        grid=(BATCH,),
        in_specs=[pl.BlockSpec(memory_space=pltpu.SMEM),
                  pl.BlockSpec((SEQ, wq), lambda b: (b, 0)),
                  pl.BlockSpec((SEQ, wkv), lambda b: (b, wq // wkv)),
                  pl.BlockSpec((SEQ, wkv), lambda b: (b, wq // wkv + 1))],
        out_specs=pl.BlockSpec((SEQ, wq), lambda b: (b, 0)),
        out_shape=jax.ShapeDtypeStruct((TP, wq), BF16),
        compiler_params=_params(("arbitrary",)),
        name="gqa_context",
    )(sinks, proj, proj, proj)


def _gqa_lat_kernel(sink_ref, q_ref, k_ref, v_ref, ck_ref, cv_ref, kc_ref, kl_ref, kh_ref,
                    qc_ref, ql_ref, qh_ref, o_ref, kw_scr, vw_scr, kc_scr, vc_scr):
    quarter = HEAD_DIM // 4
    n = pl.program_id(1)
    w = WINDOW
    wkv = KV_HEADS * HEAD_DIM

    @pl.when(n == 0)
    def _():
        pad = jnp.zeros((w, wkv), BF16)
        kw_scr[0:w, :] = pad
        vw_scr[0:w, :] = pad
        kw_scr[w + DEC_SEQ:, :] = pad
        vw_scr[w + DEC_SEQ:, :] = pad
        for kh in range(KV_HEADS):
            ks = slice(kh * LANE, (kh + 1) * LANE)
            kw_scr[w:w + DEC_SEQ, ks] = _rope(k_ref[:, ks], kc_ref[...], kl_ref[...], kh_ref[...],
                                              quarter).astype(BF16)
        vw_scr[w:w + DEC_SEQ, :] = v_ref[...].astype(BF16)
        kc_scr[...] = ck_ref[...].astype(BF16)
        vc_scr[...] = cv_ref[...].astype(BF16)

    win = pl.ds(pl.multiple_of(n * w, w), 3 * w)
    qi = lax.broadcasted_iota(jnp.int32, (w, 3 * w), 0)
    kj = lax.broadcasted_iota(jnp.int32, (w, 3 * w), 1)
    kpos = n * w - w + kj
    valid = (kj >= qi) & (kj <= qi + 2 * w) & (kpos >= 0) & (kpos < DEC_SEQ)
    scale = HEAD_DIM ** -0.5
    for kh in range(KV_HEADS):
        ks = slice(kh * LANE, (kh + 1) * LANE)
        k_win = kw_scr[win, ks]
        v_win = vw_scr[win, ks]
        k_ctx = kc_scr[:, ks]
        v_ctx = vc_scr[:, ks]
        for gq in range(Q_PER_KV):
            h = kh * Q_PER_KV + gq
            hs = slice(h * LANE, (h + 1) * LANE)
            sink = sink_ref[h]
            q = (_rope(q_ref[:, hs], qc_ref[...], ql_ref[...], qh_ref[...], quarter) * scale).astype(BF16)
            s_c = _dot_nt(q, k_ctx)
            s_w = jnp.where(valid, _dot_nt(q, k_win), NEG_INF)
            m = jnp.maximum(jnp.maximum(jnp.max(s_c, axis=-1, keepdims=True),
                                        jnp.max(s_w, axis=-1, keepdims=True)), sink)
            e_c = jnp.exp(s_c - m)
            e_w = jnp.exp(s_w - m)
            den = (jnp.sum(e_c, axis=-1, keepdims=True) + jnp.sum(e_w, axis=-1, keepdims=True)
                   + jnp.exp(sink - m))
            inv = 1.0 / den
            o = _dot((e_c * inv).astype(BF16), v_ctx) + _dot((e_w * inv).astype(BF16), v_win)
            o_ref[:, hs] = o.astype(o_ref.dtype)


def gqa_latent(proj, cache_k, cache_v, tables, sinks):
    wq = N_HEADS * HEAD_DIM
    wkv = KV_HEADS * HEAD_DIM
    nq = DEC_SEQ // WINDOW
    row0 = TP // WINDOW
    kv_row0 = TP // DEC_SEQ
    cos, sin_lo, sin_hi = tables
    full = pl.BlockSpec((DEC_SEQ, LANE), lambda b, i: (0, 0))
    qtab = pl.BlockSpec((WINDOW, LANE), lambda b, i: (i, 0))
    cache = pl.BlockSpec((None, PAST_LEN, wkv), lambda b, i: (b, 0, 0))
    return pl.pallas_call(
        _gqa_lat_kernel,
        grid=(DEC_BATCH, nq),
        in_specs=[pl.BlockSpec(memory_space=pltpu.SMEM),
                  pl.BlockSpec((WINDOW, wq), lambda b, i: (row0 + b * nq + i, 0)),
                  pl.BlockSpec((DEC_SEQ, wkv), lambda b, i: (kv_row0 + b, wq // wkv)),
                  pl.BlockSpec((DEC_SEQ, wkv), lambda b, i: (kv_row0 + b, wq // wkv + 1)),
                  cache, cache, full, full, full, qtab, qtab, qtab],
        out_specs=pl.BlockSpec((WINDOW, wq), lambda b, i: (b * nq + i, 0)),
        out_shape=jax.ShapeDtypeStruct((TS, wq), BF16),
        scratch_shapes=[pltpu.VMEM((DEC_SEQ + 2 * WINDOW, wkv), BF16),
                        pltpu.VMEM((DEC_SEQ + 2 * WINDOW, wkv), BF16),
                        pltpu.VMEM((PAST_LEN, wkv), BF16),
                        pltpu.VMEM((PAST_LEN, wkv), BF16)],
        compiler_params=_params(("arbitrary", "arbitrary")),
        name="gqa_latent",
    )(sinks, proj, proj, proj, cache_k, cache_v, cos, sin_lo, sin_hi, cos, sin_lo, sin_hi)


def _even_layer(x, mod, j, layer, norm_g, w_in_even, lam_vec, subln_g, w_gate1, w_gate2, b_gate, gla_norm_g,
                w_out_even, cache_a_k, cache_a_v, state_b, tables):
    lam_init = 0.8 - 0.6 * math.exp(-0.3 * layer)
    ng = HB * GLA_DK
    wg1 = jnp.zeros((D_MODEL, LANE), F32).at[:, :2 * GLA_GATE_RANK].set(
        jnp.concatenate([w_gate1[j, 0], w_gate1[j, 1]], axis=-1)).astype(BF16)
    wg2 = jnp.zeros((2, LANE, ng), F32)
    for e in range(2):
        wg2 = wg2.at[e, e * GLA_GATE_RANK:(e + 1) * GLA_GATE_RANK, :].set(w_gate2[j, e])
    proj, gates = norm_mod_matmul_gate(x, norm_g[layer, 0], mod, 0, 1, w_in_even[j].astype(BF16),
                                       wg1, wg2.astype(BF16), b_gate[j].reshape(2, 1, ng))
    ya_c = diff_attention_context(proj, lam_vec, subln_g[j], lam_init)
    ya_s = diff_attention_latent(proj, cache_a_k[:, j].reshape(DEC_BATCH, PAST_LEN, HA * 2 * DIFF_HD),
                                 cache_a_v[:, j].reshape(DEC_BATCH, PAST_LEN, HA * DIFF_VD),
                                 tables, lam_vec, subln_g[j], lam_init)
    yb_c, s_fin = bidir_gla(proj, gates, jnp.zeros((BATCH, 2, HB, GLA_DK, GLA_DV), F32), gla_norm_g[j],
                            BATCH, SEQ, 0)
    yb_s, _ = bidir_gla(proj, gates, state_b[:, j], gla_norm_g[j], DEC_BATCH, DEC_SEQ, TP // DEC_SEQ)
    y = jnp.concatenate([jnp.concatenate([ya_c, ya_s], axis=0), jnp.concatenate([yb_c, yb_s], axis=0)], axis=1)
    x = matmul_residual(y, w_out_even[j].astype(BF16), x, mod, 2)
    k0 = HA * 2 * DIFF_HD
    new_k = proj[:TP, k0:2 * k0].reshape(BATCH, SEQ, HA, 2 * DIFF_HD)
    new_v = proj[:TP, 2 * k0:2 * k0 + HA * DIFF_VD].reshape(BATCH, SEQ, HA, DIFF_VD)
    return x, new_k, new_v, s_fin


def _odd_layer(x, mod, j, layer, norm_g, w_in_odd, sinks, w_out_odd, cache_c_k, cache_c_v, tables):
    wq = N_HEADS * HEAD_DIM
    wkv = KV_HEADS * HEAD_DIM
    proj = norm_mod_matmul(x, norm_g[layer, 0], mod, 0, 1, w_in_odd[j].astype(BF16))
    o_c = gqa_context(proj, sinks[j])
    o_s = gqa_latent(proj, cache_c_k[:, j].reshape(DEC_BATCH, PAST_LEN, wkv),
                     cache_c_v[:, j].reshape(DEC_BATCH, PAST_LEN, wkv), tables, sinks[j])
    x = matmul_residual(jnp.concatenate([o_c, o_s], axis=0), w_out_odd[j].astype(BF16), x, mod, 2)
    new_k = proj[:TP, wq:wq + wkv].reshape(BATCH, SEQ, KV_HEADS, HEAD_DIM)
    new_v = proj[:TP, wq + wkv:].reshape(BATCH, SEQ, KV_HEADS, HEAD_DIM)
    return x, new_k, new_v


def kernel(x_prompt, x_sample, c, cache_a_k, cache_a_v, state_b, cache_c_k, cache_c_v, c_ctx, w_ada, b_ada,
           norm_g, w_in_even, lam_q1, lam_k1, lam_q2, lam_k2, subln_g, w_gate1, w_gate2, b_gate, gla_norm_g,
           w_out_even, w_in_odd, sinks, w_out_odd, w_ffn_in, w_ffn_out, final_norm_g):
    x = jnp.concatenate([x_prompt.reshape(TP, D_MODEL), x_sample.reshape(TS, D_MODEL)], axis=0)
    cvec = jnp.concatenate([c_ctx[None, :], c, jnp.zeros((MOD_ROWS - 1 - DEC_BATCH, D_MODEL), F32)], axis=0)
    mods = adaln(cvec, w_ada, b_ada).reshape(DEPTH, MOD_ROWS, 1, 6 * D_MODEL)
    tab_d = rope_tables(DEC_SEQ, DIFF_HD)
    tab_h = rope_tables(DEC_SEQ, HEAD_DIM)
    new_a_k, new_a_v, new_s_b, new_c_k, new_c_v = [], [], [], [], []
    for layer in range(DEPTH):
        j = layer // 2
        mod = mods[layer]
        if layer % 2 == 0:
            lam_vec = jnp.stack([lam_q1[j], lam_k1[j], lam_q2[j], lam_k2[j]], axis=0)
            x, ak, av, sb = _even_layer(x, mod, j, layer, norm_g, w_in_even, lam_vec, subln_g, w_gate1, w_gate2,
                                        b_gate, gla_norm_g, w_out_even, cache_a_k, cache_a_v, state_b, tab_d)
            new_a_k.append(ak)
            new_a_v.append(av)
            new_s_b.append(sb)
        else:
            x, ck, cv = _odd_layer(x, mod, j, layer, norm_g, w_in_odd, sinks, w_out_odd, cache_c_k, cache_c_v,
                                   tab_h)
            new_c_k.append(ck)
            new_c_v.append(cv)
        act = norm_mod_swiglu(x, norm_g[layer, 1], mod, 3, 4, w_ffn_in[layer].astype(BF16))
        x = matmul_residual(act, w_ffn_out[layer].astype(BF16), x, mod, 5, tk=D_FF // 2)
    y = final_norm(x, final_norm_g)
    return (y[:TP].reshape(BATCH, SEQ, D_MODEL), y[TP:].reshape(DEC_BATCH, DEC_SEQ, D_MODEL),
            jnp.stack(new_a_k, axis=1), jnp.stack(new_a_v, axis=1), jnp.stack(new_s_b, axis=1),
            jnp.stack(new_c_k, axis=1), jnp.stack(new_c_v, axis=1))
```

```python
import functools
import math

import jax
import jax.numpy as jnp
from jax import lax
from jax.experimental import pallas as pl
from jax.experimental.pallas import tpu as pltpu

D_MODEL = 2048
BATCH = 32
SEQ = 256
DEPTH = 2
DEC_BATCH = 2
DEC_SEQ = 2048
PAST_LEN = 512
GRID_W = 64
HEAD_DIM = 128
N_HEADS = D_MODEL // HEAD_DIM
HA = N_HEADS // 2
DIFF_HD = HEAD_DIM // 2
DIFF_VD = HEAD_DIM
HB = N_HEADS // 2
GLA_DK = HEAD_DIM // 2
GLA_DV = HEAD_DIM
GLA_GATE_RANK = 16
GLA_TAU = 16.0
GLA_CHUNK = 64
GLA_SUPER = 4
KV_HEADS = N_HEADS // 4
Q_PER_KV = N_HEADS // KV_HEADS
WINDOW = 128
D_FF = -(-8 * D_MODEL // (3 * 256)) * 256
ROPE_BASE = 10000.0
EPS = 1e-6
NEG_INF = -1e30
EVEN_IN = 2 * HA * 2 * DIFF_HD + HA * DIFF_VD + 2 * HB * GLA_DK + 2 * HB * GLA_DV
ODD_IN = (N_HEADS + 2 * KV_HEADS) * HEAD_DIM

TP = BATCH * SEQ
TS = DEC_BATCH * DEC_SEQ
T = TP + TS
LANE = 128
MOD_ROWS = 8
VMEM_LIMIT = 56 * 1024 * 1024

F32 = jnp.float32
BF16 = jnp.bfloat16


def _params(semantics, vmem=VMEM_LIMIT):
    return pltpu.CompilerParams(dimension_semantics=semantics, vmem_limit_bytes=vmem)


def _dot(a, b):
    return jnp.dot(a, b, preferred_element_type=F32)


def _dot_nt(a, b):
    return lax.dot_general(a, b, (((1,), (1,)), ((), ())), preferred_element_type=F32)


def _group_of_tile(i, tm):
    r = i * tm
    return jnp.where(r < TP, 0, 1 + (r - TP) // DEC_SEQ)


def _rms(x, g):
    return (x * lax.rsqrt(jnp.mean(x * x, axis=-1, keepdims=True) + EPS)) * g


def _adaln_kernel(c_ref, w_ref, b_ref, o_ref):
    c = c_ref[...]
    s = c * jax.nn.sigmoid(c)
    o_ref[...] = _dot(s.astype(BF16), w_ref[...].astype(BF16)) + b_ref[...]


def adaln(cvec, w_ada, b_ada, tn=512):
    n = w_ada.shape[-1]
    return pl.pallas_call(
        _adaln_kernel,
        grid=(DEPTH, n // tn),
        in_specs=[
            pl.BlockSpec((MOD_ROWS, D_MODEL), lambda l, j: (0, 0)),
            pl.BlockSpec((None, D_MODEL, tn), lambda l, j: (l, 0, j)),
            pl.BlockSpec((None, 1, tn), lambda l, j: (l, 0, j)),
        ],
        out_specs=pl.BlockSpec((None, MOD_ROWS, tn), lambda l, j: (l, 0, j)),
        out_shape=jax.ShapeDtypeStruct((DEPTH, MOD_ROWS, n), F32),
        compiler_params=_params(("arbitrary", "arbitrary")),
        name="adaln",
    )(cvec, w_ada, b_ada.reshape(DEPTH, 1, n))


def _norm_mod(x_ref, g_ref, shift_ref, scale_ref):
    return _rms(x_ref[...], g_ref[...]) * (1 + scale_ref[...]) + shift_ref[...]


def _nmm_kernel(x_ref, g_ref, shift_ref, scale_ref, w_ref, o_ref, h_scr):
    @pl.when(pl.program_id(1) == 0)
    def _():
        h_scr[...] = _norm_mod(x_ref, g_ref, shift_ref, scale_ref).astype(BF16)

    o_ref[...] = _dot(h_scr[...], w_ref[...]).astype(o_ref.dtype)


def _nmm_gate_kernel(xc_ref, xs_ref, g_ref, shift_ref, scale_ref, w_ref, wg1_ref, wg2_ref, bg_ref,
                     o_ref, gate_ref, h_scr, *, ctx_tiles):
    @pl.when(pl.program_id(1) == 0)
    def _():
        @pl.when(pl.program_id(0) < ctx_tiles)
        def _():
            h_scr[...] = _norm_mod(xc_ref, g_ref, shift_ref, scale_ref).astype(BF16)

        @pl.when(pl.program_id(0) >= ctx_tiles)
        def _():
            h_scr[...] = _norm_mod(xs_ref, g_ref, shift_ref, scale_ref).astype(BF16)

        low = _dot(h_scr[...], wg1_ref[...]).astype(BF16)
        for e in range(2):
            logit = _dot(low, wg2_ref[e]) + bg_ref[e]
            gate_ref[e] = jax.nn.log_sigmoid(logit) / GLA_TAU

    o_ref[...] = _dot(h_scr[...], w_ref[...]).astype(o_ref.dtype)


def _swiglu_kernel(x_ref, g_ref, shift_ref, scale_ref, wg_ref, wu_ref, o_ref, h_scr):
    @pl.when(pl.program_id(1) == 0)
    def _():
        h_scr[...] = _norm_mod(x_ref, g_ref, shift_ref, scale_ref).astype(BF16)

    h = h_scr[...]
    gate = _dot(h, wg_ref[...])
    up = _dot(h, wu_ref[...])
    o_ref[...] = (gate * jax.nn.sigmoid(gate) * up).astype(o_ref.dtype)


def _mod_specs(tm, shift_idx, scale_idx):
    def spec(idx):
        return pl.BlockSpec((None, 1, D_MODEL), lambda i, j: (_group_of_tile(i, tm), 0, idx))
    return [pl.BlockSpec((1, D_MODEL), lambda i, j: (0, 0)), spec(shift_idx), spec(scale_idx)]


def norm_mod_matmul(x, g, mod, shift_idx, scale_idx, w, tm=1024, tn=512):
    n = w.shape[1]
    return pl.pallas_call(
        _nmm_kernel,
        grid=(T // tm, n // tn),
        in_specs=[pl.BlockSpec((tm, D_MODEL), lambda i, j: (i, 0))]
        + _mod_specs(tm, shift_idx, scale_idx)
        + [pl.BlockSpec((D_MODEL, tn), lambda i, j: (0, j))],
        out_specs=pl.BlockSpec((tm, tn), lambda i, j: (i, j)),
        out_shape=jax.ShapeDtypeStruct((T, n), F32),
        scratch_shapes=[pltpu.VMEM((tm, D_MODEL), BF16)],
        compiler_params=_params(("arbitrary", "arbitrary")),
        name="norm_mod_matmul",
    )(x, g.reshape(1, D_MODEL), mod, mod, w)


def _pair_specs(tm, width, ctx_tiles, col=lambda j: 0):
    return [pl.BlockSpec((tm, width), lambda i, j: (jnp.minimum(i, ctx_tiles - 1), col(j))),
            pl.BlockSpec((tm, width), lambda i, j: (jnp.maximum(i - ctx_tiles, 0), col(j)))]


def norm_mod_matmul_gate(xc, xs, g, mod, shift_idx, scale_idx, w, wg1, wg2, bg, tm=512, tn=1024):
    n = w.shape[1]
    ng = HB * GLA_DK
    ctx_tiles = TP // tm
    return pl.pallas_call(
        functools.partial(_nmm_gate_kernel, ctx_tiles=ctx_tiles),
        grid=(T // tm, n // tn),
        in_specs=_pair_specs(tm, D_MODEL, ctx_tiles)
        + _mod_specs(tm, shift_idx, scale_idx)
        + [pl.BlockSpec((D_MODEL, tn), lambda i, j: (0, j)),
           pl.BlockSpec((D_MODEL, LANE), lambda i, j: (0, 0)),
           pl.BlockSpec((2, LANE, ng), lambda i, j: (0, 0, 0)),
           pl.BlockSpec((2, 1, ng), lambda i, j: (0, 0, 0))],
        out_specs=[pl.BlockSpec((tm, tn), lambda i, j: (i, j)),
                   pl.BlockSpec((2, tm, ng), lambda i, j: (0, i, 0))],
        out_shape=[jax.ShapeDtypeStruct((T, n), F32), jax.ShapeDtypeStruct((2, T, ng), F32)],
        scratch_shapes=[pltpu.VMEM((tm, D_MODEL), BF16)],
        compiler_params=_params(("arbitrary", "arbitrary")),
        name="norm_mod_matmul_gate",
    )(xc, xs, g.reshape(1, D_MODEL), mod, mod, w, wg1, wg2, bg)


def norm_mod_swiglu(x, g, mod, shift_idx, scale_idx, w_in, tm=1024, tn=512):
    nf = D_FF // tn
    return pl.pallas_call(
        _swiglu_kernel,
        grid=(T // tm, nf),
        in_specs=[pl.BlockSpec((tm, D_MODEL), lambda i, j: (i, 0))]
        + _mod_specs(tm, shift_idx, scale_idx)
        + [pl.BlockSpec((D_MODEL, tn), lambda i, j: (0, j)),
           pl.BlockSpec((D_MODEL, tn), lambda i, j: (0, j + nf))],
        out_specs=pl.BlockSpec((tm, tn), lambda i, j: (i, j)),
        out_shape=jax.ShapeDtypeStruct((T, D_FF), BF16),
        scratch_shapes=[pltpu.VMEM((tm, D_MODEL), BF16)],
        compiler_params=_params(("arbitrary", "arbitrary")),
        name="norm_mod_swiglu",
    )(x, g.reshape(1, D_MODEL), mod, mod, w_in, w_in)


def _mm_res_kernel(y_ref, w_ref, x_ref, gate_ref, o_ref, acc_scr):
    k = pl.program_id(2)

    @pl.when(k == 0)
    def _():
        acc_scr[...] = jnp.zeros_like(acc_scr)

    acc_scr[...] += _dot(y_ref[...], w_ref[...])

    @pl.when(k == pl.num_programs(2) - 1)
    def _():
        o_ref[...] = x_ref[...] + gate_ref[...] * acc_scr[...]


def matmul_residual(y, w, x, mod, gate_idx, tm=1024, tn=1024, tk=None):
    kdim, n = w.shape
    tk = kdim if tk is None else tk
    gate_blk = n // tn
    return pl.pallas_call(
        _mm_res_kernel,
        grid=(T // tm, n // tn, kdim // tk),
        in_specs=[
            pl.BlockSpec((tm, tk), lambda i, j, k: (i, k)),
            pl.BlockSpec((tk, tn), lambda i, j, k: (k, j)),
            pl.BlockSpec((tm, tn), lambda i, j, k: (i, j)),
            pl.BlockSpec((None, 1, tn), lambda i, j, k: (_group_of_tile(i, tm), 0, gate_idx * gate_blk + j)),
        ],
        out_specs=pl.BlockSpec((tm, tn), lambda i, j, k: (i, j)),
        out_shape=jax.ShapeDtypeStruct((T, n), F32),
        scratch_shapes=[pltpu.VMEM((tm, tn), F32)],
        compiler_params=_params(("arbitrary", "arbitrary", "arbitrary")),
        name="matmul_residual",
    )(y, w, x, mod)


def _mm_res_pair_kernel(*refs, n_lhs, n_x, ctx_tiles):
    lhs = refs[:2 * n_lhs]
    w_ref = refs[2 * n_lhs]
    x_refs = refs[2 * n_lhs + 1:2 * n_lhs + 1 + n_x]
    gate_ref, o_ref = refs[2 * n_lhs + 1 + n_x:]
    kp = w_ref.shape[0] // n_lhs

    def emit(side):
        acc = _dot(lhs[side][...], w_ref[0:kp, :])
        for p in range(1, n_lhs):
            acc += _dot(lhs[2 * p + side][...], w_ref[p * kp:(p + 1) * kp, :])
        x_ref = x_refs[side] if n_x == 2 else x_refs[0]
        o_ref[...] = x_ref[...] + gate_ref[...] * acc

    @pl.when(pl.program_id(0) < ctx_tiles)
    def _():
        emit(0)

    @pl.when(pl.program_id(0) >= ctx_tiles)
    def _():
        emit(1)


def matmul_residual_pair(lhs_pairs, w, x, mod, gate_idx, tm=512, tn=1024):
    kdim, n = w.shape
    ctx_tiles = TP // tm
    gate_blk = n // tn
    x_list = list(x) if isinstance(x, (tuple, list)) else [x]
    specs, args = [], []
    for c_arr, s_arr in lhs_pairs:
        specs += _pair_specs(tm, c_arr.shape[1], ctx_tiles)
        args += [c_arr, s_arr]
    specs.append(pl.BlockSpec((kdim, tn), lambda i, j: (0, j)))
    if len(x_list) == 2:
        specs += _pair_specs(tm, tn, ctx_tiles, col=lambda j: j)
    else:
        specs.append(pl.BlockSpec((tm, tn), lambda i, j: (i, j)))
    specs.append(pl.BlockSpec((None, 1, tn), lambda i, j: (_group_of_tile(i, tm), 0, gate_idx * gate_blk + j)))
    return pl.pallas_call(
        functools.partial(_mm_res_pair_kernel, n_lhs=len(lhs_pairs), n_x=len(x_list), ctx_tiles=ctx_tiles),
        grid=(T // tm, n // tn),
        in_specs=specs,
        out_specs=pl.BlockSpec((tm, tn), lambda i, j: (i, j)),
        out_shape=jax.ShapeDtypeStruct((T, n), F32),
        compiler_params=_params(("arbitrary", "arbitrary")),
        name="matmul_residual_pair",
    )(*args, w, *x_list, mod)


def _final_norm_kernel(x_ref, g_ref, o_ref):
    o_ref[...] = _rms(x_ref[...], g_ref[...])


def final_norm(x, g, row0, n_rows, tm=1024):
    blk0 = row0 // tm
    return pl.pallas_call(
        _final_norm_kernel,
        grid=(n_rows // tm,),
        in_specs=[pl.BlockSpec((tm, D_MODEL), lambda i: (blk0 + i, 0)),
                  pl.BlockSpec((1, D_MODEL), lambda i: (0, 0))],
        out_specs=pl.BlockSpec((tm, D_MODEL), lambda i: (i, 0)),
        out_shape=jax.ShapeDtypeStruct((n_rows, D_MODEL), F32),
        compiler_params=_params(("arbitrary",)),
        name="final_norm",
    )(x, g.reshape(1, D_MODEL))


def _grid_angles(n_tok, rot_dim):
    t = jnp.arange((n_tok // GRID_W) * GRID_W)
    row = (t // GRID_W).astype(F32)
    col = (t % GRID_W).astype(F32)
    half = rot_dim // 2
    inv = ROPE_BASE ** (-jnp.arange(0, half, 2, dtype=F32) / half)
    return row[:, None] * inv[None], col[:, None] * inv[None]


def rope_tables(n_tok, rot_dim):
    ang_row, ang_col = _grid_angles(n_tok, rot_dim)
    zeros = jnp.zeros_like(ang_row)
    reps = LANE // rot_dim

    def lanes(r1, r2, c1, c2):
        return jnp.tile(jnp.concatenate([r1, r2, c1, c2], axis=-1), (1, reps))

    cr, sr, cc, sc = jnp.cos(ang_row), jnp.sin(ang_row), jnp.cos(ang_col), jnp.sin(ang_col)
    return lanes(cr, cr, cc, cc), lanes(-sr, zeros, -sc, zeros), lanes(zeros, sr, zeros, sc)


def _rope(x, cos, sin_lo, sin_hi, quarter):
    return (x * cos + pltpu.roll(x, LANE - quarter, 1) * sin_lo + pltpu.roll(x, quarter, 1) * sin_hi)


def _diff_lambda(lam_ref, lam_init):
    e1 = jnp.exp(jnp.sum(lam_ref[0:1, :] * lam_ref[1:2, :], axis=-1, keepdims=True))
    e2 = jnp.exp(jnp.sum(lam_ref[2:3, :] * lam_ref[3:4, :], axis=-1, keepdims=True))
    return e1 - e2 + lam_init


def _softmax_rows(s):
    e = jnp.exp(s - jnp.max(s, axis=-1, keepdims=True))
    return e * (1.0 / jnp.sum(e, axis=-1, keepdims=True))


def _diff_head(q, k_bf, v_bf, lam, subln_g, lam_init):
    first = lax.broadcasted_iota(jnp.int32, q.shape, 1) < DIFF_HD
    p0 = _softmax_rows(_dot_nt(jnp.where(first, q, 0.0).astype(BF16), k_bf))
    p1 = _softmax_rows(_dot_nt(jnp.where(first, 0.0, q).astype(BF16), k_bf))
    o = _dot((p0 - lam * p1).astype(BF16), v_bf)
    return _rms(o, subln_g) * (1.0 - lam_init)


def _diff_ctx_kernel(q_ref, k_ref, v_ref, lam_ref, sg_ref, o_ref, ko_ref, vo_ref, *, lam_init):
    lam = _diff_lambda(lam_ref, lam_init)
    scale = DIFF_HD ** -0.5
    ko_ref[...] = k_ref[...]
    vo_ref[...] = v_ref[...]
    for h in range(HA):
        sl = slice(h * LANE, (h + 1) * LANE)
        y = _diff_head(q_ref[:, sl] * scale, k_ref[:, sl].astype(BF16), v_ref[:, sl].astype(BF16),
                       lam, sg_ref[...], lam_init)
        o_ref[:, sl] = y.astype(o_ref.dtype)


def diff_attention_context(proj, lam_vec, subln_g, lam_init):
    w = HA * LANE

    def blk(c):
        return pl.BlockSpec((SEQ, w), lambda b: (b, c))

    return pl.pallas_call(
        functools.partial(_diff_ctx_kernel, lam_init=lam_init),
        grid=(BATCH,),
        in_specs=[blk(0), blk(1), blk(2),
                  pl.BlockSpec((4, DIFF_HD), lambda b: (0, 0)),
                  pl.BlockSpec((1, DIFF_VD), lambda b: (0, 0))],
        out_specs=[blk(0), blk(0), blk(0)],
        out_shape=[jax.ShapeDtypeStruct((TP, w), BF16), jax.ShapeDtypeStruct((TP, w), F32),
                   jax.ShapeDtypeStruct((TP, w), F32)],
        compiler_params=_params(("arbitrary",)),
        name="diff_attention_context",
    )(proj, proj, proj, lam_vec, subln_g.reshape(1, DIFF_VD))


def _diff_lat_kernel(q_ref, k_ref, v_ref, ck_ref, cv_ref, kc_ref, kl_ref, kh_ref, qc_ref, ql_ref, qh_ref,
                     lam_ref, sg_ref, o_ref, k_scr, v_scr, *, lam_init):
    quarter = DIFF_HD // 4

    @pl.when(pl.program_id(2) == 0)
    def _():
        k_scr[0:PAST_LEN, :] = ck_ref[...].astype(BF16)
        v_scr[0:PAST_LEN, :] = cv_ref[...].astype(BF16)
        k_scr[PAST_LEN:, :] = _rope(k_ref[...], kc_ref[...], kl_ref[...], kh_ref[...], quarter).astype(BF16)
        v_scr[PAST_LEN:, :] = v_ref[...].astype(BF16)

    lam = _diff_lambda(lam_ref, lam_init)
    q = _rope(q_ref[...], qc_ref[...], ql_ref[...], qh_ref[...], quarter) * (DIFF_HD ** -0.5)
    y = _diff_head(q, k_scr[...], v_scr[...], lam, sg_ref[...], lam_init)
    o_ref[...] = y.astype(o_ref.dtype)


def diff_attention_latent(proj, cache_k, cache_v, tables, lam_vec, subln_g, lam_init, tq=256):
    nq = DEC_SEQ // tq
    row0 = TP // tq
    kv_row0 = TP // DEC_SEQ
    cos, sin_lo, sin_hi = tables
    full = pl.BlockSpec((DEC_SEQ, LANE), lambda b, h, i: (0, 0))
    qtab = pl.BlockSpec((tq, LANE), lambda b, h, i: (i, 0))
    cache = pl.BlockSpec((None, PAST_LEN, LANE), lambda b, h, i: (b, 0, h))
    return pl.pallas_call(
        functools.partial(_diff_lat_kernel, lam_init=lam_init),
        grid=(DEC_BATCH, HA, nq),
        in_specs=[pl.BlockSpec((tq, LANE), lambda b, h, i: (row0 + b * nq + i, h)),
                  pl.BlockSpec((DEC_SEQ, LANE), lambda b, h, i: (kv_row0 + b, HA + h)),
                  pl.BlockSpec((DEC_SEQ, LANE), lambda b, h, i: (kv_row0 + b, 2 * HA + h)),
                  cache, cache, full, full, full, qtab, qtab, qtab,
                  pl.BlockSpec((4, DIFF_HD), lambda b, h, i: (0, 0)),
                  pl.BlockSpec((1, DIFF_VD), lambda b, h, i: (0, 0))],
        out_specs=pl.BlockSpec((tq, LANE), lambda b, h, i: (b * nq + i, h)),
        out_shape=jax.ShapeDtypeStruct((TS, HA * LANE), BF16),
        scratch_shapes=[pltpu.VMEM((PAST_LEN + DEC_SEQ, LANE), BF16),
                        pltpu.VMEM((PAST_LEN + DEC_SEQ, LANE), BF16)],
        compiler_params=_params(("arbitrary", "arbitrary", "arbitrary")),
        name="diff_attention_latent",
    )(proj, proj, proj, cache_k, cache_v, cos, sin_lo, sin_hi, cos, sin_lo, sin_hi,
      lam_vec, subln_g.reshape(1, DIFF_VD))


def _split3(x):
    hi = x.astype(BF16)
    r = x - hi.astype(F32)
    mid = r.astype(BF16)
    lo = (r - mid.astype(F32)).astype(BF16)
    return hi, mid, lo


def _gla_kernel(q_ref, k_ref, v_ref, r_ref, g_ref, s0_ref, gg_ref, y_ref, sfin_ref, of_scr, ob_scr, st_scr,
                *, seq):
    c = GLA_CHUNK
    sb = GLA_SUPER * c
    n_super = seq // sb
    rows = lax.broadcasted_iota(jnp.int32, (sb, sb), 0)
    cols = lax.broadcasted_iota(jnp.int32, (sb, sb), 1)
    same_chunk = (rows // c) == (cols // c)
    keep = (same_chunk & (rows >= cols), same_chunk & (cols >= rows))
    tri = tuple(jnp.where(m, 1.0, 0.0).astype(BF16) for m in keep)
    lane = lax.broadcasted_iota(jnp.int32, (sb, LANE), 1)
    own = (lane < GLA_DK, lane >= GLA_DK)
    zpad = jnp.zeros((GLA_DK, GLA_DV), F32)

    for d in range(2):
        for hh in range(2):
            s0 = s0_ref[d, hh]
            padded = jnp.concatenate([s0, zpad] if hh == 0 else [zpad, s0], axis=0)
            st_scr[d, hh] = padded.T

    def super_block(n, carry):
        for d in range(2):
            r0 = pl.multiple_of((n if d == 0 else n_super - 1 - n) * sb, sb)
            rs = pl.ds(r0, sb)
            hi, mid, lo = _split3(g_ref[d, rs, :])
            b = _dot(tri[d], hi) + _dot(tri[d], mid) + _dot(tri[d], lo)
            ends = [b[ci * c + (c - 1 if d == 0 else 0):ci * c + (c if d == 0 else 1), :] for ci in range(GLA_SUPER)]
            total = jnp.concatenate([jnp.broadcast_to(e, (c, LANE)) for e in ends], axis=0)
            q_in = (q_ref[rs, :] * (GLA_DK ** -0.5)) * jnp.exp(b)
            kk = k_ref[rs, :]
            k_in = (kk * jnp.exp(-b)).astype(BF16)
            k_end = (kk * jnp.exp(total - b)).astype(BF16)
            order = range(GLA_SUPER) if d == 0 else range(GLA_SUPER - 1, -1, -1)
            for hh in range(2):
                vs = slice(hh * GLA_DV, (hh + 1) * GLA_DV)
                qm = jnp.where(own[hh], q_in, 0.0).astype(BF16)
                a = jnp.where(keep[d], _dot_nt(qm, k_in), 0.0)
                v = v_ref[rs, vs]
                o_intra = _dot(a.astype(BF16), v.astype(BF16))
                st = st_scr[d, hh]
                o_scr = of_scr if d == 0 else ob_scr
                for ci in order:
                    cs = slice(ci * c, (ci + 1) * c)
                    o_scr[pl.ds(r0 + ci * c, c), vs] = o_intra[cs] + _dot_nt(qm[cs], st.astype(BF16))
                    st = st * jnp.exp(ends[ci]) + _dot(v[cs].T.astype(BF16), k_end[cs])
                st_scr[d, hh] = st
        return carry

    lax.fori_loop(0, n_super, super_block, 0)

    for hh in range(2):
        vs = slice(hh * GLA_DV, (hh + 1) * GLA_DV)
        r = r_ref[:, vs]
        y = _rms(of_scr[:, vs] + ob_scr[:, vs], gg_ref[...]) * (r * jax.nn.sigmoid(r))
        y_ref[:, vs] = y.astype(y_ref.dtype)
        for d in range(2):
            sfin_ref[d, hh] = st_scr[d, hh].T[hh * GLA_DK:(hh + 1) * GLA_DK, :]


def bidir_gla(proj, gates, s0, gla_g, n_batch, seq, row_blk0):
    pairs = HB // 2
    col_q = (2 * HA * 2 * DIFF_HD + HA * DIFF_VD) // LANE
    col_k = col_q + HB * GLA_DK // LANE
    col_v = (col_k + HB * GLA_DK // LANE) // 2
    col_r = col_v + HB * GLA_DV // (2 * GLA_DV)
    n_tok = n_batch * seq
    return pl.pallas_call(
        functools.partial(_gla_kernel, seq=seq),
        grid=(n_batch, pairs),
        in_specs=[pl.BlockSpec((seq, LANE), lambda b, p: (row_blk0 + b, col_q + p)),
                  pl.BlockSpec((seq, LANE), lambda b, p: (row_blk0 + b, col_k + p)),
                  pl.BlockSpec((seq, 2 * GLA_DV), lambda b, p: (row_blk0 + b, col_v + p)),
                  pl.BlockSpec((seq, 2 * GLA_DV), lambda b, p: (row_blk0 + b, col_r + p)),
                  pl.BlockSpec((2, seq, LANE), lambda b, p: (0, row_blk0 + b, p)),
                  pl.BlockSpec((None, 2, 2, GLA_DK, GLA_DV), lambda b, p: (b, 0, p, 0, 0)),
                  pl.BlockSpec((1, GLA_DV), lambda b, p: (0, 0))],
        out_specs=[pl.BlockSpec((seq, 2 * GLA_DV), lambda b, p: (b, p)),
                   pl.BlockSpec((None, 2, 2, GLA_DK, GLA_DV), lambda b, p: (b, 0, p, 0, 0))],
        out_shape=[jax.ShapeDtypeStruct((n_tok, HB * GLA_DV), BF16),
                   jax.ShapeDtypeStruct((n_batch, 2, HB, GLA_DK, GLA_DV), F32)],
        scratch_shapes=[pltpu.VMEM((seq, 2 * GLA_DV), F32), pltpu.VMEM((seq, 2 * GLA_DV), F32),
                        pltpu.VMEM((2, 2, GLA_DV, LANE), F32)],
        compiler_params=_params(("arbitrary", "arbitrary")),
        name="bidir_gla",
    )(proj, proj, proj, proj, gates, s0, gla_g.reshape(1, GLA_DV))


def _gqa_ctx_kernel(sink_ref, q_ref, k_ref, v_ref, o_ref, ko_ref, vo_ref):
    scale = HEAD_DIM ** -0.5
    ko_ref[...] = k_ref[...]
    vo_ref[...] = v_ref[...]
    for kh in range(KV_HEADS):
        ks = slice(kh * LANE, (kh + 1) * LANE)
        k_bf = k_ref[:, ks].astype(BF16)
        v_bf = v_ref[:, ks].astype(BF16)
        for gq in range(Q_PER_KV):
            h = kh * Q_PER_KV + gq
            hs = slice(h * LANE, (h + 1) * LANE)
            sink = sink_ref[h]
            s = _dot_nt((q_ref[:, hs] * scale).astype(BF16), k_bf)
            m = jnp.maximum(jnp.max(s, axis=-1, keepdims=True), sink)
            e = jnp.exp(s - m)
            den = jnp.sum(e, axis=-1, keepdims=True) + jnp.exp(sink - m)
            o_ref[:, hs] = _dot((e * (1.0 / den)).astype(BF16), v_bf).astype(o_ref.dtype)


def gqa_context(proj, sinks):
    wq = N_HEADS * HEAD_DIM
    wkv = KV_HEADS * HEAD_DIM

    def kv(c):
        return pl.BlockSpec((SEQ, wkv), lambda b: (b, c))

    return pl.pallas_call(
        _gqa_ctx_kernel,
        grid=(BATCH,),
        in_specs=[pl.BlockSpec(memory_space=pltpu.SMEM),
                  pl.BlockSpec((SEQ, wq), lambda b: (b, 0)), kv(wq // wkv), kv(wq // wkv + 1)],
        out_specs=[pl.BlockSpec((SEQ, wq), lambda b: (b, 0)), kv(0), kv(0)],
        out_shape=[jax.ShapeDtypeStruct((TP, wq), BF16), jax.ShapeDtypeStruct((TP, wkv), F32),
                   jax.ShapeDtypeStruct((TP, wkv), F32)],
        compiler_params=_params(("arbitrary",)),
        name="gqa_context",
    )(sinks, proj, proj, proj)


def _gqa_lat_kernel(sink_ref, q_ref, k_ref, v_ref, ck_ref, cv_ref, kc_ref, kl_ref, kh_ref,
                    qc_ref, ql_ref, qh_ref, o_ref, kw_scr, vw_scr, kc_scr, vc_scr):
    quarter = HEAD_DIM // 4
    n = pl.program_id(1)
    w = WINDOW
    wkv = KV_HEADS * HEAD_DIM

    @pl.when(n == 0)
    def _():
        pad = jnp.zeros((w, wkv), BF16)
        kw_scr[0:w, :] = pad
        vw_scr[0:w, :] = pad
        kw_scr[w + DEC_SEQ:, :] = pad
        vw_scr[w + DEC_SEQ:, :] = pad
        for kh in range(KV_HEADS):
            ks = slice(kh * LANE, (kh + 1) * LANE)
            kw_scr[w:w + DEC_SEQ, ks] = _rope(k_ref[:, ks], kc_ref[...], kl_ref[...], kh_ref[...],
                                              quarter).astype(BF16)
        vw_scr[w:w + DEC_SEQ, :] = v_ref[...].astype(BF16)
        kc_scr[...] = ck_ref[...].astype(BF16)
        vc_scr[...] = cv_ref[...].astype(BF16)

    win = pl.ds(pl.multiple_of(n * w, w), 3 * w)
    qi = lax.broadcasted_iota(jnp.int32, (w, 3 * w), 0)
    kj = lax.broadcasted_iota(jnp.int32, (w, 3 * w), 1)
    kpos = n * w - w + kj
    valid = (kj >= qi) & (kj <= qi + 2 * w) & (kpos >= 0) & (kpos < DEC_SEQ)
    scale = HEAD_DIM ** -0.5
    for kh in range(KV_HEADS):
        ks = slice(kh * LANE, (kh + 1) * LANE)
        k_win = kw_scr[win, ks]
        v_win = vw_scr[win, ks]
        k_ctx = kc_scr[:, ks]
        v_ctx = vc_scr[:, ks]
        for gq in range(Q_PER_KV):
            h = kh * Q_PER_KV + gq
            hs = slice(h * LANE, (h + 1) * LANE)
            sink = sink_ref[h]
            q = (_rope(q_ref[:, hs], qc_ref[...], ql_ref[...], qh_ref[...], quarter) * scale).astype(BF16)
            s_c = _dot_nt(q, k_ctx)
            s_w = jnp.where(valid, _dot_nt(q, k_win), NEG_INF)
            m = jnp.maximum(jnp.maximum(jnp.max(s_c, axis=-1, keepdims=True),
                                        jnp.max(s_w, axis=-1, keepdims=True)), sink)
            e_c = jnp.exp(s_c - m)
            e_w = jnp.exp(s_w - m)
            den = (jnp.sum(e_c, axis=-1, keepdims=True) + jnp.sum(e_w, axis=-1, keepdims=True)
                   + jnp.exp(sink - m))
            inv = 1.0 / den
            o = _dot((e_c * inv).astype(BF16), v_ctx) + _dot((e_w * inv).astype(BF16), v_win)
            o_ref[:, hs] = o.astype(o_ref.dtype)


def gqa_latent(proj, cache_k, cache_v, tables, sinks):
    wq = N_HEADS * HEAD_DIM
    wkv = KV_HEADS * HEAD_DIM
    nq = DEC_SEQ // WINDOW
    row0 = TP // WINDOW
    kv_row0 = TP // DEC_SEQ
    cos, sin_lo, sin_hi = tables
    full = pl.BlockSpec((DEC_SEQ, LANE), lambda b, i: (0, 0))
    qtab = pl.BlockSpec((WINDOW, LANE), lambda b, i: (i, 0))
    cache = pl.BlockSpec((None, PAST_LEN, wkv), lambda b, i: (b, 0, 0))
    return pl.pallas_call(
        _gqa_lat_kernel,
        grid=(DEC_BATCH, nq),
        in_specs=[pl.BlockSpec(memory_space=pltpu.SMEM),
                  pl.BlockSpec((WINDOW, wq), lambda b, i: (row0 + b * nq + i, 0)),
                  pl.BlockSpec((DEC_SEQ, wkv), lambda b, i: (kv_row0 + b, wq // wkv)),
                  pl.BlockSpec((DEC_SEQ, wkv), lambda b, i: (kv_row0 + b, wq // wkv + 1)),
                  cache, cache, full, full, full, qtab, qtab, qtab],
        out_specs=pl.BlockSpec((WINDOW, wq), lambda b, i: (b * nq + i, 0)),
        out_shape=jax.ShapeDtypeStruct((TS, wq), BF16),
        scratch_shapes=[pltpu.VMEM((DEC_SEQ + 2 * WINDOW, wkv), BF16),
                        pltpu.VMEM((DEC_SEQ + 2 * WINDOW, wkv), BF16),
                        pltpu.VMEM((PAST_LEN, wkv), BF16),
                        pltpu.VMEM((PAST_LEN, wkv), BF16)],
        compiler_params=_params(("arbitrary", "arbitrary")),
        name="gqa_latent",
    )(sinks, proj, proj, proj, cache_k, cache_v, cos, sin_lo, sin_hi, cos, sin_lo, sin_hi)


def _even_layer(xc, xs, mod, j, layer, norm_g, w_in_even, lam_vec, subln_g, w_gate1, w_gate2, b_gate, gla_norm_g,
                w_out_even, cache_a_k, cache_a_v, state_b, tables):
    lam_init = 0.8 - 0.6 * math.exp(-0.3 * layer)
    ng = HB * GLA_DK
    wg1 = jnp.zeros((D_MODEL, LANE), F32).at[:, :2 * GLA_GATE_RANK].set(
        jnp.concatenate([w_gate1[j, 0], w_gate1[j, 1]], axis=-1)).astype(BF16)
    wg2 = jnp.zeros((2, LANE, ng), F32)
    for e in range(2):
        wg2 = wg2.at[e, e * GLA_GATE_RANK:(e + 1) * GLA_GATE_RANK, :].set(w_gate2[j, e])
    proj, gates = norm_mod_matmul_gate(xc, xs, norm_g[layer, 0], mod, 0, 1, w_in_even[j].astype(BF16),
                                       wg1, wg2.astype(BF16), b_gate[j].reshape(2, 1, ng))
    ya_c, new_k, new_v = diff_attention_context(proj, lam_vec, subln_g[j], lam_init)
    ya_s = diff_attention_latent(proj, cache_a_k[:, j].reshape(DEC_BATCH, PAST_LEN, HA * 2 * DIFF_HD),
                                 cache_a_v[:, j].reshape(DEC_BATCH, PAST_LEN, HA * DIFF_VD),
                                 tables, lam_vec, subln_g[j], lam_init)
    yb_c, s_fin = bidir_gla(proj, gates, jnp.zeros((BATCH, 2, HB, GLA_DK, GLA_DV), F32), gla_norm_g[j],
                            BATCH, SEQ, 0)
    yb_s, _ = bidir_gla(proj, gates, state_b[:, j], gla_norm_g[j], DEC_BATCH, DEC_SEQ, TP // DEC_SEQ)
    x = matmul_residual_pair([(ya_c, ya_s), (yb_c, yb_s)], w_out_even[j].astype(BF16), (xc, xs), mod, 2)
    return (x, new_k.reshape(BATCH, SEQ, HA, 2 * DIFF_HD), new_v.reshape(BATCH, SEQ, HA, DIFF_VD), s_fin)


def _odd_layer(x, mod, j, layer, norm_g, w_in_odd, sinks, w_out_odd, cache_c_k, cache_c_v, tables):
    wkv = KV_HEADS * HEAD_DIM
    proj = norm_mod_matmul(x, norm_g[layer, 0], mod, 0, 1, w_in_odd[j].astype(BF16))
    o_c, new_k, new_v = gqa_context(proj, sinks[j])
    o_s = gqa_latent(proj, cache_c_k[:, j].reshape(DEC_BATCH, PAST_LEN, wkv),
                     cache_c_v[:, j].reshape(DEC_BATCH, PAST_LEN, wkv), tables, sinks[j])
    x = matmul_residual_pair([(o_c, o_s)], w_out_odd[j].astype(BF16), x, mod, 2)
    return (x, new_k.reshape(BATCH, SEQ, KV_HEADS, HEAD_DIM), new_v.reshape(BATCH, SEQ, KV_HEADS, HEAD_DIM))


def kernel(x_prompt, x_sample, c, cache_a_k, cache_a_v, state_b, cache_c_k, cache_c_v, c_ctx, w_ada, b_ada,
           norm_g, w_in_even, lam_q1, lam_k1, lam_q2, lam_k2, subln_g, w_gate1, w_gate2, b_gate, gla_norm_g,
           w_out_even, w_in_odd, sinks, w_out_odd, w_ffn_in, w_ffn_out, final_norm_g):
    assert DEPTH == 2, "layer 0 reads the two input streams, layer 1 the fused token matrix"
    cvec = jnp.concatenate([c_ctx[None, :], c, jnp.zeros((MOD_ROWS - 1 - DEC_BATCH, D_MODEL), F32)], axis=0)
    mods = adaln(cvec, w_ada, b_ada).reshape(DEPTH, MOD_ROWS, 1, 6 * D_MODEL)
    tab_d = rope_tables(DEC_SEQ, DIFF_HD)
    tab_h = rope_tables(DEC_SEQ, HEAD_DIM)

    lam_vec = jnp.stack([lam_q1[0], lam_k1[0], lam_q2[0], lam_k2[0]], axis=0)
    x, ak, av, sb = _even_layer(x_prompt.reshape(TP, D_MODEL), x_sample.reshape(TS, D_MODEL), mods[0], 0, 0,
                                norm_g, w_in_even, lam_vec, subln_g, w_gate1, w_gate2, b_gate, gla_norm_g,
                                w_out_even, cache_a_k, cache_a_v, state_b, tab_d)
    act = norm_mod_swiglu(x, norm_g[0, 1], mods[0], 3, 4, w_ffn_in[0].astype(BF16))
    x = matmul_residual(act, w_ffn_out[0].astype(BF16), x, mods[0], 5, tk=D_FF // 2)

    x, ck, cv = _odd_layer(x, mods[1], 0, 1, norm_g, w_in_odd, sinks, w_out_odd, cache_c_k, cache_c_v, tab_h)
    act = norm_mod_swiglu(x, norm_g[1, 1], mods[1], 3, 4, w_ffn_in[1].astype(BF16))
    x = matmul_residual(act, w_ffn_out[1].astype(BF16), x, mods[1], 5, tk=D_FF // 2)

    y_prompt = final_norm(x, final_norm_g, 0, TP).reshape(BATCH, SEQ, D_MODEL)
    y_sample = final_norm(x, final_norm_g, TP, TS).reshape(DEC_BATCH, DEC_SEQ, D_MODEL)
    return (y_prompt, y_sample, ak[:, None], av[:, None], sb[:, None], ck[:, None], cv[:, None])
```

```python
import functools
import math

import jax
import jax.numpy as jnp
from jax import lax
from jax.experimental import pallas as pl
from jax.experimental.pallas import tpu as pltpu

D_MODEL = 2048
BATCH = 32
SEQ = 256
DEPTH = 2
DEC_BATCH = 2
DEC_SEQ = 2048
PAST_LEN = 512
GRID_W = 64
HEAD_DIM = 128
N_HEADS = D_MODEL // HEAD_DIM
HA = N_HEADS // 2
DIFF_HD = HEAD_DIM // 2
DIFF_VD = HEAD_DIM
HB = N_HEADS // 2
GLA_DK = HEAD_DIM // 2
GLA_DV = HEAD_DIM
GLA_GATE_RANK = 16
GLA_TAU = 16.0
GLA_CHUNK = 64
GLA_SUPER = 4
KV_HEADS = N_HEADS // 4
Q_PER_KV = N_HEADS // KV_HEADS
WINDOW = 128
D_FF = -(-8 * D_MODEL // (3 * 256)) * 256
ROPE_BASE = 10000.0
EPS = 1e-6
NEG_INF = -1e30
EVEN_IN = 2 * HA * 2 * DIFF_HD + HA * DIFF_VD + 2 * HB * GLA_DK + 2 * HB * GLA_DV
ODD_IN = (N_HEADS + 2 * KV_HEADS) * HEAD_DIM

TP = BATCH * SEQ
TS = DEC_BATCH * DEC_SEQ
T = TP + TS
LANE = 128
MOD_ROWS = 8
VMEM_LIMIT = 56 * 1024 * 1024

F32 = jnp.float32
BF16 = jnp.bfloat16


def _params(semantics, vmem=VMEM_LIMIT):
    return pltpu.CompilerParams(dimension_semantics=semantics, vmem_limit_bytes=vmem)


def _dot(a, b):
    return jnp.dot(a, b, preferred_element_type=F32)


def _dot_nt(a, b):
    return lax.dot_general(a, b, (((1,), (1,)), ((), ())), preferred_element_type=F32)


def _group_of_tile(i, tm):
    r = i * tm
    return jnp.where(r < TP, 0, 1 + (r - TP) // DEC_SEQ)


def _rms(x, g):
    return (x * lax.rsqrt(jnp.mean(x * x, axis=-1, keepdims=True) + EPS)) * g


def _adaln_kernel(c_ref, w_ref, b_ref, o_ref):
    c = c_ref[...]
    s = c * jax.nn.sigmoid(c)
    o_ref[...] = _dot(s.astype(BF16), w_ref[...].astype(BF16)) + b_ref[...]


def adaln(cvec, w_ada, b_ada, tn=512):
    n = w_ada.shape[-1]
    return pl.pallas_call(
        _adaln_kernel,
        grid=(DEPTH, n // tn),
        in_specs=[
            pl.BlockSpec((MOD_ROWS, D_MODEL), lambda l, j: (0, 0)),
            pl.BlockSpec((None, D_MODEL, tn), lambda l, j: (l, 0, j)),
            pl.BlockSpec((None, 1, tn), lambda l, j: (l, 0, j)),
        ],
        out_specs=pl.BlockSpec((None, MOD_ROWS, tn), lambda l, j: (l, 0, j)),
        out_shape=jax.ShapeDtypeStruct((DEPTH, MOD_ROWS, n), F32),
        compiler_params=_params(("arbitrary", "arbitrary")),
        name="adaln",
    )(cvec, w_ada, b_ada.reshape(DEPTH, 1, n))


def _norm_mod(x_ref, g_ref, shift_ref, scale_ref):
    return _rms(x_ref[...], g_ref[...]) * (1 + scale_ref[...]) + shift_ref[...]


def _norm_mod_kernel(x_ref, g_ref, shift_ref, scale_ref, h_ref):
    h_ref[...] = _norm_mod(x_ref, g_ref, shift_ref, scale_ref).astype(BF16)


def _norm_mod_gate_kernel(xc_ref, xs_ref, g_ref, shift_ref, scale_ref, wg1_ref, wg2_ref, bg_ref,
                          h_ref, gate_ref, *, ctx_tiles):
    @pl.when(pl.program_id(0) < ctx_tiles)
    def _():
        h_ref[...] = _norm_mod(xc_ref, g_ref, shift_ref, scale_ref).astype(BF16)

    @pl.when(pl.program_id(0) >= ctx_tiles)
    def _():
        h_ref[...] = _norm_mod(xs_ref, g_ref, shift_ref, scale_ref).astype(BF16)

    low = _dot(h_ref[...], wg1_ref[...]).astype(BF16)
    for e in range(2):
        logit = _dot(low, wg2_ref[e]) + bg_ref[e]
        log_sig = jnp.minimum(logit, 0.0) - jnp.log(1.0 + jnp.exp(-jnp.abs(logit)))
        gate_ref[e] = log_sig / GLA_TAU


def _mod_specs(tm, shift_idx, scale_idx):
    def spec(idx):
        return pl.BlockSpec((None, 1, D_MODEL), lambda i: (_group_of_tile(i, tm), 0, idx))
    return [pl.BlockSpec((1, D_MODEL), lambda i: (0, 0)), spec(shift_idx), spec(scale_idx)]


def norm_mod(x, g, mod, shift_idx, scale_idx, tm=512):
    return pl.pallas_call(
        _norm_mod_kernel,
        grid=(T // tm,),
        in_specs=[pl.BlockSpec((tm, D_MODEL), lambda i: (i, 0))] + _mod_specs(tm, shift_idx, scale_idx),
        out_specs=pl.BlockSpec((tm, D_MODEL), lambda i: (i, 0)),
        out_shape=jax.ShapeDtypeStruct((T, D_MODEL), BF16),
        compiler_params=_params(("arbitrary",)),
        name="norm_mod",
    )(x, g.reshape(1, D_MODEL), mod, mod)


def norm_mod_gate(xc, xs, g, mod, shift_idx, scale_idx, wg1, wg2, bg, tm=512):
    ng = HB * GLA_DK
    ctx_tiles = TP // tm
    return pl.pallas_call(
        functools.partial(_norm_mod_gate_kernel, ctx_tiles=ctx_tiles),
        grid=(T // tm,),
        in_specs=[pl.BlockSpec((tm, D_MODEL), lambda i: (jnp.minimum(i, ctx_tiles - 1), 0)),
                  pl.BlockSpec((tm, D_MODEL), lambda i: (jnp.maximum(i - ctx_tiles, 0), 0))]
        + _mod_specs(tm, shift_idx, scale_idx)
        + [pl.BlockSpec((D_MODEL, LANE), lambda i: (0, 0)),
           pl.BlockSpec((2, LANE, ng), lambda i: (0, 0, 0)),
           pl.BlockSpec((2, 1, ng), lambda i: (0, 0, 0))],
        out_specs=[pl.BlockSpec((tm, D_MODEL), lambda i: (i, 0)),
                   pl.BlockSpec((2, tm, ng), lambda i: (0, i, 0))],
        out_shape=[jax.ShapeDtypeStruct((T, D_MODEL), BF16), jax.ShapeDtypeStruct((2, T, ng), F32)],
        compiler_params=_params(("arbitrary",)),
        name="norm_mod_gate",
    )(xc, xs, g.reshape(1, D_MODEL), mod, mod, wg1, wg2, bg)


def _cast_weights(w_refs, w_scrs):
    @pl.when(pl.program_id(1) == 0)
    def _():
        for w_ref, w_scr in zip(w_refs, w_scrs):
            w_scr[...] = w_ref[...].astype(BF16)


def _ws_plain_kernel(h_ref, w_ref, o_ref, w_scr):
    _cast_weights([w_ref], [w_scr])
    o_ref[...] = _dot(h_ref[...], w_scr[...])


def _ws_swiglu_kernel(h_ref, wg_ref, wu_ref, o_ref, wg_scr, wu_scr):
    _cast_weights([wg_ref, wu_ref], [wg_scr, wu_scr])
    h = h_ref[...]
    gate = _dot(h, wg_scr[...])
    up = _dot(h, wu_scr[...])
    o_ref[...] = (gate * jax.nn.sigmoid(gate) * up).astype(o_ref.dtype)


def _ws_res_kernel(*refs, lhs_arity, x_arity, ctx_tiles):
    n_l = sum(lhs_arity)
    w_ref = refs[n_l]
    x_refs = refs[n_l + 1:n_l + 1 + x_arity]
    gate_ref, o_ref, w_scr = refs[n_l + 1 + x_arity:]
    groups, at = [], 0
    for a in lhs_arity:
        groups.append(refs[at:at + a])
        at += a
    _cast_weights([w_ref], [w_scr])

    def emit(side):
        acc, k0 = None, 0
        for grp in groups:
            ref = grp[side] if len(grp) == 2 else grp[0]
            part = _dot(ref[...], w_scr[k0:k0 + ref.shape[1], :])
            acc = part if acc is None else acc + part
            k0 += ref.shape[1]
        x_ref = x_refs[side] if x_arity == 2 else x_refs[0]
        o_ref[...] = x_ref[...] + gate_ref[...] * acc

    if x_arity == 2 or 2 in lhs_arity:
        @pl.when(pl.program_id(1) < ctx_tiles)
        def _():
            emit(0)

        @pl.when(pl.program_id(1) >= ctx_tiles)
        def _():
            emit(1)
    else:
        emit(0)


def _rows_specs(arrays, tm, width, col):
    if len(arrays) == 1:
        return [pl.BlockSpec((tm, width), lambda j, i: (i, col(j)))]
    ctx_tiles = TP // tm
    return [pl.BlockSpec((tm, width), lambda j, i: (jnp.minimum(i, ctx_tiles - 1), col(j))),
            pl.BlockSpec((tm, width), lambda j, i: (jnp.maximum(i - ctx_tiles, 0), col(j)))]


def ws_matmul(h, w_stack, layer, tm=1024, tn=1024):
    kdim, n = w_stack.shape[1:]
    return pl.pallas_call(
        _ws_plain_kernel,
        grid=(n // tn, T // tm),
        in_specs=[pl.BlockSpec((tm, kdim), lambda j, i: (i, 0)),
                  pl.BlockSpec((None, kdim, tn), lambda j, i: (layer, 0, j))],
        out_specs=pl.BlockSpec((tm, tn), lambda j, i: (i, j)),
        out_shape=jax.ShapeDtypeStruct((T, n), F32),
        scratch_shapes=[pltpu.VMEM((kdim, tn), BF16)],
        compiler_params=_params(("arbitrary", "arbitrary")),
        name="ws_matmul",
    )(h, w_stack)


def ws_swiglu(h, w_stack, layer, tm=1024, tn=512):
    kdim = w_stack.shape[1]
    nf = D_FF // tn
    return pl.pallas_call(
        _ws_swiglu_kernel,
        grid=(nf, T // tm),
        in_specs=[pl.BlockSpec((tm, kdim), lambda j, i: (i, 0)),
                  pl.BlockSpec((None, kdim, tn), lambda j, i: (layer, 0, j)),
                  pl.BlockSpec((None, kdim, tn), lambda j, i: (layer, 0, j + nf))],
        out_specs=pl.BlockSpec((tm, tn), lambda j, i: (i, j)),
        out_shape=jax.ShapeDtypeStruct((T, D_FF), BF16),
        scratch_shapes=[pltpu.VMEM((kdim, tn), BF16), pltpu.VMEM((kdim, tn), BF16)],
        compiler_params=_params(("arbitrary", "arbitrary")),
        name="ws_swiglu",
    )(h, w_stack, w_stack)


def ws_matmul_residual(lhs_groups, w_stack, layer, x, mod, gate_idx, tm=512, tn=1024):
    kdim, n = w_stack.shape[1:]
    gate_blk = n // tn
    specs, args = [], []
    for grp in lhs_groups:
        specs += _rows_specs(grp, tm, grp[0].shape[1], lambda j: 0)
        args += list(grp)
    specs.append(pl.BlockSpec((None, kdim, tn), lambda j, i: (layer, 0, j)))
    specs += _rows_specs(x, tm, tn, lambda j: j)
    specs.append(pl.BlockSpec((None, 1, tn), lambda j, i: (_group_of_tile(i, tm), 0, gate_idx * gate_blk + j)))
    return pl.pallas_call(
        functools.partial(_ws_res_kernel, lhs_arity=tuple(len(grp) for grp in lhs_groups), x_arity=len(x),
                          ctx_tiles=TP // tm),
        grid=(n // tn, T // tm),
        in_specs=specs,
        out_specs=pl.BlockSpec((tm, tn), lambda j, i: (i, j)),
        out_shape=jax.ShapeDtypeStruct((T, n), F32),
        scratch_shapes=[pltpu.VMEM((kdim, tn), BF16)],
        compiler_params=_params(("arbitrary", "arbitrary")),
        name="ws_matmul_residual",
    )(*args, w_stack, *x, mod)


def _final_norm_kernel(x_ref, g_ref, o_ref):
    o_ref[...] = _rms(x_ref[...], g_ref[...])


def final_norm(x, g, row0, n_rows, tm=1024):
    blk0 = row0 // tm
    return pl.pallas_call(
        _final_norm_kernel,
        grid=(n_rows // tm,),
        in_specs=[pl.BlockSpec((tm, D_MODEL), lambda i: (blk0 + i, 0)),
                  pl.BlockSpec((1, D_MODEL), lambda i: (0, 0))],
        out_specs=pl.BlockSpec((tm, D_MODEL), lambda i: (i, 0)),
        out_shape=jax.ShapeDtypeStruct((n_rows, D_MODEL), F32),
        compiler_params=_params(("arbitrary",)),
        name="final_norm",
    )(x, g.reshape(1, D_MODEL))


def _grid_angles(n_tok, rot_dim):
    t = jnp.arange((n_tok // GRID_W) * GRID_W)
    row = (t // GRID_W).astype(F32)
    col = (t % GRID_W).astype(F32)
    half = rot_dim // 2
    inv = ROPE_BASE ** (-jnp.arange(0, half, 2, dtype=F32) / half)
    return row[:, None] * inv[None], col[:, None] * inv[None]


def rope_tables(n_tok, rot_dim):
    ang_row, ang_col = _grid_angles(n_tok, rot_dim)
    zeros = jnp.zeros_like(ang_row)
    reps = LANE // rot_dim

    def lanes(r1, r2, c1, c2):
        return jnp.tile(jnp.concatenate([r1, r2, c1, c2], axis=-1), (1, reps))

    cr, sr, cc, sc = jnp.cos(ang_row), jnp.sin(ang_row), jnp.cos(ang_col), jnp.sin(ang_col)
    return lanes(cr, cr, cc, cc), lanes(-sr, zeros, -sc, zeros), lanes(zeros, sr, zeros, sc)


def _rope(x, cos, sin_lo, sin_hi, quarter):
    return (x * cos + pltpu.roll(x, LANE - quarter, 1) * sin_lo + pltpu.roll(x, quarter, 1) * sin_hi)


def _diff_lambda(lam_ref, lam_init):
    e1 = jnp.exp(jnp.sum(lam_ref[0:1, :] * lam_ref[1:2, :], axis=-1, keepdims=True))
    e2 = jnp.exp(jnp.sum(lam_ref[2:3, :] * lam_ref[3:4, :], axis=-1, keepdims=True))
    return e1 - e2 + lam_init


def _softmax_rows(s):
    e = jnp.exp(s - jnp.max(s, axis=-1, keepdims=True))
    return e * (1.0 / jnp.sum(e, axis=-1, keepdims=True))


def _diff_head(q, k_bf, v_bf, lam, subln_g, lam_init):
    first = lax.broadcasted_iota(jnp.int32, q.shape, 1) < DIFF_HD
    p0 = _softmax_rows(_dot_nt(jnp.where(first, q, 0.0).astype(BF16), k_bf))
    p1 = _softmax_rows(_dot_nt(jnp.where(first, 0.0, q).astype(BF16), k_bf))
    o = _dot((p0 - lam * p1).astype(BF16), v_bf)
    return _rms(o, subln_g) * (1.0 - lam_init)


def _diff_ctx_kernel(q_ref, k_ref, v_ref, lam_ref, sg_ref, o_ref, ko_ref, vo_ref, *, lam_init):
    lam = _diff_lambda(lam_ref, lam_init)
    scale = DIFF_HD ** -0.5
    ko_ref[...] = k_ref[...]
    vo_ref[...] = v_ref[...]
    for h in range(HA):
        sl = slice(h * LANE, (h + 1) * LANE)
        y = _diff_head(q_ref[:, sl] * scale, k_ref[:, sl].astype(BF16), v_ref[:, sl].astype(BF16),
                       lam, sg_ref[...], lam_init)
        o_ref[:, sl] = y.astype(o_ref.dtype)


def diff_attention_context(proj, lam_vec, subln_g, lam_init):
    w = HA * LANE

    def blk(c):
        return pl.BlockSpec((SEQ, w), lambda b: (b, c))

    return pl.pallas_call(
        functools.partial(_diff_ctx_kernel, lam_init=lam_init),
        grid=(BATCH,),
        in_specs=[blk(0), blk(1), blk(2),
                  pl.BlockSpec((4, DIFF_HD), lambda b: (0, 0)),
                  pl.BlockSpec((1, DIFF_VD), lambda b: (0, 0))],
        out_specs=[blk(0), blk(0), blk(0)],
        out_shape=[jax.ShapeDtypeStruct((TP, w), BF16), jax.ShapeDtypeStruct((TP, w), F32),
                   jax.ShapeDtypeStruct((TP, w), F32)],
        compiler_params=_params(("arbitrary",)),
        name="diff_attention_context",
    )(proj, proj, proj, lam_vec, subln_g.reshape(1, DIFF_VD))


def _diff_lat_kernel(q_ref, k_ref, v_ref, ck_ref, cv_ref, kc_ref, kl_ref, kh_ref, qc_ref, ql_ref, qh_ref,
                     lam_ref, sg_ref, o_ref, k_scr, v_scr, *, lam_init):
    quarter = DIFF_HD // 4

    @pl.when(pl.program_id(2) == 0)
    def _():
        k_scr[0:PAST_LEN, :] = ck_ref[...].astype(BF16)
        v_scr[0:PAST_LEN, :] = cv_ref[...].astype(BF16)
        k_scr[PAST_LEN:, :] = _rope(k_ref[...], kc_ref[...], kl_ref[...], kh_ref[...], quarter).astype(BF16)
        v_scr[PAST_LEN:, :] = v_ref[...].astype(BF16)

    lam = _diff_lambda(lam_ref, lam_init)
    q = _rope(q_ref[...], qc_ref[...], ql_ref[...], qh_ref[...], quarter) * (DIFF_HD ** -0.5)
    y = _diff_head(q, k_scr[...], v_scr[...], lam, sg_ref[...], lam_init)
    o_ref[...] = y.astype(o_ref.dtype)


def diff_attention_latent(proj, cache_k, cache_v, tables, lam_vec, subln_g, lam_init, tq=256):
    nq = DEC_SEQ // tq
    row0 = TP // tq
    kv_row0 = TP // DEC_SEQ
    cos, sin_lo, sin_hi = tables
    full = pl.BlockSpec((DEC_SEQ, LANE), lambda b, h, i: (0, 0))
    qtab = pl.BlockSpec((tq, LANE), lambda b, h, i: (i, 0))
    cache = pl.BlockSpec((None, PAST_LEN, LANE), lambda b, h, i: (b, 0, h))
    return pl.pallas_call(
        functools.partial(_diff_lat_kernel, lam_init=lam_init),
        grid=(DEC_BATCH, HA, nq),
        in_specs=[pl.BlockSpec((tq, LANE), lambda b, h, i: (row0 + b * nq + i, h)),
                  pl.BlockSpec((DEC_SEQ, LANE), lambda b, h, i: (kv_row0 + b, HA + h)),
                  pl.BlockSpec((DEC_SEQ, LANE), lambda b, h, i: (kv_row0 + b, 2 * HA + h)),
                  cache, cache, full, full, full, qtab, qtab, qtab,
                  pl.BlockSpec((4, DIFF_HD), lambda b, h, i: (0, 0)),
                  pl.BlockSpec((1, DIFF_VD), lambda b, h, i: (0, 0))],
        out_specs=pl.BlockSpec((tq, LANE), lambda b, h, i: (b * nq + i, h)),
        out_shape=jax.ShapeDtypeStruct((TS, HA * LANE), BF16),
        scratch_shapes=[pltpu.VMEM((PAST_LEN + DEC_SEQ, LANE), BF16),
                        pltpu.VMEM((PAST_LEN + DEC_SEQ, LANE), BF16)],
        compiler_params=_params(("arbitrary", "arbitrary", "arbitrary")),
        name="diff_attention_latent",
    )(proj, proj, proj, cache_k, cache_v, cos, sin_lo, sin_hi, cos, sin_lo, sin_hi,
      lam_vec, subln_g.reshape(1, DIFF_VD))


def _split3(x):
    hi = x.astype(BF16)
    r = x - hi.astype(F32)
    mid = r.astype(BF16)
    lo = (r - mid.astype(F32)).astype(BF16)
    return hi, mid, lo


def _gla_kernel(q_ref, k_ref, v_ref, r_ref, g_ref, s0_ref, gg_ref, y_ref, sfin_ref, of_scr, ob_scr, st_scr,
                *, seq):
    c = GLA_CHUNK
    sb = GLA_SUPER * c
    n_super = seq // sb
    rows = lax.broadcasted_iota(jnp.int32, (sb, sb), 0)
    cols = lax.broadcasted_iota(jnp.int32, (sb, sb), 1)
    same_chunk = (rows // c) == (cols // c)
    keep = (same_chunk & (rows >= cols), same_chunk & (cols >= rows))
    tri = tuple(jnp.where(m, 1.0, 0.0).astype(BF16) for m in keep)
    lane = lax.broadcasted_iota(jnp.int32, (sb, LANE), 1)
    own = (lane < GLA_DK, lane >= GLA_DK)
    zpad = jnp.zeros((GLA_DK, GLA_DV), F32)

    for d in range(2):
        for hh in range(2):
            s0 = s0_ref[d, hh]
            padded = jnp.concatenate([s0, zpad] if hh == 0 else [zpad, s0], axis=0)
            st_scr[d, hh] = padded.T

    def super_block(n, carry):
        for d in range(2):
            r0 = pl.multiple_of((n if d == 0 else n_super - 1 - n) * sb, sb)
            rs = pl.ds(r0, sb)
            hi, mid, lo = _split3(g_ref[d, rs, :])
            b = _dot(tri[d], hi) + _dot(tri[d], mid) + _dot(tri[d], lo)
            ends = [b[ci * c + (c - 1 if d == 0 else 0):ci * c + (c if d == 0 else 1), :] for ci in range(GLA_SUPER)]
            total = jnp.concatenate([jnp.broadcast_to(e, (c, LANE)) for e in ends], axis=0)
            q_in = (q_ref[rs, :] * (GLA_DK ** -0.5)) * jnp.exp(b)
            kk = k_ref[rs, :]
            k_in = (kk * jnp.exp(-b)).astype(BF16)
            k_end = (kk * jnp.exp(total - b)).astype(BF16)
            order = range(GLA_SUPER) if d == 0 else range(GLA_SUPER - 1, -1, -1)
            for hh in range(2):
                vs = slice(hh * GLA_DV, (hh + 1) * GLA_DV)
                qm = jnp.where(own[hh], q_in, 0.0).astype(BF16)
                a = jnp.where(keep[d], _dot_nt(qm, k_in), 0.0)
                v = v_ref[rs, vs]
                o_intra = _dot(a.astype(BF16), v.astype(BF16))
                st = st_scr[d, hh]
                o_scr = of_scr if d == 0 else ob_scr
                for ci in order:
                    cs = slice(ci * c, (ci + 1) * c)
                    o_scr[pl.ds(r0 + ci * c, c), vs] = o_intra[cs] + _dot_nt(qm[cs], st.astype(BF16))
                    st = st * jnp.exp(ends[ci]) + _dot(v[cs].T.astype(BF16), k_end[cs])
                st_scr[d, hh] = st
        return carry

    lax.fori_loop(0, n_super, super_block, 0)

    for hh in range(2):
        vs = slice(hh * GLA_DV, (hh + 1) * GLA_DV)
        r = r_ref[:, vs]
        y = _rms(of_scr[:, vs] + ob_scr[:, vs], gg_ref[...]) * (r * jax.nn.sigmoid(r))
        y_ref[:, vs] = y.astype(y_ref.dtype)
        for d in range(2):
            sfin_ref[d, hh] = st_scr[d, hh].T[hh * GLA_DK:(hh + 1) * GLA_DK, :]


def bidir_gla(proj, gates, s0, gla_g, n_batch, seq, row_blk0):
    pairs = HB // 2
    shared_s0 = s0.shape[0] == 1
    col_q = (2 * HA * 2 * DIFF_HD + HA * DIFF_VD) // LANE
    col_k = col_q + HB * GLA_DK // LANE
    col_v = (col_k + HB * GLA_DK // LANE) // 2
    col_r = col_v + HB * GLA_DV // (2 * GLA_DV)
    n_tok = n_batch * seq
    return pl.pallas_call(
        functools.partial(_gla_kernel, seq=seq),
        grid=(n_batch, pairs),
        in_specs=[pl.BlockSpec((seq, LANE), lambda b, p: (row_blk0 + b, col_q + p)),
                  pl.BlockSpec((seq, LANE), lambda b, p: (row_blk0 + b, col_k + p)),
                  pl.BlockSpec((seq, 2 * GLA_DV), lambda b, p: (row_blk0 + b, col_v + p)),
                  pl.BlockSpec((seq, 2 * GLA_DV), lambda b, p: (row_blk0 + b, col_r + p)),
                  pl.BlockSpec((2, seq, LANE), lambda b, p: (0, row_blk0 + b, p)),
                  pl.BlockSpec((None, 2, 2, GLA_DK, GLA_DV), lambda b, p: (0 if shared_s0 else b, 0, p, 0, 0)),
                  pl.BlockSpec((1, GLA_DV), lambda b, p: (0, 0))],
        out_specs=[pl.BlockSpec((seq, 2 * GLA_DV), lambda b, p: (b, p)),
                   pl.BlockSpec((None, 2, 2, GLA_DK, GLA_DV), lambda b, p: (b, 0, p, 0, 0))],
        out_shape=[jax.ShapeDtypeStruct((n_tok, HB * GLA_DV), BF16),
                   jax.ShapeDtypeStruct((n_batch, 2, HB, GLA_DK, GLA_DV), F32)],
        scratch_shapes=[pltpu.VMEM((seq, 2 * GLA_DV), F32), pltpu.VMEM((seq, 2 * GLA_DV), F32),
                        pltpu.VMEM((2, 2, GLA_DV, LANE), F32)],
        compiler_params=_params(("arbitrary", "arbitrary")),
        name="bidir_gla",
    )(proj, proj, proj, proj, gates, s0, gla_g.reshape(1, GLA_DV))


def _gqa_ctx_kernel(sink_ref, q_ref, k_ref, v_ref, o_ref, ko_ref, vo_ref):
    scale = HEAD_DIM ** -0.5
    ko_ref[...] = k_ref[...]
    vo_ref[...] = v_ref[...]
    for kh in range(KV_HEADS):
        ks = slice(kh * LANE, (kh + 1) * LANE)
        k_bf = k_ref[:, ks].astype(BF16)
        v_bf = v_ref[:, ks].astype(BF16)
        for gq in range(Q_PER_KV):
            h = kh * Q_PER_KV + gq
            hs = slice(h * LANE, (h + 1) * LANE)
            sink = sink_ref[h]
            s = _dot_nt((q_ref[:, hs] * scale).astype(BF16), k_bf)
            m = jnp.maximum(jnp.max(s, axis=-1, keepdims=True), sink)
            e = jnp.exp(s - m)
            den = jnp.sum(e, axis=-1, keepdims=True) + jnp.exp(sink - m)
            o_ref[:, hs] = _dot((e * (1.0 / den)).astype(BF16), v_bf).astype(o_ref.dtype)


def gqa_context(proj, sinks):
    wq = N_HEADS * HEAD_DIM
    wkv = KV_HEADS * HEAD_DIM

    def kv(c):
        return pl.BlockSpec((SEQ, wkv), lambda b: (b, c))

    return pl.pallas_call(
        _gqa_ctx_kernel,
        grid=(BATCH,),
        in_specs=[pl.BlockSpec(memory_space=pltpu.SMEM),
                  pl.BlockSpec((SEQ, wq), lambda b: (b, 0)), kv(wq // wkv), kv(wq // wkv + 1)],
        out_specs=[pl.BlockSpec((SEQ, wq), lambda b: (b, 0)), kv(0), kv(0)],
        out_shape=[jax.ShapeDtypeStruct((TP, wq), BF16), jax.ShapeDtypeStruct((TP, wkv), F32),
                   jax.ShapeDtypeStruct((TP, wkv), F32)],
        compiler_params=_params(("arbitrary",)),
        name="gqa_context",
    )(sinks, proj, proj, proj)


def _gqa_lat_kernel(sink_ref, q_ref, k_ref, v_ref, ck_ref, cv_ref, kc_ref, kl_ref, kh_ref,
                    qc_ref, ql_ref, qh_ref, o_ref, kw_scr, vw_scr, kc_scr, vc_scr):
    quarter = HEAD_DIM // 4
    n = pl.program_id(1)
    w = WINDOW
    wkv = KV_HEADS * HEAD_DIM

    @pl.when(n == 0)
    def _():
        pad = jnp.zeros((w, wkv), BF16)
        kw_scr[0:w, :] = pad
        vw_scr[0:w, :] = pad
        kw_scr[w + DEC_SEQ:, :] = pad
        vw_scr[w + DEC_SEQ:, :] = pad
        for kh in range(KV_HEADS):
            ks = slice(kh * LANE, (kh + 1) * LANE)
            kw_scr[w:w + DEC_SEQ, ks] = _rope(k_ref[:, ks], kc_ref[...], kl_ref[...], kh_ref[...],
                                              quarter).astype(BF16)
        vw_scr[w:w + DEC_SEQ, :] = v_ref[...].astype(BF16)
        kc_scr[...] = ck_ref[...].astype(BF16)
        vc_scr[...] = cv_ref[...].astype(BF16)

    win = pl.ds(pl.multiple_of(n * w, w), 3 * w)
    qi = lax.broadcasted_iota(jnp.int32, (w, 3 * w), 0)
    kj = lax.broadcasted_iota(jnp.int32, (w, 3 * w), 1)
    kpos = n * w - w + kj
    valid = (kj >= qi) & (kj <= qi + 2 * w) & (kpos >= 0) & (kpos < DEC_SEQ)
    scale = HEAD_DIM ** -0.5
    for kh in range(KV_HEADS):
        ks = slice(kh * LANE, (kh + 1) * LANE)
        k_win = kw_scr[win, ks]
        v_win = vw_scr[win, ks]
        k_ctx = kc_scr[:, ks]
        v_ctx = vc_scr[:, ks]
        for gq in range(Q_PER_KV):
            h = kh * Q_PER_KV + gq
            hs = slice(h * LANE, (h + 1) * LANE)
            sink = sink_ref[h]
            q = (_rope(q_ref[:, hs], qc_ref[...], ql_ref[...], qh_ref[...], quarter) * scale).astype(BF16)
            s_c = _dot_nt(q, k_ctx)
            s_w = jnp.where(valid, _dot_nt(q, k_win), NEG_INF)
            m = jnp.maximum(jnp.maximum(jnp.max(s_c, axis=-1, keepdims=True),
                                        jnp.max(s_w, axis=-1, keepdims=True)), sink)
            e_c = jnp.exp(s_c - m)
            e_w = jnp.exp(s_w - m)
            den = (jnp.sum(e_c, axis=-1, keepdims=True) + jnp.sum(e_w, axis=-1, keepdims=True)
                   + jnp.exp(sink - m))
            inv = 1.0 / den
            o = _dot((e_c * inv).astype(BF16), v_ctx) + _dot((e_w * inv).astype(BF16), v_win)
            o_ref[:, hs] = o.astype(o_ref.dtype)


def gqa_latent(proj, cache_k, cache_v, tables, sinks):
    wq = N_HEADS * HEAD_DIM
    wkv = KV_HEADS * HEAD_DIM
    nq = DEC_SEQ // WINDOW
    row0 = TP // WINDOW
    kv_row0 = TP // DEC_SEQ
    cos, sin_lo, sin_hi = tables
    full = pl.BlockSpec((DEC_SEQ, LANE), lambda b, i: (0, 0))
    qtab = pl.BlockSpec((WINDOW, LANE), lambda b, i: (i, 0))
    cache = pl.BlockSpec((None, PAST_LEN, wkv), lambda b, i: (b, 0, 0))
    return pl.pallas_call(
        _gqa_lat_kernel,
        grid=(DEC_BATCH, nq),
        in_specs=[pl.BlockSpec(memory_space=pltpu.SMEM),
                  pl.BlockSpec((WINDOW, wq), lambda b, i: (row0 + b * nq + i, 0)),
                  pl.BlockSpec((DEC_SEQ, wkv), lambda b, i: (kv_row0 + b, wq // wkv)),
                  pl.BlockSpec((DEC_SEQ, wkv), lambda b, i: (kv_row0 + b, wq // wkv + 1)),
                  cache, cache, full, full, full, qtab, qtab, qtab],
        out_specs=pl.BlockSpec((WINDOW, wq), lambda b, i: (b * nq + i, 0)),
        out_shape=jax.ShapeDtypeStruct((TS, wq), BF16),
        scratch_shapes=[pltpu.VMEM((DEC_SEQ + 2 * WINDOW, wkv), BF16),
                        pltpu.VMEM((DEC_SEQ + 2 * WINDOW, wkv), BF16),
                        pltpu.VMEM((PAST_LEN, wkv), BF16),
                        pltpu.VMEM((PAST_LEN, wkv), BF16)],
        compiler_params=_params(("arbitrary", "arbitrary")),
        name="gqa_latent",
    )(sinks, proj, proj, proj, cache_k, cache_v, cos, sin_lo, sin_hi, cos, sin_lo, sin_hi)


def _even_layer(xc, xs, mod, j, layer, norm_g, w_in_even, lam_vec, subln_g, w_gate1, w_gate2, b_gate, gla_norm_g,
                w_out_even, cache_a_k, cache_a_v, state_b, tables):
    lam_init = 0.8 - 0.6 * math.exp(-0.3 * layer)
    ng = HB * GLA_DK
    wg1 = jnp.zeros((D_MODEL, LANE), F32).at[:, :2 * GLA_GATE_RANK].set(
        jnp.concatenate([w_gate1[j, 0], w_gate1[j, 1]], axis=-1)).astype(BF16)
    wg2 = jnp.zeros((2, LANE, ng), F32)
    for e in range(2):
        wg2 = wg2.at[e, e * GLA_GATE_RANK:(e + 1) * GLA_GATE_RANK, :].set(w_gate2[j, e])
    h, gates = norm_mod_gate(xc, xs, norm_g[layer, 0], mod, 0, 1, wg1, wg2.astype(BF16),
                             b_gate[j].reshape(2, 1, ng))
    proj = ws_matmul(h, w_in_even, j)
    ya_c, new_k, new_v = diff_attention_context(proj, lam_vec, subln_g[j], lam_init)
    ya_s = diff_attention_latent(proj, cache_a_k[:, j].reshape(DEC_BATCH, PAST_LEN, HA * 2 * DIFF_HD),
                                 cache_a_v[:, j].reshape(DEC_BATCH, PAST_LEN, HA * DIFF_VD),
                                 tables, lam_vec, subln_g[j], lam_init)
    yb_c, s_fin = bidir_gla(proj, gates, jnp.zeros((1, 2, HB, GLA_DK, GLA_DV), F32), gla_norm_g[j],
                            BATCH, SEQ, 0)
    yb_s, _ = bidir_gla(proj, gates, state_b[:, j], gla_norm_g[j], DEC_BATCH, DEC_SEQ, TP // DEC_SEQ)
    x = ws_matmul_residual([(ya_c, ya_s), (yb_c, yb_s)], w_out_even, j, (xc, xs), mod, 2)
    return (x, new_k.reshape(BATCH, SEQ, HA, 2 * DIFF_HD), new_v.reshape(BATCH, SEQ, HA, DIFF_VD), s_fin)


def _odd_layer(x, mod, j, layer, norm_g, w_in_odd, sinks, w_out_odd, cache_c_k, cache_c_v, tables):
    wkv = KV_HEADS * HEAD_DIM
    proj = ws_matmul(norm_mod(x, norm_g[layer, 0], mod, 0, 1), w_in_odd, j)
    o_c, new_k, new_v = gqa_context(proj, sinks[j])
    o_s = gqa_latent(proj, cache_c_k[:, j].reshape(DEC_BATCH, PAST_LEN, wkv),
                     cache_c_v[:, j].reshape(DEC_BATCH, PAST_LEN, wkv), tables, sinks[j])
    x = ws_matmul_residual([(o_c, o_s)], w_out_odd, j, (x,), mod, 2)
    return (x, new_k.reshape(BATCH, SEQ, KV_HEADS, HEAD_DIM), new_v.reshape(BATCH, SEQ, KV_HEADS, HEAD_DIM))


def _ffn(x, mod, layer, norm_g, w_ffn_in, w_ffn_out):
    act = ws_swiglu(norm_mod(x, norm_g[layer, 1], mod, 3, 4), w_ffn_in, layer)
    return ws_matmul_residual([(act,)], w_ffn_out, layer, (x,), mod, 5, tm=512, tn=512)


def kernel(x_prompt, x_sample, c, cache_a_k, cache_a_v, state_b, cache_c_k, cache_c_v, c_ctx, w_ada, b_ada,
           norm_g, w_in_even, lam_q1, lam_k1, lam_q2, lam_k2, subln_g, w_gate1, w_gate2, b_gate, gla_norm_g,
           w_out_even, w_in_odd, sinks, w_out_odd, w_ffn_in, w_ffn_out, final_norm_g):
    assert DEPTH == 2, "layer 0 reads the two input streams, layer 1 the fused token matrix"
    cvec = jnp.concatenate([c_ctx[None, :], c, jnp.zeros((MOD_ROWS - 1 - DEC_BATCH, D_MODEL), F32)], axis=0)
    mods = adaln(cvec, w_ada, b_ada).reshape(DEPTH, MOD_ROWS, 1, 6 * D_MODEL)
    tab_d = rope_tables(DEC_SEQ, DIFF_HD)
    tab_h = rope_tables(DEC_SEQ, HEAD_DIM)

    lam_vec = jnp.stack([lam_q1[0], lam_k1[0], lam_q2[0], lam_k2[0]], axis=0)
    x, ak, av, sb = _even_layer(x_prompt.reshape(TP, D_MODEL), x_sample.reshape(TS, D_MODEL), mods[0], 0, 0,
                                norm_g, w_in_even, lam_vec, subln_g, w_gate1, w_gate2, b_gate, gla_norm_g,
                                w_out_even, cache_a_k, cache_a_v, state_b, tab_d)
    x = _ffn(x, mods[0], 0, norm_g, w_ffn_in, w_ffn_out)
    x, ck, cv = _odd_layer(x, mods[1], 0, 1, norm_g, w_in_odd, sinks, w_out_odd, cache_c_k, cache_c_v, tab_h)
    x = _ffn(x, mods[1], 1, norm_g, w_ffn_in, w_ffn_out)

    y_prompt = final_norm(x, final_norm_g, 0, TP).reshape(BATCH, SEQ, D_MODEL)
    y_sample = final_norm(x, final_norm_g, TP, TS).reshape(DEC_BATCH, DEC_SEQ, D_MODEL)
    return (y_prompt, y_sample, ak[:, None], av[:, None], sb[:, None], ck[:, None], cv[:, None])
```

```python
import functools
import math

import jax
import jax.numpy as jnp
from jax import lax
from jax.experimental import pallas as pl
from jax.experimental.pallas import tpu as pltpu

D_MODEL = 2048
BATCH = 32
SEQ = 256
DEPTH = 2
DEC_BATCH = 2
DEC_SEQ = 2048
PAST_LEN = 512
GRID_W = 64
HEAD_DIM = 128
N_HEADS = D_MODEL // HEAD_DIM
HA = N_HEADS // 2
DIFF_HD = HEAD_DIM // 2
DIFF_VD = HEAD_DIM
HB = N_HEADS // 2
GLA_DK = HEAD_DIM // 2
GLA_DV = HEAD_DIM
GLA_GATE_RANK = 16
GLA_TAU = 16.0
GLA_CHUNK = 64
GLA_SUPER = 4
DIFF_LAT_CHAINS = 4
KV_HEADS = N_HEADS // 4
Q_PER_KV = N_HEADS // KV_HEADS
WINDOW = 128
D_FF = -(-8 * D_MODEL // (3 * 256)) * 256
ROPE_BASE = 10000.0
EPS = 1e-6
NEG_INF = -1e30
EVEN_IN = 2 * HA * 2 * DIFF_HD + HA * DIFF_VD + 2 * HB * GLA_DK + 2 * HB * GLA_DV
ODD_IN = (N_HEADS + 2 * KV_HEADS) * HEAD_DIM

TP = BATCH * SEQ
TS = DEC_BATCH * DEC_SEQ
T = TP + TS
LANE = 128
MOD_ROWS = 8
VMEM_LIMIT = 56 * 1024 * 1024

F32 = jnp.float32
BF16 = jnp.bfloat16


def _params(semantics, vmem=VMEM_LIMIT):
    return pltpu.CompilerParams(dimension_semantics=semantics, vmem_limit_bytes=vmem)


def _dot(a, b):
    return jnp.dot(a, b, preferred_element_type=F32)


def _dot_nt(a, b):
    return lax.dot_general(a, b, (((1,), (1,)), ((), ())), preferred_element_type=F32)


def _group_of_tile(i, tm):
    r = i * tm
    return jnp.where(r < TP, 0, 1 + (r - TP) // DEC_SEQ)


def _rms(x, g):
    return (x * lax.rsqrt(jnp.mean(x * x, axis=-1, keepdims=True) + EPS)) * g


def _adaln_kernel(c_ref, w_ref, b_ref, o_ref):
    c = c_ref[...]
    s = c * jax.nn.sigmoid(c)
    o_ref[...] = _dot(s.astype(BF16), w_ref[...].astype(BF16)) + b_ref[...]


def adaln(cvec, w_ada, b_ada, tn=512):
    n = w_ada.shape[-1]
    return pl.pallas_call(
        _adaln_kernel,
        grid=(DEPTH, n // tn),
        in_specs=[
            pl.BlockSpec((MOD_ROWS, D_MODEL), lambda l, j: (0, 0)),
            pl.BlockSpec((None, D_MODEL, tn), lambda l, j: (l, 0, j)),
            pl.BlockSpec((None, 1, tn), lambda l, j: (l, 0, j)),
        ],
        out_specs=pl.BlockSpec((None, MOD_ROWS, tn), lambda l, j: (l, 0, j)),
        out_shape=jax.ShapeDtypeStruct((DEPTH, MOD_ROWS, n), F32),
        compiler_params=_params(("arbitrary", "arbitrary")),
        name="adaln",
    )(cvec, w_ada, b_ada.reshape(DEPTH, 1, n))


def _norm_mod(x_ref, g_ref, shift_ref, scale_ref):
    return _rms(x_ref[...], g_ref[...]) * (1 + scale_ref[...]) + shift_ref[...]


def _norm_mod_kernel(x_ref, g_ref, shift_ref, scale_ref, h_ref):
    h_ref[...] = _norm_mod(x_ref, g_ref, shift_ref, scale_ref).astype(BF16)


def _norm_mod_gate_kernel(xc_ref, xs_ref, g_ref, shift_ref, scale_ref, wg1_ref, wg2_ref, bg_ref,
                          h_ref, gate_ref, *, ctx_tiles):
    @pl.when(pl.program_id(0) < ctx_tiles)
    def _():
        h_ref[...] = _norm_mod(xc_ref, g_ref, shift_ref, scale_ref).astype(BF16)

    @pl.when(pl.program_id(0) >= ctx_tiles)
    def _():
        h_ref[...] = _norm_mod(xs_ref, g_ref, shift_ref, scale_ref).astype(BF16)

    low = _dot(h_ref[...], wg1_ref[...]).astype(BF16)
    for e in range(2):
        logit = _dot(low, wg2_ref[e]) + bg_ref[e]
        log_sig = jnp.minimum(logit, 0.0) - jnp.log(1.0 + jnp.exp(-jnp.abs(logit)))
        gate_ref[e] = log_sig / GLA_TAU


def _mod_specs(tm, shift_idx, scale_idx):
    def spec(idx):
        return pl.BlockSpec((None, 1, D_MODEL), lambda i: (_group_of_tile(i, tm), 0, idx))
    return [pl.BlockSpec((1, D_MODEL), lambda i: (0, 0)), spec(shift_idx), spec(scale_idx)]


def norm_mod(x, g, mod, shift_idx, scale_idx, tm=512):
    return pl.pallas_call(
        _norm_mod_kernel,
        grid=(T // tm,),
        in_specs=[pl.BlockSpec((tm, D_MODEL), lambda i: (i, 0))] + _mod_specs(tm, shift_idx, scale_idx),
        out_specs=pl.BlockSpec((tm, D_MODEL), lambda i: (i, 0)),
        out_shape=jax.ShapeDtypeStruct((T, D_MODEL), BF16),
        compiler_params=_params(("arbitrary",)),
        name="norm_mod",
    )(x, g.reshape(1, D_MODEL), mod, mod)


def norm_mod_gate(xc, xs, g, mod, shift_idx, scale_idx, wg1, wg2, bg, tm=512):
    ng = HB * GLA_DK
    ctx_tiles = TP // tm
    return pl.pallas_call(
        functools.partial(_norm_mod_gate_kernel, ctx_tiles=ctx_tiles),
        grid=(T // tm,),
        in_specs=[pl.BlockSpec((tm, D_MODEL), lambda i: (jnp.minimum(i, ctx_tiles - 1), 0)),
                  pl.BlockSpec((tm, D_MODEL), lambda i: (jnp.maximum(i - ctx_tiles, 0), 0))]
        + _mod_specs(tm, shift_idx, scale_idx)
        + [pl.BlockSpec((D_MODEL, LANE), lambda i: (0, 0)),
           pl.BlockSpec((2, LANE, ng), lambda i: (0, 0, 0)),
           pl.BlockSpec((2, 1, ng), lambda i: (0, 0, 0))],
        out_specs=[pl.BlockSpec((tm, D_MODEL), lambda i: (i, 0)),
                   pl.BlockSpec((2, tm, ng), lambda i: (0, i, 0))],
        out_shape=[jax.ShapeDtypeStruct((T, D_MODEL), BF16), jax.ShapeDtypeStruct((2, T, ng), F32)],
        compiler_params=_params(("arbitrary",)),
        name="norm_mod_gate",
    )(xc, xs, g.reshape(1, D_MODEL), mod, mod, wg1, wg2, bg)


def _cast_weights(w_refs, w_scrs):
    @pl.when(pl.program_id(1) == 0)
    def _():
        for w_ref, w_scr in zip(w_refs, w_scrs):
            w_scr[...] = w_ref[...].astype(BF16)


def _ws_plain_kernel(h_ref, w_ref, o_ref, w_scr):
    _cast_weights([w_ref], [w_scr])
    o_ref[...] = _dot(h_ref[...], w_scr[...])


def _ws_swiglu_kernel(h_ref, wg_ref, wu_ref, o_ref, wg_scr, wu_scr):
    _cast_weights([wg_ref, wu_ref], [wg_scr, wu_scr])
    h = h_ref[...]
    gate = _dot(h, wg_scr[...])
    up = _dot(h, wu_scr[...])
    o_ref[...] = (gate * jax.nn.sigmoid(gate) * up).astype(o_ref.dtype)


def _ws_res_kernel(*refs, lhs_arity, x_arity, ctx_tiles):
    n_l = sum(lhs_arity)
    w_ref = refs[n_l]
    x_refs = refs[n_l + 1:n_l + 1 + x_arity]
    gate_ref, o_ref, w_scr = refs[n_l + 1 + x_arity:]
    groups, at = [], 0
    for a in lhs_arity:
        groups.append(refs[at:at + a])
        at += a
    _cast_weights([w_ref], [w_scr])

    def emit(side):
        acc, k0 = None, 0
        for grp in groups:
            ref = grp[side] if len(grp) == 2 else grp[0]
            part = _dot(ref[...], w_scr[k0:k0 + ref.shape[1], :])
            acc = part if acc is None else acc + part
            k0 += ref.shape[1]
        x_ref = x_refs[side] if x_arity == 2 else x_refs[0]
        o_ref[...] = x_ref[...] + gate_ref[...] * acc

    if x_arity == 2 or 2 in lhs_arity:
        @pl.when(pl.program_id(1) < ctx_tiles)
        def _():
            emit(0)

        @pl.when(pl.program_id(1) >= ctx_tiles)
        def _():
            emit(1)
    else:
        emit(0)


def _rows_specs(arrays, tm, width, col):
    if len(arrays) == 1:
        return [pl.BlockSpec((tm, width), lambda j, i: (i, col(j)))]
    ctx_tiles = TP // tm
    return [pl.BlockSpec((tm, width), lambda j, i: (jnp.minimum(i, ctx_tiles - 1), col(j))),
            pl.BlockSpec((tm, width), lambda j, i: (jnp.maximum(i - ctx_tiles, 0), col(j)))]


def ws_matmul(h, w_stack, layer, tm=1024, tn=1024):
    kdim, n = w_stack.shape[1:]
    return pl.pallas_call(
        _ws_plain_kernel,
        grid=(n // tn, T // tm),
        in_specs=[pl.BlockSpec((tm, kdim), lambda j, i: (i, 0)),
                  pl.BlockSpec((None, kdim, tn), lambda j, i: (layer, 0, j))],
        out_specs=pl.BlockSpec((tm, tn), lambda j, i: (i, j)),
        out_shape=jax.ShapeDtypeStruct((T, n), F32),
        scratch_shapes=[pltpu.VMEM((kdim, tn), BF16)],
        compiler_params=_params(("arbitrary", "arbitrary")),
        name="ws_matmul",
    )(h, w_stack)


def ws_swiglu(h, w_stack, layer, tm=1024, tn=512):
    kdim = w_stack.shape[1]
    nf = D_FF // tn
    return pl.pallas_call(
        _ws_swiglu_kernel,
        grid=(nf, T // tm),
        in_specs=[pl.BlockSpec((tm, kdim), lambda j, i: (i, 0)),
                  pl.BlockSpec((None, kdim, tn), lambda j, i: (layer, 0, j)),
                  pl.BlockSpec((None, kdim, tn), lambda j, i: (layer, 0, j + nf))],
        out_specs=pl.BlockSpec((tm, tn), lambda j, i: (i, j)),
        out_shape=jax.ShapeDtypeStruct((T, D_FF), BF16),
        scratch_shapes=[pltpu.VMEM((kdim, tn), BF16), pltpu.VMEM((kdim, tn), BF16)],
        compiler_params=_params(("arbitrary", "arbitrary")),
        name="ws_swiglu",
    )(h, w_stack, w_stack)


def ws_matmul_residual(lhs_groups, w_stack, layer, x, mod, gate_idx, tm=512, tn=1024):
    kdim, n = w_stack.shape[1:]
    gate_blk = n // tn
    specs, args = [], []
    for grp in lhs_groups:
        specs += _rows_specs(grp, tm, grp[0].shape[1], lambda j: 0)
        args += list(grp)
    specs.append(pl.BlockSpec((None, kdim, tn), lambda j, i: (layer, 0, j)))
    specs += _rows_specs(x, tm, tn, lambda j: j)
    specs.append(pl.BlockSpec((None, 1, tn), lambda j, i: (_group_of_tile(i, tm), 0, gate_idx * gate_blk + j)))
    return pl.pallas_call(
        functools.partial(_ws_res_kernel, lhs_arity=tuple(len(grp) for grp in lhs_groups), x_arity=len(x),
                          ctx_tiles=TP // tm),
        grid=(n // tn, T // tm),
        in_specs=specs,
        out_specs=pl.BlockSpec((tm, tn), lambda j, i: (i, j)),
        out_shape=jax.ShapeDtypeStruct((T, n), F32),
        scratch_shapes=[pltpu.VMEM((kdim, tn), BF16)],
        compiler_params=_params(("arbitrary", "arbitrary")),
        name="ws_matmul_residual",
    )(*args, w_stack, *x, mod)


def _final_norm_kernel(x_ref, g_ref, o_ref):
    o_ref[...] = _rms(x_ref[...], g_ref[...])


def final_norm(x, g, row0, n_rows, tm=1024):
    blk0 = row0 // tm
    return pl.pallas_call(
        _final_norm_kernel,
        grid=(n_rows // tm,),
        in_specs=[pl.BlockSpec((tm, D_MODEL), lambda i: (blk0 + i, 0)),
                  pl.BlockSpec((1, D_MODEL), lambda i: (0, 0))],
        out_specs=pl.BlockSpec((tm, D_MODEL), lambda i: (i, 0)),
        out_shape=jax.ShapeDtypeStruct((n_rows, D_MODEL), F32),
        compiler_params=_params(("arbitrary",)),
        name="final_norm",
    )(x, g.reshape(1, D_MODEL))


def _grid_angles(n_tok, rot_dim):
    t = jnp.arange((n_tok // GRID_W) * GRID_W)
    row = (t // GRID_W).astype(F32)
    col = (t % GRID_W).astype(F32)
    half = rot_dim // 2
    inv = ROPE_BASE ** (-jnp.arange(0, half, 2, dtype=F32) / half)
    return row[:, None] * inv[None], col[:, None] * inv[None]


def rope_tables(n_tok, rot_dim):
    ang_row, ang_col = _grid_angles(n_tok, rot_dim)
    zeros = jnp.zeros_like(ang_row)
    reps = LANE // rot_dim

    def lanes(r1, r2, c1, c2):
        return jnp.tile(jnp.concatenate([r1, r2, c1, c2], axis=-1), (1, reps))

    cr, sr, cc, sc = jnp.cos(ang_row), jnp.sin(ang_row), jnp.cos(ang_col), jnp.sin(ang_col)
    return lanes(cr, cr, cc, cc), lanes(-sr, zeros, -sc, zeros), lanes(zeros, sr, zeros, sc)


def _rope(x, cos, sin_lo, sin_hi, quarter):
    return (x * cos + pltpu.roll(x, LANE - quarter, 1) * sin_lo + pltpu.roll(x, quarter, 1) * sin_hi)


def _diff_lambda(lam_ref, lam_init):
    e1 = jnp.exp(jnp.sum(lam_ref[0:1, :] * lam_ref[1:2, :], axis=-1, keepdims=True))
    e2 = jnp.exp(jnp.sum(lam_ref[2:3, :] * lam_ref[3:4, :], axis=-1, keepdims=True))
    return e1 - e2 + lam_init


def _with_ones(v_bf):
    return jnp.concatenate([v_bf, jnp.ones(v_bf.shape, BF16)], axis=1)


def _softmax_av(q_bf, k_bf, v1_bf):
    s = _dot_nt(q_bf, k_bf)
    e = jnp.exp(s - jnp.max(s, axis=-1, keepdims=True))
    if v1_bf.shape[1] == LANE:
        return _dot(e.astype(BF16), v1_bf) * (1.0 / jnp.sum(e, axis=-1, keepdims=True))
    ov = _dot(e.astype(BF16), v1_bf)
    return ov[:, :LANE] * (1.0 / ov[:, LANE:])


def _diff_head(q, k_bf, v1_bf, lam, subln_g, lam_init, stack_maps):
    n = q.shape[0]
    first = lax.broadcasted_iota(jnp.int32, q.shape, 1) < DIFF_HD
    q0 = jnp.where(first, q, 0.0).astype(BF16)
    q1 = jnp.where(first, 0.0, q).astype(BF16)
    if stack_maps:
        r = _softmax_av(jnp.concatenate([q0, q1], axis=0), k_bf, v1_bf)
        o = r[:n] - lam * r[n:]
    else:
        o = _softmax_av(q0, k_bf, v1_bf) - lam * _softmax_av(q1, k_bf, v1_bf)
    return _rms(o, subln_g) * (1.0 - lam_init)


def _diff_ctx_kernel(q_ref, k_ref, v_ref, lam_ref, sg_ref, o_ref, ko_ref, vo_ref, *, lam_init):
    lam = _diff_lambda(lam_ref, lam_init)
    scale = DIFF_HD ** -0.5
    ko_ref[...] = k_ref[...]
    vo_ref[...] = v_ref[...]
    for h in range(HA):
        sl = slice(h * LANE, (h + 1) * LANE)
        y = _diff_head(q_ref[:, sl] * scale, k_ref[:, sl].astype(BF16), _with_ones(v_ref[:, sl].astype(BF16)),
                       lam, sg_ref[...], lam_init, stack_maps=True)
        o_ref[:, sl] = y.astype(o_ref.dtype)


def diff_attention_context(proj, lam_vec, subln_g, lam_init):
    w = HA * LANE

    def blk(c):
        return pl.BlockSpec((SEQ, w), lambda b: (b, c))

    return pl.pallas_call(
        functools.partial(_diff_ctx_kernel, lam_init=lam_init),
        grid=(BATCH,),
        in_specs=[blk(0), blk(1), blk(2),
                  pl.BlockSpec((4, DIFF_HD), lambda b: (0, 0)),
                  pl.BlockSpec((1, DIFF_VD), lambda b: (0, 0))],
        out_specs=[blk(0), blk(0), blk(0)],
        out_shape=[jax.ShapeDtypeStruct((TP, w), BF16), jax.ShapeDtypeStruct((TP, w), F32),
                   jax.ShapeDtypeStruct((TP, w), F32)],
        compiler_params=_params(("arbitrary",)),
        name="diff_attention_context",
    )(proj, proj, proj, lam_vec, subln_g.reshape(1, DIFF_VD))


def _diff_lat_kernel(q_ref, k_ref, v_ref, ck_ref, cv_ref, kc_ref, kl_ref, kh_ref, qc_ref, ql_ref, qh_ref,
                     lam_ref, sg_ref, o_ref, k_scr, v_scr, *, lam_init):
    quarter = DIFF_HD // 4

    @pl.when(pl.program_id(2) == 0)
    def _():
        k_scr[0:PAST_LEN, :] = ck_ref[...].astype(BF16)
        k_scr[PAST_LEN:, :] = _rope(k_ref[...], kc_ref[...], kl_ref[...], kh_ref[...], quarter).astype(BF16)
        v_scr[0:PAST_LEN, :] = cv_ref[...].astype(BF16)
        v_scr[PAST_LEN:, :] = v_ref[...].astype(BF16)

    lam = _diff_lambda(lam_ref, lam_init)
    sub = q_ref.shape[0] // DIFF_LAT_CHAINS
    for c in range(DIFF_LAT_CHAINS):
        rs = slice(c * sub, (c + 1) * sub)
        q = _rope(q_ref[rs, :], qc_ref[rs, :], ql_ref[rs, :], qh_ref[rs, :], quarter) * (DIFF_HD ** -0.5)
        y = _diff_head(q, k_scr[...], v_scr[...], lam, sg_ref[...], lam_init, stack_maps=False)
        o_ref[rs, :] = y.astype(o_ref.dtype)


def diff_attention_latent(proj, cache_k, cache_v, tables, lam_vec, subln_g, lam_init, tq=1024):
    nq = DEC_SEQ // tq
    row0 = TP // tq
    kv_row0 = TP // DEC_SEQ
    cos, sin_lo, sin_hi = tables
    full = pl.BlockSpec((DEC_SEQ, LANE), lambda b, h, i: (0, 0))
    qtab = pl.BlockSpec((tq, LANE), lambda b, h, i: (i, 0))
    cache = pl.BlockSpec((None, PAST_LEN, LANE), lambda b, h, i: (b, 0, h))
    return pl.pallas_call(
        functools.partial(_diff_lat_kernel, lam_init=lam_init),
        grid=(DEC_BATCH, HA, nq),
        in_specs=[pl.BlockSpec((tq, LANE), lambda b, h, i: (row0 + b * nq + i, h)),
                  pl.BlockSpec((DEC_SEQ, LANE), lambda b, h, i: (kv_row0 + b, HA + h)),
                  pl.BlockSpec((DEC_SEQ, LANE), lambda b, h, i: (kv_row0 + b, 2 * HA + h)),
                  cache, cache, full, full, full, qtab, qtab, qtab,
                  pl.BlockSpec((4, DIFF_HD), lambda b, h, i: (0, 0)),
                  pl.BlockSpec((1, DIFF_VD), lambda b, h, i: (0, 0))],
        out_specs=pl.BlockSpec((tq, LANE), lambda b, h, i: (b * nq + i, h)),
        out_shape=jax.ShapeDtypeStruct((TS, HA * LANE), BF16),
        scratch_shapes=[pltpu.VMEM((PAST_LEN + DEC_SEQ, LANE), BF16),
                        pltpu.VMEM((PAST_LEN + DEC_SEQ, LANE), BF16)],
        compiler_params=_params(("arbitrary", "arbitrary", "arbitrary")),
        name="diff_attention_latent",
    )(proj, proj, proj, cache_k, cache_v, cos, sin_lo, sin_hi, cos, sin_lo, sin_hi,
      lam_vec, subln_g.reshape(1, DIFF_VD))


def _split3(x):
    hi = x.astype(BF16)
    r = x - hi.astype(F32)
    mid = r.astype(BF16)
    lo = (r - mid.astype(F32)).astype(BF16)
    return hi, mid, lo


def _gla_kernel(q_ref, k_ref, v_ref, r_ref, g_ref, s0_ref, gg_ref, y_ref, sfin_ref, of_scr, ob_scr, st_scr,
                *, seq):
    c = GLA_CHUNK
    sb = GLA_SUPER * c
    n_super = seq // sb
    rows = lax.broadcasted_iota(jnp.int32, (sb, sb), 0)
    cols = lax.broadcasted_iota(jnp.int32, (sb, sb), 1)
    same_chunk = (rows // c) == (cols // c)
    keep = (same_chunk & (rows >= cols), same_chunk & (cols >= rows))
    tri = tuple(jnp.where(m, 1.0, 0.0).astype(BF16) for m in keep)
    lane = lax.broadcasted_iota(jnp.int32, (sb, LANE), 1)
    own = (lane < GLA_DK, lane >= GLA_DK)
    chunk_of_row = lax.broadcasted_iota(jnp.int32, (sb, LANE), 0) // c
    st_rows = lax.broadcasted_iota(jnp.int32, (2 * GLA_DV, LANE), 0)
    st_lane = lax.broadcasted_iota(jnp.int32, (2 * GLA_DV, LANE), 1)
    st_own = (st_rows < GLA_DV) == (st_lane < GLA_DK)
    zpad = jnp.zeros((GLA_DK, GLA_DV), F32)

    def expand(x):
        return jnp.concatenate([jnp.where(chunk_of_row == ci, x, 0.0) for ci in range(GLA_SUPER)],
                               axis=1).astype(BF16)

    for d in range(2):
        s0 = [s0_ref[d, hh] for hh in range(2)]
        st_scr[d] = jnp.concatenate([jnp.concatenate([s0[0], zpad], axis=0).T,
                                     jnp.concatenate([zpad, s0[1]], axis=0).T], axis=0)

    def super_block(n, carry):
        for d in range(2):
            r0 = pl.multiple_of((n if d == 0 else n_super - 1 - n) * sb, sb)
            rs = pl.ds(r0, sb)
            hi, mid, lo = _split3(g_ref[d, rs, :])
            b = _dot(tri[d], hi) + _dot(tri[d], mid) + _dot(tri[d], lo)
            ends = [b[ci * c + (c - 1 if d == 0 else 0):ci * c + (c if d == 0 else 1), :] for ci in range(GLA_SUPER)]
            total = jnp.concatenate([jnp.broadcast_to(e, (c, LANE)) for e in ends], axis=0)
            q_in = (q_ref[rs, :] * (GLA_DK ** -0.5)) * jnp.exp(b)
            kk = k_ref[rs, :]
            k_in = (kk * jnp.exp(-b)).astype(BF16)
            k_end = kk * jnp.exp(total - b)
            v = v_ref[rs, :]
            v_bf = v.astype(BF16)
            kv_all = _dot(v.T.astype(BF16), expand(k_end))
            q2 = jnp.concatenate([jnp.where(own[0], q_in, 0.0), jnp.where(own[1], q_in, 0.0)], axis=0)
            a2 = _dot_nt(q2.astype(BF16), k_in)
            o = jnp.concatenate(
                [_dot(jnp.where(keep[d], a2[hh * sb:(hh + 1) * sb], 0.0).astype(BF16),
                      v_bf[:, hh * GLA_DV:(hh + 1) * GLA_DV]) for hh in range(2)], axis=1)
            st = st_scr[d]
            entering = [None] * GLA_SUPER
            for ci in (range(GLA_SUPER) if d == 0 else range(GLA_SUPER - 1, -1, -1)):
                entering[ci] = st
                st = st * jnp.exp(ends[ci]) + jnp.where(st_own, kv_all[:, ci * LANE:(ci + 1) * LANE], 0.0)
            st_scr[d] = st
            o += _dot_nt(expand(q_in), jnp.concatenate(entering, axis=1).astype(BF16))
            if d == 0:
                of_scr[rs, :] = o
            else:
                ob_scr[rs, :] = o
        return carry

    lax.fori_loop(0, n_super, super_block, 0)

    for hh in range(2):
        vs = slice(hh * GLA_DV, (hh + 1) * GLA_DV)
        r = r_ref[:, vs]
        y = _rms(of_scr[:, vs] + ob_scr[:, vs], gg_ref[...]) * (r * jax.nn.sigmoid(r))
        y_ref[:, vs] = y.astype(y_ref.dtype)
        for d in range(2):
            sfin_ref[d, hh] = st_scr[d, hh * GLA_DV:(hh + 1) * GLA_DV, :].T[hh * GLA_DK:(hh + 1) * GLA_DK, :]


def bidir_gla(proj, gates, s0, gla_g, n_batch, seq, row_blk0):
    pairs = HB // 2
    shared_s0 = s0.shape[0] == 1
    col_q = (2 * HA * 2 * DIFF_HD + HA * DIFF_VD) // LANE
    col_k = col_q + HB * GLA_DK // LANE
    col_v = (col_k + HB * GLA_DK // LANE) // 2
    col_r = col_v + HB * GLA_DV // (2 * GLA_DV)
    n_tok = n_batch * seq
    return pl.pallas_call(
        functools.partial(_gla_kernel, seq=seq),
        grid=(n_batch, pairs),
        in_specs=[pl.BlockSpec((seq, LANE), lambda b, p: (row_blk0 + b, col_q + p)),
                  pl.BlockSpec((seq, LANE), lambda b, p: (row_blk0 + b, col_k + p)),
                  pl.BlockSpec((seq, 2 * GLA_DV), lambda b, p: (row_blk0 + b, col_v + p)),
                  pl.BlockSpec((seq, 2 * GLA_DV), lambda b, p: (row_blk0 + b, col_r + p)),
                  pl.BlockSpec((2, seq, LANE), lambda b, p: (0, row_blk0 + b, p)),
                  pl.BlockSpec((None, 2, 2, GLA_DK, GLA_DV), lambda b, p: (0 if shared_s0 else b, 0, p, 0, 0)),
                  pl.BlockSpec((1, GLA_DV), lambda b, p: (0, 0))],
        out_specs=[pl.BlockSpec((seq, 2 * GLA_DV), lambda b, p: (b, p)),
                   pl.BlockSpec((None, 2, 2, GLA_DK, GLA_DV), lambda b, p: (b, 0, p, 0, 0))],
        out_shape=[jax.ShapeDtypeStruct((n_tok, HB * GLA_DV), BF16),
                   jax.ShapeDtypeStruct((n_batch, 2, HB, GLA_DK, GLA_DV), F32)],
        scratch_shapes=[pltpu.VMEM((seq, 2 * GLA_DV), F32), pltpu.VMEM((seq, 2 * GLA_DV), F32),
                        pltpu.VMEM((2, 2 * GLA_DV, LANE), F32)],
        compiler_params=_params(("arbitrary", "arbitrary")),
        name="bidir_gla",
    )(proj, proj, proj, proj, gates, s0, gla_g.reshape(1, GLA_DV))


def _head_slice(kh, gq):
    h = kh * Q_PER_KV + gq
    return slice(h * LANE, (h + 1) * LANE)


def _sink_column(sink_ref, kh, rows):
    return jnp.concatenate([jnp.full((rows, 1), sink_ref[kh * Q_PER_KV + gq], F32) for gq in range(Q_PER_KV)],
                           axis=0)


def _gqa_ctx_kernel(sink_ref, q_ref, k_ref, v_ref, o_ref, ko_ref, vo_ref):
    scale = HEAD_DIM ** -0.5
    ko_ref[...] = k_ref[...]
    vo_ref[...] = v_ref[...]
    for kh in range(KV_HEADS):
        ks = slice(kh * LANE, (kh + 1) * LANE)
        k_bf = k_ref[:, ks].astype(BF16)
        v1_bf = _with_ones(v_ref[:, ks].astype(BF16))
        for gq in range(Q_PER_KV):
            hs = _head_slice(kh, gq)
            sink = sink_ref[kh * Q_PER_KV + gq]
            s = _dot_nt((q_ref[:, hs] * scale).astype(BF16), k_bf)
            m = jnp.maximum(jnp.max(s, axis=-1, keepdims=True), sink)
            ov = _dot(jnp.exp(s - m).astype(BF16), v1_bf)
            o_ref[:, hs] = (ov[:, :LANE] * (1.0 / (ov[:, LANE:] + jnp.exp(sink - m)))).astype(o_ref.dtype)


def gqa_context(proj, sinks):
    wq = N_HEADS * HEAD_DIM
    wkv = KV_HEADS * HEAD_DIM

    def kv(c):
        return pl.BlockSpec((SEQ, wkv), lambda b: (b, c))

    return pl.pallas_call(
        _gqa_ctx_kernel,
        grid=(BATCH,),
        in_specs=[pl.BlockSpec(memory_space=pltpu.SMEM),
                  pl.BlockSpec((SEQ, wq), lambda b: (b, 0)), kv(wq // wkv), kv(wq // wkv + 1)],
        out_specs=[pl.BlockSpec((SEQ, wq), lambda b: (b, 0)), kv(0), kv(0)],
        out_shape=[jax.ShapeDtypeStruct((TP, wq), BF16), jax.ShapeDtypeStruct((TP, wkv), F32),
                   jax.ShapeDtypeStruct((TP, wkv), F32)],
        compiler_params=_params(("arbitrary",)),
        name="gqa_context",
    )(sinks, proj, proj, proj)


def _gqa_lat_kernel(sink_ref, q_ref, k_ref, v_ref, ck_ref, cv_ref, kc_ref, kl_ref, kh_ref,
                    qc_ref, ql_ref, qh_ref, o_ref, kw_scr, vw_scr, kc_scr, vc_scr):
    quarter = HEAD_DIM // 4
    n = pl.program_id(1)
    w = WINDOW
    wkv = KV_HEADS * HEAD_DIM

    @pl.when(n == 0)
    def _():
        kw_scr[0:w, :] = jnp.zeros((w, wkv), BF16)
        kw_scr[w + DEC_SEQ:, :] = jnp.zeros((w, wkv), BF16)
        vw_scr[0:w, :] = jnp.zeros((w, 2 * wkv), BF16)
        vw_scr[w + DEC_SEQ:, :] = jnp.zeros((w, 2 * wkv), BF16)
        for kh in range(KV_HEADS):
            ks = slice(kh * LANE, (kh + 1) * LANE)
            vs = slice(2 * kh * LANE, 2 * (kh + 1) * LANE)
            kw_scr[w:w + DEC_SEQ, ks] = _rope(k_ref[:, ks], kc_ref[...], kl_ref[...], kh_ref[...],
                                              quarter).astype(BF16)
            vw_scr[w:w + DEC_SEQ, vs] = _with_ones(v_ref[:, ks].astype(BF16))
            vc_scr[:, vs] = _with_ones(cv_ref[:, ks].astype(BF16))
        kc_scr[...] = ck_ref[...].astype(BF16)

    win = pl.ds(pl.multiple_of(n * w, w), 3 * w)
    qi = lax.broadcasted_iota(jnp.int32, (Q_PER_KV * w, 3 * w), 0) % w
    kj = lax.broadcasted_iota(jnp.int32, (Q_PER_KV * w, 3 * w), 1)
    kpos = n * w - w + kj
    valid = (kj >= qi) & (kj <= qi + 2 * w) & (kpos >= 0) & (kpos < DEC_SEQ)
    scale = HEAD_DIM ** -0.5
    for kh in range(KV_HEADS):
        ks = slice(kh * LANE, (kh + 1) * LANE)
        vs = slice(2 * kh * LANE, 2 * (kh + 1) * LANE)
        q = jnp.concatenate(
            [(_rope(q_ref[:, _head_slice(kh, gq)], qc_ref[...], ql_ref[...], qh_ref[...], quarter) * scale
              ).astype(BF16) for gq in range(Q_PER_KV)], axis=0)
        sink = _sink_column(sink_ref, kh, w)
        s_c = _dot_nt(q, kc_scr[:, ks])
        s_w = jnp.where(valid, _dot_nt(q, kw_scr[win, ks]), NEG_INF)
        m = jnp.maximum(jnp.maximum(jnp.max(s_c, axis=-1, keepdims=True),
                                    jnp.max(s_w, axis=-1, keepdims=True)), sink)
        ov = (_dot(jnp.exp(s_c - m).astype(BF16), vc_scr[:, vs])
              + _dot(jnp.exp(s_w - m).astype(BF16), vw_scr[win, vs]))
        o = ov[:, :LANE] * (1.0 / (ov[:, LANE:] + jnp.exp(sink - m)))
        for gq in range(Q_PER_KV):
            o_ref[:, _head_slice(kh, gq)] = o[gq * w:(gq + 1) * w].astype(o_ref.dtype)


def gqa_latent(proj, cache_k, cache_v, tables, sinks):
    wq = N_HEADS * HEAD_DIM
    wkv = KV_HEADS * HEAD_DIM
    nq = DEC_SEQ // WINDOW
    row0 = TP // WINDOW
    kv_row0 = TP // DEC_SEQ
    cos, sin_lo, sin_hi = tables
    full = pl.BlockSpec((DEC_SEQ, LANE), lambda b, i: (0, 0))
    qtab = pl.BlockSpec((WINDOW, LANE), lambda b, i: (i, 0))
    cache = pl.BlockSpec((None, PAST_LEN, wkv), lambda b, i: (b, 0, 0))
    return pl.pallas_call(
        _gqa_lat_kernel,
        grid=(DEC_BATCH, nq),
        in_specs=[pl.BlockSpec(memory_space=pltpu.SMEM),
                  pl.BlockSpec((WINDOW, wq), lambda b, i: (row0 + b * nq + i, 0)),
                  pl.BlockSpec((DEC_SEQ, wkv), lambda b, i: (kv_row0 + b, wq // wkv)),
                  pl.BlockSpec((DEC_SEQ, wkv), lambda b, i: (kv_row0 + b, wq // wkv + 1)),
                  cache, cache, full, full, full, qtab, qtab, qtab],
        out_specs=pl.BlockSpec((WINDOW, wq), lambda b, i: (b * nq + i, 0)),
        out_shape=jax.ShapeDtypeStruct((TS, wq), BF16),
        scratch_shapes=[pltpu.VMEM((DEC_SEQ + 2 * WINDOW, wkv), BF16),
                        pltpu.VMEM((DEC_SEQ + 2 * WINDOW, 2 * wkv), BF16),
                        pltpu.VMEM((PAST_LEN, wkv), BF16),
                        pltpu.VMEM((PAST_LEN, 2 * wkv), BF16)],
        compiler_params=_params(("arbitrary", "arbitrary")),
        name="gqa_latent",
    )(sinks, proj, proj, proj, cache_k, cache_v, cos, sin_lo, sin_hi, cos, sin_lo, sin_hi)


def _even_layer(xc, xs, mod, j, layer, norm_g, w_in_even, lam_vec, subln_g, w_gate1, w_gate2, b_gate, gla_norm_g,
                w_out_even, cache_a_k, cache_a_v, state_b, tables):
    lam_init = 0.8 - 0.6 * math.exp(-0.3 * layer)
    ng = HB * GLA_DK
    wg1 = jnp.zeros((D_MODEL, LANE), F32).at[:, :2 * GLA_GATE_RANK].set(
        jnp.concatenate([w_gate1[j, 0], w_gate1[j, 1]], axis=-1)).astype(BF16)
    wg2 = jnp.zeros((2, LANE, ng), F32)
    for e in range(2):
        wg2 = wg2.at[e, e * GLA_GATE_RANK:(e + 1) * GLA_GATE_RANK, :].set(w_gate2[j, e])
    h, gates = norm_mod_gate(xc, xs, norm_g[layer, 0], mod, 0, 1, wg1, wg2.astype(BF16),
                             b_gate[j].reshape(2, 1, ng))
    proj = ws_matmul(h, w_in_even, j)
    ya_c, new_k, new_v = diff_attention_context(proj, lam_vec, subln_g[j], lam_init)
    ya_s = diff_attention_latent(proj, cache_a_k[:, j].reshape(DEC_BATCH, PAST_LEN, HA * 2 * DIFF_HD),
                                 cache_a_v[:, j].reshape(DEC_BATCH, PAST_LEN, HA * DIFF_VD),
                                 tables, lam_vec, subln_g[j], lam_init)
    yb_c, s_fin = bidir_gla(proj, gates, jnp.zeros((1, 2, HB, GLA_DK, GLA_DV), F32), gla_norm_g[j],
                            BATCH, SEQ, 0)
    yb_s, _ = bidir_gla(proj, gates, state_b[:, j], gla_norm_g[j], DEC_BATCH, DEC_SEQ, TP // DEC_SEQ)
    x = ws_matmul_residual([(ya_c, ya_s), (yb_c, yb_s)], w_out_even, j, (xc, xs), mod, 2)
    return (x, new_k.reshape(BATCH, SEQ, HA, 2 * DIFF_HD), new_v.reshape(BATCH, SEQ, HA, DIFF_VD), s_fin)


def _odd_layer(x, mod, j, layer, norm_g, w_in_odd, sinks, w_out_odd, cache_c_k, cache_c_v, tables):
    wkv = KV_HEADS * HEAD_DIM
    proj = ws_matmul(norm_mod(x, norm_g[layer, 0], mod, 0, 1), w_in_odd, j)
    o_c, new_k, new_v = gqa_context(proj, sinks[j])
    o_s = gqa_latent(proj, cache_c_k[:, j].reshape(DEC_BATCH, PAST_LEN, wkv),
                     cache_c_v[:, j].reshape(DEC_BATCH, PAST_LEN, wkv), tables, sinks[j])
    x = ws_matmul_residual([(o_c, o_s)], w_out_odd, j, (x,), mod, 2)
    return (x, new_k.reshape(BATCH, SEQ, KV_HEADS, HEAD_DIM), new_v.reshape(BATCH, SEQ, KV_HEADS, HEAD_DIM))


def _ffn(x, mod, layer, norm_g, w_ffn_in, w_ffn_out):
    act = ws_swiglu(norm_mod(x, norm_g[layer, 1], mod, 3, 4), w_ffn_in, layer)
    return ws_matmul_residual([(act,)], w_ffn_out, layer, (x,), mod, 5, tm=512, tn=512)


def kernel(x_prompt, x_sample, c, cache_a_k, cache_a_v, state_b, cache_c_k, cache_c_v, c_ctx, w_ada, b_ada,
           norm_g, w_in_even, lam_q1, lam_k1, lam_q2, lam_k2, subln_g, w_gate1, w_gate2, b_gate, gla_norm_g,
           w_out_even, w_in_odd, sinks, w_out_odd, w_ffn_in, w_ffn_out, final_norm_g):
    assert DEPTH == 2, "layer 0 reads the two input streams, layer 1 the fused token matrix"
    cvec = jnp.concatenate([c_ctx[None, :], c, jnp.zeros((MOD_ROWS - 1 - DEC_BATCH, D_MODEL), F32)], axis=0)
    mods = adaln(cvec, w_ada, b_ada).reshape(DEPTH, MOD_ROWS, 1, 6 * D_MODEL)
    tab_d = rope_tables(DEC_SEQ, DIFF_HD)
    tab_h = rope_tables(DEC_SEQ, HEAD_DIM)

    lam_vec = jnp.stack([lam_q1[0], lam_k1[0], lam_q2[0], lam_k2[0]], axis=0)
    x, ak, av, sb = _even_layer(x_prompt.reshape(TP, D_MODEL), x_sample.reshape(TS, D_MODEL), mods[0], 0, 0,
                                norm_g, w_in_even, lam_vec, subln_g, w_gate1, w_gate2, b_gate, gla_norm_g,
                                w_out_even, cache_a_k, cache_a_v, state_b, tab_d)
    x = _ffn(x, mods[0], 0, norm_g, w_ffn_in, w_ffn_out)
    x, ck, cv = _odd_layer(x, mods[1], 0, 1, norm_g, w_in_odd, sinks, w_out_odd, cache_c_k, cache_c_v, tab_h)
    x = _ffn(x, mods[1], 1, norm_g, w_ffn_in, w_ffn_out)

    y_prompt = final_norm(x, final_norm_g, 0, TP).reshape(BATCH, SEQ, D_MODEL)
    y_sample = final_norm(x, final_norm_g, TP, TS).reshape(DEC_BATCH, DEC_SEQ, D_MODEL)
    return (y_prompt, y_sample, ak[:, None], av[:, None], sb[:, None], ck[:, None], cv[:, None])
```

```python
import functools
import math

import jax
import jax.numpy as jnp
from jax import lax
from jax.experimental import pallas as pl
from jax.experimental.pallas import tpu as pltpu

D_MODEL = 2048
BATCH = 32
SEQ = 256
DEPTH = 2
DEC_BATCH = 2
DEC_SEQ = 2048
PAST_LEN = 512
GRID_W = 64
HEAD_DIM = 128
N_HEADS = D_MODEL // HEAD_DIM
HA = N_HEADS // 2
DIFF_HD = HEAD_DIM // 2
DIFF_VD = HEAD_DIM
HB = N_HEADS // 2
GLA_DK = HEAD_DIM // 2
GLA_DV = HEAD_DIM
GLA_GATE_RANK = 16
GLA_TAU = 16.0
GLA_CHUNK = 64
GLA_SUPER = 4
DIFF_LAT_CHAINS = 4
KV_HEADS = N_HEADS // 4
Q_PER_KV = N_HEADS // KV_HEADS
WINDOW = 128
D_FF = -(-8 * D_MODEL // (3 * 256)) * 256
ROPE_BASE = 10000.0
EPS = 1e-6
NEG_INF = -1e30
EVEN_IN = 2 * HA * 2 * DIFF_HD + HA * DIFF_VD + 2 * HB * GLA_DK + 2 * HB * GLA_DV
ODD_IN = (N_HEADS + 2 * KV_HEADS) * HEAD_DIM

TP = BATCH * SEQ
TS = DEC_BATCH * DEC_SEQ
T = TP + TS
LANE = 128
MOD_ROWS = 8
NORM_ROWS = 16
NORM_UNROLL = 8
VMEM_LIMIT = 56 * 1024 * 1024

F32 = jnp.float32
BF16 = jnp.bfloat16


def _params(semantics, vmem=VMEM_LIMIT):
    return pltpu.CompilerParams(dimension_semantics=semantics, vmem_limit_bytes=vmem)


def _dot(a, b):
    return jnp.dot(a, b, preferred_element_type=F32)


def _dot_nt(a, b):
    return lax.dot_general(a, b, (((1,), (1,)), ((), ())), preferred_element_type=F32)


def _group_of_tile(i, tm):
    r = i * tm
    return jnp.where(r < TP, 0, 1 + (r - TP) // DEC_SEQ)


def _rms(x, g):
    return (x * lax.rsqrt(jnp.mean(x * x, axis=-1, keepdims=True) + EPS)) * g


def _adaln_kernel(c_ref, w_ref, b_ref, o_ref):
    c = c_ref[...]
    s = c * jax.nn.sigmoid(c)
    o_ref[...] = _dot(s.astype(BF16), w_ref[...].astype(BF16)) + b_ref[...]


def adaln(cvec, w_ada, b_ada, tn=512):
    n = w_ada.shape[-1]
    return pl.pallas_call(
        _adaln_kernel,
        grid=(DEPTH, n // tn),
        in_specs=[
            pl.BlockSpec((MOD_ROWS, D_MODEL), lambda l, j: (0, 0)),
            pl.BlockSpec((None, D_MODEL, tn), lambda l, j: (l, 0, j)),
            pl.BlockSpec((None, 1, tn), lambda l, j: (l, 0, j)),
        ],
        out_specs=pl.BlockSpec((None, MOD_ROWS, tn), lambda l, j: (l, 0, j)),
        out_shape=jax.ShapeDtypeStruct((DEPTH, MOD_ROWS, n), F32),
        compiler_params=_params(("arbitrary", "arbitrary")),
        name="adaln",
    )(cvec, w_ada, b_ada.reshape(DEPTH, 1, n))


def _by_row_groups(x_ref, o_ref, fn):
    def body(r, carry):
        rs = pl.ds(pl.multiple_of(r * NORM_ROWS, NORM_ROWS), NORM_ROWS)
        o_ref[rs, :] = fn(x_ref[rs, :]).astype(o_ref.dtype)
        return carry

    lax.fori_loop(0, x_ref.shape[0] // NORM_ROWS, body, 0, unroll=NORM_UNROLL)


def _norm_mod_into(h_ref, x_ref, g_ref, shift_ref, scale_ref):
    _by_row_groups(x_ref, h_ref, lambda x: _rms(x, g_ref[...]) * (1 + scale_ref[...]) + shift_ref[...])


def _norm_mod_kernel(x_ref, g_ref, shift_ref, scale_ref, h_ref):
    _norm_mod_into(h_ref, x_ref, g_ref, shift_ref, scale_ref)


def _norm_mod_gate_kernel(xc_ref, xs_ref, g_ref, shift_ref, scale_ref, wg1_ref, wg2_ref, bg_ref,
                          h_ref, gate_ref, *, ctx_tiles):
    @pl.when(pl.program_id(0) < ctx_tiles)
    def _():
        _norm_mod_into(h_ref, xc_ref, g_ref, shift_ref, scale_ref)

    @pl.when(pl.program_id(0) >= ctx_tiles)
    def _():
        _norm_mod_into(h_ref, xs_ref, g_ref, shift_ref, scale_ref)

    low = _dot(h_ref[...], wg1_ref[...]).astype(BF16)
    for e in range(2):
        logit = _dot(low, wg2_ref[e]) + bg_ref[e]
        log_sig = jnp.minimum(logit, 0.0) - jnp.log(1.0 + jnp.exp(-jnp.abs(logit)))
        gate_ref[e] = log_sig / GLA_TAU


def _mod_specs(tm, shift_idx, scale_idx):
    def spec(idx):
        return pl.BlockSpec((None, 1, D_MODEL), lambda i: (_group_of_tile(i, tm), 0, idx))
    return [pl.BlockSpec((1, D_MODEL), lambda i: (0, 0)), spec(shift_idx), spec(scale_idx)]


def norm_mod(x, g, mod, shift_idx, scale_idx, tm=512):
    return pl.pallas_call(
        _norm_mod_kernel,
        grid=(T // tm,),
        in_specs=[pl.BlockSpec((tm, D_MODEL), lambda i: (i, 0))] + _mod_specs(tm, shift_idx, scale_idx),
        out_specs=pl.BlockSpec((tm, D_MODEL), lambda i: (i, 0)),
        out_shape=jax.ShapeDtypeStruct((T, D_MODEL), BF16),
        compiler_params=_params(("arbitrary",)),
        name="norm_mod",
    )(x, g.reshape(1, D_MODEL), mod, mod)


def norm_mod_gate(xc, xs, g, mod, shift_idx, scale_idx, wg1, wg2, bg, tm=512):
    ng = HB * GLA_DK
    ctx_tiles = TP // tm
    return pl.pallas_call(
        functools.partial(_norm_mod_gate_kernel, ctx_tiles=ctx_tiles),
        grid=(T // tm,),
        in_specs=[pl.BlockSpec((tm, D_MODEL), lambda i: (jnp.minimum(i, ctx_tiles - 1), 0)),
                  pl.BlockSpec((tm, D_MODEL), lambda i: (jnp.maximum(i - ctx_tiles, 0), 0))]
        + _mod_specs(tm, shift_idx, scale_idx)
        + [pl.BlockSpec((D_MODEL, LANE), lambda i: (0, 0)),
           pl.BlockSpec((2, LANE, ng), lambda i: (0, 0, 0)),
           pl.BlockSpec((2, 1, ng), lambda i: (0, 0, 0))],
        out_specs=[pl.BlockSpec((tm, D_MODEL), lambda i: (i, 0)),
                   pl.BlockSpec((2, tm, ng), lambda i: (0, i, 0))],
        out_shape=[jax.ShapeDtypeStruct((T, D_MODEL), BF16), jax.ShapeDtypeStruct((2, T, ng), F32)],
        compiler_params=_params(("arbitrary",)),
        name="norm_mod_gate",
    )(xc, xs, g.reshape(1, D_MODEL), mod, mod, wg1, wg2, bg)


def _cast_weights(w_refs, w_scrs):
    @pl.when(pl.program_id(1) == 0)
    def _():
        for w_ref, w_scr in zip(w_refs, w_scrs):
            w_scr[...] = w_ref[...].astype(BF16)


def _ws_plain_kernel(h_ref, w_ref, o_ref, w_scr):
    _cast_weights([w_ref], [w_scr])
    o_ref[...] = _dot(h_ref[...], w_scr[...])


def _ws_swiglu_kernel(h_ref, wg_ref, wu_ref, o_ref, wg_scr, wu_scr):
    _cast_weights([wg_ref, wu_ref], [wg_scr, wu_scr])
    h = h_ref[...]
    gate = _dot(h, wg_scr[...])
    up = _dot(h, wu_scr[...])
    o_ref[...] = (gate * jax.nn.sigmoid(gate) * up).astype(o_ref.dtype)


def _ws_res_kernel(*refs, lhs_arity, x_arity, ctx_tiles):
    n_l = sum(lhs_arity)
    w_ref = refs[n_l]
    x_refs = refs[n_l + 1:n_l + 1 + x_arity]
    gate_ref, o_ref, w_scr = refs[n_l + 1 + x_arity:]
    groups, at = [], 0
    for a in lhs_arity:
        groups.append(refs[at:at + a])
        at += a
    _cast_weights([w_ref], [w_scr])

    def emit(side):
        acc, k0 = None, 0
        for grp in groups:
            ref = grp[side] if len(grp) == 2 else grp[0]
            part = _dot(ref[...], w_scr[k0:k0 + ref.shape[1], :])
            acc = part if acc is None else acc + part
            k0 += ref.shape[1]
        x_ref = x_refs[side] if x_arity == 2 else x_refs[0]
        o_ref[...] = x_ref[...] + gate_ref[...] * acc

    if x_arity == 2 or 2 in lhs_arity:
        @pl.when(pl.program_id(1) < ctx_tiles)
        def _():
            emit(0)

        @pl.when(pl.program_id(1) >= ctx_tiles)
        def _():
            emit(1)
    else:
        emit(0)


def _rows_specs(arrays, tm, width, col):
    if len(arrays) == 1:
        return [pl.BlockSpec((tm, width), lambda j, i: (i, col(j)))]
    ctx_tiles = TP // tm
    return [pl.BlockSpec((tm, width), lambda j, i: (jnp.minimum(i, ctx_tiles - 1), col(j))),
            pl.BlockSpec((tm, width), lambda j, i: (jnp.maximum(i - ctx_tiles, 0), col(j)))]


def ws_matmul(h, w_stack, layer, tm=1024, tn=1024):
    kdim, n = w_stack.shape[1:]
    return pl.pallas_call(
        _ws_plain_kernel,
        grid=(n // tn, T // tm),
        in_specs=[pl.BlockSpec((tm, kdim), lambda j, i: (i, 0)),
                  pl.BlockSpec((None, kdim, tn), lambda j, i: (layer, 0, j))],
        out_specs=pl.BlockSpec((tm, tn), lambda j, i: (i, j)),
        out_shape=jax.ShapeDtypeStruct((T, n), F32),
        scratch_shapes=[pltpu.VMEM((kdim, tn), BF16)],
        compiler_params=_params(("arbitrary", "arbitrary")),
        name="ws_matmul",
    )(h, w_stack)


def ws_swiglu(h, w_stack, layer, tm=1024, tn=512):
    kdim = w_stack.shape[1]
    nf = D_FF // tn
    return pl.pallas_call(
        _ws_swiglu_kernel,
        grid=(nf, T // tm),
        in_specs=[pl.BlockSpec((tm, kdim), lambda j, i: (i, 0)),
                  pl.BlockSpec((None, kdim, tn), lambda j, i: (layer, 0, j)),
                  pl.BlockSpec((None, kdim, tn), lambda j, i: (layer, 0, j + nf))],
        out_specs=pl.BlockSpec((tm, tn), lambda j, i: (i, j)),
        out_shape=jax.ShapeDtypeStruct((T, D_FF), BF16),
        scratch_shapes=[pltpu.VMEM((kdim, tn), BF16), pltpu.VMEM((kdim, tn), BF16)],
        compiler_params=_params(("arbitrary", "arbitrary")),
        name="ws_swiglu",
    )(h, w_stack, w_stack)


def ws_matmul_residual(lhs_groups, w_stack, layer, x, mod, gate_idx, tm=512, tn=1024):
    kdim, n = w_stack.shape[1:]
    gate_blk = n // tn
    specs, args = [], []
    for grp in lhs_groups:
        specs += _rows_specs(grp, tm, grp[0].shape[1], lambda j: 0)
        args += list(grp)
    specs.append(pl.BlockSpec((None, kdim, tn), lambda j, i: (layer, 0, j)))
    specs += _rows_specs(x, tm, tn, lambda j: j)
    specs.append(pl.BlockSpec((None, 1, tn), lambda j, i: (_group_of_tile(i, tm), 0, gate_idx * gate_blk + j)))
    return pl.pallas_call(
        functools.partial(_ws_res_kernel, lhs_arity=tuple(len(grp) for grp in lhs_groups), x_arity=len(x),
                          ctx_tiles=TP // tm),
        grid=(n // tn, T // tm),
        in_specs=specs,
        out_specs=pl.BlockSpec((tm, tn), lambda j, i: (i, j)),
        out_shape=jax.ShapeDtypeStruct((T, n), F32),
        scratch_shapes=[pltpu.VMEM((kdim, tn), BF16)],
        compiler_params=_params(("arbitrary", "arbitrary")),
        name="ws_matmul_residual",
    )(*args, w_stack, *x, mod)


def _final_norm_kernel(x_ref, g_ref, o_ref):
    o_ref[...] = _rms(x_ref[...], g_ref[...])


def final_norm(x, g, row0, n_rows, tm=1024):
    blk0 = row0 // tm
    return pl.pallas_call(
        _final_norm_kernel,
        grid=(n_rows // tm,),
        in_specs=[pl.BlockSpec((tm, D_MODEL), lambda i: (blk0 + i, 0)),
                  pl.BlockSpec((1, D_MODEL), lambda i: (0, 0))],
        out_specs=pl.BlockSpec((tm, D_MODEL), lambda i: (i, 0)),
        out_shape=jax.ShapeDtypeStruct((n_rows, D_MODEL), F32),
        compiler_params=_params(("arbitrary",)),
        name="final_norm",
    )(x, g.reshape(1, D_MODEL))


def _grid_angles(n_tok, rot_dim):
    t = jnp.arange((n_tok // GRID_W) * GRID_W)
    row = (t // GRID_W).astype(F32)
    col = (t % GRID_W).astype(F32)
    half = rot_dim // 2
    inv = ROPE_BASE ** (-jnp.arange(0, half, 2, dtype=F32) / half)
    return row[:, None] * inv[None], col[:, None] * inv[None]


def rope_tables(n_tok, rot_dim):
    ang_row, ang_col = _grid_angles(n_tok, rot_dim)
    zeros = jnp.zeros_like(ang_row)
    reps = LANE // rot_dim

    def lanes(r1, r2, c1, c2):
        return jnp.tile(jnp.concatenate([r1, r2, c1, c2], axis=-1), (1, reps))

    cr, sr, cc, sc = jnp.cos(ang_row), jnp.sin(ang_row), jnp.cos(ang_col), jnp.sin(ang_col)
    return lanes(cr, cr, cc, cc), lanes(-sr, zeros, -sc, zeros), lanes(zeros, sr, zeros, sc)


def _rope(x, cos, sin_lo, sin_hi, quarter):
    return (x * cos + pltpu.roll(x, LANE - quarter, 1) * sin_lo + pltpu.roll(x, quarter, 1) * sin_hi)


def _diff_lambda(lam_ref, lam_init):
    e1 = jnp.exp(jnp.sum(lam_ref[0:1, :] * lam_ref[1:2, :], axis=-1, keepdims=True))
    e2 = jnp.exp(jnp.sum(lam_ref[2:3, :] * lam_ref[3:4, :], axis=-1, keepdims=True))
    return e1 - e2 + lam_init


def _with_ones(v_bf):
    return jnp.concatenate([v_bf, jnp.ones(v_bf.shape, BF16)], axis=1)


def _softmax_av(q_bf, k_bf, v1_bf):
    s = _dot_nt(q_bf, k_bf)
    e = jnp.exp(s - jnp.max(s, axis=-1, keepdims=True))
    if v1_bf.shape[1] == LANE:
        return _dot(e.astype(BF16), v1_bf) * (1.0 / jnp.sum(e, axis=-1, keepdims=True))
    ov = _dot(e.astype(BF16), v1_bf)
    return ov[:, :LANE] * (1.0 / ov[:, LANE:])


def _diff_head(q, k_bf, v1_bf, lam, subln_g, lam_init, stack_maps):
    n = q.shape[0]
    first = lax.broadcasted_iota(jnp.int32, q.shape, 1) < DIFF_HD
    q0 = jnp.where(first, q, 0.0).astype(BF16)
    q1 = jnp.where(first, 0.0, q).astype(BF16)
    if stack_maps:
        r = _softmax_av(jnp.concatenate([q0, q1], axis=0), k_bf, v1_bf)
        o = r[:n] - lam * r[n:]
    else:
        o = _softmax_av(q0, k_bf, v1_bf) - lam * _softmax_av(q1, k_bf, v1_bf)
    return _rms(o, subln_g) * (1.0 - lam_init)


def _diff_ctx_kernel(q_ref, k_ref, v_ref, lam_ref, sg_ref, o_ref, ko_ref, vo_ref, *, lam_init):
    lam = _diff_lambda(lam_ref, lam_init)
    scale = DIFF_HD ** -0.5
    ko_ref[...] = k_ref[...]
    vo_ref[...] = v_ref[...]
    for h in range(HA):
        sl = slice(h * LANE, (h + 1) * LANE)
        y = _diff_head(q_ref[:, sl] * scale, k_ref[:, sl].astype(BF16), _with_ones(v_ref[:, sl].astype(BF16)),
                       lam, sg_ref[...], lam_init, stack_maps=True)
        o_ref[:, sl] = y.astype(o_ref.dtype)


def diff_attention_context(proj, lam_vec, subln_g, lam_init):
    w = HA * LANE

    def blk(c):
        return pl.BlockSpec((SEQ, w), lambda b: (b, c))

    return pl.pallas_call(
        functools.partial(_diff_ctx_kernel, lam_init=lam_init),
        grid=(BATCH,),
        in_specs=[blk(0), blk(1), blk(2),
                  pl.BlockSpec((4, DIFF_HD), lambda b: (0, 0)),
                  pl.BlockSpec((1, DIFF_VD), lambda b: (0, 0))],
        out_specs=[blk(0), blk(0), blk(0)],
        out_shape=[jax.ShapeDtypeStruct((TP, w), BF16), jax.ShapeDtypeStruct((TP, w), F32),
                   jax.ShapeDtypeStruct((TP, w), F32)],
        compiler_params=_params(("arbitrary",)),
        name="diff_attention_context",
    )(proj, proj, proj, lam_vec, subln_g.reshape(1, DIFF_VD))


def _diff_lat_kernel(q_ref, k_ref, v_ref, ck_ref, cv_ref, kc_ref, kl_ref, kh_ref, qc_ref, ql_ref, qh_ref,
                     lam_ref, sg_ref, o_ref, k_scr, v_scr, *, lam_init):
    quarter = DIFF_HD // 4

    @pl.when(pl.program_id(2) == 0)
    def _():
        k_scr[0:PAST_LEN, :] = ck_ref[...].astype(BF16)
        k_scr[PAST_LEN:, :] = _rope(k_ref[...], kc_ref[...], kl_ref[...], kh_ref[...], quarter).astype(BF16)
        v_scr[0:PAST_LEN, :] = cv_ref[...].astype(BF16)
        v_scr[PAST_LEN:, :] = v_ref[...].astype(BF16)

    lam = _diff_lambda(lam_ref, lam_init)
    sub = q_ref.shape[0] // DIFF_LAT_CHAINS
    for c in range(DIFF_LAT_CHAINS):
        rs = slice(c * sub, (c + 1) * sub)
        q = _rope(q_ref[rs, :], qc_ref[rs, :], ql_ref[rs, :], qh_ref[rs, :], quarter) * (DIFF_HD ** -0.5)
        y = _diff_head(q, k_scr[...], v_scr[...], lam, sg_ref[...], lam_init, stack_maps=False)
        o_ref[rs, :] = y.astype(o_ref.dtype)


def diff_attention_latent(proj, cache_k, cache_v, tables, lam_vec, subln_g, lam_init, tq=1024):
    nq = DEC_SEQ // tq
    row0 = TP // tq
    kv_row0 = TP // DEC_SEQ
    cos, sin_lo, sin_hi = tables
    full = pl.BlockSpec((DEC_SEQ, LANE), lambda b, h, i: (0, 0))
    qtab = pl.BlockSpec((tq, LANE), lambda b, h, i: (i, 0))
    cache = pl.BlockSpec((None, PAST_LEN, LANE), lambda b, h, i: (b, 0, h))
    return pl.pallas_call(
        functools.partial(_diff_lat_kernel, lam_init=lam_init),
        grid=(DEC_BATCH, HA, nq),
        in_specs=[pl.BlockSpec((tq, LANE), lambda b, h, i: (row0 + b * nq + i, h)),
                  pl.BlockSpec((DEC_SEQ, LANE), lambda b, h, i: (kv_row0 + b, HA + h)),
                  pl.BlockSpec((DEC_SEQ, LANE), lambda b, h, i: (kv_row0 + b, 2 * HA + h)),
                  cache, cache, full, full, full, qtab, qtab, qtab,
                  pl.BlockSpec((4, DIFF_HD), lambda b, h, i: (0, 0)),
                  pl.BlockSpec((1, DIFF_VD), lambda b, h, i: (0, 0))],
        out_specs=pl.BlockSpec((tq, LANE), lambda b, h, i: (b * nq + i, h)),
        out_shape=jax.ShapeDtypeStruct((TS, HA * LANE), BF16),
        scratch_shapes=[pltpu.VMEM((PAST_LEN + DEC_SEQ, LANE), BF16),
                        pltpu.VMEM((PAST_LEN + DEC_SEQ, LANE), BF16)],
        compiler_params=_params(("arbitrary", "arbitrary", "arbitrary")),
        name="diff_attention_latent",
    )(proj, proj, proj, cache_k, cache_v, cos, sin_lo, sin_hi, cos, sin_lo, sin_hi,
      lam_vec, subln_g.reshape(1, DIFF_VD))


def _split3(x):
    hi = x.astype(BF16)
    r = x - hi.astype(F32)
    mid = r.astype(BF16)
    lo = (r - mid.astype(F32)).astype(BF16)
    return hi, mid, lo


def _gla_kernel(q_ref, k_ref, v_ref, r_ref, g_ref, s0_ref, gg_ref, y_ref, sfin_ref, of_scr, ob_scr, st_scr,
                *, seq, pairs):
    c = GLA_CHUNK
    sb = GLA_SUPER * c
    n_super = seq // sb
    rows = lax.broadcasted_iota(jnp.int32, (sb, sb), 0)
    cols = lax.broadcasted_iota(jnp.int32, (sb, sb), 1)
    same_chunk = (rows // c) == (cols // c)
    keep = (same_chunk & (rows >= cols), same_chunk & (cols >= rows))
    tri = tuple(jnp.where(m, 1.0, 0.0).astype(BF16) for m in keep)
    lane = lax.broadcasted_iota(jnp.int32, (sb, LANE), 1)
    own = (lane < GLA_DK, lane >= GLA_DK)
    chunk_of_row = lax.broadcasted_iota(jnp.int32, (sb, LANE), 0) // c
    st_rows = lax.broadcasted_iota(jnp.int32, (2 * GLA_DV, LANE), 0)
    st_lane = lax.broadcasted_iota(jnp.int32, (2 * GLA_DV, LANE), 1)
    st_own = (st_rows < GLA_DV) == (st_lane < GLA_DK)
    zpad = jnp.zeros((GLA_DK, GLA_DV), F32)

    def expand(x):
        return jnp.concatenate([jnp.where(chunk_of_row == ci, x, 0.0) for ci in range(GLA_SUPER)],
                               axis=1).astype(BF16)

    for p in range(pairs):
        for d in range(2):
            s0 = [s0_ref[d, 2 * p + hh] for hh in range(2)]
            st_scr[p, d] = jnp.concatenate([jnp.concatenate([s0[0], zpad], axis=0).T,
                                            jnp.concatenate([zpad, s0[1]], axis=0).T], axis=0)

    def one_pair_direction(n, p, d):
        kl = slice(p * LANE, (p + 1) * LANE)
        vl = slice(2 * p * GLA_DV, 2 * (p + 1) * GLA_DV)
        r0 = (n if d == 0 else n_super - 1 - n) * sb
        rs = pl.ds(r0, sb) if n_super == 1 else pl.ds(pl.multiple_of(r0, sb), sb)
        hi, mid, lo = _split3(g_ref[d, rs, kl])
        b = _dot(tri[d], hi) + _dot(tri[d], mid) + _dot(tri[d], lo)
        ends = [b[ci * c + (c - 1 if d == 0 else 0):ci * c + (c if d == 0 else 1), :] for ci in range(GLA_SUPER)]
        total = jnp.concatenate([jnp.broadcast_to(e, (c, LANE)) for e in ends], axis=0)
        q_in = (q_ref[rs, kl] * (GLA_DK ** -0.5)) * jnp.exp(b)
        kk = k_ref[rs, kl]
        k_in = (kk * jnp.exp(-b)).astype(BF16)
        k_end = kk * jnp.exp(total - b)
        v = v_ref[rs, vl]
        v_bf = v.astype(BF16)
        kv_all = _dot(v.T.astype(BF16), expand(k_end))
        q2 = jnp.concatenate([jnp.where(own[0], q_in, 0.0), jnp.where(own[1], q_in, 0.0)], axis=0)
        a2 = _dot_nt(q2.astype(BF16), k_in)
        o = jnp.concatenate(
            [_dot(jnp.where(keep[d], a2[hh * sb:(hh + 1) * sb], 0.0).astype(BF16),
                  v_bf[:, hh * GLA_DV:(hh + 1) * GLA_DV]) for hh in range(2)], axis=1)
        st = st_scr[p, d]
        entering = [None] * GLA_SUPER
        for ci in (range(GLA_SUPER) if d == 0 else range(GLA_SUPER - 1, -1, -1)):
            entering[ci] = st
            st = st * jnp.exp(ends[ci]) + jnp.where(st_own, kv_all[:, ci * LANE:(ci + 1) * LANE], 0.0)
        st_scr[p, d] = st
        o += _dot_nt(expand(q_in), jnp.concatenate(entering, axis=1).astype(BF16))
        if d == 0:
            of_scr[rs, vl] = o
        else:
            ob_scr[rs, vl] = o

    def super_block(n, carry):
        for p in range(pairs):
            for d in range(2):
                one_pair_direction(n, p, d)
        return carry

    if n_super == 1:
        super_block(0, 0)
    else:
        lax.fori_loop(0, n_super, super_block, 0)

    for h in range(2 * pairs):
        vs = slice(h * GLA_DV, (h + 1) * GLA_DV)
        r = r_ref[:, vs]
        y = _rms(of_scr[:, vs] + ob_scr[:, vs], gg_ref[...]) * (r * jax.nn.sigmoid(r))
        y_ref[:, vs] = y.astype(y_ref.dtype)
        p, hh = divmod(h, 2)
        for d in range(2):
            sfin_ref[d, h] = st_scr[p, d, hh * GLA_DV:(hh + 1) * GLA_DV, :].T[hh * GLA_DK:(hh + 1) * GLA_DK, :]


def bidir_gla(proj, gates, s0, gla_g, n_batch, seq, row_blk0):
    shared_s0 = s0.shape[0] == 1
    pairs = HB // 2 if seq * HB * GLA_DV * 4 <= 1024 * 1024 else 1
    wqk = pairs * LANE
    wv = pairs * 2 * GLA_DV
    col_q = (2 * HA * 2 * DIFF_HD + HA * DIFF_VD) // wqk
    col_k = col_q + HB * GLA_DK // wqk
    col_v = (col_k * wqk + HB * GLA_DK) // wv
    col_r = col_v + HB * GLA_DV // wv
    n_tok = n_batch * seq
    return pl.pallas_call(
        functools.partial(_gla_kernel, seq=seq, pairs=pairs),
        grid=(n_batch, HB // 2 // pairs),
        in_specs=[pl.BlockSpec((seq, wqk), lambda b, p: (row_blk0 + b, col_q + p)),
                  pl.BlockSpec((seq, wqk), lambda b, p: (row_blk0 + b, col_k + p)),
                  pl.BlockSpec((seq, wv), lambda b, p: (row_blk0 + b, col_v + p)),
                  pl.BlockSpec((seq, wv), lambda b, p: (row_blk0 + b, col_r + p)),
                  pl.BlockSpec((2, seq, wqk), lambda b, p: (0, row_blk0 + b, p)),
                  pl.BlockSpec((None, 2, 2 * pairs, GLA_DK, GLA_DV),
                               lambda b, p: (0 if shared_s0 else b, 0, p, 0, 0)),
                  pl.BlockSpec((1, GLA_DV), lambda b, p: (0, 0))],
        out_specs=[pl.BlockSpec((seq, wv), lambda b, p: (b, p)),
                   pl.BlockSpec((None, 2, 2 * pairs, GLA_DK, GLA_DV), lambda b, p: (b, 0, p, 0, 0))],
        out_shape=[jax.ShapeDtypeStruct((n_tok, HB * GLA_DV), BF16),
                   jax.ShapeDtypeStruct((n_batch, 2, HB, GLA_DK, GLA_DV), F32)],
        scratch_shapes=[pltpu.VMEM((seq, wv), F32), pltpu.VMEM((seq, wv), F32),
                        pltpu.VMEM((pairs, 2, 2 * GLA_DV, LANE), F32)],
        compiler_params=_params(("arbitrary", "arbitrary")),
        name="bidir_gla",
    )(proj, proj, proj, proj, gates, s0, gla_g.reshape(1, GLA_DV))


def _head_slice(kh, gq):
    h = kh * Q_PER_KV + gq
    return slice(h * LANE, (h + 1) * LANE)


def _sink_column(sink_ref, kh, rows):
    return jnp.concatenate([jnp.full((rows, 1), sink_ref[kh * Q_PER_KV + gq], F32) for gq in range(Q_PER_KV)],
                           axis=0)


def _gqa_ctx_kernel(sink_ref, q_ref, k_ref, v_ref, o_ref, ko_ref, vo_ref):
    scale = HEAD_DIM ** -0.5
    ko_ref[...] = k_ref[...]
    vo_ref[...] = v_ref[...]
    for kh in range(KV_HEADS):
        ks = slice(kh * LANE, (kh + 1) * LANE)
        k_bf = k_ref[:, ks].astype(BF16)
        v1_bf = _with_ones(v_ref[:, ks].astype(BF16))
        for gq in range(Q_PER_KV):
            hs = _head_slice(kh, gq)
            sink = sink_ref[kh * Q_PER_KV + gq]
            s = _dot_nt((q_ref[:, hs] * scale).astype(BF16), k_bf)
            m = jnp.maximum(jnp.max(s, axis=-1, keepdims=True), sink)
            ov = _dot(jnp.exp(s - m).astype(BF16), v1_bf)
            o_ref[:, hs] = (ov[:, :LANE] * (1.0 / (ov[:, LANE:] + jnp.exp(sink - m)))).astype(o_ref.dtype)


def gqa_context(proj, sinks):
    wq = N_HEADS * HEAD_DIM
    wkv = KV_HEADS * HEAD_DIM

    def kv(c):
        return pl.BlockSpec((SEQ, wkv), lambda b: (b, c))

    return pl.pallas_call(
        _gqa_ctx_kernel,
        grid=(BATCH,),
        in_specs=[pl.BlockSpec(memory_space=pltpu.SMEM),
                  pl.BlockSpec((SEQ, wq), lambda b: (b, 0)), kv(wq // wkv), kv(wq // wkv + 1)],
        out_specs=[pl.BlockSpec((SEQ, wq), lambda b: (b, 0)), kv(0), kv(0)],
        out_shape=[jax.ShapeDtypeStruct((TP, wq), BF16), jax.ShapeDtypeStruct((TP, wkv), F32),
                   jax.ShapeDtypeStruct((TP, wkv), F32)],
        compiler_params=_params(("arbitrary",)),
        name="gqa_context",
    )(sinks, proj, proj, proj)


def _gqa_lat_kernel(sink_ref, q_ref, k_ref, v_ref, ck_ref, cv_ref, kc_ref, kl_ref, kh_ref,
                    qc_ref, ql_ref, qh_ref, o_ref, kw_scr, vw_scr, kc_scr, vc_scr):
    quarter = HEAD_DIM // 4
    n = pl.program_id(1)
    w = WINDOW
    wkv = KV_HEADS * HEAD_DIM

    @pl.when(n == 0)
    def _():
        kw_scr[0:w, :] = jnp.zeros((w, wkv), BF16)
        kw_scr[w + DEC_SEQ:, :] = jnp.zeros((w, wkv), BF16)
        vw_scr[0:w, :] = jnp.zeros((w, 2 * wkv), BF16)
        vw_scr[w + DEC_SEQ:, :] = jnp.zeros((w, 2 * wkv), BF16)
        for kh in range(KV_HEADS):
            ks = slice(kh * LANE, (kh + 1) * LANE)
            vs = slice(2 * kh * LANE, 2 * (kh + 1) * LANE)
            kw_scr[w:w + DEC_SEQ, ks] = _rope(k_ref[:, ks], kc_ref[...], kl_ref[...], kh_ref[...],
                                              quarter).astype(BF16)
            vw_scr[w:w + DEC_SEQ, vs] = _with_ones(v_ref[:, ks].astype(BF16))
            vc_scr[:, vs] = _with_ones(cv_ref[:, ks].astype(BF16))
        kc_scr[...] = ck_ref[...].astype(BF16)

    win = pl.ds(pl.multiple_of(n * w, w), 3 * w)
    qi = lax.broadcasted_iota(jnp.int32, (Q_PER_KV * w, 3 * w), 0) % w
    kj = lax.broadcasted_iota(jnp.int32, (Q_PER_KV * w, 3 * w), 1)
    kpos = n * w - w + kj
    valid = (kj >= qi) & (kj <= qi + 2 * w) & (kpos >= 0) & (kpos < DEC_SEQ)
    scale = HEAD_DIM ** -0.5
    for kh in range(KV_HEADS):
        ks = slice(kh * LANE, (kh + 1) * LANE)
        vs = slice(2 * kh * LANE, 2 * (kh + 1) * LANE)
        q = jnp.concatenate(
            [(_rope(q_ref[:, _head_slice(kh, gq)], qc_ref[...], ql_ref[...], qh_ref[...], quarter) * scale
              ).astype(BF16) for gq in range(Q_PER_KV)], axis=0)
        sink = _sink_column(sink_ref, kh, w)
        s_c = _dot_nt(q, kc_scr[:, ks])
        s_w = jnp.where(valid, _dot_nt(q, kw_scr[win, ks]), NEG_INF)
        m = jnp.maximum(jnp.maximum(jnp.max(s_c, axis=-1, keepdims=True),
                                    jnp.max(s_w, axis=-1, keepdims=True)), sink)
        ov = (_dot(jnp.exp(s_c - m).astype(BF16), vc_scr[:, vs])
              + _dot(jnp.exp(s_w - m).astype(BF16), vw_scr[win, vs]))
        o = ov[:, :LANE] * (1.0 / (ov[:, LANE:] + jnp.exp(sink - m)))
        for gq in range(Q_PER_KV):
            o_ref[:, _head_slice(kh, gq)] = o[gq * w:(gq + 1) * w].astype(o_ref.dtype)


def gqa_latent(proj, cache_k, cache_v, tables, sinks):
    wq = N_HEADS * HEAD_DIM
    wkv = KV_HEADS * HEAD_DIM
    nq = DEC_SEQ // WINDOW
    row0 = TP // WINDOW
    kv_row0 = TP // DEC_SEQ
    cos, sin_lo, sin_hi = tables
    full = pl.BlockSpec((DEC_SEQ, LANE), lambda b, i: (0, 0))
    qtab = pl.BlockSpec((WINDOW, LANE), lambda b, i: (i, 0))
    cache = pl.BlockSpec((None, PAST_LEN, wkv), lambda b, i: (b, 0, 0))
    return pl.pallas_call(
        _gqa_lat_kernel,
        grid=(DEC_BATCH, nq),
        in_specs=[pl.BlockSpec(memory_space=pltpu.SMEM),
                  pl.BlockSpec((WINDOW, wq), lambda b, i: (row0 + b * nq + i, 0)),
                  pl.BlockSpec((DEC_SEQ, wkv), lambda b, i: (kv_row0 + b, wq // wkv)),
                  pl.BlockSpec((DEC_SEQ, wkv), lambda b, i: (kv_row0 + b, wq // wkv + 1)),
                  cache, cache, full, full, full, qtab, qtab, qtab],
        out_specs=pl.BlockSpec((WINDOW, wq), lambda b, i: (b * nq + i, 0)),
        out_shape=jax.ShapeDtypeStruct((TS, wq), BF16),
        scratch_shapes=[pltpu.VMEM((DEC_SEQ + 2 * WINDOW, wkv), BF16),
                        pltpu.VMEM((DEC_SEQ + 2 * WINDOW, 2 * wkv), BF16),
                        pltpu.VMEM((PAST_LEN, wkv), BF16),
                        pltpu.VMEM((PAST_LEN, 2 * wkv), BF16)],
        compiler_params=_params(("arbitrary", "arbitrary")),
        name="gqa_latent",
    )(sinks, proj, proj, proj, cache_k, cache_v, cos, sin_lo, sin_hi, cos, sin_lo, sin_hi)


def _even_layer(xc, xs, mod, j, layer, norm_g, w_in_even, lam_vec, subln_g, w_gate1, w_gate2, b_gate, gla_norm_g,
                w_out_even, cache_a_k, cache_a_v, state_b, tables):
    lam_init = 0.8 - 0.6 * math.exp(-0.3 * layer)
    ng = HB * GLA_DK
    wg1 = jnp.zeros((D_MODEL, LANE), F32).at[:, :2 * GLA_GATE_RANK].set(
        jnp.concatenate([w_gate1[j, 0], w_gate1[j, 1]], axis=-1)).astype(BF16)
    wg2 = jnp.zeros((2, LANE, ng), F32)
    for e in range(2):
        wg2 = wg2.at[e, e * GLA_GATE_RANK:(e + 1) * GLA_GATE_RANK, :].set(w_gate2[j, e])
    h, gates = norm_mod_gate(xc, xs, norm_g[layer, 0], mod, 0, 1, wg1, wg2.astype(BF16),
                             b_gate[j].reshape(2, 1, ng))
    proj = ws_matmul(h, w_in_even, j)
    ya_c, new_k, new_v = diff_attention_context(proj, lam_vec, subln_g[j], lam_init)
    ya_s = diff_attention_latent(proj, cache_a_k[:, j].reshape(DEC_BATCH, PAST_LEN, HA * 2 * DIFF_HD),
                                 cache_a_v[:, j].reshape(DEC_BATCH, PAST_LEN, HA * DIFF_VD),
                                 tables, lam_vec, subln_g[j], lam_init)
    yb_c, s_fin = bidir_gla(proj, gates, jnp.zeros((1, 2, HB, GLA_DK, GLA_DV), F32), gla_norm_g[j],
                            BATCH, SEQ, 0)
    yb_s, _ = bidir_gla(proj, gates, state_b[:, j], gla_norm_g[j], DEC_BATCH, DEC_SEQ, TP // DEC_SEQ)
    x = ws_matmul_residual([(ya_c, ya_s), (yb_c, yb_s)], w_out_even, j, (xc, xs), mod, 2)
    return (x, new_k.reshape(BATCH, SEQ, HA, 2 * DIFF_HD), new_v.reshape(BATCH, SEQ, HA, DIFF_VD), s_fin)


def _odd_layer(x, mod, j, layer, norm_g, w_in_odd, sinks, w_out_odd, cache_c_k, cache_c_v, tables):
    wkv = KV_HEADS * HEAD_DIM
    proj = ws_matmul(norm_mod(x, norm_g[layer, 0], mod, 0, 1), w_in_odd, j)
    o_c, new_k, new_v = gqa_context(proj, sinks[j])
    o_s = gqa_latent(proj, cache_c_k[:, j].reshape(DEC_BATCH, PAST_LEN, wkv),
                     cache_c_v[:, j].reshape(DEC_BATCH, PAST_LEN, wkv), tables, sinks[j])
    x = ws_matmul_residual([(o_c, o_s)], w_out_odd, j, (x,), mod, 2)
    return (x, new_k.reshape(BATCH, SEQ, KV_HEADS, HEAD_DIM), new_v.reshape(BATCH, SEQ, KV_HEADS, HEAD_DIM))


def _ffn(x, mod, layer, norm_g, w_ffn_in, w_ffn_out):
    act = ws_swiglu(norm_mod(x, norm_g[layer, 1], mod, 3, 4), w_ffn_in, layer)
    return ws_matmul_residual([(act,)], w_ffn_out, layer, (x,), mod, 5, tm=512, tn=512)


def kernel(x_prompt, x_sample, c, cache_a_k, cache_a_v, state_b, cache_c_k, cache_c_v, c_ctx, w_ada, b_ada,
           norm_g, w_in_even, lam_q1, lam_k1, lam_q2, lam_k2, subln_g, w_gate1, w_gate2, b_gate, gla_norm_g,
           w_out_even, w_in_odd, sinks, w_out_odd, w_ffn_in, w_ffn_out, final_norm_g):
    assert DEPTH == 2, "layer 0 reads the two input streams, layer 1 the fused token matrix"
    cvec = jnp.concatenate([c_ctx[None, :], c, jnp.zeros((MOD_ROWS - 1 - DEC_BATCH, D_MODEL), F32)], axis=0)
    mods = adaln(cvec, w_ada, b_ada).reshape(DEPTH, MOD_ROWS, 1, 6 * D_MODEL)
    tab_d = rope_tables(DEC_SEQ, DIFF_HD)
    tab_h = rope_tables(DEC_SEQ, HEAD_DIM)

    lam_vec = jnp.stack([lam_q1[0], lam_k1[0], lam_q2[0], lam_k2[0]], axis=0)
    x, ak, av, sb = _even_layer(x_prompt.reshape(TP, D_MODEL), x_sample.reshape(TS, D_MODEL), mods[0], 0, 0,
                                norm_g, w_in_even, lam_vec, subln_g, w_gate1, w_gate2, b_gate, gla_norm_g,
                                w_out_even, cache_a_k, cache_a_v, state_b, tab_d)
    x = _ffn(x, mods[0], 0, norm_g, w_ffn_in, w_ffn_out)
    x, ck, cv = _odd_layer(x, mods[1], 0, 1, norm_g, w_in_odd, sinks, w_out_odd, cache_c_k, cache_c_v, tab_h)
    x = _ffn(x, mods[1], 1, norm_g, w_ffn_in, w_ffn_out)

    y_prompt = final_norm(x, final_norm_g, 0, TP).reshape(BATCH, SEQ, D_MODEL)
    y_sample = final_norm(x, final_norm_g, TP, TS).reshape(DEC_BATCH, DEC_SEQ, D_MODEL)
    return (y_prompt, y_sample, ak[:, None], av[:, None], sb[:, None], ck[:, None], cv[:, None])
```

```python
import functools
import math

import jax
import jax.numpy as jnp
from jax import lax
from jax.experimental import pallas as pl
from jax.experimental.pallas import tpu as pltpu

D_MODEL = 2048
BATCH = 32
SEQ = 256
DEPTH = 2
DEC_BATCH = 2
DEC_SEQ = 2048
PAST_LEN = 512
GRID_W = 64
HEAD_DIM = 128
N_HEADS = D_MODEL // HEAD_DIM
HA = N_HEADS // 2
DIFF_HD = HEAD_DIM // 2
DIFF_VD = HEAD_DIM
HB = N_HEADS // 2
GLA_DK = HEAD_DIM // 2
GLA_DV = HEAD_DIM
GLA_GATE_RANK = 16
GLA_TAU = 16.0
GLA_CHUNK = 64
GLA_SUPER = 4
DIFF_LAT_CHAINS = 4
KV_HEADS = N_HEADS // 4
Q_PER_KV = N_HEADS // KV_HEADS
WINDOW = 128
D_FF = -(-8 * D_MODEL // (3 * 256)) * 256
ROPE_BASE = 10000.0
EPS = 1e-6
NEG_INF = -1e30
EVEN_IN = 2 * HA * 2 * DIFF_HD + HA * DIFF_VD + 2 * HB * GLA_DK + 2 * HB * GLA_DV
ODD_IN = (N_HEADS + 2 * KV_HEADS) * HEAD_DIM

TP = BATCH * SEQ
TS = DEC_BATCH * DEC_SEQ
T = TP + TS
LANE = 128
MOD_ROWS = 8
NORM_ROWS = 16
NORM_UNROLL = 8
VMEM_LIMIT = 56 * 1024 * 1024

F32 = jnp.float32
BF16 = jnp.bfloat16


def _params(semantics, vmem=VMEM_LIMIT):
    return pltpu.CompilerParams(dimension_semantics=semantics, vmem_limit_bytes=vmem)


def _dot(a, b):
    return jnp.dot(a, b, preferred_element_type=F32)


def _dot_nt(a, b):
    return lax.dot_general(a, b, (((1,), (1,)), ((), ())), preferred_element_type=F32)


def _group_of_tile(i, tm):
    r = i * tm
    return jnp.where(r < TP, 0, 1 + (r - TP) // DEC_SEQ)


def _rms(x, g):
    return (x * lax.rsqrt(jnp.mean(x * x, axis=-1, keepdims=True) + EPS)) * g


def _adaln_kernel(c_ref, w_ref, b_ref, o_ref):
    c = c_ref[...]
    s = c * jax.nn.sigmoid(c)
    o_ref[...] = _dot(s.astype(BF16), w_ref[...].astype(BF16)) + b_ref[...]


def adaln(cvec, w_ada, b_ada, tn=512):
    n = w_ada.shape[-1]
    return pl.pallas_call(
        _adaln_kernel,
        grid=(DEPTH, n // tn),
        in_specs=[
            pl.BlockSpec((MOD_ROWS, D_MODEL), lambda l, j: (0, 0)),
            pl.BlockSpec((None, D_MODEL, tn), lambda l, j: (l, 0, j)),
            pl.BlockSpec((None, 1, tn), lambda l, j: (l, 0, j)),
        ],
        out_specs=pl.BlockSpec((None, MOD_ROWS, tn), lambda l, j: (l, 0, j)),
        out_shape=jax.ShapeDtypeStruct((DEPTH, MOD_ROWS, n), F32),
        compiler_params=_params(("arbitrary", "arbitrary")),
        name="adaln",
    )(cvec, w_ada, b_ada.reshape(DEPTH, 1, n))


def _by_row_groups(x_ref, o_ref, fn):
    def body(r, carry):
        rs = pl.ds(pl.multiple_of(r * NORM_ROWS, NORM_ROWS), NORM_ROWS)
        o_ref[rs, :] = fn(x_ref[rs, :]).astype(o_ref.dtype)
        return carry

    lax.fori_loop(0, x_ref.shape[0] // NORM_ROWS, body, 0, unroll=NORM_UNROLL)


def _norm_mod_into(h_ref, x_ref, g_ref, shift_ref, scale_ref):
    _by_row_groups(x_ref, h_ref, lambda x: _rms(x, g_ref[...]) * (1 + scale_ref[...]) + shift_ref[...])


def _norm_mod_gate_kernel(xc_ref, xs_ref, g_ref, shift_ref, scale_ref, wg1_ref, wg2_ref, bg_ref,
                          h_ref, gate_ref, *, ctx_tiles):
    @pl.when(pl.program_id(0) < ctx_tiles)
    def _():
        _norm_mod_into(h_ref, xc_ref, g_ref, shift_ref, scale_ref)

    @pl.when(pl.program_id(0) >= ctx_tiles)
    def _():
        _norm_mod_into(h_ref, xs_ref, g_ref, shift_ref, scale_ref)

    low = _dot(h_ref[...], wg1_ref[...]).astype(BF16)
    for e in range(2):
        logit = _dot(low, wg2_ref[e]) + bg_ref[e]
        log_sig = jnp.minimum(logit, 0.0) - jnp.log(1.0 + jnp.exp(-jnp.abs(logit)))
        gate_ref[e] = log_sig / GLA_TAU


def _mod_specs(tm, shift_idx, scale_idx):
    def spec(idx):
        return pl.BlockSpec((None, 1, D_MODEL), lambda i: (_group_of_tile(i, tm), 0, idx))
    return [pl.BlockSpec((1, D_MODEL), lambda i: (0, 0)), spec(shift_idx), spec(scale_idx)]


def norm_mod_gate(xc, xs, g, mod, shift_idx, scale_idx, wg1, wg2, bg, tm=512):
    ng = HB * GLA_DK
    ctx_tiles = TP // tm
    return pl.pallas_call(
        functools.partial(_norm_mod_gate_kernel, ctx_tiles=ctx_tiles),
        grid=(T // tm,),
        in_specs=[pl.BlockSpec((tm, D_MODEL), lambda i: (jnp.minimum(i, ctx_tiles - 1), 0)),
                  pl.BlockSpec((tm, D_MODEL), lambda i: (jnp.maximum(i - ctx_tiles, 0), 0))]
        + _mod_specs(tm, shift_idx, scale_idx)
        + [pl.BlockSpec((D_MODEL, LANE), lambda i: (0, 0)),
           pl.BlockSpec((2, LANE, ng), lambda i: (0, 0, 0)),
           pl.BlockSpec((2, 1, ng), lambda i: (0, 0, 0))],
        out_specs=[pl.BlockSpec((tm, D_MODEL), lambda i: (i, 0)),
                   pl.BlockSpec((2, tm, ng), lambda i: (0, i, 0))],
        out_shape=[jax.ShapeDtypeStruct((T, D_MODEL), BF16), jax.ShapeDtypeStruct((2, T, ng), F32)],
        compiler_params=_params(("arbitrary",)),
        name="norm_mod_gate",
    )(xc, xs, g.reshape(1, D_MODEL), mod, mod, wg1, wg2, bg)


def _cast_weights(w_refs, w_scrs):
    @pl.when(pl.program_id(1) == 0)
    def _():
        for w_ref, w_scr in zip(w_refs, w_scrs):
            w_scr[...] = w_ref[...].astype(BF16)


def _ws_plain_kernel(h_ref, w_ref, o_ref, w_scr):
    _cast_weights([w_ref], [w_scr])
    o_ref[...] = _dot(h_ref[...], w_scr[...])


def _fold_weights(w_refs, gcol_ref, scol_ref, shift_ref, w_scrs, b_scrs, tm):
    row0 = pl.program_id(1) * tm

    @pl.when((row0 == 0) | ((row0 >= TP) & ((row0 - TP) % DEC_SEQ == 0)))
    def _():
        row_gain = gcol_ref[...] * (1 + scol_ref[...])
        shift = jnp.broadcast_to(shift_ref[...], (MOD_ROWS, D_MODEL)).astype(BF16)
        for w_ref, w_scr, b_scr in zip(w_refs, w_scrs, b_scrs):
            w = w_ref[...]
            w_scr[...] = (w * row_gain).astype(BF16)
            b_scr[...] = _dot(shift, w.astype(BF16))


def _row_scale(ssq_ref):
    ssq = jnp.sum(ssq_ref[...], axis=0)
    return lax.rsqrt(jnp.sum(ssq, axis=-1, keepdims=True) / D_MODEL + EPS)


def _ws_fold_kernel(xb_ref, ssq_ref, w_ref, gcol_ref, scol_ref, shift_ref, o_ref, w_scr, b_scr, *, tm):
    _fold_weights([w_ref], gcol_ref, scol_ref, shift_ref, [w_scr], [b_scr], tm)
    o_ref[...] = _dot(xb_ref[...], w_scr[...]) * _row_scale(ssq_ref) + b_scr[0:1, :]


def _ws_swiglu_fold_kernel(xb_ref, ssq_ref, wg_ref, wu_ref, gcol_ref, scol_ref, shift_ref, o_ref,
                           wg_scr, wu_scr, bg_scr, bu_scr, *, tm):
    _fold_weights([wg_ref, wu_ref], gcol_ref, scol_ref, shift_ref, [wg_scr, wu_scr], [bg_scr, bu_scr], tm)
    xb = xb_ref[...]
    rs = _row_scale(ssq_ref)
    gate = _dot(xb, wg_scr[...]) * rs + bg_scr[0:1, :]
    up = _dot(xb, wu_scr[...]) * rs + bu_scr[0:1, :]
    o_ref[...] = (gate * jax.nn.sigmoid(gate) * up).astype(o_ref.dtype)


def _ws_res_kernel(*refs, lhs_arity, x_arity, ctx_tiles, with_stats):
    n_l = sum(lhs_arity)
    w_ref = refs[n_l]
    x_refs = refs[n_l + 1:n_l + 1 + x_arity]
    gate_ref, o_ref = refs[n_l + 1 + x_arity:n_l + 3 + x_arity]
    w_scr = refs[-1]
    groups, at = [], 0
    for a in lhs_arity:
        groups.append(refs[at:at + a])
        at += a
    _cast_weights([w_ref], [w_scr])

    def emit(side):
        acc, k0 = None, 0
        for grp in groups:
            ref = grp[side] if len(grp) == 2 else grp[0]
            part = _dot(ref[...], w_scr[k0:k0 + ref.shape[1], :])
            acc = part if acc is None else acc + part
            k0 += ref.shape[1]
        x_ref = x_refs[side] if x_arity == 2 else x_refs[0]
        x_new = x_ref[...] + gate_ref[...] * acc
        o_ref[...] = x_new
        if with_stats:
            xb_ref, ssq_ref = refs[n_l + 3 + x_arity:n_l + 5 + x_arity]
            xb_ref[...] = x_new.astype(BF16)
            sq = x_new * x_new
            part = sq[:, 0:LANE]
            for c0 in range(LANE, sq.shape[1], LANE):
                part = part + sq[:, c0:c0 + LANE]
            ssq_ref[...] = part

    if x_arity == 2 or 2 in lhs_arity:
        @pl.when(pl.program_id(1) < ctx_tiles)
        def _():
            emit(0)

        @pl.when(pl.program_id(1) >= ctx_tiles)
        def _():
            emit(1)
    else:
        emit(0)


def _rows_specs(arrays, tm, width, col):
    if len(arrays) == 1:
        return [pl.BlockSpec((tm, width), lambda j, i: (i, col(j)))]
    ctx_tiles = TP // tm
    return [pl.BlockSpec((tm, width), lambda j, i: (jnp.minimum(i, ctx_tiles - 1), col(j))),
            pl.BlockSpec((tm, width), lambda j, i: (jnp.maximum(i - ctx_tiles, 0), col(j)))]


def ws_matmul(h, w_stack, layer, tm=1024, tn=1024):
    kdim, n = w_stack.shape[1:]
    return pl.pallas_call(
        _ws_plain_kernel,
        grid=(n // tn, T // tm),
        in_specs=[pl.BlockSpec((tm, kdim), lambda j, i: (i, 0)),
                  pl.BlockSpec((None, kdim, tn), lambda j, i: (layer, 0, j))],
        out_specs=pl.BlockSpec((tm, tn), lambda j, i: (i, j)),
        out_shape=jax.ShapeDtypeStruct((T, n), F32),
        scratch_shapes=[pltpu.VMEM((kdim, tn), BF16)],
        compiler_params=_params(("arbitrary", "arbitrary")),
        name="ws_matmul",
    )(h, w_stack)


def _fold_specs(stats, g, mod, mod_t, shift_idx, scale_idx, tm):
    xb, ssq = stats
    specs = [pl.BlockSpec((tm, D_MODEL), lambda j, i: (i, 0)),
             pl.BlockSpec((ssq.shape[0], tm, LANE), lambda j, i: (0, i, 0))]
    mod_specs = [pl.BlockSpec((D_MODEL, 1), lambda j, i: (0, 0)),
                 pl.BlockSpec((None, None, D_MODEL, 1), lambda j, i: (_group_of_tile(i, tm), scale_idx, 0, 0)),
                 pl.BlockSpec((None, 1, D_MODEL), lambda j, i: (_group_of_tile(i, tm), 0, shift_idx))]
    return specs, mod_specs, (xb, ssq), (g.reshape(D_MODEL, 1), mod_t, mod)


def ws_matmul_folded(stats, g, mod, mod_t, shift_idx, scale_idx, w_stack, layer, tm=1024, tn=1024):
    kdim, n = w_stack.shape[1:]
    x_specs, mod_specs, x_args, mod_args = _fold_specs(stats, g, mod, mod_t, shift_idx, scale_idx, tm)
    return pl.pallas_call(
        functools.partial(_ws_fold_kernel, tm=tm),
        grid=(n // tn, T // tm),
        in_specs=x_specs + [pl.BlockSpec((None, kdim, tn), lambda j, i: (layer, 0, j))] + mod_specs,
        out_specs=pl.BlockSpec((tm, tn), lambda j, i: (i, j)),
        out_shape=jax.ShapeDtypeStruct((T, n), F32),
        scratch_shapes=[pltpu.VMEM((kdim, tn), BF16), pltpu.VMEM((MOD_ROWS, tn), F32)],
        compiler_params=_params(("arbitrary", "arbitrary")),
        name="ws_matmul_folded",
    )(*x_args, w_stack, *mod_args)


def ws_swiglu_folded(stats, g, mod, mod_t, shift_idx, scale_idx, w_stack, layer, tm=1024, tn=512):
    kdim = w_stack.shape[1]
    nf = D_FF // tn
    x_specs, mod_specs, x_args, mod_args = _fold_specs(stats, g, mod, mod_t, shift_idx, scale_idx, tm)
    return pl.pallas_call(
        functools.partial(_ws_swiglu_fold_kernel, tm=tm),
        grid=(nf, T // tm),
        in_specs=x_specs + [pl.BlockSpec((None, kdim, tn), lambda j, i: (layer, 0, j)),
                            pl.BlockSpec((None, kdim, tn), lambda j, i: (layer, 0, j + nf))] + mod_specs,
        out_specs=pl.BlockSpec((tm, tn), lambda j, i: (i, j)),
        out_shape=jax.ShapeDtypeStruct((T, D_FF), BF16),
        scratch_shapes=[pltpu.VMEM((kdim, tn), BF16), pltpu.VMEM((kdim, tn), BF16),
                        pltpu.VMEM((MOD_ROWS, tn), F32), pltpu.VMEM((MOD_ROWS, tn), F32)],
        compiler_params=_params(("arbitrary", "arbitrary")),
        name="ws_swiglu_folded",
    )(*x_args, w_stack, w_stack, *mod_args)


def ws_matmul_residual(lhs_groups, w_stack, layer, x, mod, gate_idx, tm=512, tn=1024, with_stats=True):
    kdim, n = w_stack.shape[1:]
    gate_blk = n // tn
    specs, args = [], []
    for grp in lhs_groups:
        specs += _rows_specs(grp, tm, grp[0].shape[1], lambda j: 0)
        args += list(grp)
    specs.append(pl.BlockSpec((None, kdim, tn), lambda j, i: (layer, 0, j)))
    specs += _rows_specs(x, tm, tn, lambda j: j)
    specs.append(pl.BlockSpec((None, 1, tn), lambda j, i: (_group_of_tile(i, tm), 0, gate_idx * gate_blk + j)))
    out_specs = [pl.BlockSpec((tm, tn), lambda j, i: (i, j))]
    out_shape = [jax.ShapeDtypeStruct((T, n), F32)]
    if with_stats:
        out_specs += [pl.BlockSpec((tm, tn), lambda j, i: (i, j)),
                      pl.BlockSpec((None, tm, LANE), lambda j, i: (j, i, 0))]
        out_shape += [jax.ShapeDtypeStruct((T, n), BF16), jax.ShapeDtypeStruct((n // tn, T, LANE), F32)]
    outs = pl.pallas_call(
        functools.partial(_ws_res_kernel, lhs_arity=tuple(len(grp) for grp in lhs_groups), x_arity=len(x),
                          ctx_tiles=TP // tm, with_stats=with_stats),
        grid=(n // tn, T // tm),
        in_specs=specs,
        out_specs=out_specs,
        out_shape=out_shape,
        scratch_shapes=[pltpu.VMEM((kdim, tn), BF16)],
        compiler_params=_params(("arbitrary", "arbitrary")),
        name="ws_matmul_residual",
    )(*args, w_stack, *x, mod)
    return (outs[0], (outs[1], outs[2])) if with_stats else outs[0]


def _final_norm_kernel(x_ref, g_ref, o_ref):
    o_ref[...] = _rms(x_ref[...], g_ref[...])


def final_norm(x, g, row0, n_rows, tm=1024):
    blk0 = row0 // tm
    return pl.pallas_call(
        _final_norm_kernel,
        grid=(n_rows // tm,),
        in_specs=[pl.BlockSpec((tm, D_MODEL), lambda i: (blk0 + i, 0)),
                  pl.BlockSpec((1, D_MODEL), lambda i: (0, 0))],
        out_specs=pl.BlockSpec((tm, D_MODEL), lambda i: (i, 0)),
        out_shape=jax.ShapeDtypeStruct((n_rows, D_MODEL), F32),
        compiler_params=_params(("arbitrary",)),
        name="final_norm",
    )(x, g.reshape(1, D_MODEL))


def _grid_angles(n_tok, rot_dim):
    t = jnp.arange((n_tok // GRID_W) * GRID_W)
    row = (t // GRID_W).astype(F32)
    col = (t % GRID_W).astype(F32)
    half = rot_dim // 2
    inv = ROPE_BASE ** (-jnp.arange(0, half, 2, dtype=F32) / half)
    return row[:, None] * inv[None], col[:, None] * inv[None]


def rope_tables(n_tok, rot_dim):
    ang_row, ang_col = _grid_angles(n_tok, rot_dim)
    zeros = jnp.zeros_like(ang_row)
    reps = LANE // rot_dim

    def lanes(r1, r2, c1, c2):
        return jnp.tile(jnp.concatenate([r1, r2, c1, c2], axis=-1), (1, reps))

    cr, sr, cc, sc = jnp.cos(ang_row), jnp.sin(ang_row), jnp.cos(ang_col), jnp.sin(ang_col)
    return lanes(cr, cr, cc, cc), lanes(-sr, zeros, -sc, zeros), lanes(zeros, sr, zeros, sc)


def _rope(x, cos, sin_lo, sin_hi, quarter):
    return (x * cos + pltpu.roll(x, LANE - quarter, 1) * sin_lo + pltpu.roll(x, quarter, 1) * sin_hi)


def _diff_lambda(lam_ref, lam_init):
    e1 = jnp.exp(jnp.sum(lam_ref[0:1, :] * lam_ref[1:2, :], axis=-1, keepdims=True))
    e2 = jnp.exp(jnp.sum(lam_ref[2:3, :] * lam_ref[3:4, :], axis=-1, keepdims=True))
    return e1 - e2 + lam_init


def _with_ones(v_bf):
    return jnp.concatenate([v_bf, jnp.ones(v_bf.shape, BF16)], axis=1)


def _softmax_av(q_bf, k_bf, v1_bf):
    s = _dot_nt(q_bf, k_bf)
    e = jnp.exp(s - jnp.max(s, axis=-1, keepdims=True))
    if v1_bf.shape[1] == LANE:
        return _dot(e.astype(BF16), v1_bf) * (1.0 / jnp.sum(e, axis=-1, keepdims=True))
    ov = _dot(e.astype(BF16), v1_bf)
    return ov[:, :LANE] * (1.0 / ov[:, LANE:])


def _diff_head(q, k_bf, v1_bf, lam, subln_g, lam_init, stack_maps):
    n = q.shape[0]
    first = lax.broadcasted_iota(jnp.int32, q.shape, 1) < DIFF_HD
    q0 = jnp.where(first, q, 0.0).astype(BF16)
    q1 = jnp.where(first, 0.0, q).astype(BF16)
    if stack_maps:
        r = _softmax_av(jnp.concatenate([q0, q1], axis=0), k_bf, v1_bf)
        o = r[:n] - lam * r[n:]
    else:
        o = _softmax_av(q0, k_bf, v1_bf) - lam * _softmax_av(q1, k_bf, v1_bf)
    return _rms(o, subln_g) * (1.0 - lam_init)


def _diff_ctx_kernel(q_ref, k_ref, v_ref, lam_ref, sg_ref, o_ref, ko_ref, vo_ref, *, lam_init):
    lam = _diff_lambda(lam_ref, lam_init)
    scale = DIFF_HD ** -0.5
    ko_ref[...] = k_ref[...]
    vo_ref[...] = v_ref[...]
    for h in range(HA):
        sl = slice(h * LANE, (h + 1) * LANE)
        y = _diff_head(q_ref[:, sl] * scale, k_ref[:, sl].astype(BF16), _with_ones(v_ref[:, sl].astype(BF16)),
                       lam, sg_ref[...], lam_init, stack_maps=True)
        o_ref[:, sl] = y.astype(o_ref.dtype)


def diff_attention_context(proj, lam_vec, subln_g, lam_init):
    w = HA * LANE

    def blk(c):
        return pl.BlockSpec((SEQ, w), lambda b: (b, c))

    return pl.pallas_call(
        functools.partial(_diff_ctx_kernel, lam_init=lam_init),
        grid=(BATCH,),
        in_specs=[blk(0), blk(1), blk(2),
                  pl.BlockSpec((4, DIFF_HD), lambda b: (0, 0)),
                  pl.BlockSpec((1, DIFF_VD), lambda b: (0, 0))],
        out_specs=[blk(0), blk(0), blk(0)],
        out_shape=[jax.ShapeDtypeStruct((TP, w), BF16), jax.ShapeDtypeStruct((TP, w), F32),
                   jax.ShapeDtypeStruct((TP, w), F32)],
        compiler_params=_params(("arbitrary",)),
        name="diff_attention_context",
    )(proj, proj, proj, lam_vec, subln_g.reshape(1, DIFF_VD))


def _diff_lat_kernel(q_ref, k_ref, v_ref, ck_ref, cv_ref, kc_ref, kl_ref, kh_ref, qc_ref, ql_ref, qh_ref,
                     lam_ref, sg_ref, o_ref, k_scr, v_scr, *, lam_init):
    quarter = DIFF_HD // 4

    @pl.when(pl.program_id(2) == 0)
    def _():
        k_scr[0:PAST_LEN, :] = ck_ref[...].astype(BF16)
        k_scr[PAST_LEN:, :] = _rope(k_ref[...], kc_ref[...], kl_ref[...], kh_ref[...], quarter).astype(BF16)
        v_scr[0:PAST_LEN, :] = cv_ref[...].astype(BF16)
        v_scr[PAST_LEN:, :] = v_ref[...].astype(BF16)

    lam = _diff_lambda(lam_ref, lam_init)
    sub = q_ref.shape[0] // DIFF_LAT_CHAINS
    for c in range(DIFF_LAT_CHAINS):
        rs = slice(c * sub, (c + 1) * sub)
        q = _rope(q_ref[rs, :], qc_ref[rs, :], ql_ref[rs, :], qh_ref[rs, :], quarter) * (DIFF_HD ** -0.5)
        y = _diff_head(q, k_scr[...], v_scr[...], lam, sg_ref[...], lam_init, stack_maps=False)
        o_ref[rs, :] = y.astype(o_ref.dtype)


def diff_attention_latent(proj, cache_k, cache_v, tables, lam_vec, subln_g, lam_init, tq=1024):
    nq = DEC_SEQ // tq
    row0 = TP // tq
    kv_row0 = TP // DEC_SEQ
    cos, sin_lo, sin_hi = tables
    full = pl.BlockSpec((DEC_SEQ, LANE), lambda b, h, i: (0, 0))
    qtab = pl.BlockSpec((tq, LANE), lambda b, h, i: (i, 0))
    cache = pl.BlockSpec((None, PAST_LEN, LANE), lambda b, h, i: (b, 0, h))
    return pl.pallas_call(
        functools.partial(_diff_lat_kernel, lam_init=lam_init),
        grid=(DEC_BATCH, HA, nq),
        in_specs=[pl.BlockSpec((tq, LANE), lambda b, h, i: (row0 + b * nq + i, h)),
                  pl.BlockSpec((DEC_SEQ, LANE), lambda b, h, i: (kv_row0 + b, HA + h)),
                  pl.BlockSpec((DEC_SEQ, LANE), lambda b, h, i: (kv_row0 + b, 2 * HA + h)),
                  cache, cache, full, full, full, qtab, qtab, qtab,
                  pl.BlockSpec((4, DIFF_HD), lambda b, h, i: (0, 0)),
                  pl.BlockSpec((1, DIFF_VD), lambda b, h, i: (0, 0))],
        out_specs=pl.BlockSpec((tq, LANE), lambda b, h, i: (b * nq + i, h)),
        out_shape=jax.ShapeDtypeStruct((TS, HA * LANE), BF16),
        scratch_shapes=[pltpu.VMEM((PAST_LEN + DEC_SEQ, LANE), BF16),
                        pltpu.VMEM((PAST_LEN + DEC_SEQ, LANE), BF16)],
        compiler_params=_params(("arbitrary", "arbitrary", "arbitrary")),
        name="diff_attention_latent",
    )(proj, proj, proj, cache_k, cache_v, cos, sin_lo, sin_hi, cos, sin_lo, sin_hi,
      lam_vec, subln_g.reshape(1, DIFF_VD))


def _split3(x):
    hi = x.astype(BF16)
    r = x - hi.astype(F32)
    mid = r.astype(BF16)
    lo = (r - mid.astype(F32)).astype(BF16)
    return hi, mid, lo


def _gla_kernel(q_ref, k_ref, v_ref, r_ref, g_ref, s0_ref, gg_ref, y_ref, sfin_ref, of_scr, ob_scr, st_scr,
                *, seq, pairs):
    c = GLA_CHUNK
    sb = GLA_SUPER * c
    n_super = seq // sb
    rows = lax.broadcasted_iota(jnp.int32, (sb, sb), 0)
    cols = lax.broadcasted_iota(jnp.int32, (sb, sb), 1)
    same_chunk = (rows // c) == (cols // c)
    keep = (same_chunk & (rows >= cols), same_chunk & (cols >= rows))
    tri = tuple(jnp.where(m, 1.0, 0.0).astype(BF16) for m in keep)
    lane = lax.broadcasted_iota(jnp.int32, (sb, LANE), 1)
    own = (lane < GLA_DK, lane >= GLA_DK)
    chunk_of_row = lax.broadcasted_iota(jnp.int32, (sb, LANE), 0) // c
    st_rows = lax.broadcasted_iota(jnp.int32, (2 * GLA_DV, LANE), 0)
    st_lane = lax.broadcasted_iota(jnp.int32, (2 * GLA_DV, LANE), 1)
    st_own = (st_rows < GLA_DV) == (st_lane < GLA_DK)
    zpad = jnp.zeros((GLA_DK, GLA_DV), F32)

    def expand(x):
        return jnp.concatenate([jnp.where(chunk_of_row == ci, x, 0.0) for ci in range(GLA_SUPER)],
                               axis=1).astype(BF16)

    for p in range(pairs):
        for d in range(2):
            s0 = [s0_ref[d, 2 * p + hh] for hh in range(2)]
            st_scr[p, d] = jnp.concatenate([jnp.concatenate([s0[0], zpad], axis=0).T,
                                            jnp.concatenate([zpad, s0[1]], axis=0).T], axis=0)

    def one_pair_direction(n, p, d):
        kl = slice(p * LANE, (p + 1) * LANE)
        vl = slice(2 * p * GLA_DV, 2 * (p + 1) * GLA_DV)
        r0 = (n if d == 0 else n_super - 1 - n) * sb
        rs = pl.ds(r0, sb) if n_super == 1 else pl.ds(pl.multiple_of(r0, sb), sb)
        hi, mid, lo = _split3(g_ref[d, rs, kl])
        b = _dot(tri[d], hi) + _dot(tri[d], mid) + _dot(tri[d], lo)
        ends = [b[ci * c + (c - 1 if d == 0 else 0):ci * c + (c if d == 0 else 1), :] for ci in range(GLA_SUPER)]
        total = jnp.concatenate([jnp.broadcast_to(e, (c, LANE)) for e in ends], axis=0)
        q_in = (q_ref[rs, kl] * (GLA_DK ** -0.5)) * jnp.exp(b)
        kk = k_ref[rs, kl]
        k_in = (kk * jnp.exp(-b)).astype(BF16)
        k_end = kk * jnp.exp(total - b)
        v = v_ref[rs, vl]
        v_bf = v.astype(BF16)
        kv_all = _dot(v.T.astype(BF16), expand(k_end))
        q2 = jnp.concatenate([jnp.where(own[0], q_in, 0.0), jnp.where(own[1], q_in, 0.0)], axis=0)
        a2 = _dot_nt(q2.astype(BF16), k_in)
        o = jnp.concatenate(
            [_dot(jnp.where(keep[d], a2[hh * sb:(hh + 1) * sb], 0.0).astype(BF16),
                  v_bf[:, hh * GLA_DV:(hh + 1) * GLA_DV]) for hh in range(2)], axis=1)
        st = st_scr[p, d]
        entering = [None] * GLA_SUPER
        for ci in (range(GLA_SUPER) if d == 0 else range(GLA_SUPER - 1, -1, -1)):
            entering[ci] = st
            st = st * jnp.exp(ends[ci]) + jnp.where(st_own, kv_all[:, ci * LANE:(ci + 1) * LANE], 0.0)
        st_scr[p, d] = st
        o += _dot_nt(expand(q_in), jnp.concatenate(entering, axis=1).astype(BF16))
        if d == 0:
            of_scr[rs, vl] = o
        else:
            ob_scr[rs, vl] = o

    def super_block(n, carry):
        for p in range(pairs):
            for d in range(2):
                one_pair_direction(n, p, d)
        return carry

    if n_super == 1:
        super_block(0, 0)
    else:
        lax.fori_loop(0, n_super, super_block, 0)

    for h in range(2 * pairs):
        vs = slice(h * GLA_DV, (h + 1) * GLA_DV)
        r = r_ref[:, vs]
        y = _rms(of_scr[:, vs] + ob_scr[:, vs], gg_ref[...]) * (r * jax.nn.sigmoid(r))
        y_ref[:, vs] = y.astype(y_ref.dtype)
        p, hh = divmod(h, 2)
        for d in range(2):
            sfin_ref[d, h] = st_scr[p, d, hh * GLA_DV:(hh + 1) * GLA_DV, :].T[hh * GLA_DK:(hh + 1) * GLA_DK, :]


def bidir_gla(proj, gates, s0, gla_g, n_batch, seq, row_blk0):
    shared_s0 = s0.shape[0] == 1
    pairs = HB // 2 if seq * HB * GLA_DV * 4 <= 1024 * 1024 else 1
    wqk = pairs * LANE
    wv = pairs * 2 * GLA_DV
    col_q = (2 * HA * 2 * DIFF_HD + HA * DIFF_VD) // wqk
    col_k = col_q + HB * GLA_DK // wqk
    col_v = (col_k * wqk + HB * GLA_DK) // wv
    col_r = col_v + HB * GLA_DV // wv
    n_tok = n_batch * seq
    return pl.pallas_call(
        functools.partial(_gla_kernel, seq=seq, pairs=pairs),
        grid=(n_batch, HB // 2 // pairs),
        in_specs=[pl.BlockSpec((seq, wqk), lambda b, p: (row_blk0 + b, col_q + p)),
                  pl.BlockSpec((seq, wqk), lambda b, p: (row_blk0 + b, col_k + p)),
                  pl.BlockSpec((seq, wv), lambda b, p: (row_blk0 + b, col_v + p)),
                  pl.BlockSpec((seq, wv), lambda b, p: (row_blk0 + b, col_r + p)),
                  pl.BlockSpec((2, seq, wqk), lambda b, p: (0, row_blk0 + b, p)),
                  pl.BlockSpec((None, 2, 2 * pairs, GLA_DK, GLA_DV),
                               lambda b, p: (0 if shared_s0 else b, 0, p, 0, 0)),
                  pl.BlockSpec((1, GLA_DV), lambda b, p: (0, 0))],
        out_specs=[pl.BlockSpec((seq, wv), lambda b, p: (b, p)),
                   pl.BlockSpec((None, 2, 2 * pairs, GLA_DK, GLA_DV), lambda b, p: (b, 0, p, 0, 0))],
        out_shape=[jax.ShapeDtypeStruct((n_tok, HB * GLA_DV), BF16),
                   jax.ShapeDtypeStruct((n_batch, 2, HB, GLA_DK, GLA_DV), F32)],
        scratch_shapes=[pltpu.VMEM((seq, wv), F32), pltpu.VMEM((seq, wv), F32),
                        pltpu.VMEM((pairs, 2, 2 * GLA_DV, LANE), F32)],
        compiler_params=_params(("arbitrary", "arbitrary")),
        name="bidir_gla",
    )(proj, proj, proj, proj, gates, s0, gla_g.reshape(1, GLA_DV))


def _head_slice(kh, gq):
    h = kh * Q_PER_KV + gq
    return slice(h * LANE, (h + 1) * LANE)


def _sink_column(sink_ref, kh, rows):
    return jnp.concatenate([jnp.full((rows, 1), sink_ref[kh * Q_PER_KV + gq], F32) for gq in range(Q_PER_KV)],
                           axis=0)


def _gqa_ctx_kernel(sink_ref, q_ref, k_ref, v_ref, o_ref, ko_ref, vo_ref):
    scale = HEAD_DIM ** -0.5
    ko_ref[...] = k_ref[...]
    vo_ref[...] = v_ref[...]
    for kh in range(KV_HEADS):
        ks = slice(kh * LANE, (kh + 1) * LANE)
        k_bf = k_ref[:, ks].astype(BF16)
        v1_bf = _with_ones(v_ref[:, ks].astype(BF16))
        for gq in range(Q_PER_KV):
            hs = _head_slice(kh, gq)
            sink = sink_ref[kh * Q_PER_KV + gq]
            s = _dot_nt((q_ref[:, hs] * scale).astype(BF16), k_bf)
            m = jnp.maximum(jnp.max(s, axis=-1, keepdims=True), sink)
            ov = _dot(jnp.exp(s - m).astype(BF16), v1_bf)
            o_ref[:, hs] = (ov[:, :LANE] * (1.0 / (ov[:, LANE:] + jnp.exp(sink - m)))).astype(o_ref.dtype)


def gqa_context(proj, sinks):
    wq = N_HEADS * HEAD_DIM
    wkv = KV_HEADS * HEAD_DIM

    def kv(c):
        return pl.BlockSpec((SEQ, wkv), lambda b: (b, c))

    return pl.pallas_call(
        _gqa_ctx_kernel,
        grid=(BATCH,),
        in_specs=[pl.BlockSpec(memory_space=pltpu.SMEM),
                  pl.BlockSpec((SEQ, wq), lambda b: (b, 0)), kv(wq // wkv), kv(wq // wkv + 1)],
        out_specs=[pl.BlockSpec((SEQ, wq), lambda b: (b, 0)), kv(0), kv(0)],
        out_shape=[jax.ShapeDtypeStruct((TP, wq), BF16), jax.ShapeDtypeStruct((TP, wkv), F32),
                   jax.ShapeDtypeStruct((TP, wkv), F32)],
        compiler_params=_params(("arbitrary",)),
        name="gqa_context",
    )(sinks, proj, proj, proj)


def _gqa_lat_kernel(sink_ref, q_ref, k_ref, v_ref, ck_ref, cv_ref, kc_ref, kl_ref, kh_ref,
                    qc_ref, ql_ref, qh_ref, o_ref, kw_scr, vw_scr, kc_scr, vc_scr):
    quarter = HEAD_DIM // 4
    n = pl.program_id(1)
    w = WINDOW
    wkv = KV_HEADS * HEAD_DIM

    @pl.when(n == 0)
    def _():
        kw_scr[0:w, :] = jnp.zeros((w, wkv), BF16)
        kw_scr[w + DEC_SEQ:, :] = jnp.zeros((w, wkv), BF16)
        vw_scr[0:w, :] = jnp.zeros((w, 2 * wkv), BF16)
        vw_scr[w + DEC_SEQ:, :] = jnp.zeros((w, 2 * wkv), BF16)
        for kh in range(KV_HEADS):
            ks = slice(kh * LANE, (kh + 1) * LANE)
            vs = slice(2 * kh * LANE, 2 * (kh + 1) * LANE)
            kw_scr[w:w + DEC_SEQ, ks] = _rope(k_ref[:, ks], kc_ref[...], kl_ref[...], kh_ref[...],
                                              quarter).astype(BF16)
            vw_scr[w:w + DEC_SEQ, vs] = _with_ones(v_ref[:, ks].astype(BF16))
            vc_scr[:, vs] = _with_ones(cv_ref[:, ks].astype(BF16))
        kc_scr[...] = ck_ref[...].astype(BF16)

    win = pl.ds(pl.multiple_of(n * w, w), 3 * w)
    qi = lax.broadcasted_iota(jnp.int32, (Q_PER_KV * w, 3 * w), 0) % w
    kj = lax.broadcasted_iota(jnp.int32, (Q_PER_KV * w, 3 * w), 1)
    kpos = n * w - w + kj
    valid = (kj >= qi) & (kj <= qi + 2 * w) & (kpos >= 0) & (kpos < DEC_SEQ)
    scale = HEAD_DIM ** -0.5
    for kh in range(KV_HEADS):
        ks = slice(kh * LANE, (kh + 1) * LANE)
        vs = slice(2 * kh * LANE, 2 * (kh + 1) * LANE)
        q = jnp.concatenate(
            [(_rope(q_ref[:, _head_slice(kh, gq)], qc_ref[...], ql_ref[...], qh_ref[...], quarter) * scale
              ).astype(BF16) for gq in range(Q_PER_KV)], axis=0)
        sink = _sink_column(sink_ref, kh, w)
        s_c = _dot_nt(q, kc_scr[:, ks])
        s_w = jnp.where(valid, _dot_nt(q, kw_scr[win, ks]), NEG_INF)
        m = jnp.maximum(jnp.maximum(jnp.max(s_c, axis=-1, keepdims=True),
                                    jnp.max(s_w, axis=-1, keepdims=True)), sink)
        ov = (_dot(jnp.exp(s_c - m).astype(BF16), vc_scr[:, vs])
              + _dot(jnp.exp(s_w - m).astype(BF16), vw_scr[win, vs]))
        o = ov[:, :LANE] * (1.0 / (ov[:, LANE:] + jnp.exp(sink - m)))
        for gq in range(Q_PER_KV):
            o_ref[:, _head_slice(kh, gq)] = o[gq * w:(gq + 1) * w].astype(o_ref.dtype)


def gqa_latent(proj, cache_k, cache_v, tables, sinks):
    wq = N_HEADS * HEAD_DIM
    wkv = KV_HEADS * HEAD_DIM
    nq = DEC_SEQ // WINDOW
    row0 = TP // WINDOW
    kv_row0 = TP // DEC_SEQ
    cos, sin_lo, sin_hi = tables
    full = pl.BlockSpec((DEC_SEQ, LANE), lambda b, i: (0, 0))
    qtab = pl.BlockSpec((WINDOW, LANE), lambda b, i: (i, 0))
    cache = pl.BlockSpec((None, PAST_LEN, wkv), lambda b, i: (b, 0, 0))
    return pl.pallas_call(
        _gqa_lat_kernel,
        grid=(DEC_BATCH, nq),
        in_specs=[pl.BlockSpec(memory_space=pltpu.SMEM),
                  pl.BlockSpec((WINDOW, wq), lambda b, i: (row0 + b * nq + i, 0)),
                  pl.BlockSpec((DEC_SEQ, wkv), lambda b, i: (kv_row0 + b, wq // wkv)),
                  pl.BlockSpec((DEC_SEQ, wkv), lambda b, i: (kv_row0 + b, wq // wkv + 1)),
                  cache, cache, full, full, full, qtab, qtab, qtab],
        out_specs=pl.BlockSpec((WINDOW, wq), lambda b, i: (b * nq + i, 0)),
        out_shape=jax.ShapeDtypeStruct((TS, wq), BF16),
        scratch_shapes=[pltpu.VMEM((DEC_SEQ + 2 * WINDOW, wkv), BF16),
                        pltpu.VMEM((DEC_SEQ + 2 * WINDOW, 2 * wkv), BF16),
                        pltpu.VMEM((PAST_LEN, wkv), BF16),
                        pltpu.VMEM((PAST_LEN, 2 * wkv), BF16)],
        compiler_params=_params(("arbitrary", "arbitrary")),
        name="gqa_latent",
    )(sinks, proj, proj, proj, cache_k, cache_v, cos, sin_lo, sin_hi, cos, sin_lo, sin_hi)


def _even_layer(xc, xs, mod, j, layer, norm_g, w_in_even, lam_vec, subln_g, w_gate1, w_gate2, b_gate, gla_norm_g,
                w_out_even, cache_a_k, cache_a_v, state_b, tables):
    lam_init = 0.8 - 0.6 * math.exp(-0.3 * layer)
    ng = HB * GLA_DK
    wg1 = jnp.zeros((D_MODEL, LANE), F32).at[:, :2 * GLA_GATE_RANK].set(
        jnp.concatenate([w_gate1[j, 0], w_gate1[j, 1]], axis=-1)).astype(BF16)
    wg2 = jnp.zeros((2, LANE, ng), F32)
    for e in range(2):
        wg2 = wg2.at[e, e * GLA_GATE_RANK:(e + 1) * GLA_GATE_RANK, :].set(w_gate2[j, e])
    h, gates = norm_mod_gate(xc, xs, norm_g[layer, 0], mod, 0, 1, wg1, wg2.astype(BF16),
                             b_gate[j].reshape(2, 1, ng))
    proj = ws_matmul(h, w_in_even, j)
    ya_c, new_k, new_v = diff_attention_context(proj, lam_vec, subln_g[j], lam_init)
    ya_s = diff_attention_latent(proj, cache_a_k[:, j].reshape(DEC_BATCH, PAST_LEN, HA * 2 * DIFF_HD),
                                 cache_a_v[:, j].reshape(DEC_BATCH, PAST_LEN, HA * DIFF_VD),
                                 tables, lam_vec, subln_g[j], lam_init)
    yb_c, s_fin = bidir_gla(proj, gates, jnp.zeros((1, 2, HB, GLA_DK, GLA_DV), F32), gla_norm_g[j],
                            BATCH, SEQ, 0)
    yb_s, _ = bidir_gla(proj, gates, state_b[:, j], gla_norm_g[j], DEC_BATCH, DEC_SEQ, TP // DEC_SEQ)
    x, stats = ws_matmul_residual([(ya_c, ya_s), (yb_c, yb_s)], w_out_even, j, (xc, xs), mod, 2)
    return (x, stats, new_k.reshape(BATCH, SEQ, HA, 2 * DIFF_HD), new_v.reshape(BATCH, SEQ, HA, DIFF_VD), s_fin)


def _odd_layer(x, stats, mod, mod_t, j, layer, norm_g, w_in_odd, sinks, w_out_odd, cache_c_k, cache_c_v, tables):
    wkv = KV_HEADS * HEAD_DIM
    proj = ws_matmul_folded(stats, norm_g[layer, 0], mod, mod_t, 0, 1, w_in_odd, j)
    o_c, new_k, new_v = gqa_context(proj, sinks[j])
    o_s = gqa_latent(proj, cache_c_k[:, j].reshape(DEC_BATCH, PAST_LEN, wkv),
                     cache_c_v[:, j].reshape(DEC_BATCH, PAST_LEN, wkv), tables, sinks[j])
    x, stats = ws_matmul_residual([(o_c, o_s)], w_out_odd, j, (x,), mod, 2)
    return (x, stats, new_k.reshape(BATCH, SEQ, KV_HEADS, HEAD_DIM),
            new_v.reshape(BATCH, SEQ, KV_HEADS, HEAD_DIM))


def _ffn(x, stats, mod, mod_t, layer, norm_g, w_ffn_in, w_ffn_out, with_stats):
    act = ws_swiglu_folded(stats, norm_g[layer, 1], mod, mod_t, 3, 4, w_ffn_in, layer)
    return ws_matmul_residual([(act,)], w_ffn_out, layer, (x,), mod, 5, tm=512, tn=512, with_stats=with_stats)


def kernel(x_prompt, x_sample, c, cache_a_k, cache_a_v, state_b, cache_c_k, cache_c_v, c_ctx, w_ada, b_ada,
           norm_g, w_in_even, lam_q1, lam_k1, lam_q2, lam_k2, subln_g, w_gate1, w_gate2, b_gate, gla_norm_g,
           w_out_even, w_in_odd, sinks, w_out_odd, w_ffn_in, w_ffn_out, final_norm_g):
    assert DEPTH == 2, "layer 0 reads the two input streams, layer 1 the fused token matrix"
    cvec = jnp.concatenate([c_ctx[None, :], c, jnp.zeros((MOD_ROWS - 1 - DEC_BATCH, D_MODEL), F32)], axis=0)
    mod_flat = adaln(cvec, w_ada, b_ada)
    mods = mod_flat.reshape(DEPTH, MOD_ROWS, 1, 6 * D_MODEL)
    mods_t = mod_flat.reshape(DEPTH, MOD_ROWS, 6, D_MODEL, 1)
    tab_d = rope_tables(DEC_SEQ, DIFF_HD)
    tab_h = rope_tables(DEC_SEQ, HEAD_DIM)

    lam_vec = jnp.stack([lam_q1[0], lam_k1[0], lam_q2[0], lam_k2[0]], axis=0)
    x, stats, ak, av, sb = _even_layer(x_prompt.reshape(TP, D_MODEL), x_sample.reshape(TS, D_MODEL), mods[0], 0, 0,
                                       norm_g, w_in_even, lam_vec, subln_g, w_gate1, w_gate2, b_gate, gla_norm_g,
                                       w_out_even, cache_a_k, cache_a_v, state_b, tab_d)
    x, stats = _ffn(x, stats, mods[0], mods_t[0], 0, norm_g, w_ffn_in, w_ffn_out, True)
    x, stats, ck, cv = _odd_layer(x, stats, mods[1], mods_t[1], 0, 1, norm_g, w_in_odd, sinks, w_out_odd,
                                  cache_c_k, cache_c_v, tab_h)
    x = _ffn(x, stats, mods[1], mods_t[1], 1, norm_g, w_ffn_in, w_ffn_out, False)

    y_prompt = final_norm(x, final_norm_g, 0, TP).reshape(BATCH, SEQ, D_MODEL)
    y_sample = final_norm(x, final_norm_g, TP, TS).reshape(DEC_BATCH, DEC_SEQ, D_MODEL)
    return (y_prompt, y_sample, ak[:, None], av[:, None], sb[:, None], ck[:, None], cv[:, None])
```

```python
import functools
import math

import jax
import jax.numpy as jnp
from jax import lax
from jax.experimental import pallas as pl
from jax.experimental.pallas import tpu as pltpu

D_MODEL = 2048
BATCH = 32
SEQ = 256
DEPTH = 2
DEC_BATCH = 2
DEC_SEQ = 2048
PAST_LEN = 512
GRID_W = 64
HEAD_DIM = 128
N_HEADS = D_MODEL // HEAD_DIM
HA = N_HEADS // 2
DIFF_HD = HEAD_DIM // 2
DIFF_VD = HEAD_DIM
HB = N_HEADS // 2
GLA_DK = HEAD_DIM // 2
GLA_DV = HEAD_DIM
GLA_GATE_RANK = 16
GLA_TAU = 16.0
GLA_CHUNK = 64
GLA_SUPER = 4
DIFF_LAT_CHAINS = 4
KV_HEADS = N_HEADS // 4
Q_PER_KV = N_HEADS // KV_HEADS
WINDOW = 128
D_FF = -(-8 * D_MODEL // (3 * 256)) * 256
ROPE_BASE = 10000.0
EPS = 1e-6
NEG_INF = -1e30
EVEN_IN = 2 * HA * 2 * DIFF_HD + HA * DIFF_VD + 2 * HB * GLA_DK + 2 * HB * GLA_DV
ODD_IN = (N_HEADS + 2 * KV_HEADS) * HEAD_DIM

TP = BATCH * SEQ
TS = DEC_BATCH * DEC_SEQ
T = TP + TS
LANE = 128
MOD_ROWS = 8
NORM_ROWS = 16
NORM_UNROLL = 8
VMEM_LIMIT = 56 * 1024 * 1024

F32 = jnp.float32
BF16 = jnp.bfloat16


def _params(semantics, vmem=VMEM_LIMIT):
    return pltpu.CompilerParams(dimension_semantics=semantics, vmem_limit_bytes=vmem)


def _dot(a, b):
    return jnp.dot(a, b, preferred_element_type=F32)


def _dot_nt(a, b):
    return lax.dot_general(a, b, (((1,), (1,)), ((), ())), preferred_element_type=F32)


def _group_of_tile(i, tm):
    r = i * tm
    return jnp.where(r < TP, 0, 1 + (r - TP) // DEC_SEQ)


def _rms(x, g):
    return (x * lax.rsqrt(jnp.mean(x * x, axis=-1, keepdims=True) + EPS)) * g


def _adaln_kernel(c_ref, w_ref, b_ref, o_ref):
    c = c_ref[...]
    s = c * jax.nn.sigmoid(c)
    o_ref[...] = _dot(s.astype(BF16), w_ref[...].astype(BF16)) + b_ref[...]


def adaln(cvec, w_ada, b_ada, tn=512):
    n = w_ada.shape[-1]
    return pl.pallas_call(
        _adaln_kernel,
        grid=(DEPTH, n // tn),
        in_specs=[
            pl.BlockSpec((MOD_ROWS, D_MODEL), lambda l, j: (0, 0)),
            pl.BlockSpec((None, D_MODEL, tn), lambda l, j: (l, 0, j)),
            pl.BlockSpec((None, 1, tn), lambda l, j: (l, 0, j)),
        ],
        out_specs=pl.BlockSpec((None, MOD_ROWS, tn), lambda l, j: (l, 0, j)),
        out_shape=jax.ShapeDtypeStruct((DEPTH, MOD_ROWS, n), F32),
        compiler_params=_params(("arbitrary", "arbitrary")),
        name="adaln",
    )(cvec, w_ada, b_ada.reshape(DEPTH, 1, n))


def _by_row_groups(x_ref, o_ref, fn):
    def body(r, carry):
        rs = pl.ds(pl.multiple_of(r * NORM_ROWS, NORM_ROWS), NORM_ROWS)
        o_ref[rs, :] = fn(x_ref[rs, :]).astype(o_ref.dtype)
        return carry

    lax.fori_loop(0, x_ref.shape[0] // NORM_ROWS, body, 0, unroll=NORM_UNROLL)


def _norm_mod_into(h_ref, x_ref, g_ref, shift_ref, scale_ref):
    _by_row_groups(x_ref, h_ref, lambda x: _rms(x, g_ref[...]) * (1 + scale_ref[...]) + shift_ref[...])


def _norm_mod_kernel(x_ref, g_ref, shift_ref, scale_ref, h_ref):
    _norm_mod_into(h_ref, x_ref, g_ref, shift_ref, scale_ref)


def _norm_mod_gate_kernel(xc_ref, xs_ref, g_ref, shift_ref, scale_ref, wg1_ref, wg2_ref, bg_ref,
                          h_ref, gate_ref, *, ctx_tiles):
    @pl.when(pl.program_id(0) < ctx_tiles)
    def _():
        _norm_mod_into(h_ref, xc_ref, g_ref, shift_ref, scale_ref)

    @pl.when(pl.program_id(0) >= ctx_tiles)
    def _():
        _norm_mod_into(h_ref, xs_ref, g_ref, shift_ref, scale_ref)

    low = _dot(h_ref[...], wg1_ref[...]).astype(BF16)
    for e in range(2):
        logit = _dot(low, wg2_ref[e]) + bg_ref[e]
        log_sig = jnp.minimum(logit, 0.0) - jnp.log(1.0 + jnp.exp(-jnp.abs(logit)))
        gate_ref[e] = log_sig / GLA_TAU


def _mod_specs(tm, shift_idx, scale_idx):
    def spec(idx):
        return pl.BlockSpec((None, 1, D_MODEL), lambda i: (_group_of_tile(i, tm), 0, idx))
    return [pl.BlockSpec((1, D_MODEL), lambda i: (0, 0)), spec(shift_idx), spec(scale_idx)]


def norm_mod(x, g, mod, shift_idx, scale_idx, tm=1024):
    return pl.pallas_call(
        _norm_mod_kernel,
        grid=(T // tm,),
        in_specs=[pl.BlockSpec((tm, D_MODEL), lambda i: (i, 0))] + _mod_specs(tm, shift_idx, scale_idx),
        out_specs=pl.BlockSpec((tm, D_MODEL), lambda i: (i, 0)),
        out_shape=jax.ShapeDtypeStruct((T, D_MODEL), BF16),
        compiler_params=_params(("arbitrary",)),
        name="norm_mod",
    )(x, g.reshape(1, D_MODEL), mod, mod)


def norm_mod_gate(xc, xs, g, mod, shift_idx, scale_idx, wg1, wg2, bg, tm=512):
    ng = HB * GLA_DK
    ctx_tiles = TP // tm
    return pl.pallas_call(
        functools.partial(_norm_mod_gate_kernel, ctx_tiles=ctx_tiles),
        grid=(T // tm,),
        in_specs=[pl.BlockSpec((tm, D_MODEL), lambda i: (jnp.minimum(i, ctx_tiles - 1), 0)),
                  pl.BlockSpec((tm, D_MODEL), lambda i: (jnp.maximum(i - ctx_tiles, 0), 0))]
        + _mod_specs(tm, shift_idx, scale_idx)
        + [pl.BlockSpec((D_MODEL, LANE), lambda i: (0, 0)),
           pl.BlockSpec((2, LANE, ng), lambda i: (0, 0, 0)),
           pl.BlockSpec((2, 1, ng), lambda i: (0, 0, 0))],
        out_specs=[pl.BlockSpec((tm, D_MODEL), lambda i: (i, 0)),
                   pl.BlockSpec((2, tm, ng), lambda i: (0, i, 0))],
        out_shape=[jax.ShapeDtypeStruct((T, D_MODEL), BF16), jax.ShapeDtypeStruct((2, T, ng), F32)],
        compiler_params=_params(("arbitrary",)),
        name="norm_mod_gate",
    )(xc, xs, g.reshape(1, D_MODEL), mod, mod, wg1, wg2, bg)


def _cast_weights(w_refs, w_scrs):
    @pl.when(pl.program_id(1) == 0)
    def _():
        for w_ref, w_scr in zip(w_refs, w_scrs):
            w_scr[...] = w_ref[...].astype(BF16)


def _ws_plain_kernel(h_ref, w_ref, o_ref, w_scr):
    _cast_weights([w_ref], [w_scr])
    o_ref[...] = _dot(h_ref[...], w_scr[...])


def _ws_swiglu_kernel(h_ref, wg_ref, wu_ref, o_ref, wg_scr, wu_scr):
    _cast_weights([wg_ref, wu_ref], [wg_scr, wu_scr])
    h = h_ref[...]
    gate = _dot(h, wg_scr[...])
    up = _dot(h, wu_scr[...])
    o_ref[...] = (gate * jax.nn.sigmoid(gate) * up).astype(o_ref.dtype)


def _ws_res_kernel(*refs, lhs_arity, x_arity, ctx_tiles):
    n_l = sum(lhs_arity)
    w_ref = refs[n_l]
    x_refs = refs[n_l + 1:n_l + 1 + x_arity]
    gate_ref, o_ref, w_scr = refs[n_l + 1 + x_arity:]
    groups, at = [], 0
    for a in lhs_arity:
        groups.append(refs[at:at + a])
        at += a
    _cast_weights([w_ref], [w_scr])

    def emit(side):
        acc, k0 = None, 0
        for grp in groups:
            ref = grp[side] if len(grp) == 2 else grp[0]
            part = _dot(ref[...], w_scr[k0:k0 + ref.shape[1], :])
            acc = part if acc is None else acc + part
            k0 += ref.shape[1]
        x_ref = x_refs[side] if x_arity == 2 else x_refs[0]
        o_ref[...] = x_ref[...] + gate_ref[...] * acc

    if x_arity == 2 or 2 in lhs_arity:
        @pl.when(pl.program_id(1) < ctx_tiles)
        def _():
            emit(0)

        @pl.when(pl.program_id(1) >= ctx_tiles)
        def _():
            emit(1)
    else:
        emit(0)


def _rows_specs(arrays, tm, width, col):
    if len(arrays) == 1:
        return [pl.BlockSpec((tm, width), lambda j, i: (i, col(j)))]
    ctx_tiles = TP // tm
    return [pl.BlockSpec((tm, width), lambda j, i: (jnp.minimum(i, ctx_tiles - 1), col(j))),
            pl.BlockSpec((tm, width), lambda j, i: (jnp.maximum(i - ctx_tiles, 0), col(j)))]


def ws_matmul(h, w_stack, layer, tm=1024, tn=1024):
    kdim, n = w_stack.shape[1:]
    return pl.pallas_call(
        _ws_plain_kernel,
        grid=(n // tn, T // tm),
        in_specs=[pl.BlockSpec((tm, kdim), lambda j, i: (i, 0)),
                  pl.BlockSpec((None, kdim, tn), lambda j, i: (layer, 0, j))],
        out_specs=pl.BlockSpec((tm, tn), lambda j, i: (i, j)),
        out_shape=jax.ShapeDtypeStruct((T, n), F32),
        scratch_shapes=[pltpu.VMEM((kdim, tn), BF16)],
        compiler_params=_params(("arbitrary", "arbitrary")),
        name="ws_matmul",
    )(h, w_stack)


def ws_swiglu(h, w_stack, layer, tm=1024, tn=512):
    kdim = w_stack.shape[1]
    nf = D_FF // tn
    return pl.pallas_call(
        _ws_swiglu_kernel,
        grid=(nf, T // tm),
        in_specs=[pl.BlockSpec((tm, kdim), lambda j, i: (i, 0)),
                  pl.BlockSpec((None, kdim, tn), lambda j, i: (layer, 0, j)),
                  pl.BlockSpec((None, kdim, tn), lambda j, i: (layer, 0, j + nf))],
        out_specs=pl.BlockSpec((tm, tn), lambda j, i: (i, j)),
        out_shape=jax.ShapeDtypeStruct((T, D_FF), BF16),
        scratch_shapes=[pltpu.VMEM((kdim, tn), BF16), pltpu.VMEM((kdim, tn), BF16)],
        compiler_params=_params(("arbitrary", "arbitrary")),
        name="ws_swiglu",
    )(h, w_stack, w_stack)


def ws_matmul_residual(lhs_groups, w_stack, layer, x, mod, gate_idx, tm=512, tn=1024):
    kdim, n = w_stack.shape[1:]
    gate_blk = n // tn
    specs, args = [], []
    for grp in lhs_groups:
        specs += _rows_specs(grp, tm, grp[0].shape[1], lambda j: 0)
        args += list(grp)
    specs.append(pl.BlockSpec((None, kdim, tn), lambda j, i: (layer, 0, j)))
    specs += _rows_specs(x, tm, tn, lambda j: j)
    specs.append(pl.BlockSpec((None, 1, tn), lambda j, i: (_group_of_tile(i, tm), 0, gate_idx * gate_blk + j)))
    return pl.pallas_call(
        functools.partial(_ws_res_kernel, lhs_arity=tuple(len(grp) for grp in lhs_groups), x_arity=len(x),
                          ctx_tiles=TP // tm),
        grid=(n // tn, T // tm),
        in_specs=specs,
        out_specs=pl.BlockSpec((tm, tn), lambda j, i: (i, j)),
        out_shape=jax.ShapeDtypeStruct((T, n), F32),
        scratch_shapes=[pltpu.VMEM((kdim, tn), BF16)],
        compiler_params=_params(("arbitrary", "arbitrary")),
        name="ws_matmul_residual",
    )(*args, w_stack, *x, mod)


def _final_norm_kernel(x_ref, g_ref, o_ref):
    o_ref[...] = _rms(x_ref[...], g_ref[...])


def final_norm(x, g, row0, n_rows, tm=1024):
    blk0 = row0 // tm
    return pl.pallas_call(
        _final_norm_kernel,
        grid=(n_rows // tm,),
        in_specs=[pl.BlockSpec((tm, D_MODEL), lambda i: (blk0 + i, 0)),
                  pl.BlockSpec((1, D_MODEL), lambda i: (0, 0))],
        out_specs=pl.BlockSpec((tm, D_MODEL), lambda i: (i, 0)),
        out_shape=jax.ShapeDtypeStruct((n_rows, D_MODEL), F32),
        compiler_params=_params(("arbitrary",)),
        name="final_norm",
    )(x, g.reshape(1, D_MODEL))


def _grid_angles(n_tok, rot_dim):
    t = jnp.arange((n_tok // GRID_W) * GRID_W)
    row = (t // GRID_W).astype(F32)
    col = (t % GRID_W).astype(F32)
    half = rot_dim // 2
    inv = ROPE_BASE ** (-jnp.arange(0, half, 2, dtype=F32) / half)
    return row[:, None] * inv[None], col[:, None] * inv[None]


def rope_tables(n_tok, rot_dim):
    ang_row, ang_col = _grid_angles(n_tok, rot_dim)
    zeros = jnp.zeros_like(ang_row)
    reps = LANE // rot_dim

    def lanes(r1, r2, c1, c2):
        return jnp.tile(jnp.concatenate([r1, r2, c1, c2], axis=-1), (1, reps))

    cr, sr, cc, sc = jnp.cos(ang_row), jnp.sin(ang_row), jnp.cos(ang_col), jnp.sin(ang_col)
    return lanes(cr, cr, cc, cc), lanes(-sr, zeros, -sc, zeros), lanes(zeros, sr, zeros, sc)


def _rope(x, cos, sin_lo, sin_hi, quarter):
    return (x * cos + pltpu.roll(x, LANE - quarter, 1) * sin_lo + pltpu.roll(x, quarter, 1) * sin_hi)


def _diff_lambda(lam_ref, lam_init):
    e1 = jnp.exp(jnp.sum(lam_ref[0:1, :] * lam_ref[1:2, :], axis=-1, keepdims=True))
    e2 = jnp.exp(jnp.sum(lam_ref[2:3, :] * lam_ref[3:4, :], axis=-1, keepdims=True))
    return e1 - e2 + lam_init


def _with_ones(v_bf):
    return jnp.concatenate([v_bf, jnp.ones(v_bf.shape, BF16)], axis=1)


def _softmax_av(q_bf, k_bf, v1_bf):
    s = _dot_nt(q_bf, k_bf)
    e = jnp.exp(s - jnp.max(s, axis=-1, keepdims=True))
    if v1_bf.shape[1] == LANE:
        return _dot(e.astype(BF16), v1_bf) * (1.0 / jnp.sum(e, axis=-1, keepdims=True))
    ov = _dot(e.astype(BF16), v1_bf)
    return ov[:, :LANE] * (1.0 / ov[:, LANE:])


def _diff_head(q, k_bf, v1_bf, lam, subln_g, lam_init, stack_maps):
    n = q.shape[0]
    first = lax.broadcasted_iota(jnp.int32, q.shape, 1) < DIFF_HD
    q0 = jnp.where(first, q, 0.0).astype(BF16)
    q1 = jnp.where(first, 0.0, q).astype(BF16)
    if stack_maps:
        r = _softmax_av(jnp.concatenate([q0, q1], axis=0), k_bf, v1_bf)
        o = r[:n] - lam * r[n:]
    else:
        o = _softmax_av(q0, k_bf, v1_bf) - lam * _softmax_av(q1, k_bf, v1_bf)
    return _rms(o, subln_g) * (1.0 - lam_init)


def _diff_ctx_kernel(q_ref, k_ref, v_ref, lam_ref, sg_ref, o_ref, ko_ref, vo_ref, *, lam_init):
    lam = _diff_lambda(lam_ref, lam_init)
    scale = DIFF_HD ** -0.5
    ko_ref[...] = k_ref[...]
    vo_ref[...] = v_ref[...]
    for h in range(HA):
        sl = slice(h * LANE, (h + 1) * LANE)
        y = _diff_head(q_ref[:, sl] * scale, k_ref[:, sl].astype(BF16), _with_ones(v_ref[:, sl].astype(BF16)),
                       lam, sg_ref[...], lam_init, stack_maps=True)
        o_ref[:, sl] = y.astype(o_ref.dtype)


def diff_attention_context(proj, lam_vec, subln_g, lam_init):
    w = HA * LANE

    def blk(c):
        return pl.BlockSpec((SEQ, w), lambda b: (b, c))

    return pl.pallas_call(
        functools.partial(_diff_ctx_kernel, lam_init=lam_init),
        grid=(BATCH,),
        in_specs=[blk(0), blk(1), blk(2),
                  pl.BlockSpec((4, DIFF_HD), lambda b: (0, 0)),
                  pl.BlockSpec((1, DIFF_VD), lambda b: (0, 0))],
        out_specs=[blk(0), blk(0), blk(0)],
        out_shape=[jax.ShapeDtypeStruct((TP, w), BF16), jax.ShapeDtypeStruct((TP, w), F32),
                   jax.ShapeDtypeStruct((TP, w), F32)],
        compiler_params=_params(("arbitrary",)),
        name="diff_attention_context",
    )(proj, proj, proj, lam_vec, subln_g.reshape(1, DIFF_VD))


def _diff_lat_kernel(q_ref, k_ref, v_ref, ck_ref, cv_ref, kc_ref, kl_ref, kh_ref, qc_ref, ql_ref, qh_ref,
                     lam_ref, sg_ref, o_ref, k_scr, v_scr, *, lam_init):
    quarter = DIFF_HD // 4

    @pl.when(pl.program_id(2) == 0)
    def _():
        k_scr[0:PAST_LEN, :] = ck_ref[...].astype(BF16)
        k_scr[PAST_LEN:, :] = _rope(k_ref[...], kc_ref[...], kl_ref[...], kh_ref[...], quarter).astype(BF16)
        v_scr[0:PAST_LEN, :] = cv_ref[...].astype(BF16)
        v_scr[PAST_LEN:, :] = v_ref[...].astype(BF16)

    lam = _diff_lambda(lam_ref, lam_init)
    sub = q_ref.shape[0] // DIFF_LAT_CHAINS
    for c in range(DIFF_LAT_CHAINS):
        rs = slice(c * sub, (c + 1) * sub)
        q = _rope(q_ref[rs, :], qc_ref[rs, :], ql_ref[rs, :], qh_ref[rs, :], quarter) * (DIFF_HD ** -0.5)
        y = _diff_head(q, k_scr[...], v_scr[...], lam, sg_ref[...], lam_init, stack_maps=False)
        o_ref[rs, :] = y.astype(o_ref.dtype)


def diff_attention_latent(proj, cache_k, cache_v, tables, lam_vec, subln_g, lam_init, tq=1024):
    nq = DEC_SEQ // tq
    row0 = TP // tq
    kv_row0 = TP // DEC_SEQ
    cos, sin_lo, sin_hi = tables
    full = pl.BlockSpec((DEC_SEQ, LANE), lambda b, h, i: (0, 0))
    qtab = pl.BlockSpec((tq, LANE), lambda b, h, i: (i, 0))
    cache = pl.BlockSpec((None, PAST_LEN, LANE), lambda b, h, i: (b, 0, h))
    return pl.pallas_call(
        functools.partial(_diff_lat_kernel, lam_init=lam_init),
        grid=(DEC_BATCH, HA, nq),
        in_specs=[pl.BlockSpec((tq, LANE), lambda b, h, i: (row0 + b * nq + i, h)),
                  pl.BlockSpec((DEC_SEQ, LANE), lambda b, h, i: (kv_row0 + b, HA + h)),
                  pl.BlockSpec((DEC_SEQ, LANE), lambda b, h, i: (kv_row0 + b, 2 * HA + h)),
                  cache, cache, full, full, full, qtab, qtab, qtab,
                  pl.BlockSpec((4, DIFF_HD), lambda b, h, i: (0, 0)),
                  pl.BlockSpec((1, DIFF_VD), lambda b, h, i: (0, 0))],
        out_specs=pl.BlockSpec((tq, LANE), lambda b, h, i: (b * nq + i, h)),
        out_shape=jax.ShapeDtypeStruct((TS, HA * LANE), BF16),
        scratch_shapes=[pltpu.VMEM((PAST_LEN + DEC_SEQ, LANE), BF16),
                        pltpu.VMEM((PAST_LEN + DEC_SEQ, LANE), BF16)],
        compiler_params=_params(("arbitrary", "arbitrary", "arbitrary")),
        name="diff_attention_latent",
    )(proj, proj, proj, cache_k, cache_v, cos, sin_lo, sin_hi, cos, sin_lo, sin_hi,
      lam_vec, subln_g.reshape(1, DIFF_VD))


def _split3(x):
    hi = x.astype(BF16)
    r = x - hi.astype(F32)
    mid = r.astype(BF16)
    lo = (r - mid.astype(F32)).astype(BF16)
    return hi, mid, lo


def _gla_kernel(q_ref, k_ref, v_ref, r_ref, g_ref, s0_ref, gg_ref, y_ref, sfin_ref, of_scr, ob_scr, st_scr,
                *, seq, pairs):
    c = GLA_CHUNK
    sb = GLA_SUPER * c
    n_super = seq // sb
    rows = lax.broadcasted_iota(jnp.int32, (sb, sb), 0)
    cols = lax.broadcasted_iota(jnp.int32, (sb, sb), 1)
    same_chunk = (rows // c) == (cols // c)
    keep = (same_chunk & (rows >= cols), same_chunk & (cols >= rows))
    tri = jnp.where(keep[0], 1.0, 0.0).astype(BF16)
    lane = lax.broadcasted_iota(jnp.int32, (sb, LANE), 1)
    own = (lane < GLA_DK, lane >= GLA_DK)
    chunk_of_row = lax.broadcasted_iota(jnp.int32, (sb, LANE), 0) // c
    st_rows = lax.broadcasted_iota(jnp.int32, (2 * GLA_DV, LANE), 0)
    st_lane = lax.broadcasted_iota(jnp.int32, (2 * GLA_DV, LANE), 1)
    st_own = (st_rows < GLA_DV) == (st_lane < GLA_DK)
    zpad = jnp.zeros((GLA_DK, GLA_DV), F32)

    def expand(x):
        return jnp.concatenate([jnp.where(chunk_of_row == ci, x, 0.0) for ci in range(GLA_SUPER)],
                               axis=1).astype(BF16)

    for p in range(pairs):
        for d in range(2):
            s0 = [s0_ref[d, 2 * p + hh] for hh in range(2)]
            st_scr[p, d] = jnp.concatenate([jnp.concatenate([s0[0], zpad], axis=0).T,
                                            jnp.concatenate([zpad, s0[1]], axis=0).T], axis=0)

    def rows_of(n, d):
        r0 = (n if d == 0 else n_super - 1 - n) * sb
        return pl.ds(r0, sb) if n_super == 1 else pl.ds(pl.multiple_of(r0, sb), sb)

    def chunk_totals(prefix):
        ends = [prefix[ci * c + c - 1:ci * c + c, :] for ci in range(GLA_SUPER)]
        return ends, jnp.concatenate([jnp.broadcast_to(e, (c, LANE)) for e in ends], axis=0)

    def log_decays(n, p):
        kl = slice(p * LANE, (p + 1) * LANE)
        g_b = g_ref[1, rows_of(n, 1), kl]
        hi, mid, lo = _split3(jnp.concatenate([g_ref[0, rows_of(n, 0), kl], g_b], axis=1))
        prefix = _dot(tri, hi) + _dot(tri, mid) + _dot(tri, lo)
        ends_f, total_f = chunk_totals(prefix[:, :LANE])
        ends_b, total_b = chunk_totals(prefix[:, LANE:])
        return (prefix[:, :LANE], ends_f, total_f), (total_b - prefix[:, LANE:] + g_b, ends_b, total_b)

    def one_pair_direction(n, p, d, decays):
        kl = slice(p * LANE, (p + 1) * LANE)
        vl = slice(2 * p * GLA_DV, 2 * (p + 1) * GLA_DV)
        rs = rows_of(n, d)
        b, ends, total = decays
        q_in = (q_ref[rs, kl] * (GLA_DK ** -0.5)) * jnp.exp(b)
        kk = k_ref[rs, kl]
        k_in = (kk * jnp.exp(-b)).astype(BF16)
        k_end = kk * jnp.exp(total - b)
        v = v_ref[rs, vl]
        v_bf = v.astype(BF16)
        kv_all = _dot(v.T.astype(BF16), expand(k_end))
        q2 = jnp.concatenate([jnp.where(own[0], q_in, 0.0), jnp.where(own[1], q_in, 0.0)], axis=0)
        a2 = _dot_nt(q2.astype(BF16), k_in)
        o = jnp.concatenate(
            [_dot(jnp.where(keep[d], a2[hh * sb:(hh + 1) * sb], 0.0).astype(BF16),
                  v_bf[:, hh * GLA_DV:(hh + 1) * GLA_DV]) for hh in range(2)], axis=1)
        st = st_scr[p, d]
        entering = [None] * GLA_SUPER
        for ci in (range(GLA_SUPER) if d == 0 else range(GLA_SUPER - 1, -1, -1)):
            entering[ci] = st
            st = st * jnp.exp(ends[ci]) + jnp.where(st_own, kv_all[:, ci * LANE:(ci + 1) * LANE], 0.0)
        st_scr[p, d] = st
        o += _dot_nt(expand(q_in), jnp.concatenate(entering, axis=1).astype(BF16))
        if d == 0:
            of_scr[rs, vl] = o
        else:
            ob_scr[rs, vl] = o

    def super_block(n, carry):
        for p in range(pairs):
            decays = log_decays(n, p)
            for d in range(2):
                one_pair_direction(n, p, d, decays[d])
        return carry

    if n_super == 1:
        super_block(0, 0)
    else:
        lax.fori_loop(0, n_super, super_block, 0)

    for h in range(2 * pairs):
        vs = slice(h * GLA_DV, (h + 1) * GLA_DV)
        r = r_ref[:, vs]
        y = _rms(of_scr[:, vs] + ob_scr[:, vs], gg_ref[...]) * (r * jax.nn.sigmoid(r))
        y_ref[:, vs] = y.astype(y_ref.dtype)
        p, hh = divmod(h, 2)
        for d in range(2):
            sfin_ref[d, h] = st_scr[p, d, hh * GLA_DV:(hh + 1) * GLA_DV, :].T[hh * GLA_DK:(hh + 1) * GLA_DK, :]


def bidir_gla(proj, gates, s0, gla_g, n_batch, seq, row_blk0):
    shared_s0 = s0.shape[0] == 1
    pairs = HB // 2 if seq * HB * GLA_DV * 4 <= 1024 * 1024 else 2
    wqk = pairs * LANE
    wv = pairs * 2 * GLA_DV
    col_q = (2 * HA * 2 * DIFF_HD + HA * DIFF_VD) // wqk
    col_k = col_q + HB * GLA_DK // wqk
    col_v = (col_k * wqk + HB * GLA_DK) // wv
    col_r = col_v + HB * GLA_DV // wv
    n_tok = n_batch * seq
    return pl.pallas_call(
        functools.partial(_gla_kernel, seq=seq, pairs=pairs),
        grid=(n_batch, HB // 2 // pairs),
        in_specs=[pl.BlockSpec((seq, wqk), lambda b, p: (row_blk0 + b, col_q + p)),
                  pl.BlockSpec((seq, wqk), lambda b, p: (row_blk0 + b, col_k + p)),
                  pl.BlockSpec((seq, wv), lambda b, p: (row_blk0 + b, col_v + p)),
                  pl.BlockSpec((seq, wv), lambda b, p: (row_blk0 + b, col_r + p)),
                  pl.BlockSpec((2, seq, wqk), lambda b, p: (0, row_blk0 + b, p)),
                  pl.BlockSpec((None, 2, 2 * pairs, GLA_DK, GLA_DV),
                               lambda b, p: (0 if shared_s0 else b, 0, p, 0, 0)),
                  pl.BlockSpec((1, GLA_DV), lambda b, p: (0, 0))],
        out_specs=[pl.BlockSpec((seq, wv), lambda b, p: (b, p)),
                   pl.BlockSpec((None, 2, 2 * pairs, GLA_DK, GLA_DV), lambda b, p: (b, 0, p, 0, 0))],
        out_shape=[jax.ShapeDtypeStruct((n_tok, HB * GLA_DV), BF16),
                   jax.ShapeDtypeStruct((n_batch, 2, HB, GLA_DK, GLA_DV), F32)],
        scratch_shapes=[pltpu.VMEM((seq, wv), F32), pltpu.VMEM((seq, wv), F32),
                        pltpu.VMEM((pairs, 2, 2 * GLA_DV, LANE), F32)],
        compiler_params=_params(("arbitrary", "arbitrary")),
        name="bidir_gla",
    )(proj, proj, proj, proj, gates, s0, gla_g.reshape(1, GLA_DV))


def _head_slice(kh, gq):
    h = kh * Q_PER_KV + gq
    return slice(h * LANE, (h + 1) * LANE)


def _sink_column(sink_ref, kh, rows):
    return jnp.concatenate([jnp.full((rows, 1), sink_ref[kh * Q_PER_KV + gq], F32) for gq in range(Q_PER_KV)],
                           axis=0)


def _gqa_ctx_kernel(sink_ref, q_ref, k_ref, v_ref, o_ref, ko_ref, vo_ref):
    scale = HEAD_DIM ** -0.5
    ko_ref[...] = k_ref[...]
    vo_ref[...] = v_ref[...]
    for kh in range(KV_HEADS):
        ks = slice(kh * LANE, (kh + 1) * LANE)
        k_bf = k_ref[:, ks].astype(BF16)
        v1_bf = _with_ones(v_ref[:, ks].astype(BF16))
        for gq in range(Q_PER_KV):
            hs = _head_slice(kh, gq)
            sink = sink_ref[kh * Q_PER_KV + gq]
            s = _dot_nt((q_ref[:, hs] * scale).astype(BF16), k_bf)
            m = jnp.maximum(jnp.max(s, axis=-1, keepdims=True), sink)
            ov = _dot(jnp.exp(s - m).astype(BF16), v1_bf)
            o_ref[:, hs] = (ov[:, :LANE] * (1.0 / (ov[:, LANE:] + jnp.exp(sink - m)))).astype(o_ref.dtype)


def gqa_context(proj, sinks):
    wq = N_HEADS * HEAD_DIM
    wkv = KV_HEADS * HEAD_DIM

    def kv(c):
        return pl.BlockSpec((SEQ, wkv), lambda b: (b, c))

    return pl.pallas_call(
        _gqa_ctx_kernel,
        grid=(BATCH,),
        in_specs=[pl.BlockSpec(memory_space=pltpu.SMEM),
                  pl.BlockSpec((SEQ, wq), lambda b: (b, 0)), kv(wq // wkv), kv(wq // wkv + 1)],
        out_specs=[pl.BlockSpec((SEQ, wq), lambda b: (b, 0)), kv(0), kv(0)],
        out_shape=[jax.ShapeDtypeStruct((TP, wq), BF16), jax.ShapeDtypeStruct((TP, wkv), F32),
                   jax.ShapeDtypeStruct((TP, wkv), F32)],
        compiler_params=_params(("arbitrary",)),
        name="gqa_context",
    )(sinks, proj, proj, proj)


def _gqa_lat_kernel(sink_ref, q_ref, k_ref, v_ref, ck_ref, cv_ref, kc_ref, kl_ref, kh_ref,
                    qc_ref, ql_ref, qh_ref, o_ref, kw_scr, vw_scr, kc_scr, vc_scr):
    quarter = HEAD_DIM // 4
    n = pl.program_id(1)
    w = WINDOW
    wkv = KV_HEADS * HEAD_DIM

    @pl.when(n == 0)
    def _():
        kw_scr[0:w, :] = jnp.zeros((w, wkv), BF16)
        kw_scr[w + DEC_SEQ:, :] = jnp.zeros((w, wkv), BF16)
        vw_scr[0:w, :] = jnp.zeros((w, 2 * wkv), BF16)
        vw_scr[w + DEC_SEQ:, :] = jnp.zeros((w, 2 * wkv), BF16)
        for kh in range(KV_HEADS):
            ks = slice(kh * LANE, (kh + 1) * LANE)
            vs = slice(2 * kh * LANE, 2 * (kh + 1) * LANE)
            kw_scr[w:w + DEC_SEQ, ks] = _rope(k_ref[:, ks], kc_ref[...], kl_ref[...], kh_ref[...],
                                              quarter).astype(BF16)
            vw_scr[w:w + DEC_SEQ, vs] = _with_ones(v_ref[:, ks].astype(BF16))
            vc_scr[:, vs] = _with_ones(cv_ref[:, ks].astype(BF16))
        kc_scr[...] = ck_ref[...].astype(BF16)

    win = pl.ds(pl.multiple_of(n * w, w), 3 * w)
    qi = lax.broadcasted_iota(jnp.int32, (Q_PER_KV * w, 3 * w), 0) % w
    kj = lax.broadcasted_iota(jnp.int32, (Q_PER_KV * w, 3 * w), 1)
    kpos = n * w - w + kj
    valid = (kj >= qi) & (kj <= qi + 2 * w) & (kpos >= 0) & (kpos < DEC_SEQ)
    scale = HEAD_DIM ** -0.5
    for kh in range(KV_HEADS):
        ks = slice(kh * LANE, (kh + 1) * LANE)
        vs = slice(2 * kh * LANE, 2 * (kh + 1) * LANE)
        q = jnp.concatenate(
            [(_rope(q_ref[:, _head_slice(kh, gq)], qc_ref[...], ql_ref[...], qh_ref[...], quarter) * scale
              ).astype(BF16) for gq in range(Q_PER_KV)], axis=0)
        sink = _sink_column(sink_ref, kh, w)
        s_c = _dot_nt(q, kc_scr[:, ks])
        s_w = jnp.where(valid, _dot_nt(q, kw_scr[win, ks]), NEG_INF)
        m = jnp.maximum(jnp.maximum(jnp.max(s_c, axis=-1, keepdims=True),
                                    jnp.max(s_w, axis=-1, keepdims=True)), sink)
        ov = (_dot(jnp.exp(s_c - m).astype(BF16), vc_scr[:, vs])
              + _dot(jnp.exp(s_w - m).astype(BF16), vw_scr[win, vs]))
        o = ov[:, :LANE] * (1.0 / (ov[:, LANE:] + jnp.exp(sink - m)))
        for gq in range(Q_PER_KV):
            o_ref[:, _head_slice(kh, gq)] = o[gq * w:(gq + 1) * w].astype(o_ref.dtype)


def gqa_latent(proj, cache_k, cache_v, tables, sinks):
    wq = N_HEADS * HEAD_DIM
    wkv = KV_HEADS * HEAD_DIM
    nq = DEC_SEQ // WINDOW
    row0 = TP // WINDOW
    kv_row0 = TP // DEC_SEQ
    cos, sin_lo, sin_hi = tables
    full = pl.BlockSpec((DEC_SEQ, LANE), lambda b, i: (0, 0))
    qtab = pl.BlockSpec((WINDOW, LANE), lambda b, i: (i, 0))
    cache = pl.BlockSpec((None, PAST_LEN, wkv), lambda b, i: (b, 0, 0))
    return pl.pallas_call(
        _gqa_lat_kernel,
        grid=(DEC_BATCH, nq),
        in_specs=[pl.BlockSpec(memory_space=pltpu.SMEM),
                  pl.BlockSpec((WINDOW, wq), lambda b, i: (row0 + b * nq + i, 0)),
                  pl.BlockSpec((DEC_SEQ, wkv), lambda b, i: (kv_row0 + b, wq // wkv)),
                  pl.BlockSpec((DEC_SEQ, wkv), lambda b, i: (kv_row0 + b, wq // wkv + 1)),
                  cache, cache, full, full, full, qtab, qtab, qtab],
        out_specs=pl.BlockSpec((WINDOW, wq), lambda b, i: (b * nq + i, 0)),
        out_shape=jax.ShapeDtypeStruct((TS, wq), BF16),
        scratch_shapes=[pltpu.VMEM((DEC_SEQ + 2 * WINDOW, wkv), BF16),
                        pltpu.VMEM((DEC_SEQ + 2 * WINDOW, 2 * wkv), BF16),
                        pltpu.VMEM((PAST_LEN, wkv), BF16),
                        pltpu.VMEM((PAST_LEN, 2 * wkv), BF16)],
        compiler_params=_params(("arbitrary", "arbitrary")),
        name="gqa_latent",
    )(sinks, proj, proj, proj, cache_k, cache_v, cos, sin_lo, sin_hi, cos, sin_lo, sin_hi)


def _even_layer(xc, xs, mod, j, layer, norm_g, w_in_even, lam_vec, subln_g, w_gate1, w_gate2, b_gate, gla_norm_g,
                w_out_even, cache_a_k, cache_a_v, state_b, tables):
    lam_init = 0.8 - 0.6 * math.exp(-0.3 * layer)
    ng = HB * GLA_DK
    wg1 = jnp.zeros((D_MODEL, LANE), F32).at[:, :2 * GLA_GATE_RANK].set(
        jnp.concatenate([w_gate1[j, 0], w_gate1[j, 1]], axis=-1)).astype(BF16)
    wg2 = jnp.zeros((2, LANE, ng), F32)
    for e in range(2):
        wg2 = wg2.at[e, e * GLA_GATE_RANK:(e + 1) * GLA_GATE_RANK, :].set(w_gate2[j, e])
    h, gates = norm_mod_gate(xc, xs, norm_g[layer, 0], mod, 0, 1, wg1, wg2.astype(BF16),
                             b_gate[j].reshape(2, 1, ng))
    proj = ws_matmul(h, w_in_even, j)
    ya_c, new_k, new_v = diff_attention_context(proj, lam_vec, subln_g[j], lam_init)
    ya_s = diff_attention_latent(proj, cache_a_k[:, j].reshape(DEC_BATCH, PAST_LEN, HA * 2 * DIFF_HD),
                                 cache_a_v[:, j].reshape(DEC_BATCH, PAST_LEN, HA * DIFF_VD),
                                 tables, lam_vec, subln_g[j], lam_init)
    yb_c, s_fin = bidir_gla(proj, gates, jnp.zeros((1, 2, HB, GLA_DK, GLA_DV), F32), gla_norm_g[j],
                            BATCH, SEQ, 0)
    yb_s, _ = bidir_gla(proj, gates, state_b[:, j], gla_norm_g[j], DEC_BATCH, DEC_SEQ, TP // DEC_SEQ)
    x = ws_matmul_residual([(ya_c, ya_s), (yb_c, yb_s)], w_out_even, j, (xc, xs), mod, 2)
    return (x, new_k.reshape(BATCH, SEQ, HA, 2 * DIFF_HD), new_v.reshape(BATCH, SEQ, HA, DIFF_VD), s_fin)


def _odd_layer(x, mod, j, layer, norm_g, w_in_odd, sinks, w_out_odd, cache_c_k, cache_c_v, tables):
    wkv = KV_HEADS * HEAD_DIM
    proj = ws_matmul(norm_mod(x, norm_g[layer, 0], mod, 0, 1), w_in_odd, j)
    o_c, new_k, new_v = gqa_context(proj, sinks[j])
    o_s = gqa_latent(proj, cache_c_k[:, j].reshape(DEC_BATCH, PAST_LEN, wkv),
                     cache_c_v[:, j].reshape(DEC_BATCH, PAST_LEN, wkv), tables, sinks[j])
    x = ws_matmul_residual([(o_c, o_s)], w_out_odd, j, (x,), mod, 2)
    return (x, new_k.reshape(BATCH, SEQ, KV_HEADS, HEAD_DIM), new_v.reshape(BATCH, SEQ, KV_HEADS, HEAD_DIM))


def _ffn(x, mod, layer, norm_g, w_ffn_in, w_ffn_out):
    act = ws_swiglu(norm_mod(x, norm_g[layer, 1], mod, 3, 4), w_ffn_in, layer)
    return ws_matmul_residual([(act,)], w_ffn_out, layer, (x,), mod, 5, tm=512, tn=512)


def kernel(x_prompt, x_sample, c, cache_a_k, cache_a_v, state_b, cache_c_k, cache_c_v, c_ctx, w_ada, b_ada,
           norm_g, w_in_even, lam_q1, lam_k1, lam_q2, lam_k2, subln_g, w_gate1, w_gate2, b_gate, gla_norm_g,
           w_out_even, w_in_odd, sinks, w_out_odd, w_ffn_in, w_ffn_out, final_norm_g):
    assert DEPTH == 2, "layer 0 reads the two input streams, layer 1 the fused token matrix"
    cvec = jnp.concatenate([c_ctx[None, :], c, jnp.zeros((MOD_ROWS - 1 - DEC_BATCH, D_MODEL), F32)], axis=0)
    mods = adaln(cvec, w_ada, b_ada).reshape(DEPTH, MOD_ROWS, 1, 6 * D_MODEL)
    tab_d = rope_tables(DEC_SEQ, DIFF_HD)
    tab_h = rope_tables(DEC_SEQ, HEAD_DIM)

    lam_vec = jnp.stack([lam_q1[0], lam_k1[0], lam_q2[0], lam_k2[0]], axis=0)
    x, ak, av, sb = _even_layer(x_prompt.reshape(TP, D_MODEL), x_sample.reshape(TS, D_MODEL), mods[0], 0, 0,
                                norm_g, w_in_even, lam_vec, subln_g, w_gate1, w_gate2, b_gate, gla_norm_g,
                                w_out_even, cache_a_k, cache_a_v, state_b, tab_d)
    x = _ffn(x, mods[0], 0, norm_g, w_ffn_in, w_ffn_out)
    x, ck, cv = _odd_layer(x, mods[1], 0, 1, norm_g, w_in_odd, sinks, w_out_odd, cache_c_k, cache_c_v, tab_h)
    x = _ffn(x, mods[1], 1, norm_g, w_ffn_in, w_ffn_out)

    y_prompt = final_norm(x, final_norm_g, 0, TP).reshape(BATCH, SEQ, D_MODEL)
    y_sample = final_norm(x, final_norm_g, TP, TS).reshape(DEC_BATCH, DEC_SEQ, D_MODEL)
    return (y_prompt, y_sample, ak[:, None], av[:, None], sb[:, None], ck[:, None], cv[:, None])
```

```python
import functools
import math

import jax
import jax.numpy as jnp
from jax import lax
from jax.experimental import pallas as pl
from jax.experimental.pallas import tpu as pltpu

D_MODEL = 2048
BATCH = 32
SEQ = 256
DEPTH = 2
DEC_BATCH = 2
DEC_SEQ = 2048
PAST_LEN = 512
GRID_W = 64
HEAD_DIM = 128
N_HEADS = D_MODEL // HEAD_DIM
HA = N_HEADS // 2
DIFF_HD = HEAD_DIM // 2
DIFF_VD = HEAD_DIM
HB = N_HEADS // 2
GLA_DK = HEAD_DIM // 2
GLA_DV = HEAD_DIM
GLA_GATE_RANK = 16
GLA_TAU = 16.0
GLA_CHUNK = 64
GLA_SUPER = 4
CTX_BATCHES_PER_STEP = 2
LAT_QBLOCKS_PER_STEP = 2
DIFF_LAT_CHAINS = 4
KV_HEADS = N_HEADS // 4
Q_PER_KV = N_HEADS // KV_HEADS
WINDOW = 128
D_FF = -(-8 * D_MODEL // (3 * 256)) * 256
ROPE_BASE = 10000.0
EPS = 1e-6
NEG_INF = -1e30
EVEN_IN = 2 * HA * 2 * DIFF_HD + HA * DIFF_VD + 2 * HB * GLA_DK + 2 * HB * GLA_DV
ODD_IN = (N_HEADS + 2 * KV_HEADS) * HEAD_DIM

TP = BATCH * SEQ
TS = DEC_BATCH * DEC_SEQ
T = TP + TS
LANE = 128
MOD_ROWS = 8
NORM_ROWS = 16
NORM_UNROLL = 8
VMEM_LIMIT = 56 * 1024 * 1024

F32 = jnp.float32
BF16 = jnp.bfloat16


def _params(semantics, vmem=VMEM_LIMIT):
    return pltpu.CompilerParams(dimension_semantics=semantics, vmem_limit_bytes=vmem)


def _dot(a, b):
    return jnp.dot(a, b, preferred_element_type=F32)


def _dot_nt(a, b):
    return lax.dot_general(a, b, (((1,), (1,)), ((), ())), preferred_element_type=F32)


def _group_of_tile(i, tm):
    r = i * tm
    return jnp.where(r < TP, 0, 1 + (r - TP) // DEC_SEQ)


def _rms(x, g):
    return (x * lax.rsqrt(jnp.mean(x * x, axis=-1, keepdims=True) + EPS)) * g


def _adaln_kernel(c_ref, w_ref, b_ref, o_ref):
    c = c_ref[...]
    s = c * jax.nn.sigmoid(c)
    o_ref[...] = _dot(s.astype(BF16), w_ref[...].astype(BF16)) + b_ref[...]


def adaln(cvec, w_ada, b_ada, tn=512):
    n = w_ada.shape[-1]
    return pl.pallas_call(
        _adaln_kernel,
        grid=(DEPTH, n // tn),
        in_specs=[
            pl.BlockSpec((MOD_ROWS, D_MODEL), lambda l, j: (0, 0)),
            pl.BlockSpec((None, D_MODEL, tn), lambda l, j: (l, 0, j)),
            pl.BlockSpec((None, 1, tn), lambda l, j: (l, 0, j)),
        ],
        out_specs=pl.BlockSpec((None, MOD_ROWS, tn), lambda l, j: (l, 0, j)),
        out_shape=jax.ShapeDtypeStruct((DEPTH, MOD_ROWS, n), F32),
        compiler_params=_params(("arbitrary", "arbitrary")),
        name="adaln",
    )(cvec, w_ada, b_ada.reshape(DEPTH, 1, n))


def _by_row_groups(x_ref, o_ref, fn):
    def body(r, carry):
        rs = pl.ds(pl.multiple_of(r * NORM_ROWS, NORM_ROWS), NORM_ROWS)
        o_ref[rs, :] = fn(x_ref[rs, :]).astype(o_ref.dtype)
        return carry

    lax.fori_loop(0, x_ref.shape[0] // NORM_ROWS, body, 0, unroll=NORM_UNROLL)


def _norm_mod_into(h_ref, x_ref, g_ref, shift_ref, scale_ref):
    _by_row_groups(x_ref, h_ref, lambda x: _rms(x, g_ref[...]) * (1 + scale_ref[...]) + shift_ref[...])


def _norm_mod_kernel(x_ref, g_ref, shift_ref, scale_ref, h_ref):
    _norm_mod_into(h_ref, x_ref, g_ref, shift_ref, scale_ref)


def _norm_mod_gate_kernel(xc_ref, xs_ref, g_ref, shift_ref, scale_ref, wg1_ref, wg2_ref, bg_ref,
                          h_ref, gate_ref, *, ctx_tiles):
    @pl.when(pl.program_id(0) < ctx_tiles)
    def _():
        _norm_mod_into(h_ref, xc_ref, g_ref, shift_ref, scale_ref)

    @pl.when(pl.program_id(0) >= ctx_tiles)
    def _():
        _norm_mod_into(h_ref, xs_ref, g_ref, shift_ref, scale_ref)

    low = _dot(h_ref[...], wg1_ref[...]).astype(BF16)
    for e in range(2):
        logit = _dot(low, wg2_ref[e]) + bg_ref[e]
        log_sig = jnp.minimum(logit, 0.0) - jnp.log(1.0 + jnp.exp(-jnp.abs(logit)))
        gate_ref[e] = log_sig / GLA_TAU


def _mod_specs(tm, shift_idx, scale_idx):
    def spec(idx):
        return pl.BlockSpec((None, 1, D_MODEL), lambda i: (_group_of_tile(i, tm), 0, idx))
    return [pl.BlockSpec((1, D_MODEL), lambda i: (0, 0)), spec(shift_idx), spec(scale_idx)]


def norm_mod(x, g, mod, shift_idx, scale_idx, tm=1024):
    return pl.pallas_call(
        _norm_mod_kernel,
        grid=(T // tm,),
        in_specs=[pl.BlockSpec((tm, D_MODEL), lambda i: (i, 0))] + _mod_specs(tm, shift_idx, scale_idx),
        out_specs=pl.BlockSpec((tm, D_MODEL), lambda i: (i, 0)),
        out_shape=jax.ShapeDtypeStruct((T, D_MODEL), BF16),
        compiler_params=_params(("arbitrary",)),
        name="norm_mod",
    )(x, g.reshape(1, D_MODEL), mod, mod)


def norm_mod_gate(xc, xs, g, mod, shift_idx, scale_idx, wg1, wg2, bg, tm=512):
    ng = HB * GLA_DK
    ctx_tiles = TP // tm
    return pl.pallas_call(
        functools.partial(_norm_mod_gate_kernel, ctx_tiles=ctx_tiles),
        grid=(T // tm,),
        in_specs=[pl.BlockSpec((tm, D_MODEL), lambda i: (jnp.minimum(i, ctx_tiles - 1), 0)),
                  pl.BlockSpec((tm, D_MODEL), lambda i: (jnp.maximum(i - ctx_tiles, 0), 0))]
        + _mod_specs(tm, shift_idx, scale_idx)
        + [pl.BlockSpec((D_MODEL, LANE), lambda i: (0, 0)),
           pl.BlockSpec((2, LANE, ng), lambda i: (0, 0, 0)),
           pl.BlockSpec((2, 1, ng), lambda i: (0, 0, 0))],
        out_specs=[pl.BlockSpec((tm, D_MODEL), lambda i: (i, 0)),
                   pl.BlockSpec((2, tm, ng), lambda i: (0, i, 0))],
        out_shape=[jax.ShapeDtypeStruct((T, D_MODEL), BF16), jax.ShapeDtypeStruct((2, T, ng), F32)],
        compiler_params=_params(("arbitrary",)),
        name="norm_mod_gate",
    )(xc, xs, g.reshape(1, D_MODEL), mod, mod, wg1, wg2, bg)


def _cast_weights(w_refs, w_scrs):
    @pl.when(pl.program_id(1) == 0)
    def _():
        for w_ref, w_scr in zip(w_refs, w_scrs):
            w_scr[...] = w_ref[...].astype(BF16)


def _ws_plain_kernel(h_ref, w_ref, o_ref, w_scr):
    _cast_weights([w_ref], [w_scr])
    o_ref[...] = _dot(h_ref[...], w_scr[...])


def _ws_swiglu_kernel(h_ref, wg_ref, wu_ref, o_ref, wg_scr, wu_scr):
    _cast_weights([wg_ref, wu_ref], [wg_scr, wu_scr])
    h = h_ref[...]
    gate = _dot(h, wg_scr[...])
    up = _dot(h, wu_scr[...])
    o_ref[...] = (gate * jax.nn.sigmoid(gate) * up).astype(o_ref.dtype)


def _ws_res_kernel(*refs, lhs_arity, x_arity, ctx_tiles):
    n_l = sum(lhs_arity)
    w_ref = refs[n_l]
    x_refs = refs[n_l + 1:n_l + 1 + x_arity]
    gate_ref, o_ref, w_scr = refs[n_l + 1 + x_arity:]
    groups, at = [], 0
    for a in lhs_arity:
        groups.append(refs[at:at + a])
        at += a
    _cast_weights([w_ref], [w_scr])

    def emit(side):
        acc, k0 = None, 0
        for grp in groups:
            ref = grp[side] if len(grp) == 2 else grp[0]
            part = _dot(ref[...], w_scr[k0:k0 + ref.shape[1], :])
            acc = part if acc is None else acc + part
            k0 += ref.shape[1]
        x_ref = x_refs[side] if x_arity == 2 else x_refs[0]
        o_ref[...] = x_ref[...] + gate_ref[...] * acc

    if x_arity == 2 or 2 in lhs_arity:
        @pl.when(pl.program_id(1) < ctx_tiles)
        def _():
            emit(0)

        @pl.when(pl.program_id(1) >= ctx_tiles)
        def _():
            emit(1)
    else:
        emit(0)


def _rows_specs(arrays, tm, width, col):
    if len(arrays) == 1:
        return [pl.BlockSpec((tm, width), lambda j, i: (i, col(j)))]
    ctx_tiles = TP // tm
    return [pl.BlockSpec((tm, width), lambda j, i: (jnp.minimum(i, ctx_tiles - 1), col(j))),
            pl.BlockSpec((tm, width), lambda j, i: (jnp.maximum(i - ctx_tiles, 0), col(j)))]


def ws_matmul(h, w_stack, layer, tm=1024, tn=1024):
    kdim, n = w_stack.shape[1:]
    return pl.pallas_call(
        _ws_plain_kernel,
        grid=(n // tn, T // tm),
        in_specs=[pl.BlockSpec((tm, kdim), lambda j, i: (i, 0)),
                  pl.BlockSpec((None, kdim, tn), lambda j, i: (layer, 0, j))],
        out_specs=pl.BlockSpec((tm, tn), lambda j, i: (i, j)),
        out_shape=jax.ShapeDtypeStruct((T, n), F32),
        scratch_shapes=[pltpu.VMEM((kdim, tn), BF16)],
        compiler_params=_params(("arbitrary", "arbitrary")),
        name="ws_matmul",
    )(h, w_stack)


def ws_swiglu(h, w_stack, layer, tm=1024, tn=512):
    kdim = w_stack.shape[1]
    nf = D_FF // tn
    return pl.pallas_call(
        _ws_swiglu_kernel,
        grid=(nf, T // tm),
        in_specs=[pl.BlockSpec((tm, kdim), lambda j, i: (i, 0)),
                  pl.BlockSpec((None, kdim, tn), lambda j, i: (layer, 0, j)),
                  pl.BlockSpec((None, kdim, tn), lambda j, i: (layer, 0, j + nf))],
        out_specs=pl.BlockSpec((tm, tn), lambda j, i: (i, j)),
        out_shape=jax.ShapeDtypeStruct((T, D_FF), BF16),
        scratch_shapes=[pltpu.VMEM((kdim, tn), BF16), pltpu.VMEM((kdim, tn), BF16)],
        compiler_params=_params(("arbitrary", "arbitrary")),
        name="ws_swiglu",
    )(h, w_stack, w_stack)


def ws_matmul_residual(lhs_groups, w_stack, layer, x, mod, gate_idx, tm=512, tn=1024):
    kdim, n = w_stack.shape[1:]
    gate_blk = n // tn
    specs, args = [], []
    for grp in lhs_groups:
        specs += _rows_specs(grp, tm, grp[0].shape[1], lambda j: 0)
        args += list(grp)
    specs.append(pl.BlockSpec((None, kdim, tn), lambda j, i: (layer, 0, j)))
    specs += _rows_specs(x, tm, tn, lambda j: j)
    specs.append(pl.BlockSpec((None, 1, tn), lambda j, i: (_group_of_tile(i, tm), 0, gate_idx * gate_blk + j)))
    return pl.pallas_call(
        functools.partial(_ws_res_kernel, lhs_arity=tuple(len(grp) for grp in lhs_groups), x_arity=len(x),
                          ctx_tiles=TP // tm),
        grid=(n // tn, T // tm),
        in_specs=specs,
        out_specs=pl.BlockSpec((tm, tn), lambda j, i: (i, j)),
        out_shape=jax.ShapeDtypeStruct((T, n), F32),
        scratch_shapes=[pltpu.VMEM((kdim, tn), BF16)],
        compiler_params=_params(("arbitrary", "arbitrary")),
        name="ws_matmul_residual",
    )(*args, w_stack, *x, mod)


def _final_norm_kernel(x_ref, g_ref, o_ref):
    o_ref[...] = _rms(x_ref[...], g_ref[...])


def final_norm(x, g, row0, n_rows, tm=1024):
    blk0 = row0 // tm
    return pl.pallas_call(
        _final_norm_kernel,
        grid=(n_rows // tm,),
        in_specs=[pl.BlockSpec((tm, D_MODEL), lambda i: (blk0 + i, 0)),
                  pl.BlockSpec((1, D_MODEL), lambda i: (0, 0))],
        out_specs=pl.BlockSpec((tm, D_MODEL), lambda i: (i, 0)),
        out_shape=jax.ShapeDtypeStruct((n_rows, D_MODEL), F32),
        compiler_params=_params(("arbitrary",)),
        name="final_norm",
    )(x, g.reshape(1, D_MODEL))


def _grid_angles(n_tok, rot_dim):
    t = jnp.arange((n_tok // GRID_W) * GRID_W)
    row = (t // GRID_W).astype(F32)
    col = (t % GRID_W).astype(F32)
    half = rot_dim // 2
    inv = ROPE_BASE ** (-jnp.arange(0, half, 2, dtype=F32) / half)
    return row[:, None] * inv[None], col[:, None] * inv[None]


def rope_tables(n_tok, rot_dim):
    ang_row, ang_col = _grid_angles(n_tok, rot_dim)
    zeros = jnp.zeros_like(ang_row)
    reps = LANE // rot_dim

    def lanes(r1, r2, c1, c2):
        return jnp.tile(jnp.concatenate([r1, r2, c1, c2], axis=-1), (1, reps))

    cr, sr, cc, sc = jnp.cos(ang_row), jnp.sin(ang_row), jnp.cos(ang_col), jnp.sin(ang_col)
    return lanes(cr, cr, cc, cc), lanes(-sr, zeros, -sc, zeros), lanes(zeros, sr, zeros, sc)


def _rope(x, cos, sin_lo, sin_hi, quarter):
    return (x * cos + pltpu.roll(x, LANE - quarter, 1) * sin_lo + pltpu.roll(x, quarter, 1) * sin_hi)


def _diff_lambda(lam_ref, lam_init):
    e1 = jnp.exp(jnp.sum(lam_ref[0:1, :] * lam_ref[1:2, :], axis=-1, keepdims=True))
    e2 = jnp.exp(jnp.sum(lam_ref[2:3, :] * lam_ref[3:4, :], axis=-1, keepdims=True))
    return e1 - e2 + lam_init


def _with_ones(v_bf):
    return jnp.concatenate([v_bf, jnp.ones(v_bf.shape, BF16)], axis=1)


def _softmax_av(q_bf, k_bf, v1_bf):
    s = _dot_nt(q_bf, k_bf)
    e = jnp.exp(s - jnp.max(s, axis=-1, keepdims=True))
    if v1_bf.shape[1] == LANE:
        return _dot(e.astype(BF16), v1_bf) * (1.0 / jnp.sum(e, axis=-1, keepdims=True))
    ov = _dot(e.astype(BF16), v1_bf)
    return ov[:, :LANE] * (1.0 / ov[:, LANE:])


def _diff_head(q, k_bf, v1_bf, lam, subln_g, lam_init, stack_maps):
    n = q.shape[0]
    first = lax.broadcasted_iota(jnp.int32, q.shape, 1) < DIFF_HD
    q0 = jnp.where(first, q, 0.0).astype(BF16)
    q1 = jnp.where(first, 0.0, q).astype(BF16)
    if stack_maps:
        r = _softmax_av(jnp.concatenate([q0, q1], axis=0), k_bf, v1_bf)
        o = r[:n] - lam * r[n:]
    else:
        o = _softmax_av(q0, k_bf, v1_bf) - lam * _softmax_av(q1, k_bf, v1_bf)
    return _rms(o, subln_g) * (1.0 - lam_init)


def _diff_ctx_kernel(q_ref, k_ref, v_ref, lam_ref, sg_ref, o_ref, ko_ref, vo_ref, *, lam_init):
    lam = _diff_lambda(lam_ref, lam_init)
    scale = DIFF_HD ** -0.5
    ko_ref[...] = k_ref[...]
    vo_ref[...] = v_ref[...]
    for bb in range(CTX_BATCHES_PER_STEP):
        rows = slice(bb * SEQ, (bb + 1) * SEQ)
        for h in range(HA):
            sl = slice(h * LANE, (h + 1) * LANE)
            y = _diff_head(q_ref[rows, sl] * scale, k_ref[rows, sl].astype(BF16),
                           _with_ones(v_ref[rows, sl].astype(BF16)), lam, sg_ref[...], lam_init, stack_maps=True)
            o_ref[rows, sl] = y.astype(o_ref.dtype)


def diff_attention_context(proj, lam_vec, subln_g, lam_init):
    w = HA * LANE

    def blk(c):
        return pl.BlockSpec((CTX_BATCHES_PER_STEP * SEQ, w), lambda b: (b, c))

    return pl.pallas_call(
        functools.partial(_diff_ctx_kernel, lam_init=lam_init),
        grid=(BATCH // CTX_BATCHES_PER_STEP,),
        in_specs=[blk(0), blk(1), blk(2),
                  pl.BlockSpec((4, DIFF_HD), lambda b: (0, 0)),
                  pl.BlockSpec((1, DIFF_VD), lambda b: (0, 0))],
        out_specs=[blk(0), blk(0), blk(0)],
        out_shape=[jax.ShapeDtypeStruct((TP, w), BF16), jax.ShapeDtypeStruct((TP, w), F32),
                   jax.ShapeDtypeStruct((TP, w), F32)],
        compiler_params=_params(("arbitrary",)),
        name="diff_attention_context",
    )(proj, proj, proj, lam_vec, subln_g.reshape(1, DIFF_VD))


def _diff_lat_kernel(q_ref, k_ref, v_ref, ck_ref, cv_ref, kc_ref, kl_ref, kh_ref, qc_ref, ql_ref, qh_ref,
                     lam_ref, sg_ref, o_ref, k_scr, v_scr, *, lam_init):
    quarter = DIFF_HD // 4

    @pl.when(pl.program_id(2) == 0)
    def _():
        k_scr[0:PAST_LEN, :] = ck_ref[...].astype(BF16)
        k_scr[PAST_LEN:, :] = _rope(k_ref[...], kc_ref[...], kl_ref[...], kh_ref[...], quarter).astype(BF16)
        v_scr[0:PAST_LEN, :] = cv_ref[...].astype(BF16)
        v_scr[PAST_LEN:, :] = v_ref[...].astype(BF16)

    lam = _diff_lambda(lam_ref, lam_init)
    sub = q_ref.shape[0] // DIFF_LAT_CHAINS
    for c in range(DIFF_LAT_CHAINS):
        rs = slice(c * sub, (c + 1) * sub)
        q = _rope(q_ref[rs, :], qc_ref[rs, :], ql_ref[rs, :], qh_ref[rs, :], quarter) * (DIFF_HD ** -0.5)
        y = _diff_head(q, k_scr[...], v_scr[...], lam, sg_ref[...], lam_init, stack_maps=False)
        o_ref[rs, :] = y.astype(o_ref.dtype)


def diff_attention_latent(proj, cache_k, cache_v, tables, lam_vec, subln_g, lam_init, tq=1024):
    nq = DEC_SEQ // tq
    row0 = TP // tq
    kv_row0 = TP // DEC_SEQ
    cos, sin_lo, sin_hi = tables
    full = pl.BlockSpec((DEC_SEQ, LANE), lambda b, h, i: (0, 0))
    qtab = pl.BlockSpec((tq, LANE), lambda b, h, i: (i, 0))
    cache = pl.BlockSpec((None, PAST_LEN, LANE), lambda b, h, i: (b, 0, h))
    return pl.pallas_call(
        functools.partial(_diff_lat_kernel, lam_init=lam_init),
        grid=(DEC_BATCH, HA, nq),
        in_specs=[pl.BlockSpec((tq, LANE), lambda b, h, i: (row0 + b * nq + i, h)),
                  pl.BlockSpec((DEC_SEQ, LANE), lambda b, h, i: (kv_row0 + b, HA + h)),
                  pl.BlockSpec((DEC_SEQ, LANE), lambda b, h, i: (kv_row0 + b, 2 * HA + h)),
                  cache, cache, full, full, full, qtab, qtab, qtab,
                  pl.BlockSpec((4, DIFF_HD), lambda b, h, i: (0, 0)),
                  pl.BlockSpec((1, DIFF_VD), lambda b, h, i: (0, 0))],
        out_specs=pl.BlockSpec((tq, LANE), lambda b, h, i: (b * nq + i, h)),
        out_shape=jax.ShapeDtypeStruct((TS, HA * LANE), BF16),
        scratch_shapes=[pltpu.VMEM((PAST_LEN + DEC_SEQ, LANE), BF16),
                        pltpu.VMEM((PAST_LEN + DEC_SEQ, LANE), BF16)],
        compiler_params=_params(("arbitrary", "arbitrary", "arbitrary")),
        name="diff_attention_latent",
    )(proj, proj, proj, cache_k, cache_v, cos, sin_lo, sin_hi, cos, sin_lo, sin_hi,
      lam_vec, subln_g.reshape(1, DIFF_VD))


def _split3(x):
    hi = x.astype(BF16)
    r = x - hi.astype(F32)
    mid = r.astype(BF16)
    lo = (r - mid.astype(F32)).astype(BF16)
    return hi, mid, lo


def _gla_kernel(q_ref, k_ref, v_ref, r_ref, g_ref, s0_ref, gg_ref, y_ref, sfin_ref, of_scr, ob_scr, st_scr,
                *, seq, pairs):
    c = GLA_CHUNK
    sb = GLA_SUPER * c
    n_super = seq // sb
    rows = lax.broadcasted_iota(jnp.int32, (sb, sb), 0)
    cols = lax.broadcasted_iota(jnp.int32, (sb, sb), 1)
    same_chunk = (rows // c) == (cols // c)
    keep = (same_chunk & (rows >= cols), same_chunk & (cols >= rows))
    tri = jnp.where(keep[0], 1.0, 0.0).astype(BF16)
    lane = lax.broadcasted_iota(jnp.int32, (sb, LANE), 1)
    own = (lane < GLA_DK, lane >= GLA_DK)
    chunk_of_row = lax.broadcasted_iota(jnp.int32, (sb, LANE), 0) // c
    st_rows = lax.broadcasted_iota(jnp.int32, (2 * GLA_DV, LANE), 0)
    st_lane = lax.broadcasted_iota(jnp.int32, (2 * GLA_DV, LANE), 1)
    st_own = (st_rows < GLA_DV) == (st_lane < GLA_DK)
    zpad = jnp.zeros((GLA_DK, GLA_DV), F32)

    def expand(x):
        return jnp.concatenate([jnp.where(chunk_of_row == ci, x, 0.0) for ci in range(GLA_SUPER)],
                               axis=1).astype(BF16)

    for p in range(pairs):
        for d in range(2):
            s0 = [s0_ref[d, 2 * p + hh] for hh in range(2)]
            st_scr[p, d] = jnp.concatenate([jnp.concatenate([s0[0], zpad], axis=0).T,
                                            jnp.concatenate([zpad, s0[1]], axis=0).T], axis=0)

    def rows_of(n, d):
        r0 = (n if d == 0 else n_super - 1 - n) * sb
        return pl.ds(r0, sb) if n_super == 1 else pl.ds(pl.multiple_of(r0, sb), sb)

    def chunk_totals(prefix):
        ends = [prefix[ci * c + c - 1:ci * c + c, :] for ci in range(GLA_SUPER)]
        return ends, jnp.concatenate([jnp.broadcast_to(e, (c, LANE)) for e in ends], axis=0)

    def log_decays(n, p):
        kl = slice(p * LANE, (p + 1) * LANE)
        g_b = g_ref[1, rows_of(n, 1), kl]
        hi, mid, lo = _split3(jnp.concatenate([g_ref[0, rows_of(n, 0), kl], g_b], axis=1))
        prefix = _dot(tri, hi) + _dot(tri, mid) + _dot(tri, lo)
        ends_f, total_f = chunk_totals(prefix[:, :LANE])
        ends_b, total_b = chunk_totals(prefix[:, LANE:])
        return (prefix[:, :LANE], ends_f, total_f), (total_b - prefix[:, LANE:] + g_b, ends_b, total_b)

    def one_pair_direction(n, p, d, decays):
        kl = slice(p * LANE, (p + 1) * LANE)
        vl = slice(2 * p * GLA_DV, 2 * (p + 1) * GLA_DV)
        rs = rows_of(n, d)
        b, ends, total = decays
        q_in = (q_ref[rs, kl] * (GLA_DK ** -0.5)) * jnp.exp(b)
        kk = k_ref[rs, kl]
        k_in = (kk * jnp.exp(-b)).astype(BF16)
        k_end = kk * jnp.exp(total - b)
        v = v_ref[rs, vl]
        v_bf = v.astype(BF16)
        kv_all = _dot(v.T.astype(BF16), expand(k_end))
        q2 = jnp.concatenate([jnp.where(own[0], q_in, 0.0), jnp.where(own[1], q_in, 0.0)], axis=0)
        a2 = _dot_nt(q2.astype(BF16), k_in)
        o = jnp.concatenate(
            [_dot(jnp.where(keep[d], a2[hh * sb:(hh + 1) * sb], 0.0).astype(BF16),
                  v_bf[:, hh * GLA_DV:(hh + 1) * GLA_DV]) for hh in range(2)], axis=1)
        st = st_scr[p, d]
        entering = [None] * GLA_SUPER
        for ci in (range(GLA_SUPER) if d == 0 else range(GLA_SUPER - 1, -1, -1)):
            entering[ci] = st
            st = st * jnp.exp(ends[ci]) + jnp.where(st_own, kv_all[:, ci * LANE:(ci + 1) * LANE], 0.0)
        st_scr[p, d] = st
        o += _dot_nt(expand(q_in), jnp.concatenate(entering, axis=1).astype(BF16))
        if d == 0:
            of_scr[rs, vl] = o
        else:
            ob_scr[rs, vl] = o

    def super_block(n, carry):
        for p in range(pairs):
            decays = log_decays(n, p)
            for d in range(2):
                one_pair_direction(n, p, d, decays[d])
        return carry

    if n_super == 1:
        super_block(0, 0)
    else:
        lax.fori_loop(0, n_super, super_block, 0)

    for h in range(2 * pairs):
        vs = slice(h * GLA_DV, (h + 1) * GLA_DV)
        r = r_ref[:, vs]
        y = _rms(of_scr[:, vs] + ob_scr[:, vs], gg_ref[...]) * (r * jax.nn.sigmoid(r))
        y_ref[:, vs] = y.astype(y_ref.dtype)
        p, hh = divmod(h, 2)
        for d in range(2):
            sfin_ref[d, h] = st_scr[p, d, hh * GLA_DV:(hh + 1) * GLA_DV, :].T[hh * GLA_DK:(hh + 1) * GLA_DK, :]


def bidir_gla(proj, gates, s0, gla_g, n_batch, seq, row_blk0):
    shared_s0 = s0.shape[0] == 1
    pairs = HB // 2 if seq * HB * GLA_DV * 4 <= 1024 * 1024 else 2
    wqk = pairs * LANE
    wv = pairs * 2 * GLA_DV
    col_q = (2 * HA * 2 * DIFF_HD + HA * DIFF_VD) // wqk
    col_k = col_q + HB * GLA_DK // wqk
    col_v = (col_k * wqk + HB * GLA_DK) // wv
    col_r = col_v + HB * GLA_DV // wv
    n_tok = n_batch * seq
    return pl.pallas_call(
        functools.partial(_gla_kernel, seq=seq, pairs=pairs),
        grid=(n_batch, HB // 2 // pairs),
        in_specs=[pl.BlockSpec((seq, wqk), lambda b, p: (row_blk0 + b, col_q + p)),
                  pl.BlockSpec((seq, wqk), lambda b, p: (row_blk0 + b, col_k + p)),
                  pl.BlockSpec((seq, wv), lambda b, p: (row_blk0 + b, col_v + p)),
                  pl.BlockSpec((seq, wv), lambda b, p: (row_blk0 + b, col_r + p)),
                  pl.BlockSpec((2, seq, wqk), lambda b, p: (0, row_blk0 + b, p)),
                  pl.BlockSpec((None, 2, 2 * pairs, GLA_DK, GLA_DV),
                               lambda b, p: (0 if shared_s0 else b, 0, p, 0, 0)),
                  pl.BlockSpec((1, GLA_DV), lambda b, p: (0, 0))],
        out_specs=[pl.BlockSpec((seq, wv), lambda b, p: (b, p)),
                   pl.BlockSpec((None, 2, 2 * pairs, GLA_DK, GLA_DV), lambda b, p: (b, 0, p, 0, 0))],
        out_shape=[jax.ShapeDtypeStruct((n_tok, HB * GLA_DV), BF16),
                   jax.ShapeDtypeStruct((n_batch, 2, HB, GLA_DK, GLA_DV), F32)],
        scratch_shapes=[pltpu.VMEM((seq, wv), F32), pltpu.VMEM((seq, wv), F32),
                        pltpu.VMEM((pairs, 2, 2 * GLA_DV, LANE), F32)],
        compiler_params=_params(("arbitrary", "arbitrary")),
        name="bidir_gla",
    )(proj, proj, proj, proj, gates, s0, gla_g.reshape(1, GLA_DV))


def _head_slice(kh, gq):
    h = kh * Q_PER_KV + gq
    return slice(h * LANE, (h + 1) * LANE)


def _sink_column(sink_ref, kh, rows):
    return jnp.concatenate([jnp.full((rows, 1), sink_ref[kh * Q_PER_KV + gq], F32) for gq in range(Q_PER_KV)],
                           axis=0)


def _gqa_ctx_kernel(sink_ref, q_ref, k_ref, v_ref, o_ref, ko_ref, vo_ref):
    scale = HEAD_DIM ** -0.5
    ko_ref[...] = k_ref[...]
    vo_ref[...] = v_ref[...]
    for bb in range(CTX_BATCHES_PER_STEP):
        rows = slice(bb * SEQ, (bb + 1) * SEQ)
        for kh in range(KV_HEADS):
            ks = slice(kh * LANE, (kh + 1) * LANE)
            k_bf = k_ref[rows, ks].astype(BF16)
            v1_bf = _with_ones(v_ref[rows, ks].astype(BF16))
            for gq in range(Q_PER_KV):
                hs = _head_slice(kh, gq)
                sink = sink_ref[kh * Q_PER_KV + gq]
                s = _dot_nt((q_ref[rows, hs] * scale).astype(BF16), k_bf)
                m = jnp.maximum(jnp.max(s, axis=-1, keepdims=True), sink)
                ov = _dot(jnp.exp(s - m).astype(BF16), v1_bf)
                o_ref[rows, hs] = (ov[:, :LANE] * (1.0 / (ov[:, LANE:] + jnp.exp(sink - m)))).astype(o_ref.dtype)


def gqa_context(proj, sinks):
    wq = N_HEADS * HEAD_DIM
    wkv = KV_HEADS * HEAD_DIM
    rows = CTX_BATCHES_PER_STEP * SEQ

    def kv(c):
        return pl.BlockSpec((rows, wkv), lambda b: (b, c))

    return pl.pallas_call(
        _gqa_ctx_kernel,
        grid=(BATCH // CTX_BATCHES_PER_STEP,),
        in_specs=[pl.BlockSpec(memory_space=pltpu.SMEM),
                  pl.BlockSpec((rows, wq), lambda b: (b, 0)), kv(wq // wkv), kv(wq // wkv + 1)],
        out_specs=[pl.BlockSpec((rows, wq), lambda b: (b, 0)), kv(0), kv(0)],
        out_shape=[jax.ShapeDtypeStruct((TP, wq), BF16), jax.ShapeDtypeStruct((TP, wkv), F32),
                   jax.ShapeDtypeStruct((TP, wkv), F32)],
        compiler_params=_params(("arbitrary",)),
        name="gqa_context",
    )(sinks, proj, proj, proj)


def _gqa_lat_kernel(sink_ref, q_ref, k_ref, v_ref, ck_ref, cv_ref, kc_ref, kl_ref, kh_ref,
                    qc_ref, ql_ref, qh_ref, o_ref, kw_scr, vw_scr, kc_scr, vc_scr):
    quarter = HEAD_DIM // 4
    w = WINDOW
    wkv = KV_HEADS * HEAD_DIM

    @pl.when(pl.program_id(1) == 0)
    def _():
        kw_scr[0:w, :] = jnp.zeros((w, wkv), BF16)
        kw_scr[w + DEC_SEQ:, :] = jnp.zeros((w, wkv), BF16)
        vw_scr[0:w, :] = jnp.zeros((w, 2 * wkv), BF16)
        vw_scr[w + DEC_SEQ:, :] = jnp.zeros((w, 2 * wkv), BF16)
        for kh in range(KV_HEADS):
            ks = slice(kh * LANE, (kh + 1) * LANE)
            vs = slice(2 * kh * LANE, 2 * (kh + 1) * LANE)
            kw_scr[w:w + DEC_SEQ, ks] = _rope(k_ref[:, ks], kc_ref[...], kl_ref[...], kh_ref[...],
                                              quarter).astype(BF16)
            vw_scr[w:w + DEC_SEQ, vs] = _with_ones(v_ref[:, ks].astype(BF16))
            vc_scr[:, vs] = _with_ones(cv_ref[:, ks].astype(BF16))
        kc_scr[...] = ck_ref[...].astype(BF16)

    qi = lax.broadcasted_iota(jnp.int32, (Q_PER_KV * w, 3 * w), 0) % w
    kj = lax.broadcasted_iota(jnp.int32, (Q_PER_KV * w, 3 * w), 1)
    scale = HEAD_DIM ** -0.5
    for blk in range(LAT_QBLOCKS_PER_STEP):
        n = pl.program_id(1) * LAT_QBLOCKS_PER_STEP + blk
        rows = slice(blk * w, (blk + 1) * w)
        win = pl.ds(pl.multiple_of(n * w, w), 3 * w)
        kpos = n * w - w + kj
        valid = (kj >= qi) & (kj <= qi + 2 * w) & (kpos >= 0) & (kpos < DEC_SEQ)
        for kh in range(KV_HEADS):
            ks = slice(kh * LANE, (kh + 1) * LANE)
            vs = slice(2 * kh * LANE, 2 * (kh + 1) * LANE)
            q = jnp.concatenate(
                [(_rope(q_ref[rows, _head_slice(kh, gq)], qc_ref[rows, :], ql_ref[rows, :], qh_ref[rows, :],
                        quarter) * scale).astype(BF16) for gq in range(Q_PER_KV)], axis=0)
            sink = _sink_column(sink_ref, kh, w)
            s_c = _dot_nt(q, kc_scr[:, ks])
            s_w = jnp.where(valid, _dot_nt(q, kw_scr[win, ks]), NEG_INF)
            m = jnp.maximum(jnp.maximum(jnp.max(s_c, axis=-1, keepdims=True),
                                        jnp.max(s_w, axis=-1, keepdims=True)), sink)
            ov = (_dot(jnp.exp(s_c - m).astype(BF16), vc_scr[:, vs])
                  + _dot(jnp.exp(s_w - m).astype(BF16), vw_scr[win, vs]))
            o = ov[:, :LANE] * (1.0 / (ov[:, LANE:] + jnp.exp(sink - m)))
            for gq in range(Q_PER_KV):
                o_ref[rows, _head_slice(kh, gq)] = o[gq * w:(gq + 1) * w].astype(o_ref.dtype)


def gqa_latent(proj, cache_k, cache_v, tables, sinks):
    wq = N_HEADS * HEAD_DIM
    wkv = KV_HEADS * HEAD_DIM
    tq = LAT_QBLOCKS_PER_STEP * WINDOW
    nq = DEC_SEQ // tq
    row0 = TP // tq
    kv_row0 = TP // DEC_SEQ
    cos, sin_lo, sin_hi = tables
    full = pl.BlockSpec((DEC_SEQ, LANE), lambda b, i: (0, 0))
    qtab = pl.BlockSpec((tq, LANE), lambda b, i: (i, 0))
    cache = pl.BlockSpec((None, PAST_LEN, wkv), lambda b, i: (b, 0, 0))
    return pl.pallas_call(
        _gqa_lat_kernel,
        grid=(DEC_BATCH, nq),
        in_specs=[pl.BlockSpec(memory_space=pltpu.SMEM),
                  pl.BlockSpec((tq, wq), lambda b, i: (row0 + b * nq + i, 0)),
                  pl.BlockSpec((DEC_SEQ, wkv), lambda b, i: (kv_row0 + b, wq // wkv)),
                  pl.BlockSpec((DEC_SEQ, wkv), lambda b, i: (kv_row0 + b, wq // wkv + 1)),
                  cache, cache, full, full, full, qtab, qtab, qtab],
        out_specs=pl.BlockSpec((tq, wq), lambda b, i: (b * nq + i, 0)),
        out_shape=jax.ShapeDtypeStruct((TS, wq), BF16),
        scratch_shapes=[pltpu.VMEM((DEC_SEQ + 2 * WINDOW, wkv), BF16),
                        pltpu.VMEM((DEC_SEQ + 2 * WINDOW, 2 * wkv), BF16),
                        pltpu.VMEM((PAST_LEN, wkv), BF16),
                        pltpu.VMEM((PAST_LEN, 2 * wkv), BF16)],
        compiler_params=_params(("arbitrary", "arbitrary")),
        name="gqa_latent",
    )(sinks, proj, proj, proj, cache_k, cache_v, cos, sin_lo, sin_hi, cos, sin_lo, sin_hi)


def _even_layer(xc, xs, mod, j, layer, norm_g, w_in_even, lam_vec, subln_g, w_gate1, w_gate2, b_gate, gla_norm_g,
                w_out_even, cache_a_k, cache_a_v, state_b, tables):
    lam_init = 0.8 - 0.6 * math.exp(-0.3 * layer)
    ng = HB * GLA_DK
    wg1 = jnp.zeros((D_MODEL, LANE), F32).at[:, :2 * GLA_GATE_RANK].set(
        jnp.concatenate([w_gate1[j, 0], w_gate1[j, 1]], axis=-1)).astype(BF16)
    wg2 = jnp.zeros((2, LANE, ng), F32)
    for e in range(2):
        wg2 = wg2.at[e, e * GLA_GATE_RANK:(e + 1) * GLA_GATE_RANK, :].set(w_gate2[j, e])
    h, gates = norm_mod_gate(xc, xs, norm_g[layer, 0], mod, 0, 1, wg1, wg2.astype(BF16),
                             b_gate[j].reshape(2, 1, ng))
    proj = ws_matmul(h, w_in_even, j)
    ya_c, new_k, new_v = diff_attention_context(proj, lam_vec, subln_g[j], lam_init)
    ya_s = diff_attention_latent(proj, cache_a_k[:, j].reshape(DEC_BATCH, PAST_LEN, HA * 2 * DIFF_HD),
                                 cache_a_v[:, j].reshape(DEC_BATCH, PAST_LEN, HA * DIFF_VD),
                                 tables, lam_vec, subln_g[j], lam_init)
    yb_c, s_fin = bidir_gla(proj, gates, jnp.zeros((1, 2, HB, GLA_DK, GLA_DV), F32), gla_norm_g[j],
                            BATCH, SEQ, 0)
    yb_s, _ = bidir_gla(proj, gates, state_b[:, j], gla_norm_g[j], DEC_BATCH, DEC_SEQ, TP // DEC_SEQ)
    x = ws_matmul_residual([(ya_c, ya_s), (yb_c, yb_s)], w_out_even, j, (xc, xs), mod, 2)
    return (x, new_k.reshape(BATCH, SEQ, HA, 2 * DIFF_HD), new_v.reshape(BATCH, SEQ, HA, DIFF_VD), s_fin)


def _odd_layer(x, mod, j, layer, norm_g, w_in_odd, sinks, w_out_odd, cache_c_k, cache_c_v, tables):
    wkv = KV_HEADS * HEAD_DIM
    proj = ws_matmul(norm_mod(x, norm_g[layer, 0], mod, 0, 1), w_in_odd, j)
    o_c, new_k, new_v = gqa_context(proj, sinks[j])
    o_s = gqa_latent(proj, cache_c_k[:, j].reshape(DEC_BATCH, PAST_LEN, wkv),
                     cache_c_v[:, j].reshape(DEC_BATCH, PAST_LEN, wkv), tables, sinks[j])
    x = ws_matmul_residual([(o_c, o_s)], w_out_odd, j, (x,), mod, 2)
    return (x, new_k.reshape(BATCH, SEQ, KV_HEADS, HEAD_DIM), new_v.reshape(BATCH, SEQ, KV_HEADS, HEAD_DIM))


def _ffn(x, mod, layer, norm_g, w_ffn_in, w_ffn_out):
    act = ws_swiglu(norm_mod(x, norm_g[layer, 1], mod, 3, 4), w_ffn_in, layer)
    return ws_matmul_residual([(act,)], w_ffn_out, layer, (x,), mod, 5, tm=512, tn=512)


def kernel(x_prompt, x_sample, c, cache_a_k, cache_a_v, state_b, cache_c_k, cache_c_v, c_ctx, w_ada, b_ada,
           norm_g, w_in_even, lam_q1, lam_k1, lam_q2, lam_k2, subln_g, w_gate1, w_gate2, b_gate, gla_norm_g,
           w_out_even, w_in_odd, sinks, w_out_odd, w_ffn_in, w_ffn_out, final_norm_g):
    assert DEPTH == 2, "layer 0 reads the two input streams, layer 1 the fused token matrix"
    cvec = jnp.concatenate([c_ctx[None, :], c, jnp.zeros((MOD_ROWS - 1 - DEC_BATCH, D_MODEL), F32)], axis=0)
    mods = adaln(cvec, w_ada, b_ada).reshape(DEPTH, MOD_ROWS, 1, 6 * D_MODEL)
    tab_d = rope_tables(DEC_SEQ, DIFF_HD)
    tab_h = rope_tables(DEC_SEQ, HEAD_DIM)

    lam_vec = jnp.stack([lam_q1[0], lam_k1[0], lam_q2[0], lam_k2[0]], axis=0)
    x, ak, av, sb = _even_layer(x_prompt.reshape(TP, D_MODEL), x_sample.reshape(TS, D_MODEL), mods[0], 0, 0,
                                norm_g, w_in_even, lam_vec, subln_g, w_gate1, w_gate2, b_gate, gla_norm_g,
                                w_out_even, cache_a_k, cache_a_v, state_b, tab_d)
    x = _ffn(x, mods[0], 0, norm_g, w_ffn_in, w_ffn_out)
    x, ck, cv = _odd_layer(x, mods[1], 0, 1, norm_g, w_in_odd, sinks, w_out_odd, cache_c_k, cache_c_v, tab_h)
    x = _ffn(x, mods[1], 1, norm_g, w_ffn_in, w_ffn_out)

    y_prompt = final_norm(x, final_norm_g, 0, TP).reshape(BATCH, SEQ, D_MODEL)
    y_sample = final_norm(x, final_norm_g, TP, TS).reshape(DEC_BATCH, DEC_SEQ, D_MODEL)
    return (y_prompt, y_sample, ak[:, None], av[:, None], sb[:, None], ck[:, None], cv[:, None])
```

```python
import functools
import math

import jax
import jax.numpy as jnp
from jax import lax
from jax.experimental import pallas as pl
from jax.experimental.pallas import tpu as pltpu

D_MODEL = 2048
BATCH = 32
SEQ = 256
DEPTH = 2
DEC_BATCH = 2
DEC_SEQ = 2048
PAST_LEN = 512
GRID_W = 64
HEAD_DIM = 128
N_HEADS = D_MODEL // HEAD_DIM
HA = N_HEADS // 2
DIFF_HD = HEAD_DIM // 2
DIFF_VD = HEAD_DIM
HB = N_HEADS // 2
GLA_DK = HEAD_DIM // 2
GLA_DV = HEAD_DIM
GLA_GATE_RANK = 16
GLA_TAU = 16.0
GLA_CHUNK = 64
GLA_SUPER = 4
CTX_BATCHES_PER_STEP = 4
LAT_QBLOCKS_PER_STEP = 2
DIFF_LAT_CHAINS = 4
KV_HEADS = N_HEADS // 4
Q_PER_KV = N_HEADS // KV_HEADS
WINDOW = 128
D_FF = -(-8 * D_MODEL // (3 * 256)) * 256
ROPE_BASE = 10000.0
EPS = 1e-6
NEG_INF = -1e30
EVEN_IN = 2 * HA * 2 * DIFF_HD + HA * DIFF_VD + 2 * HB * GLA_DK + 2 * HB * GLA_DV
ODD_IN = (N_HEADS + 2 * KV_HEADS) * HEAD_DIM

TP = BATCH * SEQ
TS = DEC_BATCH * DEC_SEQ
T = TP + TS
LANE = 128
MOD_ROWS = 8
NORM_ROWS = 16
NORM_UNROLL = 8
VMEM_LIMIT = 56 * 1024 * 1024

F32 = jnp.float32
BF16 = jnp.bfloat16


def _params(semantics, vmem=VMEM_LIMIT):
    return pltpu.CompilerParams(dimension_semantics=semantics, vmem_limit_bytes=vmem)


def _dot(a, b):
    return jnp.dot(a, b, preferred_element_type=F32)


def _dot_nt(a, b):
    return lax.dot_general(a, b, (((1,), (1,)), ((), ())), preferred_element_type=F32)


def _group_of_tile(i, tm):
    r = i * tm
    return jnp.where(r < TP, 0, 1 + (r - TP) // DEC_SEQ)


def _rms(x, g):
    return (x * lax.rsqrt(jnp.mean(x * x, axis=-1, keepdims=True) + EPS)) * g


def _adaln_kernel(c_ref, w_ref, b_ref, o_ref):
    c = c_ref[...]
    s = c * jax.nn.sigmoid(c)
    o_ref[...] = _dot(s.astype(BF16), w_ref[...].astype(BF16)) + b_ref[...]


def adaln(cvec, w_ada, b_ada, tn=512):
    n = w_ada.shape[-1]
    return pl.pallas_call(
        _adaln_kernel,
        grid=(DEPTH, n // tn),
        in_specs=[
            pl.BlockSpec((MOD_ROWS, D_MODEL), lambda l, j: (0, 0)),
            pl.BlockSpec((None, D_MODEL, tn), lambda l, j: (l, 0, j)),
            pl.BlockSpec((None, 1, tn), lambda l, j: (l, 0, j)),
        ],
        out_specs=pl.BlockSpec((None, MOD_ROWS, tn), lambda l, j: (l, 0, j)),
        out_shape=jax.ShapeDtypeStruct((DEPTH, MOD_ROWS, n), F32),
        compiler_params=_params(("arbitrary", "arbitrary")),
        name="adaln",
    )(cvec, w_ada, b_ada.reshape(DEPTH, 1, n))


def _by_row_groups(x_ref, o_ref, fn):
    def body(r, carry):
        rs = pl.ds(pl.multiple_of(r * NORM_ROWS, NORM_ROWS), NORM_ROWS)
        o_ref[rs, :] = fn(x_ref[rs, :]).astype(o_ref.dtype)
        return carry

    lax.fori_loop(0, x_ref.shape[0] // NORM_ROWS, body, 0, unroll=NORM_UNROLL)


def _norm_mod_into(h_ref, x_ref, g_ref, shift_ref, scale_ref):
    _by_row_groups(x_ref, h_ref, lambda x: _rms(x, g_ref[...]) * (1 + scale_ref[...]) + shift_ref[...])


def _norm_mod_kernel(x_ref, g_ref, shift_ref, scale_ref, h_ref):
    _norm_mod_into(h_ref, x_ref, g_ref, shift_ref, scale_ref)


def _norm_mod_gate_kernel(xc_ref, xs_ref, g_ref, shift_ref, scale_ref, wg1_ref, wg2_ref, bg_ref,
                          h_ref, gate_ref, *, ctx_tiles):
    @pl.when(pl.program_id(0) < ctx_tiles)
    def _():
        _norm_mod_into(h_ref, xc_ref, g_ref, shift_ref, scale_ref)

    @pl.when(pl.program_id(0) >= ctx_tiles)
    def _():
        _norm_mod_into(h_ref, xs_ref, g_ref, shift_ref, scale_ref)

    low = _dot(h_ref[...], wg1_ref[...]).astype(BF16)
    for e in range(2):
        logit = _dot(low, wg2_ref[e]) + bg_ref[e]
        log_sig = jnp.minimum(logit, 0.0) - jnp.log(1.0 + jnp.exp(-jnp.abs(logit)))
        gate_ref[e] = log_sig / GLA_TAU


def _mod_specs(tm, shift_idx, scale_idx):
    def spec(idx):
        return pl.BlockSpec((None, 1, D_MODEL), lambda i: (_group_of_tile(i, tm), 0, idx))
    return [pl.BlockSpec((1, D_MODEL), lambda i: (0, 0)), spec(shift_idx), spec(scale_idx)]


def norm_mod(x, g, mod, shift_idx, scale_idx, tm=1024):
    return pl.pallas_call(
        _norm_mod_kernel,
        grid=(T // tm,),
        in_specs=[pl.BlockSpec((tm, D_MODEL), lambda i: (i, 0))] + _mod_specs(tm, shift_idx, scale_idx),
        out_specs=pl.BlockSpec((tm, D_MODEL), lambda i: (i, 0)),
        out_shape=jax.ShapeDtypeStruct((T, D_MODEL), BF16),
        compiler_params=_params(("arbitrary",)),
        name="norm_mod",
    )(x, g.reshape(1, D_MODEL), mod, mod)


def norm_mod_gate(xc, xs, g, mod, shift_idx, scale_idx, wg1, wg2, bg, tm=512):
    ng = HB * GLA_DK
    ctx_tiles = TP // tm
    return pl.pallas_call(
        functools.partial(_norm_mod_gate_kernel, ctx_tiles=ctx_tiles),
        grid=(T // tm,),
        in_specs=[pl.BlockSpec((tm, D_MODEL), lambda i: (jnp.minimum(i, ctx_tiles - 1), 0)),
                  pl.BlockSpec((tm, D_MODEL), lambda i: (jnp.maximum(i - ctx_tiles, 0), 0))]
        + _mod_specs(tm, shift_idx, scale_idx)
        + [pl.BlockSpec((D_MODEL, LANE), lambda i: (0, 0)),
           pl.BlockSpec((2, LANE, ng), lambda i: (0, 0, 0)),
           pl.BlockSpec((2, 1, ng), lambda i: (0, 0, 0))],
        out_specs=[pl.BlockSpec((tm, D_MODEL), lambda i: (i, 0)),
                   pl.BlockSpec((2, tm, ng), lambda i: (0, i, 0))],
        out_shape=[jax.ShapeDtypeStruct((T, D_MODEL), BF16), jax.ShapeDtypeStruct((2, T, ng), F32)],
        compiler_params=_params(("arbitrary",)),
        name="norm_mod_gate",
    )(xc, xs, g.reshape(1, D_MODEL), mod, mod, wg1, wg2, bg)


def _cast_weights(w_refs, w_scrs):
    @pl.when(pl.program_id(1) == 0)
    def _():
        for w_ref, w_scr in zip(w_refs, w_scrs):
            w_scr[...] = w_ref[...].astype(BF16)


def _ws_plain_kernel(h_ref, w_ref, o_ref, w_scr):
    _cast_weights([w_ref], [w_scr])
    o_ref[...] = _dot(h_ref[...], w_scr[...])


def _ws_swiglu_kernel(h_ref, wg_ref, wu_ref, o_ref, wg_scr, wu_scr):
    _cast_weights([wg_ref, wu_ref], [wg_scr, wu_scr])
    h = h_ref[...]
    gate = _dot(h, wg_scr[...])
    up = _dot(h, wu_scr[...])
    o_ref[...] = (gate * jax.nn.sigmoid(gate) * up).astype(o_ref.dtype)


def _ws_res_kernel(*refs, lhs_arity, x_arity, ctx_tiles):
    n_l = sum(lhs_arity)
    w_ref = refs[n_l]
    x_refs = refs[n_l + 1:n_l + 1 + x_arity]
    gate_ref, o_ref, w_scr = refs[n_l + 1 + x_arity:]
    groups, at = [], 0
    for a in lhs_arity:
        groups.append(refs[at:at + a])
        at += a
    _cast_weights([w_ref], [w_scr])

    def emit(side):
        acc, k0 = None, 0
        for grp in groups:
            ref = grp[side] if len(grp) == 2 else grp[0]
            part = _dot(ref[...], w_scr[k0:k0 + ref.shape[1], :])
            acc = part if acc is None else acc + part
            k0 += ref.shape[1]
        x_ref = x_refs[side] if x_arity == 2 else x_refs[0]
        o_ref[...] = x_ref[...] + gate_ref[...] * acc

    if x_arity == 2 or 2 in lhs_arity:
        @pl.when(pl.program_id(1) < ctx_tiles)
        def _():
            emit(0)

        @pl.when(pl.program_id(1) >= ctx_tiles)
        def _():
            emit(1)
    else:
        emit(0)


def _rows_specs(arrays, tm, width, col):
    if len(arrays) == 1:
        return [pl.BlockSpec((tm, width), lambda j, i: (i, col(j)))]
    ctx_tiles = TP // tm
    return [pl.BlockSpec((tm, width), lambda j, i: (jnp.minimum(i, ctx_tiles - 1), col(j))),
            pl.BlockSpec((tm, width), lambda j, i: (jnp.maximum(i - ctx_tiles, 0), col(j)))]


def ws_matmul(h, w_stack, layer, tm=1024, tn=1024):
    kdim, n = w_stack.shape[1:]
    return pl.pallas_call(
        _ws_plain_kernel,
        grid=(n // tn, T // tm),
        in_specs=[pl.BlockSpec((tm, kdim), lambda j, i: (i, 0)),
                  pl.BlockSpec((None, kdim, tn), lambda j, i: (layer, 0, j))],
        out_specs=pl.BlockSpec((tm, tn), lambda j, i: (i, j)),
        out_shape=jax.ShapeDtypeStruct((T, n), F32),
        scratch_shapes=[pltpu.VMEM((kdim, tn), BF16)],
        compiler_params=_params(("arbitrary", "arbitrary")),
        name="ws_matmul",
    )(h, w_stack)


def ws_swiglu(h, w_stack, layer, tm=1024, tn=512):
    kdim = w_stack.shape[1]
    nf = D_FF // tn
    return pl.pallas_call(
        _ws_swiglu_kernel,
        grid=(nf, T // tm),
        in_specs=[pl.BlockSpec((tm, kdim), lambda j, i: (i, 0)),
                  pl.BlockSpec((None, kdim, tn), lambda j, i: (layer, 0, j)),
                  pl.BlockSpec((None, kdim, tn), lambda j, i: (layer, 0, j + nf))],
        out_specs=pl.BlockSpec((tm, tn), lambda j, i: (i, j)),
        out_shape=jax.ShapeDtypeStruct((T, D_FF), BF16),
        scratch_shapes=[pltpu.VMEM((kdim, tn), BF16), pltpu.VMEM((kdim, tn), BF16)],
        compiler_params=_params(("arbitrary", "arbitrary")),
        name="ws_swiglu",
    )(h, w_stack, w_stack)


def ws_matmul_residual(lhs_groups, w_stack, layer, x, mod, gate_idx, tm=512, tn=1024):
    kdim, n = w_stack.shape[1:]
    gate_blk = n // tn
    specs, args = [], []
    for grp in lhs_groups:
        specs += _rows_specs(grp, tm, grp[0].shape[1], lambda j: 0)
        args += list(grp)
    specs.append(pl.BlockSpec((None, kdim, tn), lambda j, i: (layer, 0, j)))
    specs += _rows_specs(x, tm, tn, lambda j: j)
    specs.append(pl.BlockSpec((None, 1, tn), lambda j, i: (_group_of_tile(i, tm), 0, gate_idx * gate_blk + j)))
    return pl.pallas_call(
        functools.partial(_ws_res_kernel, lhs_arity=tuple(len(grp) for grp in lhs_groups), x_arity=len(x),
                          ctx_tiles=TP // tm),
        grid=(n // tn, T // tm),
        in_specs=specs,
        out_specs=pl.BlockSpec((tm, tn), lambda j, i: (i, j)),
        out_shape=jax.ShapeDtypeStruct((T, n), F32),
        scratch_shapes=[pltpu.VMEM((kdim, tn), BF16)],
        compiler_params=_params(("arbitrary", "arbitrary")),
        name="ws_matmul_residual",
    )(*args, w_stack, *x, mod)


def _final_norm_kernel(x_ref, g_ref, o_ref):
    o_ref[...] = _rms(x_ref[...], g_ref[...])


def final_norm(x, g, row0, n_rows, tm=1024):
    blk0 = row0 // tm
    return pl.pallas_call(
        _final_norm_kernel,
        grid=(n_rows // tm,),
        in_specs=[pl.BlockSpec((tm, D_MODEL), lambda i: (blk0 + i, 0)),
                  pl.BlockSpec((1, D_MODEL), lambda i: (0, 0))],
        out_specs=pl.BlockSpec((tm, D_MODEL), lambda i: (i, 0)),
        out_shape=jax.ShapeDtypeStruct((n_rows, D_MODEL), F32),
        compiler_params=_params(("arbitrary",)),
        name="final_norm",
    )(x, g.reshape(1, D_MODEL))


def _grid_angles(n_tok, rot_dim):
    t = jnp.arange((n_tok // GRID_W) * GRID_W)
    row = (t // GRID_W).astype(F32)
    col = (t % GRID_W).astype(F32)
    half = rot_dim // 2
    inv = ROPE_BASE ** (-jnp.arange(0, half, 2, dtype=F32) / half)
    return row[:, None] * inv[None], col[:, None] * inv[None]


def rope_tables(n_tok, rot_dim):
    ang_row, ang_col = _grid_angles(n_tok, rot_dim)
    zeros = jnp.zeros_like(ang_row)
    reps = LANE // rot_dim

    def lanes(r1, r2, c1, c2):
        return jnp.tile(jnp.concatenate([r1, r2, c1, c2], axis=-1), (1, reps))

    cr, sr, cc, sc = jnp.cos(ang_row), jnp.sin(ang_row), jnp.cos(ang_col), jnp.sin(ang_col)
    return lanes(cr, cr, cc, cc), lanes(-sr, zeros, -sc, zeros), lanes(zeros, sr, zeros, sc)


def _rope(x, cos, sin_lo, sin_hi, quarter):
    return (x * cos + pltpu.roll(x, LANE - quarter, 1) * sin_lo + pltpu.roll(x, quarter, 1) * sin_hi)


def _diff_lambda(lam_ref, lam_init):
    e1 = jnp.exp(jnp.sum(lam_ref[0:1, :] * lam_ref[1:2, :], axis=-1, keepdims=True))
    e2 = jnp.exp(jnp.sum(lam_ref[2:3, :] * lam_ref[3:4, :], axis=-1, keepdims=True))
    return e1 - e2 + lam_init


def _with_ones(v_bf):
    return jnp.concatenate([v_bf, jnp.ones(v_bf.shape, BF16)], axis=1)


def _softmax_av(q_bf, k_bf, v1_bf):
    s = _dot_nt(q_bf, k_bf)
    e = jnp.exp(s - jnp.max(s, axis=-1, keepdims=True))
    if v1_bf.shape[1] == LANE:
        return _dot(e.astype(BF16), v1_bf) * (1.0 / jnp.sum(e, axis=-1, keepdims=True))
    ov = _dot(e.astype(BF16), v1_bf)
    return ov[:, :LANE] * (1.0 / ov[:, LANE:])


def _diff_head(q, k_bf, v1_bf, lam, subln_g, lam_init, stack_maps):
    n = q.shape[0]
    first = lax.broadcasted_iota(jnp.int32, q.shape, 1) < DIFF_HD
    q0 = jnp.where(first, q, 0.0).astype(BF16)
    q1 = jnp.where(first, 0.0, q).astype(BF16)
    if stack_maps:
        r = _softmax_av(jnp.concatenate([q0, q1], axis=0), k_bf, v1_bf)
        o = r[:n] - lam * r[n:]
    else:
        o = _softmax_av(q0, k_bf, v1_bf) - lam * _softmax_av(q1, k_bf, v1_bf)
    return _rms(o, subln_g) * (1.0 - lam_init)


def _diff_ctx_kernel(q_ref, k_ref, v_ref, lam_ref, sg_ref, o_ref, ko_ref, vo_ref, *, lam_init):
    lam = _diff_lambda(lam_ref, lam_init)
    scale = DIFF_HD ** -0.5
    ko_ref[...] = k_ref[...]
    vo_ref[...] = v_ref[...]
    for bb in range(CTX_BATCHES_PER_STEP):
        rows = slice(bb * SEQ, (bb + 1) * SEQ)
        for h in range(HA):
            sl = slice(h * LANE, (h + 1) * LANE)
            y = _diff_head(q_ref[rows, sl] * scale, k_ref[rows, sl].astype(BF16),
                           _with_ones(v_ref[rows, sl].astype(BF16)), lam, sg_ref[...], lam_init, stack_maps=True)
            o_ref[rows, sl] = y.astype(o_ref.dtype)


def diff_attention_context(proj, lam_vec, subln_g, lam_init):
    w = HA * LANE

    def blk(c):
        return pl.BlockSpec((CTX_BATCHES_PER_STEP * SEQ, w), lambda b: (b, c))

    return pl.pallas_call(
        functools.partial(_diff_ctx_kernel, lam_init=lam_init),
        grid=(BATCH // CTX_BATCHES_PER_STEP,),
        in_specs=[blk(0), blk(1), blk(2),
                  pl.BlockSpec((4, DIFF_HD), lambda b: (0, 0)),
                  pl.BlockSpec((1, DIFF_VD), lambda b: (0, 0))],
        out_specs=[blk(0), blk(0), blk(0)],
        out_shape=[jax.ShapeDtypeStruct((TP, w), BF16), jax.ShapeDtypeStruct((TP, w), F32),
                   jax.ShapeDtypeStruct((TP, w), F32)],
        compiler_params=_params(("arbitrary",)),
        name="diff_attention_context",
    )(proj, proj, proj, lam_vec, subln_g.reshape(1, DIFF_VD))


def _diff_lat_kernel(q_ref, k_ref, v_ref, ck_ref, cv_ref, kc_ref, kl_ref, kh_ref, qc_ref, ql_ref, qh_ref,
                     lam_ref, sg_ref, o_ref, k_scr, v_scr, *, lam_init):
    quarter = DIFF_HD // 4

    @pl.when(pl.program_id(2) == 0)
    def _():
        k_scr[0:PAST_LEN, :] = ck_ref[...].astype(BF16)
        k_scr[PAST_LEN:, :] = _rope(k_ref[...], kc_ref[...], kl_ref[...], kh_ref[...], quarter).astype(BF16)
        v_scr[0:PAST_LEN, :] = cv_ref[...].astype(BF16)
        v_scr[PAST_LEN:, :] = v_ref[...].astype(BF16)

    lam = _diff_lambda(lam_ref, lam_init)
    sub = q_ref.shape[0] // DIFF_LAT_CHAINS
    for c in range(DIFF_LAT_CHAINS):
        rs = slice(c * sub, (c + 1) * sub)
        q = _rope(q_ref[rs, :], qc_ref[rs, :], ql_ref[rs, :], qh_ref[rs, :], quarter) * (DIFF_HD ** -0.5)
        y = _diff_head(q, k_scr[...], v_scr[...], lam, sg_ref[...], lam_init, stack_maps=False)
        o_ref[rs, :] = y.astype(o_ref.dtype)


def diff_attention_latent(proj, cache_k, cache_v, tables, lam_vec, subln_g, lam_init, tq=1024):
    nq = DEC_SEQ // tq
    row0 = TP // tq
    kv_row0 = TP // DEC_SEQ
    cos, sin_lo, sin_hi = tables
    full = pl.BlockSpec((DEC_SEQ, LANE), lambda b, h, i: (0, 0))
    qtab = pl.BlockSpec((tq, LANE), lambda b, h, i: (i, 0))
    cache = pl.BlockSpec((None, PAST_LEN, LANE), lambda b, h, i: (b, 0, h))
    return pl.pallas_call(
        functools.partial(_diff_lat_kernel, lam_init=lam_init),
        grid=(DEC_BATCH, HA, nq),
        in_specs=[pl.BlockSpec((tq, LANE), lambda b, h, i: (row0 + b * nq + i, h)),
                  pl.BlockSpec((DEC_SEQ, LANE), lambda b, h, i: (kv_row0 + b, HA + h)),
                  pl.BlockSpec((DEC_SEQ, LANE), lambda b, h, i: (kv_row0 + b, 2 * HA + h)),
                  cache, cache, full, full, full, qtab, qtab, qtab,
                  pl.BlockSpec((4, DIFF_HD), lambda b, h, i: (0, 0)),
                  pl.BlockSpec((1, DIFF_VD), lambda b, h, i: (0, 0))],
        out_specs=pl.BlockSpec((tq, LANE), lambda b, h, i: (b * nq + i, h)),
        out_shape=jax.ShapeDtypeStruct((TS, HA * LANE), BF16),
        scratch_shapes=[pltpu.VMEM((PAST_LEN + DEC_SEQ, LANE), BF16),
                        pltpu.VMEM((PAST_LEN + DEC_SEQ, LANE), BF16)],
        compiler_params=_params(("arbitrary", "arbitrary", "arbitrary")),
        name="diff_attention_latent",
    )(proj, proj, proj, cache_k, cache_v, cos, sin_lo, sin_hi, cos, sin_lo, sin_hi,
      lam_vec, subln_g.reshape(1, DIFF_VD))


def _gla_kernel(q_ref, k_ref, v_ref, r_ref, g_ref, s0_ref, gg_ref, y_ref, sfin_ref, of_scr, ob_scr, st_scr,
                *, seq, pairs):
    c = GLA_CHUNK
    sb = GLA_SUPER * c
    n_super = seq // sb
    rows = lax.broadcasted_iota(jnp.int32, (sb, sb), 0)
    cols = lax.broadcasted_iota(jnp.int32, (sb, sb), 1)
    same_chunk = (rows // c) == (cols // c)
    keep = (same_chunk & (rows >= cols), same_chunk & (cols >= rows))
    lane = lax.broadcasted_iota(jnp.int32, (sb, LANE), 1)
    own = (lane < GLA_DK, lane >= GLA_DK)
    chunk_of_row = lax.broadcasted_iota(jnp.int32, (sb, LANE), 0) // c
    st_rows = lax.broadcasted_iota(jnp.int32, (2 * GLA_DV, LANE), 0)
    st_lane = lax.broadcasted_iota(jnp.int32, (2 * GLA_DV, LANE), 1)
    st_own = (st_rows < GLA_DV) == (st_lane < GLA_DK)
    zpad = jnp.zeros((GLA_DK, GLA_DV), F32)

    def expand(x):
        return jnp.concatenate([jnp.where(chunk_of_row == ci, x, 0.0) for ci in range(GLA_SUPER)],
                               axis=1).astype(BF16)

    for p in range(pairs):
        for d in range(2):
            s0 = [s0_ref[d, 2 * p + hh] for hh in range(2)]
            st_scr[p, d] = jnp.concatenate([jnp.concatenate([s0[0], zpad], axis=0).T,
                                            jnp.concatenate([zpad, s0[1]], axis=0).T], axis=0)

    def rows_of(n, d):
        r0 = (n if d == 0 else n_super - 1 - n) * sb
        return pl.ds(r0, sb) if n_super == 1 else pl.ds(pl.multiple_of(r0, sb), sb)

    def chunk_totals(prefix):
        ends = [prefix[ci * c + c - 1:ci * c + c, :] for ci in range(GLA_SUPER)]
        return ends, jnp.concatenate([jnp.broadcast_to(e, (c, LANE)) for e in ends], axis=0)

    def log_decays(n, p):
        kl = slice(p * LANE, (p + 1) * LANE)
        g_b = g_ref[1, rows_of(n, 1), kl]
        prefix = jnp.concatenate([g_ref[0, rows_of(n, 0), kl], g_b], axis=1)
        pos = lax.broadcasted_iota(jnp.int32, prefix.shape, 0) % c
        step = 1
        while step < c:
            prefix = prefix + jnp.where(pos >= step, pltpu.roll(prefix, step, 0), 0.0)
            step *= 2
        ends_f, total_f = chunk_totals(prefix[:, :LANE])
        ends_b, total_b = chunk_totals(prefix[:, LANE:])
        return (prefix[:, :LANE], ends_f, total_f), (total_b - prefix[:, LANE:] + g_b, ends_b, total_b)

    def one_pair_direction(n, p, d, decays):
        kl = slice(p * LANE, (p + 1) * LANE)
        vl = slice(2 * p * GLA_DV, 2 * (p + 1) * GLA_DV)
        rs = rows_of(n, d)
        b, ends, total = decays
        q_in = (q_ref[rs, kl] * (GLA_DK ** -0.5)) * jnp.exp(b)
        kk = k_ref[rs, kl]
        k_in = (kk * jnp.exp(-b)).astype(BF16)
        k_end = kk * jnp.exp(total - b)
        v = v_ref[rs, vl]
        v_bf = v.astype(BF16)
        kv_all = _dot(v.T.astype(BF16), expand(k_end))
        q2 = jnp.concatenate([jnp.where(own[0], q_in, 0.0), jnp.where(own[1], q_in, 0.0)], axis=0)
        a2 = _dot_nt(q2.astype(BF16), k_in)
        o = jnp.concatenate(
            [_dot(jnp.where(keep[d], a2[hh * sb:(hh + 1) * sb], 0.0).astype(BF16),
                  v_bf[:, hh * GLA_DV:(hh + 1) * GLA_DV]) for hh in range(2)], axis=1)
        st = st_scr[p, d]
        entering = [None] * GLA_SUPER
        for ci in (range(GLA_SUPER) if d == 0 else range(GLA_SUPER - 1, -1, -1)):
            entering[ci] = st
            st = st * jnp.exp(ends[ci]) + jnp.where(st_own, kv_all[:, ci * LANE:(ci + 1) * LANE], 0.0)
        st_scr[p, d] = st
        o += _dot_nt(expand(q_in), jnp.concatenate(entering, axis=1).astype(BF16))
        if d == 0:
            of_scr[rs, vl] = o
        else:
            ob_scr[rs, vl] = o

    def super_block(n, carry):
        for p in range(pairs):
            decays = log_decays(n, p)
            for d in range(2):
                one_pair_direction(n, p, d, decays[d])
        return carry

    if n_super == 1:
        super_block(0, 0)
    else:
        lax.fori_loop(0, n_super, super_block, 0)

    for h in range(2 * pairs):
        vs = slice(h * GLA_DV, (h + 1) * GLA_DV)
        r = r_ref[:, vs]
        y = _rms(of_scr[:, vs] + ob_scr[:, vs], gg_ref[...]) * (r * jax.nn.sigmoid(r))
        y_ref[:, vs] = y.astype(y_ref.dtype)
        p, hh = divmod(h, 2)
        for d in range(2):
            sfin_ref[d, h] = st_scr[p, d, hh * GLA_DV:(hh + 1) * GLA_DV, :].T[hh * GLA_DK:(hh + 1) * GLA_DK, :]


def bidir_gla(proj, gates, s0, gla_g, n_batch, seq, row_blk0):
    shared_s0 = s0.shape[0] == 1
    pairs = HB // 2 if seq * HB * GLA_DV * 4 <= 1024 * 1024 else 2
    wqk = pairs * LANE
    wv = pairs * 2 * GLA_DV
    col_q = (2 * HA * 2 * DIFF_HD + HA * DIFF_VD) // wqk
    col_k = col_q + HB * GLA_DK // wqk
    col_v = (col_k * wqk + HB * GLA_DK) // wv
    col_r = col_v + HB * GLA_DV // wv
    n_tok = n_batch * seq
    return pl.pallas_call(
        functools.partial(_gla_kernel, seq=seq, pairs=pairs),
        grid=(n_batch, HB // 2 // pairs),
        in_specs=[pl.BlockSpec((seq, wqk), lambda b, p: (row_blk0 + b, col_q + p)),
                  pl.BlockSpec((seq, wqk), lambda b, p: (row_blk0 + b, col_k + p)),
                  pl.BlockSpec((seq, wv), lambda b, p: (row_blk0 + b, col_v + p)),
                  pl.BlockSpec((seq, wv), lambda b, p: (row_blk0 + b, col_r + p)),
                  pl.BlockSpec((2, seq, wqk), lambda b, p: (0, row_blk0 + b, p)),
                  pl.BlockSpec((None, 2, 2 * pairs, GLA_DK, GLA_DV),
                               lambda b, p: (0 if shared_s0 else b, 0, p, 0, 0)),
                  pl.BlockSpec((1, GLA_DV), lambda b, p: (0, 0))],
        out_specs=[pl.BlockSpec((seq, wv), lambda b, p: (b, p)),
                   pl.BlockSpec((None, 2, 2 * pairs, GLA_DK, GLA_DV), lambda b, p: (b, 0, p, 0, 0))],
        out_shape=[jax.ShapeDtypeStruct((n_tok, HB * GLA_DV), BF16),
                   jax.ShapeDtypeStruct((n_batch, 2, HB, GLA_DK, GLA_DV), F32)],
        scratch_shapes=[pltpu.VMEM((seq, wv), F32), pltpu.VMEM((seq, wv), F32),
                        pltpu.VMEM((pairs, 2, 2 * GLA_DV, LANE), F32)],
        compiler_params=_params(("arbitrary", "arbitrary")),
        name="bidir_gla",
    )(proj, proj, proj, proj, gates, s0, gla_g.reshape(1, GLA_DV))


def _head_slice(kh, gq):
    h = kh * Q_PER_KV + gq
    return slice(h * LANE, (h + 1) * LANE)


def _sink_column(sink_ref, kh, rows):
    return jnp.concatenate([jnp.full((rows, 1), sink_ref[kh * Q_PER_KV + gq], F32) for gq in range(Q_PER_KV)],
                           axis=0)


def _gqa_ctx_kernel(sink_ref, q_ref, k_ref, v_ref, o_ref, ko_ref, vo_ref):
    scale = HEAD_DIM ** -0.5
    ko_ref[...] = k_ref[...]
    vo_ref[...] = v_ref[...]
    for bb in range(CTX_BATCHES_PER_STEP):
        rows = slice(bb * SEQ, (bb + 1) * SEQ)
        for kh in range(KV_HEADS):
            ks = slice(kh * LANE, (kh + 1) * LANE)
            k_bf = k_ref[rows, ks].astype(BF16)
            v1_bf = _with_ones(v_ref[rows, ks].astype(BF16))
            for gq in range(Q_PER_KV):
                hs = _head_slice(kh, gq)
                sink = sink_ref[kh * Q_PER_KV + gq]
                s = _dot_nt((q_ref[rows, hs] * scale).astype(BF16), k_bf)
                m = jnp.maximum(jnp.max(s, axis=-1, keepdims=True), sink)
                ov = _dot(jnp.exp(s - m).astype(BF16), v1_bf)
                o_ref[rows, hs] = (ov[:, :LANE] * (1.0 / (ov[:, LANE:] + jnp.exp(sink - m)))).astype(o_ref.dtype)


def gqa_context(proj, sinks):
    wq = N_HEADS * HEAD_DIM
    wkv = KV_HEADS * HEAD_DIM
    rows = CTX_BATCHES_PER_STEP * SEQ

    def kv(c):
        return pl.BlockSpec((rows, wkv), lambda b: (b, c))

    return pl.pallas_call(
        _gqa_ctx_kernel,
        grid=(BATCH // CTX_BATCHES_PER_STEP,),
        in_specs=[pl.BlockSpec(memory_space=pltpu.SMEM),
                  pl.BlockSpec((rows, wq), lambda b: (b, 0)), kv(wq // wkv), kv(wq // wkv + 1)],
        out_specs=[pl.BlockSpec((rows, wq), lambda b: (b, 0)), kv(0), kv(0)],
        out_shape=[jax.ShapeDtypeStruct((TP, wq), BF16), jax.ShapeDtypeStruct((TP, wkv), F32),
                   jax.ShapeDtypeStruct((TP, wkv), F32)],
        compiler_params=_params(("arbitrary",)),
        name="gqa_context",
    )(sinks, proj, proj, proj)


def _gqa_lat_kernel(sink_ref, q_ref, k_ref, v_ref, ck_ref, cv_ref, kc_ref, kl_ref, kh_ref,
                    qc_ref, ql_ref, qh_ref, o_ref, kw_scr, vw_scr, kc_scr, vc_scr):
    quarter = HEAD_DIM // 4
    w = WINDOW
    wkv = KV_HEADS * HEAD_DIM

    @pl.when(pl.program_id(1) == 0)
    def _():
        kw_scr[0:w, :] = jnp.zeros((w, wkv), BF16)
        kw_scr[w + DEC_SEQ:, :] = jnp.zeros((w, wkv), BF16)
        vw_scr[0:w, :] = jnp.zeros((w, 2 * wkv), BF16)
        vw_scr[w + DEC_SEQ:, :] = jnp.zeros((w, 2 * wkv), BF16)
        for kh in range(KV_HEADS):
            ks = slice(kh * LANE, (kh + 1) * LANE)
            vs = slice(2 * kh * LANE, 2 * (kh + 1) * LANE)
            kw_scr[w:w + DEC_SEQ, ks] = _rope(k_ref[:, ks], kc_ref[...], kl_ref[...], kh_ref[...],
                                              quarter).astype(BF16)
            vw_scr[w:w + DEC_SEQ, vs] = _with_ones(v_ref[:, ks].astype(BF16))
            vc_scr[:, vs] = _with_ones(cv_ref[:, ks].astype(BF16))
        kc_scr[...] = ck_ref[...].astype(BF16)

    qi = lax.broadcasted_iota(jnp.int32, (Q_PER_KV * w, 3 * w), 0) % w
    kj = lax.broadcasted_iota(jnp.int32, (Q_PER_KV * w, 3 * w), 1)
    scale = HEAD_DIM ** -0.5
    for blk in range(LAT_QBLOCKS_PER_STEP):
        n = pl.program_id(1) * LAT_QBLOCKS_PER_STEP + blk
        rows = slice(blk * w, (blk + 1) * w)
        win = pl.ds(pl.multiple_of(n * w, w), 3 * w)
        kpos = n * w - w + kj
        valid = (kj >= qi) & (kj <= qi + 2 * w) & (kpos >= 0) & (kpos < DEC_SEQ)
        for kh in range(KV_HEADS):
            ks = slice(kh * LANE, (kh + 1) * LANE)
            vs = slice(2 * kh * LANE, 2 * (kh + 1) * LANE)
            q = jnp.concatenate(
                [(_rope(q_ref[rows, _head_slice(kh, gq)], qc_ref[rows, :], ql_ref[rows, :], qh_ref[rows, :],
                        quarter) * scale).astype(BF16) for gq in range(Q_PER_KV)], axis=0)
            sink = _sink_column(sink_ref, kh, w)
            s_c = _dot_nt(q, kc_scr[:, ks])
            s_w = jnp.where(valid, _dot_nt(q, kw_scr[win, ks]), NEG_INF)
            m = jnp.maximum(jnp.maximum(jnp.max(s_c, axis=-1, keepdims=True),
                                        jnp.max(s_w, axis=-1, keepdims=True)), sink)
            ov = (_dot(jnp.exp(s_c - m).astype(BF16), vc_scr[:, vs])
                  + _dot(jnp.exp(s_w - m).astype(BF16), vw_scr[win, vs]))
            o = ov[:, :LANE] * (1.0 / (ov[:, LANE:] + jnp.exp(sink - m)))
            for gq in range(Q_PER_KV):
                o_ref[rows, _head_slice(kh, gq)] = o[gq * w:(gq + 1) * w].astype(o_ref.dtype)


def gqa_latent(proj, cache_k, cache_v, tables, sinks):
    wq = N_HEADS * HEAD_DIM
    wkv = KV_HEADS * HEAD_DIM
    tq = LAT_QBLOCKS_PER_STEP * WINDOW
    nq = DEC_SEQ // tq
    row0 = TP // tq
    kv_row0 = TP // DEC_SEQ
    cos, sin_lo, sin_hi = tables
    full = pl.BlockSpec((DEC_SEQ, LANE), lambda b, i: (0, 0))
    qtab = pl.BlockSpec((tq, LANE), lambda b, i: (i, 0))
    cache = pl.BlockSpec((None, PAST_LEN, wkv), lambda b, i: (b, 0, 0))
    return pl.pallas_call(
        _gqa_lat_kernel,
        grid=(DEC_BATCH, nq),
        in_specs=[pl.BlockSpec(memory_space=pltpu.SMEM),
                  pl.BlockSpec((tq, wq), lambda b, i: (row0 + b * nq + i, 0)),
                  pl.BlockSpec((DEC_SEQ, wkv), lambda b, i: (kv_row0 + b, wq // wkv)),
                  pl.BlockSpec((DEC_SEQ, wkv), lambda b, i: (kv_row0 + b, wq // wkv + 1)),
                  cache, cache, full, full, full, qtab, qtab, qtab],
        out_specs=pl.BlockSpec((tq, wq), lambda b, i: (b * nq + i, 0)),
        out_shape=jax.ShapeDtypeStruct((TS, wq), BF16),
        scratch_shapes=[pltpu.VMEM((DEC_SEQ + 2 * WINDOW, wkv), BF16),
                        pltpu.VMEM((DEC_SEQ + 2 * WINDOW, 2 * wkv), BF16),
                        pltpu.VMEM((PAST_LEN, wkv), BF16),
                        pltpu.VMEM((PAST_LEN, 2 * wkv), BF16)],
        compiler_params=_params(("arbitrary", "arbitrary")),
        name="gqa_latent",
    )(sinks, proj, proj, proj, cache_k, cache_v, cos, sin_lo, sin_hi, cos, sin_lo, sin_hi)


def _even_layer(xc, xs, mod, j, layer, norm_g, w_in_even, lam_vec, subln_g, w_gate1, w_gate2, b_gate, gla_norm_g,
                w_out_even, cache_a_k, cache_a_v, state_b, tables):
    lam_init = 0.8 - 0.6 * math.exp(-0.3 * layer)
    ng = HB * GLA_DK
    wg1 = jnp.zeros((D_MODEL, LANE), F32).at[:, :2 * GLA_GATE_RANK].set(
        jnp.concatenate([w_gate1[j, 0], w_gate1[j, 1]], axis=-1)).astype(BF16)
    wg2 = jnp.zeros((2, LANE, ng), F32)
    for e in range(2):
        wg2 = wg2.at[e, e * GLA_GATE_RANK:(e + 1) * GLA_GATE_RANK, :].set(w_gate2[j, e])
    h, gates = norm_mod_gate(xc, xs, norm_g[layer, 0], mod, 0, 1, wg1, wg2.astype(BF16),
                             b_gate[j].reshape(2, 1, ng))
    proj = ws_matmul(h, w_in_even, j)
    ya_c, new_k, new_v = diff_attention_context(proj, lam_vec, subln_g[j], lam_init)
    ya_s = diff_attention_latent(proj, cache_a_k[:, j].reshape(DEC_BATCH, PAST_LEN, HA * 2 * DIFF_HD),
                                 cache_a_v[:, j].reshape(DEC_BATCH, PAST_LEN, HA * DIFF_VD),
                                 tables, lam_vec, subln_g[j], lam_init)
    yb_c, s_fin = bidir_gla(proj, gates, jnp.zeros((1, 2, HB, GLA_DK, GLA_DV), F32), gla_norm_g[j],
                            BATCH, SEQ, 0)
    yb_s, _ = bidir_gla(proj, gates, state_b[:, j], gla_norm_g[j], DEC_BATCH, DEC_SEQ, TP // DEC_SEQ)
    x = ws_matmul_residual([(ya_c, ya_s), (yb_c, yb_s)], w_out_even, j, (xc, xs), mod, 2)
    return (x, new_k.reshape(BATCH, SEQ, HA, 2 * DIFF_HD), new_v.reshape(BATCH, SEQ, HA, DIFF_VD), s_fin)


def _odd_layer(x, mod, j, layer, norm_g, w_in_odd, sinks, w_out_odd, cache_c_k, cache_c_v, tables):
    wkv = KV_HEADS * HEAD_DIM
    proj = ws_matmul(norm_mod(x, norm_g[layer, 0], mod, 0, 1), w_in_odd, j)
    o_c, new_k, new_v = gqa_context(proj, sinks[j])
    o_s = gqa_latent(proj, cache_c_k[:, j].reshape(DEC_BATCH, PAST_LEN, wkv),
                     cache_c_v[:, j].reshape(DEC_BATCH, PAST_LEN, wkv), tables, sinks[j])
    x = ws_matmul_residual([(o_c, o_s)], w_out_odd, j, (x,), mod, 2)
    return (x, new_k.reshape(BATCH, SEQ, KV_HEADS, HEAD_DIM), new_v.reshape(BATCH, SEQ, KV_HEADS, HEAD_DIM))


def _ffn(x, mod, layer, norm_g, w_ffn_in, w_ffn_out):
    act = ws_swiglu(norm_mod(x, norm_g[layer, 1], mod, 3, 4), w_ffn_in, layer)
    return ws_matmul_residual([(act,)], w_ffn_out, layer, (x,), mod, 5, tm=512, tn=512)


def kernel(x_prompt, x_sample, c, cache_a_k, cache_a_v, state_b, cache_c_k, cache_c_v, c_ctx, w_ada, b_ada,
           norm_g, w_in_even, lam_q1, lam_k1, lam_q2, lam_k2, subln_g, w_gate1, w_gate2, b_gate, gla_norm_g,
           w_out_even, w_in_odd, sinks, w_out_odd, w_ffn_in, w_ffn_out, final_norm_g):
    assert DEPTH == 2, "layer 0 reads the two input streams, layer 1 the fused token matrix"
    cvec = jnp.concatenate([c_ctx[None, :], c, jnp.zeros((MOD_ROWS - 1 - DEC_BATCH, D_MODEL), F32)], axis=0)
    mods = adaln(cvec, w_ada, b_ada).reshape(DEPTH, MOD_ROWS, 1, 6 * D_MODEL)
    tab_d = rope_tables(DEC_SEQ, DIFF_HD)
    tab_h = rope_tables(DEC_SEQ, HEAD_DIM)

    lam_vec = jnp.stack([lam_q1[0], lam_k1[0], lam_q2[0], lam_k2[0]], axis=0)
    x, ak, av, sb = _even_layer(x_prompt.reshape(TP, D_MODEL), x_sample.reshape(TS, D_MODEL), mods[0], 0, 0,
                                norm_g, w_in_even, lam_vec, subln_g, w_gate1, w_gate2, b_gate, gla_norm_g,
                                w_out_even, cache_a_k, cache_a_v, state_b, tab_d)
    x = _ffn(x, mods[0], 0, norm_g, w_ffn_in, w_ffn_out)
    x, ck, cv = _odd_layer(x, mods[1], 0, 1, norm_g, w_in_odd, sinks, w_out_odd, cache_c_k, cache_c_v, tab_h)
    x = _ffn(x, mods[1], 1, norm_g, w_ffn_in, w_ffn_out)

    y_prompt = final_norm(x, final_norm_g, 0, TP).reshape(BATCH, SEQ, D_MODEL)
    y_sample = final_norm(x, final_norm_g, TP, TS).reshape(DEC_BATCH, DEC_SEQ, D_MODEL)
    return (y_prompt, y_sample, ak[:, None], av[:, None], sb[:, None], ck[:, None], cv[:, None])
```

```python
import functools
import math

import jax
import jax.numpy as jnp
from jax import lax
from jax.experimental import pallas as pl
from jax.experimental.pallas import tpu as pltpu

D_MODEL = 2048
BATCH = 32
SEQ = 256
DEPTH = 2
DEC_BATCH = 2
DEC_SEQ = 2048
PAST_LEN = 512
GRID_W = 64
HEAD_DIM = 128
N_HEADS = D_MODEL // HEAD_DIM
HA = N_HEADS // 2
DIFF_HD = HEAD_DIM // 2
DIFF_VD = HEAD_DIM
HB = N_HEADS // 2
GLA_DK = HEAD_DIM // 2
GLA_DV = HEAD_DIM
GLA_GATE_RANK = 16
GLA_TAU = 16.0
GLA_CHUNK = 64
GLA_SUPER = 4
CTX_BATCHES_PER_STEP = 4
LAT_QBLOCKS_PER_STEP = 2
DIFF_LAT_CHAINS = 4
KV_HEADS = N_HEADS // 4
Q_PER_KV = N_HEADS // KV_HEADS
WINDOW = 128
D_FF = -(-8 * D_MODEL // (3 * 256)) * 256
ROPE_BASE = 10000.0
EPS = 1e-6
NEG_INF = -1e30
EVEN_IN = 2 * HA * 2 * DIFF_HD + HA * DIFF_VD + 2 * HB * GLA_DK + 2 * HB * GLA_DV
ODD_IN = (N_HEADS + 2 * KV_HEADS) * HEAD_DIM

TP = BATCH * SEQ
TS = DEC_BATCH * DEC_SEQ
T = TP + TS
LANE = 128
MOD_ROWS = 8
NORM_ROWS = 16
NORM_UNROLL = 8
VMEM_LIMIT = 56 * 1024 * 1024

F32 = jnp.float32
BF16 = jnp.bfloat16


def _params(semantics, vmem=VMEM_LIMIT):
    return pltpu.CompilerParams(dimension_semantics=semantics, vmem_limit_bytes=vmem)


def _dot(a, b):
    return jnp.dot(a, b, preferred_element_type=F32)


def _dot_nt(a, b):
    return lax.dot_general(a, b, (((1,), (1,)), ((), ())), preferred_element_type=F32)


def _group_of_tile(i, tm):
    r = i * tm
    return jnp.where(r < TP, 0, 1 + (r - TP) // DEC_SEQ)


def _rms(x, g):
    return (x * lax.rsqrt(jnp.mean(x * x, axis=-1, keepdims=True) + EPS)) * g


def _adaln_kernel(c_ref, w_ref, b_ref, o_ref):
    c = c_ref[...]
    s = c * jax.nn.sigmoid(c)
    o_ref[...] = _dot(s.astype(BF16), w_ref[...].astype(BF16)) + b_ref[...]


def adaln(cvec, w_ada, b_ada, tn=512):
    n = w_ada.shape[-1]
    return pl.pallas_call(
        _adaln_kernel,
        grid=(DEPTH, n // tn),
        in_specs=[
            pl.BlockSpec((MOD_ROWS, D_MODEL), lambda l, j: (0, 0)),
            pl.BlockSpec((None, D_MODEL, tn), lambda l, j: (l, 0, j)),
            pl.BlockSpec((None, 1, tn), lambda l, j: (l, 0, j)),
        ],
        out_specs=pl.BlockSpec((None, MOD_ROWS, tn), lambda l, j: (l, 0, j)),
        out_shape=jax.ShapeDtypeStruct((DEPTH, MOD_ROWS, n), F32),
        compiler_params=_params(("arbitrary", "arbitrary")),
        name="adaln",
    )(cvec, w_ada, b_ada.reshape(DEPTH, 1, n))


def _by_row_groups(x_ref, o_ref, fn):
    def body(r, carry):
        rs = pl.ds(pl.multiple_of(r * NORM_ROWS, NORM_ROWS), NORM_ROWS)
        o_ref[rs, :] = fn(x_ref[rs, :]).astype(o_ref.dtype)
        return carry

    lax.fori_loop(0, x_ref.shape[0] // NORM_ROWS, body, 0, unroll=NORM_UNROLL)


def _norm_mod_into(h_ref, x_ref, g_ref, shift_ref, scale_ref):
    _by_row_groups(x_ref, h_ref, lambda x: _rms(x, g_ref[...]) * (1 + scale_ref[...]) + shift_ref[...])


def _norm_mod_kernel(x_ref, g_ref, shift_ref, scale_ref, h_ref):
    _norm_mod_into(h_ref, x_ref, g_ref, shift_ref, scale_ref)


def _norm_mod_gate_kernel(xc_ref, xs_ref, g_ref, shift_ref, scale_ref, wg1_ref, wg2_ref, bg_ref,
                          h_ref, gate_ref, *, ctx_tiles):
    @pl.when(pl.program_id(0) < ctx_tiles)
    def _():
        _norm_mod_into(h_ref, xc_ref, g_ref, shift_ref, scale_ref)

    @pl.when(pl.program_id(0) >= ctx_tiles)
    def _():
        _norm_mod_into(h_ref, xs_ref, g_ref, shift_ref, scale_ref)

    low = _dot(h_ref[...], wg1_ref[...]).astype(BF16)
    for e in range(2):
        logit = _dot(low, wg2_ref[e]) + bg_ref[e]
        log_sig = jnp.minimum(logit, 0.0) - jnp.log(1.0 + jnp.exp(-jnp.abs(logit)))
        gate_ref[e] = log_sig / GLA_TAU


def _mod_specs(tm, shift_idx, scale_idx):
    def spec(idx):
        return pl.BlockSpec((None, 1, D_MODEL), lambda i: (_group_of_tile(i, tm), 0, idx))
    return [pl.BlockSpec((1, D_MODEL), lambda i: (0, 0)), spec(shift_idx), spec(scale_idx)]


def norm_mod(x, g, mod, shift_idx, scale_idx, tm=1024):
    return pl.pallas_call(
        _norm_mod_kernel,
        grid=(T // tm,),
        in_specs=[pl.BlockSpec((tm, D_MODEL), lambda i: (i, 0))] + _mod_specs(tm, shift_idx, scale_idx),
        out_specs=pl.BlockSpec((tm, D_MODEL), lambda i: (i, 0)),
        out_shape=jax.ShapeDtypeStruct((T, D_MODEL), BF16),
        compiler_params=_params(("arbitrary",)),
        name="norm_mod",
    )(x, g.reshape(1, D_MODEL), mod, mod)


def norm_mod_gate(xc, xs, g, mod, shift_idx, scale_idx, wg1, wg2, bg, tm=512):
    ng = HB * GLA_DK
    ctx_tiles = TP // tm
    return pl.pallas_call(
        functools.partial(_norm_mod_gate_kernel, ctx_tiles=ctx_tiles),
        grid=(T // tm,),
        in_specs=[pl.BlockSpec((tm, D_MODEL), lambda i: (jnp.minimum(i, ctx_tiles - 1), 0)),
                  pl.BlockSpec((tm, D_MODEL), lambda i: (jnp.maximum(i - ctx_tiles, 0), 0))]
        + _mod_specs(tm, shift_idx, scale_idx)
        + [pl.BlockSpec((D_MODEL, LANE), lambda i: (0, 0)),
           pl.BlockSpec((2, LANE, ng), lambda i: (0, 0, 0)),
           pl.BlockSpec((2, 1, ng), lambda i: (0, 0, 0))],
        out_specs=[pl.BlockSpec((tm, D_MODEL), lambda i: (i, 0)),
                   pl.BlockSpec((2, tm, ng), lambda i: (0, i, 0))],
        out_shape=[jax.ShapeDtypeStruct((T, D_MODEL), BF16), jax.ShapeDtypeStruct((2, T, ng), F32)],
        compiler_params=_params(("arbitrary",)),
        name="norm_mod_gate",
    )(xc, xs, g.reshape(1, D_MODEL), mod, mod, wg1, wg2, bg)


def _cast_weights(w_refs, w_scrs):
    @pl.when(pl.program_id(1) == 0)
    def _():
        for w_ref, w_scr in zip(w_refs, w_scrs):
            w_scr[...] = w_ref[...].astype(BF16)


def _ws_plain_kernel(h_ref, w_ref, o_ref, w_scr):
    _cast_weights([w_ref], [w_scr])
    o_ref[...] = _dot(h_ref[...], w_scr[...])


def _ws_swiglu_kernel(h_ref, wg_ref, wu_ref, o_ref, wg_scr, wu_scr):
    _cast_weights([wg_ref, wu_ref], [wg_scr, wu_scr])
    h = h_ref[...]
    gate = _dot(h, wg_scr[...])
    up = _dot(h, wu_scr[...])
    o_ref[...] = (gate * jax.nn.sigmoid(gate) * up).astype(o_ref.dtype)


def _ws_res_kernel(*refs, lhs_arity, x_arity, ctx_tiles):
    n_l = sum(lhs_arity)
    w_ref = refs[n_l]
    x_refs = refs[n_l + 1:n_l + 1 + x_arity]
    gate_ref, o_ref, w_scr = refs[n_l + 1 + x_arity:]
    groups, at = [], 0
    for a in lhs_arity:
        groups.append(refs[at:at + a])
        at += a
    _cast_weights([w_ref], [w_scr])

    def emit(side):
        acc, k0 = None, 0
        for grp in groups:
            ref = grp[side] if len(grp) == 2 else grp[0]
            part = _dot(ref[...], w_scr[k0:k0 + ref.shape[1], :])
            acc = part if acc is None else acc + part
            k0 += ref.shape[1]
        x_ref = x_refs[side] if x_arity == 2 else x_refs[0]
        o_ref[...] = x_ref[...] + gate_ref[...] * acc

    if x_arity == 2 or 2 in lhs_arity:
        @pl.when(pl.program_id(1) < ctx_tiles)
        def _():
            emit(0)

        @pl.when(pl.program_id(1) >= ctx_tiles)
        def _():
            emit(1)
    else:
        emit(0)


def _rows_specs(arrays, tm, width, col):
    if len(arrays) == 1:
        return [pl.BlockSpec((tm, width), lambda j, i: (i, col(j)))]
    ctx_tiles = TP // tm
    return [pl.BlockSpec((tm, width), lambda j, i: (jnp.minimum(i, ctx_tiles - 1), col(j))),
            pl.BlockSpec((tm, width), lambda j, i: (jnp.maximum(i - ctx_tiles, 0), col(j)))]


def ws_matmul(h, w_stack, layer, tm=1024, tn=1024):
    kdim, n = w_stack.shape[1:]
    return pl.pallas_call(
        _ws_plain_kernel,
        grid=(n // tn, T // tm),
        in_specs=[pl.BlockSpec((tm, kdim), lambda j, i: (i, 0)),
                  pl.BlockSpec((None, kdim, tn), lambda j, i: (layer, 0, j))],
        out_specs=pl.BlockSpec((tm, tn), lambda j, i: (i, j)),
        out_shape=jax.ShapeDtypeStruct((T, n), F32),
        scratch_shapes=[pltpu.VMEM((kdim, tn), BF16)],
        compiler_params=_params(("arbitrary", "arbitrary")),
        name="ws_matmul",
    )(h, w_stack)


def ws_swiglu(h, w_stack, layer, tm=1024, tn=512):
    kdim = w_stack.shape[1]
    nf = D_FF // tn
    return pl.pallas_call(
        _ws_swiglu_kernel,
        grid=(nf, T // tm),
        in_specs=[pl.BlockSpec((tm, kdim), lambda j, i: (i, 0)),
                  pl.BlockSpec((None, kdim, tn), lambda j, i: (layer, 0, j)),
                  pl.BlockSpec((None, kdim, tn), lambda j, i: (layer, 0, j + nf))],
        out_specs=pl.BlockSpec((tm, tn), lambda j, i: (i, j)),
        out_shape=jax.ShapeDtypeStruct((T, D_FF), BF16),
        scratch_shapes=[pltpu.VMEM((kdim, tn), BF16), pltpu.VMEM((kdim, tn), BF16)],
        compiler_params=_params(("arbitrary", "arbitrary")),
        name="ws_swiglu",
    )(h, w_stack, w_stack)


def ws_matmul_residual(lhs_groups, w_stack, layer, x, mod, gate_idx, tm=512, tn=1024):
    kdim, n = w_stack.shape[1:]
    gate_blk = n // tn
    specs, args = [], []
    for grp in lhs_groups:
        specs += _rows_specs(grp, tm, grp[0].shape[1], lambda j: 0)
        args += list(grp)
    specs.append(pl.BlockSpec((None, kdim, tn), lambda j, i: (layer, 0, j)))
    specs += _rows_specs(x, tm, tn, lambda j: j)
    specs.append(pl.BlockSpec((None, 1, tn), lambda j, i: (_group_of_tile(i, tm), 0, gate_idx * gate_blk + j)))
    return pl.pallas_call(
        functools.partial(_ws_res_kernel, lhs_arity=tuple(len(grp) for grp in lhs_groups), x_arity=len(x),
                          ctx_tiles=TP // tm),
        grid=(n // tn, T // tm),
        in_specs=specs,
        out_specs=pl.BlockSpec((tm, tn), lambda j, i: (i, j)),
        out_shape=jax.ShapeDtypeStruct((T, n), F32),
        scratch_shapes=[pltpu.VMEM((kdim, tn), BF16)],
        compiler_params=_params(("arbitrary", "arbitrary")),
        name="ws_matmul_residual",
    )(*args, w_stack, *x, mod)


def _final_norm_kernel(x_ref, g_ref, o_ref):
    o_ref[...] = _rms(x_ref[...], g_ref[...])


def final_norm(x, g, row0, n_rows, tm=1024):
    blk0 = row0 // tm
    return pl.pallas_call(
        _final_norm_kernel,
        grid=(n_rows // tm,),
        in_specs=[pl.BlockSpec((tm, D_MODEL), lambda i: (blk0 + i, 0)),
                  pl.BlockSpec((1, D_MODEL), lambda i: (0, 0))],
        out_specs=pl.BlockSpec((tm, D_MODEL), lambda i: (i, 0)),
        out_shape=jax.ShapeDtypeStruct((n_rows, D_MODEL), F32),
        compiler_params=_params(("arbitrary",)),
        name="final_norm",
    )(x, g.reshape(1, D_MODEL))


def _grid_angles(n_tok, rot_dim):
    t = jnp.arange((n_tok // GRID_W) * GRID_W)
    row = (t // GRID_W).astype(F32)
    col = (t % GRID_W).astype(F32)
    half = rot_dim // 2
    inv = ROPE_BASE ** (-jnp.arange(0, half, 2, dtype=F32) / half)
    return row[:, None] * inv[None], col[:, None] * inv[None]


def rope_tables(n_tok, rot_dim):
    ang_row, ang_col = _grid_angles(n_tok, rot_dim)
    zeros = jnp.zeros_like(ang_row)
    reps = LANE // rot_dim

    def lanes(r1, r2, c1, c2):
        return jnp.tile(jnp.concatenate([r1, r2, c1, c2], axis=-1), (1, reps))

    cr, sr, cc, sc = jnp.cos(ang_row), jnp.sin(ang_row), jnp.cos(ang_col), jnp.sin(ang_col)
    return lanes(cr, cr, cc, cc), lanes(-sr, zeros, -sc, zeros), lanes(zeros, sr, zeros, sc)


def _rope(x, cos, sin_lo, sin_hi, quarter):
    return (x * cos + pltpu.roll(x, LANE - quarter, 1) * sin_lo + pltpu.roll(x, quarter, 1) * sin_hi)


def _diff_lambda(lam_ref, lam_init):
    e1 = jnp.exp(jnp.sum(lam_ref[0:1, :] * lam_ref[1:2, :], axis=-1, keepdims=True))
    e2 = jnp.exp(jnp.sum(lam_ref[2:3, :] * lam_ref[3:4, :], axis=-1, keepdims=True))
    return e1 - e2 + lam_init


def _with_ones(v_bf):
    return jnp.concatenate([v_bf, jnp.ones(v_bf.shape, BF16)], axis=1)


def _softmax_av(q_bf, k_bf, v1_bf):
    s = _dot_nt(q_bf, k_bf)
    e = jnp.exp(s - jnp.max(s, axis=-1, keepdims=True))
    if v1_bf.shape[1] == LANE:
        return _dot(e.astype(BF16), v1_bf) * (1.0 / jnp.sum(e, axis=-1, keepdims=True))
    ov = _dot(e.astype(BF16), v1_bf)
    return ov[:, :LANE] * (1.0 / ov[:, LANE:])


def _diff_head(q, k_bf, v1_bf, lam, subln_g, lam_init, stack_maps):
    n = q.shape[0]
    first = lax.broadcasted_iota(jnp.int32, q.shape, 1) < DIFF_HD
    q0 = jnp.where(first, q, 0.0).astype(BF16)
    q1 = jnp.where(first, 0.0, q).astype(BF16)
    if stack_maps:
        r = _softmax_av(jnp.concatenate([q0, q1], axis=0), k_bf, v1_bf)
        o = r[:n] - lam * r[n:]
    else:
        o = _softmax_av(q0, k_bf, v1_bf) - lam * _softmax_av(q1, k_bf, v1_bf)
    return _rms(o, subln_g) * (1.0 - lam_init)


def _diff_ctx_kernel(q_ref, k_ref, v_ref, lam_ref, sg_ref, o_ref, ko_ref, vo_ref, *, lam_init):
    lam = _diff_lambda(lam_ref, lam_init)
    scale = DIFF_HD ** -0.5
    n_rows = CTX_BATCHES_PER_STEP * SEQ
    for h in range(HA):
        sl = slice(h * LANE, (h + 1) * LANE)
        ko_ref[pl.ds(h, n_rows, stride=HA), :] = k_ref[:, sl]
        vo_ref[pl.ds(h, n_rows, stride=HA), :] = v_ref[:, sl]
    for bb in range(CTX_BATCHES_PER_STEP):
        rows = slice(bb * SEQ, (bb + 1) * SEQ)
        for h in range(HA):
            sl = slice(h * LANE, (h + 1) * LANE)
            y = _diff_head(q_ref[rows, sl] * scale, k_ref[rows, sl].astype(BF16),
                           _with_ones(v_ref[rows, sl].astype(BF16)), lam, sg_ref[...], lam_init, stack_maps=True)
            o_ref[rows, sl] = y.astype(o_ref.dtype)


def diff_attention_context(proj, lam_vec, subln_g, lam_init):
    w = HA * LANE

    def blk(c):
        return pl.BlockSpec((CTX_BATCHES_PER_STEP * SEQ, w), lambda b: (b, c))

    cache = pl.BlockSpec((CTX_BATCHES_PER_STEP * SEQ * HA, LANE), lambda b: (b, 0))
    return pl.pallas_call(
        functools.partial(_diff_ctx_kernel, lam_init=lam_init),
        grid=(BATCH // CTX_BATCHES_PER_STEP,),
        in_specs=[blk(0), blk(1), blk(2),
                  pl.BlockSpec((4, DIFF_HD), lambda b: (0, 0)),
                  pl.BlockSpec((1, DIFF_VD), lambda b: (0, 0))],
        out_specs=[blk(0), cache, cache],
        out_shape=[jax.ShapeDtypeStruct((TP, w), BF16), jax.ShapeDtypeStruct((TP * HA, LANE), F32),
                   jax.ShapeDtypeStruct((TP * HA, LANE), F32)],
        compiler_params=_params(("arbitrary",)),
        name="diff_attention_context",
    )(proj, proj, proj, lam_vec, subln_g.reshape(1, DIFF_VD))


def _diff_lat_kernel(q_ref, k_ref, v_ref, ck_ref, cv_ref, kc_ref, kl_ref, kh_ref, qc_ref, ql_ref, qh_ref,
                     lam_ref, sg_ref, o_ref, k_scr, v_scr, *, lam_init):
    quarter = DIFF_HD // 4

    @pl.when(pl.program_id(2) == 0)
    def _():
        k_scr[0:PAST_LEN, :] = ck_ref[...].astype(BF16)
        k_scr[PAST_LEN:, :] = _rope(k_ref[...], kc_ref[...], kl_ref[...], kh_ref[...], quarter).astype(BF16)
        v_scr[0:PAST_LEN, :] = cv_ref[...].astype(BF16)
        v_scr[PAST_LEN:, :] = v_ref[...].astype(BF16)

    lam = _diff_lambda(lam_ref, lam_init)
    sub = q_ref.shape[0] // DIFF_LAT_CHAINS
    for c in range(DIFF_LAT_CHAINS):
        rs = slice(c * sub, (c + 1) * sub)
        q = _rope(q_ref[rs, :], qc_ref[rs, :], ql_ref[rs, :], qh_ref[rs, :], quarter) * (DIFF_HD ** -0.5)
        y = _diff_head(q, k_scr[...], v_scr[...], lam, sg_ref[...], lam_init, stack_maps=False)
        o_ref[rs, :] = y.astype(o_ref.dtype)


def diff_attention_latent(proj, cache_k, cache_v, tables, lam_vec, subln_g, lam_init, tq=1024):
    nq = DEC_SEQ // tq
    row0 = TP // tq
    kv_row0 = TP // DEC_SEQ
    cos, sin_lo, sin_hi = tables
    full = pl.BlockSpec((DEC_SEQ, LANE), lambda b, h, i: (0, 0))
    qtab = pl.BlockSpec((tq, LANE), lambda b, h, i: (i, 0))
    cache = pl.BlockSpec((None, PAST_LEN, LANE), lambda b, h, i: (b, 0, h))
    return pl.pallas_call(
        functools.partial(_diff_lat_kernel, lam_init=lam_init),
        grid=(DEC_BATCH, HA, nq),
        in_specs=[pl.BlockSpec((tq, LANE), lambda b, h, i: (row0 + b * nq + i, h)),
                  pl.BlockSpec((DEC_SEQ, LANE), lambda b, h, i: (kv_row0 + b, HA + h)),
                  pl.BlockSpec((DEC_SEQ, LANE), lambda b, h, i: (kv_row0 + b, 2 * HA + h)),
                  cache, cache, full, full, full, qtab, qtab, qtab,
                  pl.BlockSpec((4, DIFF_HD), lambda b, h, i: (0, 0)),
                  pl.BlockSpec((1, DIFF_VD), lambda b, h, i: (0, 0))],
        out_specs=pl.BlockSpec((tq, LANE), lambda b, h, i: (b * nq + i, h)),
        out_shape=jax.ShapeDtypeStruct((TS, HA * LANE), BF16),
        scratch_shapes=[pltpu.VMEM((PAST_LEN + DEC_SEQ, LANE), BF16),
                        pltpu.VMEM((PAST_LEN + DEC_SEQ, LANE), BF16)],
        compiler_params=_params(("arbitrary", "arbitrary", "arbitrary")),
        name="diff_attention_latent",
    )(proj, proj, proj, cache_k, cache_v, cos, sin_lo, sin_hi, cos, sin_lo, sin_hi,
      lam_vec, subln_g.reshape(1, DIFF_VD))


def _gla_kernel(q_ref, k_ref, v_ref, r_ref, g_ref, s0_ref, gg_ref, y_ref, sfin_ref, of_scr, ob_scr, st_scr,
                *, seq, pairs):
    c = GLA_CHUNK
    sb = GLA_SUPER * c
    n_super = seq // sb
    rows = lax.broadcasted_iota(jnp.int32, (sb, sb), 0)
    cols = lax.broadcasted_iota(jnp.int32, (sb, sb), 1)
    same_chunk = (rows // c) == (cols // c)
    keep = (same_chunk & (rows >= cols), same_chunk & (cols >= rows))
    lane = lax.broadcasted_iota(jnp.int32, (sb, LANE), 1)
    own = (lane < GLA_DK, lane >= GLA_DK)
    chunk_of_row = lax.broadcasted_iota(jnp.int32, (sb, LANE), 0) // c
    st_rows = lax.broadcasted_iota(jnp.int32, (2 * GLA_DV, LANE), 0)
    st_lane = lax.broadcasted_iota(jnp.int32, (2 * GLA_DV, LANE), 1)
    st_own = (st_rows < GLA_DV) == (st_lane < GLA_DK)
    zpad = jnp.zeros((GLA_DK, GLA_DV), F32)

    def expand(x):
        return jnp.concatenate([jnp.where(chunk_of_row == ci, x, 0.0) for ci in range(GLA_SUPER)],
                               axis=1).astype(BF16)

    for p in range(pairs):
        for d in range(2):
            s0 = [s0_ref[d, 2 * p + hh] for hh in range(2)]
            st_scr[p, d] = jnp.concatenate([jnp.concatenate([s0[0], zpad], axis=0).T,
                                            jnp.concatenate([zpad, s0[1]], axis=0).T], axis=0)

    def rows_of(n, d):
        r0 = (n if d == 0 else n_super - 1 - n) * sb
        return pl.ds(r0, sb) if n_super == 1 else pl.ds(pl.multiple_of(r0, sb), sb)

    def chunk_totals(prefix):
        ends = [prefix[ci * c + c - 1:ci * c + c, :] for ci in range(GLA_SUPER)]
        return ends, jnp.concatenate([jnp.broadcast_to(e, (c, LANE)) for e in ends], axis=0)

    def log_decays(n, p):
        kl = slice(p * LANE, (p + 1) * LANE)
        g_b = g_ref[1, rows_of(n, 1), kl]
        prefix = jnp.concatenate([g_ref[0, rows_of(n, 0), kl], g_b], axis=1)
        pos = lax.broadcasted_iota(jnp.int32, prefix.shape, 0) % c
        step = 1
        while step < c:
            prefix = prefix + jnp.where(pos >= step, pltpu.roll(prefix, step, 0), 0.0)
            step *= 2
        ends_f, total_f = chunk_totals(prefix[:, :LANE])
        ends_b, total_b = chunk_totals(prefix[:, LANE:])
        return (prefix[:, :LANE], ends_f, total_f), (total_b - prefix[:, LANE:] + g_b, ends_b, total_b)

    def one_pair_direction(n, p, d, decays):
        kl = slice(p * LANE, (p + 1) * LANE)
        vl = slice(2 * p * GLA_DV, 2 * (p + 1) * GLA_DV)
        rs = rows_of(n, d)
        b, ends, total = decays
        q_in = (q_ref[rs, kl] * (GLA_DK ** -0.5)) * jnp.exp(b)
        kk = k_ref[rs, kl]
        k_in = (kk * jnp.exp(-b)).astype(BF16)
        k_end = kk * jnp.exp(total - b)
        v = v_ref[rs, vl]
        v_bf = v.astype(BF16)
        kv_all = _dot(v.T.astype(BF16), expand(k_end))
        q2 = jnp.concatenate([jnp.where(own[0], q_in, 0.0), jnp.where(own[1], q_in, 0.0)], axis=0)
        a2 = _dot_nt(q2.astype(BF16), k_in)
        o = jnp.concatenate(
            [_dot(jnp.where(keep[d], a2[hh * sb:(hh + 1) * sb], 0.0).astype(BF16),
                  v_bf[:, hh * GLA_DV:(hh + 1) * GLA_DV]) for hh in range(2)], axis=1)
        st = st_scr[p, d]
        entering = [None] * GLA_SUPER
        for ci in (range(GLA_SUPER) if d == 0 else range(GLA_SUPER - 1, -1, -1)):
            entering[ci] = st
            st = st * jnp.exp(ends[ci]) + jnp.where(st_own, kv_all[:, ci * LANE:(ci + 1) * LANE], 0.0)
        st_scr[p, d] = st
        o += _dot_nt(expand(q_in), jnp.concatenate(entering, axis=1).astype(BF16))
        if d == 0:
            of_scr[rs, vl] = o
        else:
            ob_scr[rs, vl] = o

    def super_block(n, carry):
        for p in range(pairs):
            decays = log_decays(n, p)
            for d in range(2):
                one_pair_direction(n, p, d, decays[d])
        return carry

    if n_super == 1:
        super_block(0, 0)
    else:
        lax.fori_loop(0, n_super, super_block, 0)

    for h in range(2 * pairs):
        vs = slice(h * GLA_DV, (h + 1) * GLA_DV)
        r = r_ref[:, vs]
        y = _rms(of_scr[:, vs] + ob_scr[:, vs], gg_ref[...]) * (r * jax.nn.sigmoid(r))
        y_ref[:, vs] = y.astype(y_ref.dtype)
        p, hh = divmod(h, 2)
        for d in range(2):
            sfin_ref[d, h] = st_scr[p, d, hh * GLA_DV:(hh + 1) * GLA_DV, :].T[hh * GLA_DK:(hh + 1) * GLA_DK, :]


def bidir_gla(proj, gates, s0, gla_g, n_batch, seq, row_blk0):
    shared_s0 = s0.shape[0] == 1
    pairs = HB // 2 if seq * HB * GLA_DV * 4 <= 1024 * 1024 else 2
    wqk = pairs * LANE
    wv = pairs * 2 * GLA_DV
    col_q = (2 * HA * 2 * DIFF_HD + HA * DIFF_VD) // wqk
    col_k = col_q + HB * GLA_DK // wqk
    col_v = (col_k * wqk + HB * GLA_DK) // wv
    col_r = col_v + HB * GLA_DV // wv
    n_tok = n_batch * seq
    return pl.pallas_call(
        functools.partial(_gla_kernel, seq=seq, pairs=pairs),
        grid=(n_batch, HB // 2 // pairs),
        in_specs=[pl.BlockSpec((seq, wqk), lambda b, p: (row_blk0 + b, col_q + p)),
                  pl.BlockSpec((seq, wqk), lambda b, p: (row_blk0 + b, col_k + p)),
                  pl.BlockSpec((seq, wv), lambda b, p: (row_blk0 + b, col_v + p)),
                  pl.BlockSpec((seq, wv), lambda b, p: (row_blk0 + b, col_r + p)),
                  pl.BlockSpec((2, seq, wqk), lambda b, p: (0, row_blk0 + b, p)),
                  pl.BlockSpec((None, 2, 2 * pairs, GLA_DK, GLA_DV),
                               lambda b, p: (0 if shared_s0 else b, 0, p, 0, 0)),
                  pl.BlockSpec((1, GLA_DV), lambda b, p: (0, 0))],
        out_specs=[pl.BlockSpec((seq, wv), lambda b, p: (b, p)),
                   pl.BlockSpec((None, 2, 2 * pairs, GLA_DK, GLA_DV), lambda b, p: (b, 0, p, 0, 0))],
        out_shape=[jax.ShapeDtypeStruct((n_tok, HB * GLA_DV), BF16),
                   jax.ShapeDtypeStruct((n_batch, 2, HB, GLA_DK, GLA_DV), F32)],
        scratch_shapes=[pltpu.VMEM((seq, wv), F32), pltpu.VMEM((seq, wv), F32),
                        pltpu.VMEM((pairs, 2, 2 * GLA_DV, LANE), F32)],
        compiler_params=_params(("arbitrary", "arbitrary")),
        name="bidir_gla",
    )(proj, proj, proj, proj, gates, s0, gla_g.reshape(1, GLA_DV))


def _head_slice(kh, gq):
    h = kh * Q_PER_KV + gq
    return slice(h * LANE, (h + 1) * LANE)


def _sink_column(sink_ref, kh, rows):
    return jnp.concatenate([jnp.full((rows, 1), sink_ref[kh * Q_PER_KV + gq], F32) for gq in range(Q_PER_KV)],
                           axis=0)


def _gqa_ctx_kernel(sink_ref, q_ref, k_ref, v_ref, o_ref, ko_ref, vo_ref):
    scale = HEAD_DIM ** -0.5
    n_rows = CTX_BATCHES_PER_STEP * SEQ
    for kh in range(KV_HEADS):
        ks = slice(kh * LANE, (kh + 1) * LANE)
        ko_ref[pl.ds(kh, n_rows, stride=KV_HEADS), :] = k_ref[:, ks]
        vo_ref[pl.ds(kh, n_rows, stride=KV_HEADS), :] = v_ref[:, ks]
    for bb in range(CTX_BATCHES_PER_STEP):
        rows = slice(bb * SEQ, (bb + 1) * SEQ)
        for kh in range(KV_HEADS):
            ks = slice(kh * LANE, (kh + 1) * LANE)
            k_bf = k_ref[rows, ks].astype(BF16)
            v1_bf = _with_ones(v_ref[rows, ks].astype(BF16))
            for gq in range(Q_PER_KV):
                hs = _head_slice(kh, gq)
                sink = sink_ref[kh * Q_PER_KV + gq]
                s = _dot_nt((q_ref[rows, hs] * scale).astype(BF16), k_bf)
                m = jnp.maximum(jnp.max(s, axis=-1, keepdims=True), sink)
                ov = _dot(jnp.exp(s - m).astype(BF16), v1_bf)
                o_ref[rows, hs] = (ov[:, :LANE] * (1.0 / (ov[:, LANE:] + jnp.exp(sink - m)))).astype(o_ref.dtype)


def gqa_context(proj, sinks):
    wq = N_HEADS * HEAD_DIM
    wkv = KV_HEADS * HEAD_DIM
    rows = CTX_BATCHES_PER_STEP * SEQ

    def kv(c):
        return pl.BlockSpec((rows, wkv), lambda b: (b, c))

    cache = pl.BlockSpec((rows * KV_HEADS, LANE), lambda b: (b, 0))
    return pl.pallas_call(
        _gqa_ctx_kernel,
        grid=(BATCH // CTX_BATCHES_PER_STEP,),
        in_specs=[pl.BlockSpec(memory_space=pltpu.SMEM),
                  pl.BlockSpec((rows, wq), lambda b: (b, 0)), kv(wq // wkv), kv(wq // wkv + 1)],
        out_specs=[pl.BlockSpec((rows, wq), lambda b: (b, 0)), cache, cache],
        out_shape=[jax.ShapeDtypeStruct((TP, wq), BF16), jax.ShapeDtypeStruct((TP * KV_HEADS, LANE), F32),
                   jax.ShapeDtypeStruct((TP * KV_HEADS, LANE), F32)],
        compiler_params=_params(("arbitrary",)),
        name="gqa_context",
    )(sinks, proj, proj, proj)


def _gqa_lat_kernel(sink_ref, q_ref, k_ref, v_ref, ck_ref, cv_ref, kc_ref, kl_ref, kh_ref,
                    qc_ref, ql_ref, qh_ref, o_ref, kw_scr, vw_scr, kc_scr, vc_scr):
    quarter = HEAD_DIM // 4
    w = WINDOW
    wkv = KV_HEADS * HEAD_DIM

    @pl.when(pl.program_id(1) == 0)
    def _():
        kw_scr[0:w, :] = jnp.zeros((w, wkv), BF16)
        kw_scr[w + DEC_SEQ:, :] = jnp.zeros((w, wkv), BF16)
        vw_scr[0:w, :] = jnp.zeros((w, 2 * wkv), BF16)
        vw_scr[w + DEC_SEQ:, :] = jnp.zeros((w, 2 * wkv), BF16)
        for kh in range(KV_HEADS):
            ks = slice(kh * LANE, (kh + 1) * LANE)
            vs = slice(2 * kh * LANE, 2 * (kh + 1) * LANE)
            kw_scr[w:w + DEC_SEQ, ks] = _rope(k_ref[:, ks], kc_ref[...], kl_ref[...], kh_ref[...],
                                              quarter).astype(BF16)
            vw_scr[w:w + DEC_SEQ, vs] = _with_ones(v_ref[:, ks].astype(BF16))
            vc_scr[:, vs] = _with_ones(cv_ref[:, ks].astype(BF16))
        kc_scr[...] = ck_ref[...].astype(BF16)

    qi = lax.broadcasted_iota(jnp.int32, (Q_PER_KV * w, 3 * w), 0) % w
    kj = lax.broadcasted_iota(jnp.int32, (Q_PER_KV * w, 3 * w), 1)
    scale = HEAD_DIM ** -0.5
    for blk in range(LAT_QBLOCKS_PER_STEP):
        n = pl.program_id(1) * LAT_QBLOCKS_PER_STEP + blk
        rows = slice(blk * w, (blk + 1) * w)
        win = pl.ds(pl.multiple_of(n * w, w), 3 * w)
        kpos = n * w - w + kj
        valid = (kj >= qi) & (kj <= qi + 2 * w) & (kpos >= 0) & (kpos < DEC_SEQ)
        for kh in range(KV_HEADS):
            ks = slice(kh * LANE, (kh + 1) * LANE)
            vs = slice(2 * kh * LANE, 2 * (kh + 1) * LANE)
            q = jnp.concatenate(
                [(_rope(q_ref[rows, _head_slice(kh, gq)], qc_ref[rows, :], ql_ref[rows, :], qh_ref[rows, :],
                        quarter) * scale).astype(BF16) for gq in range(Q_PER_KV)], axis=0)
            sink = _sink_column(sink_ref, kh, w)
            s_c = _dot_nt(q, kc_scr[:, ks])
            s_w = jnp.where(valid, _dot_nt(q, kw_scr[win, ks]), NEG_INF)
            m = jnp.maximum(jnp.maximum(jnp.max(s_c, axis=-1, keepdims=True),
                                        jnp.max(s_w, axis=-1, keepdims=True)), sink)
            ov = (_dot(jnp.exp(s_c - m).astype(BF16), vc_scr[:, vs])
                  + _dot(jnp.exp(s_w - m).astype(BF16), vw_scr[win, vs]))
            o = ov[:, :LANE] * (1.0 / (ov[:, LANE:] + jnp.exp(sink - m)))
            for gq in range(Q_PER_KV):
                o_ref[rows, _head_slice(kh, gq)] = o[gq * w:(gq + 1) * w].astype(o_ref.dtype)


def gqa_latent(proj, cache_k, cache_v, tables, sinks):
    wq = N_HEADS * HEAD_DIM
    wkv = KV_HEADS * HEAD_DIM
    tq = LAT_QBLOCKS_PER_STEP * WINDOW
    nq = DEC_SEQ // tq
    row0 = TP // tq
    kv_row0 = TP // DEC_SEQ
    cos, sin_lo, sin_hi = tables
    full = pl.BlockSpec((DEC_SEQ, LANE), lambda b, i: (0, 0))
    qtab = pl.BlockSpec((tq, LANE), lambda b, i: (i, 0))
    cache = pl.BlockSpec((None, PAST_LEN, wkv), lambda b, i: (b, 0, 0))
    return pl.pallas_call(
        _gqa_lat_kernel,
        grid=(DEC_BATCH, nq),
        in_specs=[pl.BlockSpec(memory_space=pltpu.SMEM),
                  pl.BlockSpec((tq, wq), lambda b, i: (row0 + b * nq + i, 0)),
                  pl.BlockSpec((DEC_SEQ, wkv), lambda b, i: (kv_row0 + b, wq // wkv)),
                  pl.BlockSpec((DEC_SEQ, wkv), lambda b, i: (kv_row0 + b, wq // wkv + 1)),
                  cache, cache, full, full, full, qtab, qtab, qtab],
        out_specs=pl.BlockSpec((tq, wq), lambda b, i: (b * nq + i, 0)),
        out_shape=jax.ShapeDtypeStruct((TS, wq), BF16),
        scratch_shapes=[pltpu.VMEM((DEC_SEQ + 2 * WINDOW, wkv), BF16),
                        pltpu.VMEM((DEC_SEQ + 2 * WINDOW, 2 * wkv), BF16),
                        pltpu.VMEM((PAST_LEN, wkv), BF16),
                        pltpu.VMEM((PAST_LEN, 2 * wkv), BF16)],
        compiler_params=_params(("arbitrary", "arbitrary")),
        name="gqa_latent",
    )(sinks, proj, proj, proj, cache_k, cache_v, cos, sin_lo, sin_hi, cos, sin_lo, sin_hi)


def _even_layer(xc, xs, mod, j, layer, norm_g, w_in_even, lam_vec, subln_g, w_gate1, w_gate2, b_gate, gla_norm_g,
                w_out_even, cache_a_k, cache_a_v, state_b, tables):
    lam_init = 0.8 - 0.6 * math.exp(-0.3 * layer)
    ng = HB * GLA_DK
    wg1 = jnp.zeros((D_MODEL, LANE), F32).at[:, :2 * GLA_GATE_RANK].set(
        jnp.concatenate([w_gate1[j, 0], w_gate1[j, 1]], axis=-1)).astype(BF16)
    wg2 = jnp.zeros((2, LANE, ng), F32)
    for e in range(2):
        wg2 = wg2.at[e, e * GLA_GATE_RANK:(e + 1) * GLA_GATE_RANK, :].set(w_gate2[j, e])
    h, gates = norm_mod_gate(xc, xs, norm_g[layer, 0], mod, 0, 1, wg1, wg2.astype(BF16),
                             b_gate[j].reshape(2, 1, ng))
    proj = ws_matmul(h, w_in_even, j)
    ya_c, new_k, new_v = diff_attention_context(proj, lam_vec, subln_g[j], lam_init)
    ya_s = diff_attention_latent(proj, cache_a_k[:, j].reshape(DEC_BATCH, PAST_LEN, HA * 2 * DIFF_HD),
                                 cache_a_v[:, j].reshape(DEC_BATCH, PAST_LEN, HA * DIFF_VD),
                                 tables, lam_vec, subln_g[j], lam_init)
    yb_c, s_fin = bidir_gla(proj, gates, jnp.zeros((1, 2, HB, GLA_DK, GLA_DV), F32), gla_norm_g[j],
                            BATCH, SEQ, 0)
    yb_s, _ = bidir_gla(proj, gates, state_b[:, j], gla_norm_g[j], DEC_BATCH, DEC_SEQ, TP // DEC_SEQ)
    x = ws_matmul_residual([(ya_c, ya_s), (yb_c, yb_s)], w_out_even, j, (xc, xs), mod, 2)
    return (x, new_k.reshape(BATCH, SEQ, HA, 2 * DIFF_HD), new_v.reshape(BATCH, SEQ, HA, DIFF_VD), s_fin)


def _odd_layer(x, mod, j, layer, norm_g, w_in_odd, sinks, w_out_odd, cache_c_k, cache_c_v, tables):
    wkv = KV_HEADS * HEAD_DIM
    proj = ws_matmul(norm_mod(x, norm_g[layer, 0], mod, 0, 1), w_in_odd, j)
    o_c, new_k, new_v = gqa_context(proj, sinks[j])
    o_s = gqa_latent(proj, cache_c_k[:, j].reshape(DEC_BATCH, PAST_LEN, wkv),
                     cache_c_v[:, j].reshape(DEC_BATCH, PAST_LEN, wkv), tables, sinks[j])
    x = ws_matmul_residual([(o_c, o_s)], w_out_odd, j, (x,), mod, 2)
    return (x, new_k.reshape(BATCH, SEQ, KV_HEADS, HEAD_DIM), new_v.reshape(BATCH, SEQ, KV_HEADS, HEAD_DIM))


def _ffn(x, mod, layer, norm_g, w_ffn_in, w_ffn_out):
    act = ws_swiglu(norm_mod(x, norm_g[layer, 1], mod, 3, 4), w_ffn_in, layer)
    return ws_matmul_residual([(act,)], w_ffn_out, layer, (x,), mod, 5, tm=512, tn=512)


def kernel(x_prompt, x_sample, c, cache_a_k, cache_a_v, state_b, cache_c_k, cache_c_v, c_ctx, w_ada, b_ada,
           norm_g, w_in_even, lam_q1, lam_k1, lam_q2, lam_k2, subln_g, w_gate1, w_gate2, b_gate, gla_norm_g,
           w_out_even, w_in_odd, sinks, w_out_odd, w_ffn_in, w_ffn_out, final_norm_g):
    assert DEPTH == 2, "layer 0 reads the two input streams, layer 1 the fused token matrix"
    cvec = jnp.concatenate([c_ctx[None, :], c, jnp.zeros((MOD_ROWS - 1 - DEC_BATCH, D_MODEL), F32)], axis=0)
    mods = adaln(cvec, w_ada, b_ada).reshape(DEPTH, MOD_ROWS, 1, 6 * D_MODEL)
    tab_d = rope_tables(DEC_SEQ, DIFF_HD)
    tab_h = rope_tables(DEC_SEQ, HEAD_DIM)

    lam_vec = jnp.stack([lam_q1[0], lam_k1[0], lam_q2[0], lam_k2[0]], axis=0)
    x, ak, av, sb = _even_layer(x_prompt.reshape(TP, D_MODEL), x_sample.reshape(TS, D_MODEL), mods[0], 0, 0,
                                norm_g, w_in_even, lam_vec, subln_g, w_gate1, w_gate2, b_gate, gla_norm_g,
                                w_out_even, cache_a_k, cache_a_v, state_b, tab_d)
    x = _ffn(x, mods[0], 0, norm_g, w_ffn_in, w_ffn_out)
    x, ck, cv = _odd_layer(x, mods[1], 0, 1, norm_g, w_in_odd, sinks, w_out_odd, cache_c_k, cache_c_v, tab_h)
    x = _ffn(x, mods[1], 1, norm_g, w_ffn_in, w_ffn_out)

    y_prompt = final_norm(x, final_norm_g, 0, TP).reshape(BATCH, SEQ, D_MODEL)
    y_sample = final_norm(x, final_norm_g, TP, TS).reshape(DEC_BATCH, DEC_SEQ, D_MODEL)
    return (y_prompt, y_sample, ak[:, None], av[:, None], sb[:, None], ck[:, None], cv[:, None])
```

```python
import functools
import math

import jax
import jax.numpy as jnp
from jax import lax
from jax.experimental import pallas as pl
from jax.experimental.pallas import tpu as pltpu

D_MODEL = 2048
BATCH = 32
SEQ = 256
DEPTH = 2
DEC_BATCH = 2
DEC_SEQ = 2048
PAST_LEN = 512
GRID_W = 64
HEAD_DIM = 128
N_HEADS = D_MODEL // HEAD_DIM
HA = N_HEADS // 2
DIFF_HD = HEAD_DIM // 2
DIFF_VD = HEAD_DIM
HB = N_HEADS // 2
GLA_DK = HEAD_DIM // 2
GLA_DV = HEAD_DIM
GLA_GATE_RANK = 16
GLA_TAU = 16.0
GLA_CHUNK = 64
GLA_SUPER = 4
CTX_BATCHES_PER_STEP = 4
LAT_QBLOCKS_PER_STEP = 2
DIFF_LAT_CHAINS = 4
KV_HEADS = N_HEADS // 4
Q_PER_KV = N_HEADS // KV_HEADS
WINDOW = 128
D_FF = -(-8 * D_MODEL // (3 * 256)) * 256
ROPE_BASE = 10000.0
EPS = 1e-6
NEG_INF = -1e30
EVEN_IN = 2 * HA * 2 * DIFF_HD + HA * DIFF_VD + 2 * HB * GLA_DK + 2 * HB * GLA_DV
ODD_IN = (N_HEADS + 2 * KV_HEADS) * HEAD_DIM

TP = BATCH * SEQ
TS = DEC_BATCH * DEC_SEQ
T = TP + TS
LANE = 128
MOD_ROWS = 8
NORM_ROWS = 16
NORM_UNROLL = 8
VMEM_LIMIT = 56 * 1024 * 1024

F32 = jnp.float32
BF16 = jnp.bfloat16


def _params(semantics, vmem=VMEM_LIMIT):
    return pltpu.CompilerParams(dimension_semantics=semantics, vmem_limit_bytes=vmem)


def _dot(a, b):
    return jnp.dot(a, b, preferred_element_type=F32)


def _dot_nt(a, b):
    return lax.dot_general(a, b, (((1,), (1,)), ((), ())), preferred_element_type=F32)


def _group_of_tile(i, tm):
    r = i * tm
    return jnp.where(r < TP, 0, 1 + (r - TP) // DEC_SEQ)


def _rms(x, g):
    return (x * lax.rsqrt(jnp.mean(x * x, axis=-1, keepdims=True) + EPS)) * g


def _adaln_kernel(c_ref, w_ref, b_ref, o_ref):
    c = c_ref[...]
    s = c * jax.nn.sigmoid(c)
    o_ref[...] = _dot(s.astype(BF16), w_ref[...].astype(BF16)) + b_ref[...]


def adaln(cvec, w_ada, b_ada, tn=512):
    n = w_ada.shape[-1]
    return pl.pallas_call(
        _adaln_kernel,
        grid=(DEPTH, n // tn),
        in_specs=[
            pl.BlockSpec((MOD_ROWS, D_MODEL), lambda l, j: (0, 0)),
            pl.BlockSpec((None, D_MODEL, tn), lambda l, j: (l, 0, j)),
            pl.BlockSpec((None, 1, tn), lambda l, j: (l, 0, j)),
        ],
        out_specs=pl.BlockSpec((None, MOD_ROWS, tn), lambda l, j: (l, 0, j)),
        out_shape=jax.ShapeDtypeStruct((DEPTH, MOD_ROWS, n), F32),
        compiler_params=_params(("arbitrary", "arbitrary")),
        name="adaln",
    )(cvec, w_ada, b_ada.reshape(DEPTH, 1, n))


def _by_row_groups(x_ref, o_ref, fn):
    def body(r, carry):
        rs = pl.ds(pl.multiple_of(r * NORM_ROWS, NORM_ROWS), NORM_ROWS)
        o_ref[rs, :] = fn(x_ref[rs, :]).astype(o_ref.dtype)
        return carry

    lax.fori_loop(0, x_ref.shape[0] // NORM_ROWS, body, 0, unroll=NORM_UNROLL)


def _norm_mod_into(h_ref, x_ref, g_ref, shift_ref, scale_ref):
    _by_row_groups(x_ref, h_ref, lambda x: _rms(x, g_ref[...]) * (1 + scale_ref[...]) + shift_ref[...])


def _norm_mod_kernel(x_ref, g_ref, shift_ref, scale_ref, h_ref):
    _norm_mod_into(h_ref, x_ref, g_ref, shift_ref, scale_ref)


def _norm_mod_gate_kernel(xc_ref, xs_ref, g_ref, shift_ref, scale_ref, wg1_ref, wg2_ref, bg_ref,
                          h_ref, gate_ref, *, ctx_tiles):
    @pl.when(pl.program_id(0) < ctx_tiles)
    def _():
        _norm_mod_into(h_ref, xc_ref, g_ref, shift_ref, scale_ref)

    @pl.when(pl.program_id(0) >= ctx_tiles)
    def _():
        _norm_mod_into(h_ref, xs_ref, g_ref, shift_ref, scale_ref)

    low = _dot(h_ref[...], wg1_ref[...]).astype(BF16)
    for e in range(2):
        logit = _dot(low, wg2_ref[e]) + bg_ref[e]
        log_sig = jnp.minimum(logit, 0.0) - jnp.log(1.0 + jnp.exp(-jnp.abs(logit)))
        gate_ref[e] = log_sig / GLA_TAU


def _mod_specs(tm, shift_idx, scale_idx):
    def spec(idx):
        return pl.BlockSpec((None, 1, D_MODEL), lambda i: (_group_of_tile(i, tm), 0, idx))
    return [pl.BlockSpec((1, D_MODEL), lambda i: (0, 0)), spec(shift_idx), spec(scale_idx)]


def norm_mod(x, g, mod, shift_idx, scale_idx, tm=1024):
    return pl.pallas_call(
        _norm_mod_kernel,
        grid=(T // tm,),
        in_specs=[pl.BlockSpec((tm, D_MODEL), lambda i: (i, 0))] + _mod_specs(tm, shift_idx, scale_idx),
        out_specs=pl.BlockSpec((tm, D_MODEL), lambda i: (i, 0)),
        out_shape=jax.ShapeDtypeStruct((T, D_MODEL), BF16),
        compiler_params=_params(("arbitrary",)),
        name="norm_mod",
    )(x, g.reshape(1, D_MODEL), mod, mod)


def norm_mod_gate(xc, xs, g, mod, shift_idx, scale_idx, wg1, wg2, bg, tm=512):
    ng = HB * GLA_DK
    ctx_tiles = TP // tm
    return pl.pallas_call(
        functools.partial(_norm_mod_gate_kernel, ctx_tiles=ctx_tiles),
        grid=(T // tm,),
        in_specs=[pl.BlockSpec((tm, D_MODEL), lambda i: (jnp.minimum(i, ctx_tiles - 1), 0)),
                  pl.BlockSpec((tm, D_MODEL), lambda i: (jnp.maximum(i - ctx_tiles, 0), 0))]
        + _mod_specs(tm, shift_idx, scale_idx)
        + [pl.BlockSpec((D_MODEL, LANE), lambda i: (0, 0)),
           pl.BlockSpec((2, LANE, ng), lambda i: (0, 0, 0)),
           pl.BlockSpec((2, 1, ng), lambda i: (0, 0, 0))],
        out_specs=[pl.BlockSpec((tm, D_MODEL), lambda i: (i, 0)),
                   pl.BlockSpec((2, tm, ng), lambda i: (0, i, 0))],
        out_shape=[jax.ShapeDtypeStruct((T, D_MODEL), BF16), jax.ShapeDtypeStruct((2, T, ng), F32)],
        compiler_params=_params(("arbitrary",)),
        name="norm_mod_gate",
    )(xc, xs, g.reshape(1, D_MODEL), mod, mod, wg1, wg2, bg)


def _cast_weights(w_refs, w_scrs):
    @pl.when(pl.program_id(1) == 0)
    def _():
        for w_ref, w_scr in zip(w_refs, w_scrs):
            w_scr[...] = w_ref[...].astype(BF16)


def _ws_plain_kernel(h_ref, w_ref, o_ref, w_scr):
    _cast_weights([w_ref], [w_scr])
    o_ref[...] = _dot(h_ref[...], w_scr[...])


def _ws_swiglu_kernel(h_ref, wg_ref, wu_ref, o_ref, wg_scr, wu_scr):
    _cast_weights([wg_ref, wu_ref], [wg_scr, wu_scr])
    h = h_ref[...]
    gate = _dot(h, wg_scr[...])
    up = _dot(h, wu_scr[...])
    o_ref[...] = (gate * jax.nn.sigmoid(gate) * up).astype(o_ref.dtype)


def _ws_res_kernel(*refs, lhs_arity, x_arity, ctx_tiles):
    n_l = sum(lhs_arity)
    w_ref = refs[n_l]
    x_refs = refs[n_l + 1:n_l + 1 + x_arity]
    gate_ref, o_ref, w_scr = refs[n_l + 1 + x_arity:]
    groups, at = [], 0
    for a in lhs_arity:
        groups.append(refs[at:at + a])
        at += a
    _cast_weights([w_ref], [w_scr])

    def emit(side):
        acc, k0 = None, 0
        for grp in groups:
            ref = grp[side] if len(grp) == 2 else grp[0]
            part = _dot(ref[...], w_scr[k0:k0 + ref.shape[1], :])
            acc = part if acc is None else acc + part
            k0 += ref.shape[1]
        x_ref = x_refs[side] if x_arity == 2 else x_refs[0]
        o_ref[...] = x_ref[...] + gate_ref[...] * acc

    if x_arity == 2 or 2 in lhs_arity:
        @pl.when(pl.program_id(1) < ctx_tiles)
        def _():
            emit(0)

        @pl.when(pl.program_id(1) >= ctx_tiles)
        def _():
            emit(1)
    else:
        emit(0)


def _rows_specs(arrays, tm, width, col):
    if len(arrays) == 1:
        return [pl.BlockSpec((tm, width), lambda j, i: (i, col(j)))]
    ctx_tiles = TP // tm
    return [pl.BlockSpec((tm, width), lambda j, i: (jnp.minimum(i, ctx_tiles - 1), col(j))),
            pl.BlockSpec((tm, width), lambda j, i: (jnp.maximum(i - ctx_tiles, 0), col(j)))]


def ws_matmul(h, w_stack, layer, tm=1024, tn=1024):
    kdim, n = w_stack.shape[1:]
    return pl.pallas_call(
        _ws_plain_kernel,
        grid=(n // tn, T // tm),
        in_specs=[pl.BlockSpec((tm, kdim), lambda j, i: (i, 0)),
                  pl.BlockSpec((None, kdim, tn), lambda j, i: (layer, 0, j))],
        out_specs=pl.BlockSpec((tm, tn), lambda j, i: (i, j)),
        out_shape=jax.ShapeDtypeStruct((T, n), F32),
        scratch_shapes=[pltpu.VMEM((kdim, tn), BF16)],
        compiler_params=_params(("arbitrary", "arbitrary")),
        name="ws_matmul",
    )(h, w_stack)


def ws_swiglu(h, w_stack, layer, tm=1024, tn=512):
    kdim = w_stack.shape[1]
    nf = D_FF // tn
    return pl.pallas_call(
        _ws_swiglu_kernel,
        grid=(nf, T // tm),
        in_specs=[pl.BlockSpec((tm, kdim), lambda j, i: (i, 0)),
                  pl.BlockSpec((None, kdim, tn), lambda j, i: (layer, 0, j)),
                  pl.BlockSpec((None, kdim, tn), lambda j, i: (layer, 0, j + nf))],
        out_specs=pl.BlockSpec((tm, tn), lambda j, i: (i, j)),
        out_shape=jax.ShapeDtypeStruct((T, D_FF), BF16),
        scratch_shapes=[pltpu.VMEM((kdim, tn), BF16), pltpu.VMEM((kdim, tn), BF16)],
        compiler_params=_params(("arbitrary", "arbitrary")),
        name="ws_swiglu",
    )(h, w_stack, w_stack)


def ws_matmul_residual(lhs_groups, w_stack, layer, x, mod, gate_idx, tm=512, tn=1024):
    kdim, n = w_stack.shape[1:]
    gate_blk = n // tn
    specs, args = [], []
    for grp in lhs_groups:
        specs += _rows_specs(grp, tm, grp[0].shape[1], lambda j: 0)
        args += list(grp)
    specs.append(pl.BlockSpec((None, kdim, tn), lambda j, i: (layer, 0, j)))
    specs += _rows_specs(x, tm, tn, lambda j: j)
    specs.append(pl.BlockSpec((None, 1, tn), lambda j, i: (_group_of_tile(i, tm), 0, gate_idx * gate_blk + j)))
    return pl.pallas_call(
        functools.partial(_ws_res_kernel, lhs_arity=tuple(len(grp) for grp in lhs_groups), x_arity=len(x),
                          ctx_tiles=TP // tm),
        grid=(n // tn, T // tm),
        in_specs=specs,
        out_specs=pl.BlockSpec((tm, tn), lambda j, i: (i, j)),
        out_shape=jax.ShapeDtypeStruct((T, n), F32),
        scratch_shapes=[pltpu.VMEM((kdim, tn), BF16)],
        compiler_params=_params(("arbitrary", "arbitrary")),
        name="ws_matmul_residual",
    )(*args, w_stack, *x, mod)


def _final_norm_kernel(x_ref, g_ref, o_ref):
    o_ref[...] = _rms(x_ref[...], g_ref[...])


def final_norm(x, g, row0, n_rows, tm=1024):
    blk0 = row0 // tm
    return pl.pallas_call(
        _final_norm_kernel,
        grid=(n_rows // tm,),
        in_specs=[pl.BlockSpec((tm, D_MODEL), lambda i: (blk0 + i, 0)),
                  pl.BlockSpec((1, D_MODEL), lambda i: (0, 0))],
        out_specs=pl.BlockSpec((tm, D_MODEL), lambda i: (i, 0)),
        out_shape=jax.ShapeDtypeStruct((n_rows, D_MODEL), F32),
        compiler_params=_params(("arbitrary",)),
        name="final_norm",
    )(x, g.reshape(1, D_MODEL))


def _grid_angles(n_tok, rot_dim):
    t = jnp.arange((n_tok // GRID_W) * GRID_W)
    row = (t // GRID_W).astype(F32)
    col = (t % GRID_W).astype(F32)
    half = rot_dim // 2
    inv = ROPE_BASE ** (-jnp.arange(0, half, 2, dtype=F32) / half)
    return row[:, None] * inv[None], col[:, None] * inv[None]


def rope_tables(n_tok, rot_dim):
    ang_row, ang_col = _grid_angles(n_tok, rot_dim)
    zeros = jnp.zeros_like(ang_row)
    reps = LANE // rot_dim

    def lanes(r1, r2, c1, c2):
        return jnp.tile(jnp.concatenate([r1, r2, c1, c2], axis=-1), (1, reps))

    cr, sr, cc, sc = jnp.cos(ang_row), jnp.sin(ang_row), jnp.cos(ang_col), jnp.sin(ang_col)
    return lanes(cr, cr, cc, cc), lanes(-sr, zeros, -sc, zeros), lanes(zeros, sr, zeros, sc)


def _rope(x, cos, sin_lo, sin_hi, quarter):
    return (x * cos + pltpu.roll(x, LANE - quarter, 1) * sin_lo + pltpu.roll(x, quarter, 1) * sin_hi)


def _diff_lambda(lam_ref, lam_init):
    e1 = jnp.exp(jnp.sum(lam_ref[0:1, :] * lam_ref[1:2, :], axis=-1, keepdims=True))
    e2 = jnp.exp(jnp.sum(lam_ref[2:3, :] * lam_ref[3:4, :], axis=-1, keepdims=True))
    return e1 - e2 + lam_init


def _with_ones(v_bf):
    return jnp.concatenate([v_bf, jnp.ones(v_bf.shape, BF16)], axis=1)


def _softmax_av(q_bf, k_bf, v1_bf):
    s = _dot_nt(q_bf, k_bf)
    e = jnp.exp(s - jnp.max(s, axis=-1, keepdims=True))
    if v1_bf.shape[1] == LANE:
        return _dot(e.astype(BF16), v1_bf) * (1.0 / jnp.sum(e, axis=-1, keepdims=True))
    ov = _dot(e.astype(BF16), v1_bf)
    return ov[:, :LANE] * (1.0 / ov[:, LANE:])


def _diff_head(q, k_bf, v1_bf, lam, subln_g, lam_init, stack_maps):
    n = q.shape[0]
    first = lax.broadcasted_iota(jnp.int32, q.shape, 1) < DIFF_HD
    q0 = jnp.where(first, q, 0.0).astype(BF16)
    q1 = jnp.where(first, 0.0, q).astype(BF16)
    if stack_maps:
        r = _softmax_av(jnp.concatenate([q0, q1], axis=0), k_bf, v1_bf)
        o = r[:n] - lam * r[n:]
    else:
        o = _softmax_av(q0, k_bf, v1_bf) - lam * _softmax_av(q1, k_bf, v1_bf)
    return _rms(o, subln_g) * (1.0 - lam_init)


def _diff_ctx_kernel(q_ref, k_ref, v_ref, lam_ref, sg_ref, o_ref, ko_ref, vo_ref, *, lam_init):
    lam = _diff_lambda(lam_ref, lam_init)
    scale = DIFF_HD ** -0.5
    n_rows = CTX_BATCHES_PER_STEP * SEQ
    for h in range(HA):
        sl = slice(h * LANE, (h + 1) * LANE)
        ko_ref[pl.ds(h, n_rows, stride=HA), :] = k_ref[:, sl]
        vo_ref[pl.ds(h, n_rows, stride=HA), :] = v_ref[:, sl]
    for bb in range(CTX_BATCHES_PER_STEP):
        rows = slice(bb * SEQ, (bb + 1) * SEQ)
        for h in range(HA):
            sl = slice(h * LANE, (h + 1) * LANE)
            y = _diff_head(q_ref[rows, sl] * scale, k_ref[rows, sl].astype(BF16),
                           _with_ones(v_ref[rows, sl].astype(BF16)), lam, sg_ref[...], lam_init, stack_maps=True)
            o_ref[rows, sl] = y.astype(o_ref.dtype)


def diff_attention_context(proj, lam_vec, subln_g, lam_init):
    w = HA * LANE

    def blk(c):
        return pl.BlockSpec((CTX_BATCHES_PER_STEP * SEQ, w), lambda b: (b, c))

    cache = pl.BlockSpec((CTX_BATCHES_PER_STEP * SEQ * HA, LANE), lambda b: (b, 0))
    return pl.pallas_call(
        functools.partial(_diff_ctx_kernel, lam_init=lam_init),
        grid=(BATCH // CTX_BATCHES_PER_STEP,),
        in_specs=[blk(0), blk(1), blk(2),
                  pl.BlockSpec((4, DIFF_HD), lambda b: (0, 0)),
                  pl.BlockSpec((1, DIFF_VD), lambda b: (0, 0))],
        out_specs=[blk(0), cache, cache],
        out_shape=[jax.ShapeDtypeStruct((TP, w), BF16), jax.ShapeDtypeStruct((TP * HA, LANE), F32),
                   jax.ShapeDtypeStruct((TP * HA, LANE), F32)],
        compiler_params=_params(("arbitrary",)),
        name="diff_attention_context",
    )(proj, proj, proj, lam_vec, subln_g.reshape(1, DIFF_VD))


def _diff_lat_kernel(q_ref, k_ref, v_ref, ck_ref, cv_ref, kc_ref, kl_ref, kh_ref, qc_ref, ql_ref, qh_ref,
                     lam_ref, sg_ref, o_ref, k_scr, v_scr, *, lam_init):
    quarter = DIFF_HD // 4

    @pl.when(pl.program_id(2) == 0)
    def _():
        head_rows = pl.ds(pl.program_id(1), PAST_LEN, stride=HA)
        k_scr[0:PAST_LEN, :] = ck_ref[head_rows, :].astype(BF16)
        k_scr[PAST_LEN:, :] = _rope(k_ref[...], kc_ref[...], kl_ref[...], kh_ref[...], quarter).astype(BF16)
        v_scr[0:PAST_LEN, :] = cv_ref[head_rows, :].astype(BF16)
        v_scr[PAST_LEN:, :] = v_ref[...].astype(BF16)

    lam = _diff_lambda(lam_ref, lam_init)
    sub = q_ref.shape[0] // DIFF_LAT_CHAINS
    for c in range(DIFF_LAT_CHAINS):
        rs = slice(c * sub, (c + 1) * sub)
        q = _rope(q_ref[rs, :], qc_ref[rs, :], ql_ref[rs, :], qh_ref[rs, :], quarter) * (DIFF_HD ** -0.5)
        y = _diff_head(q, k_scr[...], v_scr[...], lam, sg_ref[...], lam_init, stack_maps=False)
        o_ref[rs, :] = y.astype(o_ref.dtype)


def diff_attention_latent(proj, cache_k, cache_v, tables, lam_vec, subln_g, lam_init, tq=1024):
    nq = DEC_SEQ // tq
    row0 = TP // tq
    kv_row0 = TP // DEC_SEQ
    cos, sin_lo, sin_hi = tables
    full = pl.BlockSpec((DEC_SEQ, LANE), lambda b, h, i: (0, 0))
    qtab = pl.BlockSpec((tq, LANE), lambda b, h, i: (i, 0))
    cache = pl.BlockSpec((None, PAST_LEN * HA, LANE), lambda b, h, i: (b, 0, 0))
    return pl.pallas_call(
        functools.partial(_diff_lat_kernel, lam_init=lam_init),
        grid=(DEC_BATCH, HA, nq),
        in_specs=[pl.BlockSpec((tq, LANE), lambda b, h, i: (row0 + b * nq + i, h)),
                  pl.BlockSpec((DEC_SEQ, LANE), lambda b, h, i: (kv_row0 + b, HA + h)),
                  pl.BlockSpec((DEC_SEQ, LANE), lambda b, h, i: (kv_row0 + b, 2 * HA + h)),
                  cache, cache, full, full, full, qtab, qtab, qtab,
                  pl.BlockSpec((4, DIFF_HD), lambda b, h, i: (0, 0)),
                  pl.BlockSpec((1, DIFF_VD), lambda b, h, i: (0, 0))],
        out_specs=pl.BlockSpec((tq, LANE), lambda b, h, i: (b * nq + i, h)),
        out_shape=jax.ShapeDtypeStruct((TS, HA * LANE), BF16),
        scratch_shapes=[pltpu.VMEM((PAST_LEN + DEC_SEQ, LANE), BF16),
                        pltpu.VMEM((PAST_LEN + DEC_SEQ, LANE), BF16)],
        compiler_params=_params(("arbitrary", "arbitrary", "arbitrary")),
        name="diff_attention_latent",
    )(proj, proj, proj, cache_k, cache_v, cos, sin_lo, sin_hi, cos, sin_lo, sin_hi,
      lam_vec, subln_g.reshape(1, DIFF_VD))


def _gla_kernel(q_ref, k_ref, v_ref, r_ref, g_ref, s0_ref, gg_ref, y_ref, sfin_ref, of_scr, ob_scr, st_scr,
                *, seq, pairs):
    c = GLA_CHUNK
    sb = GLA_SUPER * c
    n_super = seq // sb
    rows = lax.broadcasted_iota(jnp.int32, (sb, sb), 0)
    cols = lax.broadcasted_iota(jnp.int32, (sb, sb), 1)
    same_chunk = (rows // c) == (cols // c)
    keep = (same_chunk & (rows >= cols), same_chunk & (cols >= rows))
    lane = lax.broadcasted_iota(jnp.int32, (sb, LANE), 1)
    own = (lane < GLA_DK, lane >= GLA_DK)
    chunk_of_row = lax.broadcasted_iota(jnp.int32, (sb, LANE), 0) // c
    st_rows = lax.broadcasted_iota(jnp.int32, (2 * GLA_DV, LANE), 0)
    st_lane = lax.broadcasted_iota(jnp.int32, (2 * GLA_DV, LANE), 1)
    st_own = (st_rows < GLA_DV) == (st_lane < GLA_DK)
    zpad = jnp.zeros((GLA_DK, GLA_DV), F32)

    def expand(x):
        return jnp.concatenate([jnp.where(chunk_of_row == ci, x, 0.0) for ci in range(GLA_SUPER)],
                               axis=1).astype(BF16)

    for p in range(pairs):
        for d in range(2):
            s0 = [s0_ref[d, 2 * p + hh] for hh in range(2)]
            st_scr[p, d] = jnp.concatenate([jnp.concatenate([s0[0], zpad], axis=0).T,
                                            jnp.concatenate([zpad, s0[1]], axis=0).T], axis=0)

    def rows_of(n, d):
        r0 = (n if d == 0 else n_super - 1 - n) * sb
        return pl.ds(r0, sb) if n_super == 1 else pl.ds(pl.multiple_of(r0, sb), sb)

    def chunk_totals(prefix):
        ends = [prefix[ci * c + c - 1:ci * c + c, :] for ci in range(GLA_SUPER)]
        return ends, jnp.concatenate([jnp.broadcast_to(e, (c, LANE)) for e in ends], axis=0)

    def log_decays(n, p):
        kl = slice(p * LANE, (p + 1) * LANE)
        g_b = g_ref[1, rows_of(n, 1), kl]
        prefix = jnp.concatenate([g_ref[0, rows_of(n, 0), kl], g_b], axis=1)
        pos = lax.broadcasted_iota(jnp.int32, prefix.shape, 0) % c
        step = 1
        while step < c:
            prefix = prefix + jnp.where(pos >= step, pltpu.roll(prefix, step, 0), 0.0)
            step *= 2
        ends_f, total_f = chunk_totals(prefix[:, :LANE])
        ends_b, total_b = chunk_totals(prefix[:, LANE:])
        return (prefix[:, :LANE], ends_f, total_f), (total_b - prefix[:, LANE:] + g_b, ends_b, total_b)

    def one_pair_direction(n, p, d, decays):
        kl = slice(p * LANE, (p + 1) * LANE)
        vl = slice(2 * p * GLA_DV, 2 * (p + 1) * GLA_DV)
        rs = rows_of(n, d)
        b, ends, total = decays
        q_in = (q_ref[rs, kl] * (GLA_DK ** -0.5)) * jnp.exp(b)
        kk = k_ref[rs, kl]
        k_in = (kk * jnp.exp(-b)).astype(BF16)
        k_end = kk * jnp.exp(total - b)
        v = v_ref[rs, vl]
        v_bf = v.astype(BF16)
        kv_all = _dot(v.T.astype(BF16), expand(k_end))
        q2 = jnp.concatenate([jnp.where(own[0], q_in, 0.0), jnp.where(own[1], q_in, 0.0)], axis=0)
        a2 = _dot_nt(q2.astype(BF16), k_in)
        o = jnp.concatenate(
            [_dot(jnp.where(keep[d], a2[hh * sb:(hh + 1) * sb], 0.0).astype(BF16),
                  v_bf[:, hh * GLA_DV:(hh + 1) * GLA_DV]) for hh in range(2)], axis=1)
        st = st_scr[p, d]
        entering = [None] * GLA_SUPER
        for ci in (range(GLA_SUPER) if d == 0 else range(GLA_SUPER - 1, -1, -1)):
            entering[ci] = st
            st = st * jnp.exp(ends[ci]) + jnp.where(st_own, kv_all[:, ci * LANE:(ci + 1) * LANE], 0.0)
        st_scr[p, d] = st
        o += _dot_nt(expand(q_in), jnp.concatenate(entering, axis=1).astype(BF16))
        if d == 0:
            of_scr[rs, vl] = o
        else:
            ob_scr[rs, vl] = o

    def super_block(n, carry):
        for p in range(pairs):
            decays = log_decays(n, p)
            for d in range(2):
                one_pair_direction(n, p, d, decays[d])
        return carry

    if n_super == 1:
        super_block(0, 0)
    else:
        lax.fori_loop(0, n_super, super_block, 0)

    for h in range(2 * pairs):
        vs = slice(h * GLA_DV, (h + 1) * GLA_DV)
        r = r_ref[:, vs]
        y = _rms(of_scr[:, vs] + ob_scr[:, vs], gg_ref[...]) * (r * jax.nn.sigmoid(r))
        y_ref[:, vs] = y.astype(y_ref.dtype)
        p, hh = divmod(h, 2)
        for d in range(2):
            sfin_ref[d, h] = st_scr[p, d, hh * GLA_DV:(hh + 1) * GLA_DV, :].T[hh * GLA_DK:(hh + 1) * GLA_DK, :]


def bidir_gla(proj, gates, s0, gla_g, n_batch, seq, row_blk0):
    shared_s0 = s0.shape[0] == 1
    pairs = HB // 2 if seq * HB * GLA_DV * 4 <= 1024 * 1024 else 2
    wqk = pairs * LANE
    wv = pairs * 2 * GLA_DV
    col_q = (2 * HA * 2 * DIFF_HD + HA * DIFF_VD) // wqk
    col_k = col_q + HB * GLA_DK // wqk
    col_v = (col_k * wqk + HB * GLA_DK) // wv
    col_r = col_v + HB * GLA_DV // wv
    n_tok = n_batch * seq
    return pl.pallas_call(
        functools.partial(_gla_kernel, seq=seq, pairs=pairs),
        grid=(n_batch, HB // 2 // pairs),
        in_specs=[pl.BlockSpec((seq, wqk), lambda b, p: (row_blk0 + b, col_q + p)),
                  pl.BlockSpec((seq, wqk), lambda b, p: (row_blk0 + b, col_k + p)),
                  pl.BlockSpec((seq, wv), lambda b, p: (row_blk0 + b, col_v + p)),
                  pl.BlockSpec((seq, wv), lambda b, p: (row_blk0 + b, col_r + p)),
                  pl.BlockSpec((2, seq, wqk), lambda b, p: (0, row_blk0 + b, p)),
                  pl.BlockSpec((None, 2, 2 * pairs, GLA_DK, GLA_DV),
                               lambda b, p: (0 if shared_s0 else b, 0, p, 0, 0)),
                  pl.BlockSpec((1, GLA_DV), lambda b, p: (0, 0))],
        out_specs=[pl.BlockSpec((seq, wv), lambda b, p: (b, p)),
                   pl.BlockSpec((None, 2, 2 * pairs, GLA_DK, GLA_DV), lambda b, p: (b, 0, p, 0, 0))],
        out_shape=[jax.ShapeDtypeStruct((n_tok, HB * GLA_DV), BF16),
                   jax.ShapeDtypeStruct((n_batch, 2, HB, GLA_DK, GLA_DV), F32)],
        scratch_shapes=[pltpu.VMEM((seq, wv), F32), pltpu.VMEM((seq, wv), F32),
                        pltpu.VMEM((pairs, 2, 2 * GLA_DV, LANE), F32)],
        compiler_params=_params(("arbitrary", "arbitrary")),
        name="bidir_gla",
    )(proj, proj, proj, proj, gates, s0, gla_g.reshape(1, GLA_DV))


def _head_slice(kh, gq):
    h = kh * Q_PER_KV + gq
    return slice(h * LANE, (h + 1) * LANE)


def _sink_column(sink_ref, kh, rows):
    return jnp.concatenate([jnp.full((rows, 1), sink_ref[kh * Q_PER_KV + gq], F32) for gq in range(Q_PER_KV)],
                           axis=0)


def _gqa_ctx_kernel(sink_ref, q_ref, k_ref, v_ref, o_ref, ko_ref, vo_ref):
    scale = HEAD_DIM ** -0.5
    n_rows = CTX_BATCHES_PER_STEP * SEQ
    for kh in range(KV_HEADS):
        ks = slice(kh * LANE, (kh + 1) * LANE)
        ko_ref[pl.ds(kh, n_rows, stride=KV_HEADS), :] = k_ref[:, ks]
        vo_ref[pl.ds(kh, n_rows, stride=KV_HEADS), :] = v_ref[:, ks]
    for bb in range(CTX_BATCHES_PER_STEP):
        rows = slice(bb * SEQ, (bb + 1) * SEQ)
        for kh in range(KV_HEADS):
            ks = slice(kh * LANE, (kh + 1) * LANE)
            k_bf = k_ref[rows, ks].astype(BF16)
            v1_bf = _with_ones(v_ref[rows, ks].astype(BF16))
            for gq in range(Q_PER_KV):
                hs = _head_slice(kh, gq)
                sink = sink_ref[kh * Q_PER_KV + gq]
                s = _dot_nt((q_ref[rows, hs] * scale).astype(BF16), k_bf)
                m = jnp.maximum(jnp.max(s, axis=-1, keepdims=True), sink)
                ov = _dot(jnp.exp(s - m).astype(BF16), v1_bf)
                o_ref[rows, hs] = (ov[:, :LANE] * (1.0 / (ov[:, LANE:] + jnp.exp(sink - m)))).astype(o_ref.dtype)


def gqa_context(proj, sinks):
    wq = N_HEADS * HEAD_DIM
    wkv = KV_HEADS * HEAD_DIM
    rows = CTX_BATCHES_PER_STEP * SEQ

    def kv(c):
        return pl.BlockSpec((rows, wkv), lambda b: (b, c))

    cache = pl.BlockSpec((rows * KV_HEADS, LANE), lambda b: (b, 0))
    return pl.pallas_call(
        _gqa_ctx_kernel,
        grid=(BATCH // CTX_BATCHES_PER_STEP,),
        in_specs=[pl.BlockSpec(memory_space=pltpu.SMEM),
                  pl.BlockSpec((rows, wq), lambda b: (b, 0)), kv(wq // wkv), kv(wq // wkv + 1)],
        out_specs=[pl.BlockSpec((rows, wq), lambda b: (b, 0)), cache, cache],
        out_shape=[jax.ShapeDtypeStruct((TP, wq), BF16), jax.ShapeDtypeStruct((TP * KV_HEADS, LANE), F32),
                   jax.ShapeDtypeStruct((TP * KV_HEADS, LANE), F32)],
        compiler_params=_params(("arbitrary",)),
        name="gqa_context",
    )(sinks, proj, proj, proj)


def _gqa_lat_kernel(sink_ref, q_ref, k_ref, v_ref, ck_ref, cv_ref, kc_ref, kl_ref, kh_ref,
                    qc_ref, ql_ref, qh_ref, o_ref, kw_scr, vw_scr, kc_scr, vc_scr):
    quarter = HEAD_DIM // 4
    w = WINDOW
    wkv = KV_HEADS * HEAD_DIM

    @pl.when(pl.program_id(1) == 0)
    def _():
        kw_scr[0:w, :] = jnp.zeros((w, wkv), BF16)
        kw_scr[w + DEC_SEQ:, :] = jnp.zeros((w, wkv), BF16)
        vw_scr[0:w, :] = jnp.zeros((w, 2 * wkv), BF16)
        vw_scr[w + DEC_SEQ:, :] = jnp.zeros((w, 2 * wkv), BF16)
        for kh in range(KV_HEADS):
            ks = slice(kh * LANE, (kh + 1) * LANE)
            vs = slice(2 * kh * LANE, 2 * (kh + 1) * LANE)
            kw_scr[w:w + DEC_SEQ, ks] = _rope(k_ref[:, ks], kc_ref[...], kl_ref[...], kh_ref[...],
                                              quarter).astype(BF16)
            vw_scr[w:w + DEC_SEQ, vs] = _with_ones(v_ref[:, ks].astype(BF16))
            head_rows = pl.ds(kh, PAST_LEN, stride=KV_HEADS)
            vc_scr[:, vs] = _with_ones(cv_ref[head_rows, :].astype(BF16))
            kc_scr[:, ks] = ck_ref[head_rows, :].astype(BF16)

    qi = lax.broadcasted_iota(jnp.int32, (Q_PER_KV * w, 3 * w), 0) % w
    kj = lax.broadcasted_iota(jnp.int32, (Q_PER_KV * w, 3 * w), 1)
    scale = HEAD_DIM ** -0.5
    for blk in range(LAT_QBLOCKS_PER_STEP):
        n = pl.program_id(1) * LAT_QBLOCKS_PER_STEP + blk
        rows = slice(blk * w, (blk + 1) * w)
        win = pl.ds(pl.multiple_of(n * w, w), 3 * w)
        kpos = n * w - w + kj
        valid = (kj >= qi) & (kj <= qi + 2 * w) & (kpos >= 0) & (kpos < DEC_SEQ)
        for kh in range(KV_HEADS):
            ks = slice(kh * LANE, (kh + 1) * LANE)
            vs = slice(2 * kh * LANE, 2 * (kh + 1) * LANE)
            q = jnp.concatenate(
                [(_rope(q_ref[rows, _head_slice(kh, gq)], qc_ref[rows, :], ql_ref[rows, :], qh_ref[rows, :],
                        quarter) * scale).astype(BF16) for gq in range(Q_PER_KV)], axis=0)
            sink = _sink_column(sink_ref, kh, w)
            s_c = _dot_nt(q, kc_scr[:, ks])
            s_w = jnp.where(valid, _dot_nt(q, kw_scr[win, ks]), NEG_INF)
            m = jnp.maximum(jnp.maximum(jnp.max(s_c, axis=-1, keepdims=True),
                                        jnp.max(s_w, axis=-1, keepdims=True)), sink)
            ov = (_dot(jnp.exp(s_c - m).astype(BF16), vc_scr[:, vs])
                  + _dot(jnp.exp(s_w - m).astype(BF16), vw_scr[win, vs]))
            o = ov[:, :LANE] * (1.0 / (ov[:, LANE:] + jnp.exp(sink - m)))
            for gq in range(Q_PER_KV):
                o_ref[rows, _head_slice(kh, gq)] = o[gq * w:(gq + 1) * w].astype(o_ref.dtype)


def gqa_latent(proj, cache_k, cache_v, tables, sinks):
    wq = N_HEADS * HEAD_DIM
    wkv = KV_HEADS * HEAD_DIM
    tq = LAT_QBLOCKS_PER_STEP * WINDOW
    nq = DEC_SEQ // tq
    row0 = TP // tq
    kv_row0 = TP // DEC_SEQ
    cos, sin_lo, sin_hi = tables
    full = pl.BlockSpec((DEC_SEQ, LANE), lambda b, i: (0, 0))
    qtab = pl.BlockSpec((tq, LANE), lambda b, i: (i, 0))
    cache = pl.BlockSpec((None, PAST_LEN * KV_HEADS, LANE), lambda b, i: (b, 0, 0))
    return pl.pallas_call(
        _gqa_lat_kernel,
        grid=(DEC_BATCH, nq),
        in_specs=[pl.BlockSpec(memory_space=pltpu.SMEM),
                  pl.BlockSpec((tq, wq), lambda b, i: (row0 + b * nq + i, 0)),
                  pl.BlockSpec((DEC_SEQ, wkv), lambda b, i: (kv_row0 + b, wq // wkv)),
                  pl.BlockSpec((DEC_SEQ, wkv), lambda b, i: (kv_row0 + b, wq // wkv + 1)),
                  cache, cache, full, full, full, qtab, qtab, qtab],
        out_specs=pl.BlockSpec((tq, wq), lambda b, i: (b * nq + i, 0)),
        out_shape=jax.ShapeDtypeStruct((TS, wq), BF16),
        scratch_shapes=[pltpu.VMEM((DEC_SEQ + 2 * WINDOW, wkv), BF16),
                        pltpu.VMEM((DEC_SEQ + 2 * WINDOW, 2 * wkv), BF16),
                        pltpu.VMEM((PAST_LEN, wkv), BF16),
                        pltpu.VMEM((PAST_LEN, 2 * wkv), BF16)],
        compiler_params=_params(("arbitrary", "arbitrary")),
        name="gqa_latent",
    )(sinks, proj, proj, proj, cache_k, cache_v, cos, sin_lo, sin_hi, cos, sin_lo, sin_hi)


def _even_layer(xc, xs, mod, j, layer, norm_g, w_in_even, lam_vec, subln_g, w_gate1, w_gate2, b_gate, gla_norm_g,
                w_out_even, cache_a_k, cache_a_v, state_b, tables):
    lam_init = 0.8 - 0.6 * math.exp(-0.3 * layer)
    ng = HB * GLA_DK
    wg1 = jnp.zeros((D_MODEL, LANE), F32).at[:, :2 * GLA_GATE_RANK].set(
        jnp.concatenate([w_gate1[j, 0], w_gate1[j, 1]], axis=-1)).astype(BF16)
    wg2 = jnp.zeros((2, LANE, ng), F32)
    for e in range(2):
        wg2 = wg2.at[e, e * GLA_GATE_RANK:(e + 1) * GLA_GATE_RANK, :].set(w_gate2[j, e])
    h, gates = norm_mod_gate(xc, xs, norm_g[layer, 0], mod, 0, 1, wg1, wg2.astype(BF16),
                             b_gate[j].reshape(2, 1, ng))
    proj = ws_matmul(h, w_in_even, j)
    ya_c, new_k, new_v = diff_attention_context(proj, lam_vec, subln_g[j], lam_init)
    ya_s = diff_attention_latent(proj, cache_a_k[:, j].reshape(DEC_BATCH, PAST_LEN * HA, 2 * DIFF_HD),
                                 cache_a_v[:, j].reshape(DEC_BATCH, PAST_LEN * HA, DIFF_VD),
                                 tables, lam_vec, subln_g[j], lam_init)
    yb_c, s_fin = bidir_gla(proj, gates, jnp.zeros((1, 2, HB, GLA_DK, GLA_DV), F32), gla_norm_g[j],
                            BATCH, SEQ, 0)
    yb_s, _ = bidir_gla(proj, gates, state_b[:, j], gla_norm_g[j], DEC_BATCH, DEC_SEQ, TP // DEC_SEQ)
    x = ws_matmul_residual([(ya_c, ya_s), (yb_c, yb_s)], w_out_even, j, (xc, xs), mod, 2)
    return (x, new_k.reshape(BATCH, SEQ, HA, 2 * DIFF_HD), new_v.reshape(BATCH, SEQ, HA, DIFF_VD), s_fin)


def _odd_layer(x, mod, j, layer, norm_g, w_in_odd, sinks, w_out_odd, cache_c_k, cache_c_v, tables):
    proj = ws_matmul(norm_mod(x, norm_g[layer, 0], mod, 0, 1), w_in_odd, j)
    o_c, new_k, new_v = gqa_context(proj, sinks[j])
    o_s = gqa_latent(proj, cache_c_k[:, j].reshape(DEC_BATCH, PAST_LEN * KV_HEADS, HEAD_DIM),
                     cache_c_v[:, j].reshape(DEC_BATCH, PAST_LEN * KV_HEADS, HEAD_DIM), tables, sinks[j])
    x = ws_matmul_residual([(o_c, o_s)], w_out_odd, j, (x,), mod, 2)
    return (x, new_k.reshape(BATCH, SEQ, KV_HEADS, HEAD_DIM), new_v.reshape(BATCH, SEQ, KV_HEADS, HEAD_DIM))


def _ffn(x, mod, layer, norm_g, w_ffn_in, w_ffn_out):
    act = ws_swiglu(norm_mod(x, norm_g[layer, 1], mod, 3, 4), w_ffn_in, layer)
    return ws_matmul_residual([(act,)], w_ffn_out, layer, (x,), mod, 5, tm=512, tn=512)


def kernel(x_prompt, x_sample, c, cache_a_k, cache_a_v, state_b, cache_c_k, cache_c_v, c_ctx, w_ada, b_ada,
           norm_g, w_in_even, lam_q1, lam_k1, lam_q2, lam_k2, subln_g, w_gate1, w_gate2, b_gate, gla_norm_g,
           w_out_even, w_in_odd, sinks, w_out_odd, w_ffn_in, w_ffn_out, final_norm_g):
    assert DEPTH == 2, "layer 0 reads the two input streams, layer 1 the fused token matrix"
    cvec = jnp.concatenate([c_ctx[None, :], c, jnp.zeros((MOD_ROWS - 1 - DEC_BATCH, D_MODEL), F32)], axis=0)
    mods = adaln(cvec, w_ada, b_ada).reshape(DEPTH, MOD_ROWS, 1, 6 * D_MODEL)
    tab_d = rope_tables(DEC_SEQ, DIFF_HD)
    tab_h = rope_tables(DEC_SEQ, HEAD_DIM)

    lam_vec = jnp.stack([lam_q1[0], lam_k1[0], lam_q2[0], lam_k2[0]], axis=0)
    x, ak, av, sb = _even_layer(x_prompt.reshape(TP, D_MODEL), x_sample.reshape(TS, D_MODEL), mods[0], 0, 0,
                                norm_g, w_in_even, lam_vec, subln_g, w_gate1, w_gate2, b_gate, gla_norm_g,
                                w_out_even, cache_a_k, cache_a_v, state_b, tab_d)
    x = _ffn(x, mods[0], 0, norm_g, w_ffn_in, w_ffn_out)
    x, ck, cv = _odd_layer(x, mods[1], 0, 1, norm_g, w_in_odd, sinks, w_out_odd, cache_c_k, cache_c_v, tab_h)
    x = _ffn(x, mods[1], 1, norm_g, w_ffn_in, w_ffn_out)

    y_prompt = final_norm(x, final_norm_g, 0, TP).reshape(BATCH, SEQ, D_MODEL)
    y_sample = final_norm(x, final_norm_g, TP, TS).reshape(DEC_BATCH, DEC_SEQ, D_MODEL)
    return (y_prompt, y_sample, ak[:, None], av[:, None], sb[:, None], ck[:, None], cv[:, None])
```

```python
import functools
import math

import jax
import jax.numpy as jnp
from jax import lax
from jax.experimental import pallas as pl
from jax.experimental.pallas import tpu as pltpu

D_MODEL = 2048
BATCH = 32
SEQ = 256
DEPTH = 2
DEC_BATCH = 2
DEC_SEQ = 2048
PAST_LEN = 512
GRID_W = 64
HEAD_DIM = 128
N_HEADS = D_MODEL // HEAD_DIM
HA = N_HEADS // 2
DIFF_HD = HEAD_DIM // 2
DIFF_VD = HEAD_DIM
HB = N_HEADS // 2
GLA_DK = HEAD_DIM // 2
GLA_DV = HEAD_DIM
GLA_GATE_RANK = 16
GLA_TAU = 16.0
GLA_CHUNK = 64
GLA_SUPER = 4
CTX_BATCHES_PER_STEP = 4
LAT_QBLOCKS_PER_STEP = 2
DIFF_LAT_CHAINS = 4
KV_HEADS = N_HEADS // 4
Q_PER_KV = N_HEADS // KV_HEADS
WINDOW = 128
D_FF = -(-8 * D_MODEL // (3 * 256)) * 256
ROPE_BASE = 10000.0
EPS = 1e-6
NEG_INF = -1e30
EVEN_IN = 2 * HA * 2 * DIFF_HD + HA * DIFF_VD + 2 * HB * GLA_DK + 2 * HB * GLA_DV
ODD_IN = (N_HEADS + 2 * KV_HEADS) * HEAD_DIM

TP = BATCH * SEQ
TS = DEC_BATCH * DEC_SEQ
T = TP + TS
LANE = 128
MOD_ROWS = 8
NORM_ROWS = 16
NORM_UNROLL = 8
OUT_PROJ_ROWS = 256
VMEM_LIMIT = 56 * 1024 * 1024

F32 = jnp.float32
BF16 = jnp.bfloat16


def _params(semantics, vmem=VMEM_LIMIT):
    return pltpu.CompilerParams(dimension_semantics=semantics, vmem_limit_bytes=vmem)


def _dot(a, b):
    return jnp.dot(a, b, preferred_element_type=F32)


def _dot_nt(a, b):
    return lax.dot_general(a, b, (((1,), (1,)), ((), ())), preferred_element_type=F32)


def _group_of_tile(i, tm):
    r = i * tm
    return jnp.where(r < TP, 0, 1 + (r - TP) // DEC_SEQ)


def _rms(x, g):
    return (x * lax.rsqrt(jnp.mean(x * x, axis=-1, keepdims=True) + EPS)) * g


def _adaln_kernel(c_ref, w_ref, b_ref, o_ref):
    c = c_ref[...]
    s = c * jax.nn.sigmoid(c)
    o_ref[...] = _dot(s.astype(BF16), w_ref[...].astype(BF16)) + b_ref[...]


def adaln(cvec, w_ada, b_ada, tn=512):
    n = w_ada.shape[-1]
    return pl.pallas_call(
        _adaln_kernel,
        grid=(DEPTH, n // tn),
        in_specs=[
            pl.BlockSpec((MOD_ROWS, D_MODEL), lambda l, j: (0, 0)),
            pl.BlockSpec((None, D_MODEL, tn), lambda l, j: (l, 0, j)),
            pl.BlockSpec((None, 1, tn), lambda l, j: (l, 0, j)),
        ],
        out_specs=pl.BlockSpec((None, MOD_ROWS, tn), lambda l, j: (l, 0, j)),
        out_shape=jax.ShapeDtypeStruct((DEPTH, MOD_ROWS, n), F32),
        compiler_params=_params(("arbitrary", "arbitrary")),
        name="adaln",
    )(cvec, w_ada, b_ada.reshape(DEPTH, 1, n))


def _by_row_groups(x_ref, o_ref, fn):
    def body(r, carry):
        rs = pl.ds(pl.multiple_of(r * NORM_ROWS, NORM_ROWS), NORM_ROWS)
        o_ref[rs, :] = fn(x_ref[rs, :]).astype(o_ref.dtype)
        return carry

    lax.fori_loop(0, x_ref.shape[0] // NORM_ROWS, body, 0, unroll=NORM_UNROLL)


def _norm_mod_into(h_ref, x_ref, g_ref, shift_ref, scale_ref):
    _by_row_groups(x_ref, h_ref, lambda x: _rms(x, g_ref[...]) * (1 + scale_ref[...]) + shift_ref[...])


def _norm_mod_kernel(x_ref, g_ref, shift_ref, scale_ref, h_ref):
    _norm_mod_into(h_ref, x_ref, g_ref, shift_ref, scale_ref)


def _norm_mod_gate_kernel(xc_ref, xs_ref, g_ref, shift_ref, scale_ref, wg1_ref, wg2_ref, bg_ref,
                          h_ref, gate_ref, *, ctx_tiles):
    @pl.when(pl.program_id(0) < ctx_tiles)
    def _():
        _norm_mod_into(h_ref, xc_ref, g_ref, shift_ref, scale_ref)

    @pl.when(pl.program_id(0) >= ctx_tiles)
    def _():
        _norm_mod_into(h_ref, xs_ref, g_ref, shift_ref, scale_ref)

    low = _dot(h_ref[...], wg1_ref[...]).astype(BF16)
    for e in range(2):
        logit = _dot(low, wg2_ref[e]) + bg_ref[e]
        log_sig = jnp.minimum(logit, 0.0) - jnp.log(1.0 + jnp.exp(-jnp.abs(logit)))
        gate_ref[e] = log_sig / GLA_TAU


def _mod_specs(tm, shift_idx, scale_idx):
    def spec(idx):
        return pl.BlockSpec((None, 1, D_MODEL), lambda i: (_group_of_tile(i, tm), 0, idx))
    return [pl.BlockSpec((1, D_MODEL), lambda i: (0, 0)), spec(shift_idx), spec(scale_idx)]


def norm_mod(x, g, mod, shift_idx, scale_idx, tm=1024):
    return pl.pallas_call(
        _norm_mod_kernel,
        grid=(T // tm,),
        in_specs=[pl.BlockSpec((tm, D_MODEL), lambda i: (i, 0))] + _mod_specs(tm, shift_idx, scale_idx),
        out_specs=pl.BlockSpec((tm, D_MODEL), lambda i: (i, 0)),
        out_shape=jax.ShapeDtypeStruct((T, D_MODEL), BF16),
        compiler_params=_params(("arbitrary",)),
        name="norm_mod",
    )(x, g.reshape(1, D_MODEL), mod, mod)


def norm_mod_gate(xc, xs, g, mod, shift_idx, scale_idx, wg1, wg2, bg, tm=512):
    ng = HB * GLA_DK
    ctx_tiles = TP // tm
    return pl.pallas_call(
        functools.partial(_norm_mod_gate_kernel, ctx_tiles=ctx_tiles),
        grid=(T // tm,),
        in_specs=[pl.BlockSpec((tm, D_MODEL), lambda i: (jnp.minimum(i, ctx_tiles - 1), 0)),
                  pl.BlockSpec((tm, D_MODEL), lambda i: (jnp.maximum(i - ctx_tiles, 0), 0))]
        + _mod_specs(tm, shift_idx, scale_idx)
        + [pl.BlockSpec((D_MODEL, LANE), lambda i: (0, 0)),
           pl.BlockSpec((2, LANE, ng), lambda i: (0, 0, 0)),
           pl.BlockSpec((2, 1, ng), lambda i: (0, 0, 0))],
        out_specs=[pl.BlockSpec((tm, D_MODEL), lambda i: (i, 0)),
                   pl.BlockSpec((2, tm, ng), lambda i: (0, i, 0))],
        out_shape=[jax.ShapeDtypeStruct((T, D_MODEL), BF16), jax.ShapeDtypeStruct((2, T, ng), F32)],
        compiler_params=_params(("arbitrary",)),
        name="norm_mod_gate",
    )(xc, xs, g.reshape(1, D_MODEL), mod, mod, wg1, wg2, bg)


def _cast_weights(w_refs, w_scrs):
    @pl.when(pl.program_id(1) == 0)
    def _():
        for w_ref, w_scr in zip(w_refs, w_scrs):
            w_scr[...] = w_ref[...].astype(BF16)


def _ws_plain_kernel(h_ref, w_ref, o_ref, w_scr):
    _cast_weights([w_ref], [w_scr])
    o_ref[...] = _dot(h_ref[...], w_scr[...])


def _ws_swiglu_kernel(h_ref, wg_ref, wu_ref, o_ref, wg_scr, wu_scr):
    _cast_weights([wg_ref, wu_ref], [wg_scr, wu_scr])
    h = h_ref[...]
    gate = _dot(h, wg_scr[...])
    up = _dot(h, wu_scr[...])
    o_ref[...] = (gate * jax.nn.sigmoid(gate) * up).astype(o_ref.dtype)


def _ws_res_kernel(*refs, lhs_arity, x_arity, ctx_tiles, next_norm):
    n_l = sum(lhs_arity)
    w_ref = refs[n_l]
    x_refs = refs[n_l + 1:n_l + 1 + x_arity]
    rest = refs[n_l + 1 + x_arity:]
    if next_norm:
        gate_ref, g2_ref, shift2_ref, scale2_ref, o_ref, h_ref, w_scr = rest
    else:
        gate_ref, o_ref, w_scr = rest
    groups, at = [], 0
    for a in lhs_arity:
        groups.append(refs[at:at + a])
        at += a
    _cast_weights([w_ref], [w_scr])

    def emit(side):
        acc, k0 = None, 0
        for grp in groups:
            ref = grp[side] if len(grp) == 2 else grp[0]
            part = _dot(ref[...], w_scr[k0:k0 + ref.shape[1], :])
            acc = part if acc is None else acc + part
            k0 += ref.shape[1]
        x_ref = x_refs[side] if x_arity == 2 else x_refs[0]
        x_new = x_ref[...] + gate_ref[...] * acc
        o_ref[...] = x_new
        if next_norm:
            h_ref[...] = (_rms(x_new, g2_ref[...]) * (1 + scale2_ref[...]) + shift2_ref[...]).astype(BF16)

    if x_arity == 2 or 2 in lhs_arity:
        @pl.when(pl.program_id(1) < ctx_tiles)
        def _():
            emit(0)

        @pl.when(pl.program_id(1) >= ctx_tiles)
        def _():
            emit(1)
    else:
        emit(0)


def _rows_specs(arrays, tm, width, col):
    if len(arrays) == 1:
        return [pl.BlockSpec((tm, width), lambda j, i: (i, col(j)))]
    ctx_tiles = TP // tm
    return [pl.BlockSpec((tm, width), lambda j, i: (jnp.minimum(i, ctx_tiles - 1), col(j))),
            pl.BlockSpec((tm, width), lambda j, i: (jnp.maximum(i - ctx_tiles, 0), col(j)))]


def ws_matmul(h, w_stack, layer, tm=1024, tn=1024):
    kdim, n = w_stack.shape[1:]
    return pl.pallas_call(
        _ws_plain_kernel,
        grid=(n // tn, T // tm),
        in_specs=[pl.BlockSpec((tm, kdim), lambda j, i: (i, 0)),
                  pl.BlockSpec((None, kdim, tn), lambda j, i: (layer, 0, j))],
        out_specs=pl.BlockSpec((tm, tn), lambda j, i: (i, j)),
        out_shape=jax.ShapeDtypeStruct((T, n), F32),
        scratch_shapes=[pltpu.VMEM((kdim, tn), BF16)],
        compiler_params=_params(("arbitrary", "arbitrary")),
        name="ws_matmul",
    )(h, w_stack)


def ws_swiglu(h, w_stack, layer, tm=1024, tn=512):
    kdim = w_stack.shape[1]
    nf = D_FF // tn
    return pl.pallas_call(
        _ws_swiglu_kernel,
        grid=(nf, T // tm),
        in_specs=[pl.BlockSpec((tm, kdim), lambda j, i: (i, 0)),
                  pl.BlockSpec((None, kdim, tn), lambda j, i: (layer, 0, j)),
                  pl.BlockSpec((None, kdim, tn), lambda j, i: (layer, 0, j + nf))],
        out_specs=pl.BlockSpec((tm, tn), lambda j, i: (i, j)),
        out_shape=jax.ShapeDtypeStruct((T, D_FF), BF16),
        scratch_shapes=[pltpu.VMEM((kdim, tn), BF16), pltpu.VMEM((kdim, tn), BF16)],
        compiler_params=_params(("arbitrary", "arbitrary")),
        name="ws_swiglu",
    )(h, w_stack, w_stack)


def ws_matmul_residual(lhs_groups, w_stack, layer, x, mod, gate_idx, tm=512, tn=1024, next_norm=None):
    kdim, n = w_stack.shape[1:]
    if next_norm is not None:
        tn = n
    gate_blk = n // tn
    specs, args = [], []
    for grp in lhs_groups:
        specs += _rows_specs(grp, tm, grp[0].shape[1], lambda j: 0)
        args += list(grp)
    w_mode = {} if next_norm is None else dict(pipeline_mode=pl.Buffered(1))
    specs.append(pl.BlockSpec((None, kdim, tn), lambda j, i: (layer, 0, j), **w_mode))
    specs += _rows_specs(x, tm, tn, lambda j: j)
    specs.append(pl.BlockSpec((None, 1, tn), lambda j, i: (_group_of_tile(i, tm), 0, gate_idx * gate_blk + j)))
    tail = [mod]
    out_specs = [pl.BlockSpec((tm, tn), lambda j, i: (i, j))]
    out_shape = [jax.ShapeDtypeStruct((T, n), F32)]
    if next_norm is not None:
        gain, shift_idx, scale_idx = next_norm
        specs.append(pl.BlockSpec((1, n), lambda j, i: (0, 0)))
        specs += [pl.BlockSpec((None, 1, n), lambda j, i, idx=idx: (_group_of_tile(i, tm), 0, idx))
                  for idx in (shift_idx, scale_idx)]
        tail += [gain.reshape(1, n), mod, mod]
        out_specs.append(pl.BlockSpec((tm, tn), lambda j, i: (i, j)))
        out_shape.append(jax.ShapeDtypeStruct((T, n), BF16))
    outs = pl.pallas_call(
        functools.partial(_ws_res_kernel, lhs_arity=tuple(len(grp) for grp in lhs_groups), x_arity=len(x),
                          ctx_tiles=TP // tm, next_norm=next_norm is not None),
        grid=(n // tn, T // tm),
        in_specs=specs,
        out_specs=out_specs,
        out_shape=out_shape,
        scratch_shapes=[pltpu.VMEM((kdim, tn), BF16)],
        compiler_params=_params(("arbitrary", "arbitrary")),
        name="ws_matmul_residual",
    )(*args, w_stack, *x, *tail)
    return outs[0] if next_norm is None else tuple(outs)


def _final_norm_kernel(x_ref, g_ref, o_ref):
    o_ref[...] = _rms(x_ref[...], g_ref[...])


def final_norm(x, g, row0, n_rows, tm=1024):
    blk0 = row0 // tm
    return pl.pallas_call(
        _final_norm_kernel,
        grid=(n_rows // tm,),
        in_specs=[pl.BlockSpec((tm, D_MODEL), lambda i: (blk0 + i, 0)),
                  pl.BlockSpec((1, D_MODEL), lambda i: (0, 0))],
        out_specs=pl.BlockSpec((tm, D_MODEL), lambda i: (i, 0)),
        out_shape=jax.ShapeDtypeStruct((n_rows, D_MODEL), F32),
        compiler_params=_params(("arbitrary",)),
        name="final_norm",
    )(x, g.reshape(1, D_MODEL))


def _grid_angles(n_tok, rot_dim):
    t = jnp.arange((n_tok // GRID_W) * GRID_W)
    row = (t // GRID_W).astype(F32)
    col = (t % GRID_W).astype(F32)
    half = rot_dim // 2
    inv = ROPE_BASE ** (-jnp.arange(0, half, 2, dtype=F32) / half)
    return row[:, None] * inv[None], col[:, None] * inv[None]


def rope_tables(n_tok, rot_dim):
    ang_row, ang_col = _grid_angles(n_tok, rot_dim)
    zeros = jnp.zeros_like(ang_row)
    reps = LANE // rot_dim

    def lanes(r1, r2, c1, c2):
        return jnp.tile(jnp.concatenate([r1, r2, c1, c2], axis=-1), (1, reps))

    cr, sr, cc, sc = jnp.cos(ang_row), jnp.sin(ang_row), jnp.cos(ang_col), jnp.sin(ang_col)
    return lanes(cr, cr, cc, cc), lanes(-sr, zeros, -sc, zeros), lanes(zeros, sr, zeros, sc)


def _rope(x, cos, sin_lo, sin_hi, quarter):
    return (x * cos + pltpu.roll(x, LANE - quarter, 1) * sin_lo + pltpu.roll(x, quarter, 1) * sin_hi)


def _diff_lambda(lam_ref, lam_init):
    e1 = jnp.exp(jnp.sum(lam_ref[0:1, :] * lam_ref[1:2, :], axis=-1, keepdims=True))
    e2 = jnp.exp(jnp.sum(lam_ref[2:3, :] * lam_ref[3:4, :], axis=-1, keepdims=True))
    return e1 - e2 + lam_init


def _with_ones(v_bf):
    return jnp.concatenate([v_bf, jnp.ones(v_bf.shape, BF16)], axis=1)


def _softmax_av(q_bf, k_bf, v1_bf):
    s = _dot_nt(q_bf, k_bf)
    e = jnp.exp(s - jnp.max(s, axis=-1, keepdims=True))
    if v1_bf.shape[1] == LANE:
        return _dot(e.astype(BF16), v1_bf) * (1.0 / jnp.sum(e, axis=-1, keepdims=True))
    ov = _dot(e.astype(BF16), v1_bf)
    return ov[:, :LANE] * (1.0 / ov[:, LANE:])


def _diff_head(q, k_bf, v1_bf, lam, subln_g, lam_init, stack_maps):
    n = q.shape[0]
    first = lax.broadcasted_iota(jnp.int32, q.shape, 1) < DIFF_HD
    q0 = jnp.where(first, q, 0.0).astype(BF16)
    q1 = jnp.where(first, 0.0, q).astype(BF16)
    if stack_maps:
        r = _softmax_av(jnp.concatenate([q0, q1], axis=0), k_bf, v1_bf)
        o = r[:n] - lam * r[n:]
    else:
        o = _softmax_av(q0, k_bf, v1_bf) - lam * _softmax_av(q1, k_bf, v1_bf)
    return _rms(o, subln_g) * (1.0 - lam_init)


def _diff_ctx_kernel(q_ref, k_ref, v_ref, lam_ref, sg_ref, o_ref, ko_ref, vo_ref, *, lam_init):
    lam = _diff_lambda(lam_ref, lam_init)
    scale = DIFF_HD ** -0.5
    n_rows = CTX_BATCHES_PER_STEP * SEQ
    for h in range(HA):
        sl = slice(h * LANE, (h + 1) * LANE)
        ko_ref[pl.ds(h, n_rows, stride=HA), :] = k_ref[:, sl]
        vo_ref[pl.ds(h, n_rows, stride=HA), :] = v_ref[:, sl]
    for bb in range(CTX_BATCHES_PER_STEP):
        rows = slice(bb * SEQ, (bb + 1) * SEQ)
        for h in range(HA):
            sl = slice(h * LANE, (h + 1) * LANE)
            y = _diff_head(q_ref[rows, sl] * scale, k_ref[rows, sl].astype(BF16),
                           _with_ones(v_ref[rows, sl].astype(BF16)), lam, sg_ref[...], lam_init, stack_maps=True)
            o_ref[rows, sl] = y.astype(o_ref.dtype)


def diff_attention_context(proj, lam_vec, subln_g, lam_init):
    w = HA * LANE

    def blk(c):
        return pl.BlockSpec((CTX_BATCHES_PER_STEP * SEQ, w), lambda b: (b, c))

    cache = pl.BlockSpec((CTX_BATCHES_PER_STEP * SEQ * HA, LANE), lambda b: (b, 0))
    return pl.pallas_call(
        functools.partial(_diff_ctx_kernel, lam_init=lam_init),
        grid=(BATCH // CTX_BATCHES_PER_STEP,),
        in_specs=[blk(0), blk(1), blk(2),
                  pl.BlockSpec((4, DIFF_HD), lambda b: (0, 0)),
                  pl.BlockSpec((1, DIFF_VD), lambda b: (0, 0))],
        out_specs=[blk(0), cache, cache],
        out_shape=[jax.ShapeDtypeStruct((TP, w), BF16), jax.ShapeDtypeStruct((TP * HA, LANE), F32),
                   jax.ShapeDtypeStruct((TP * HA, LANE), F32)],
        compiler_params=_params(("arbitrary",)),
        name="diff_attention_context",
    )(proj, proj, proj, lam_vec, subln_g.reshape(1, DIFF_VD))


def _diff_lat_kernel(q_ref, k_ref, v_ref, ck_ref, cv_ref, kc_ref, kl_ref, kh_ref, qc_ref, ql_ref, qh_ref,
                     lam_ref, sg_ref, o_ref, k_scr, v_scr, *, lam_init):
    quarter = DIFF_HD // 4

    @pl.when(pl.program_id(2) == 0)
    def _():
        k_scr[0:PAST_LEN, :] = ck_ref[...].astype(BF16)
        k_scr[PAST_LEN:, :] = _rope(k_ref[...], kc_ref[...], kl_ref[...], kh_ref[...], quarter).astype(BF16)
        v_scr[0:PAST_LEN, :] = cv_ref[...].astype(BF16)
        v_scr[PAST_LEN:, :] = v_ref[...].astype(BF16)

    lam = _diff_lambda(lam_ref, lam_init)
    sub = q_ref.shape[0] // DIFF_LAT_CHAINS
    for c in range(DIFF_LAT_CHAINS):
        rs = slice(c * sub, (c + 1) * sub)
        q = _rope(q_ref[rs, :], qc_ref[rs, :], ql_ref[rs, :], qh_ref[rs, :], quarter) * (DIFF_HD ** -0.5)
        y = _diff_head(q, k_scr[...], v_scr[...], lam, sg_ref[...], lam_init, stack_maps=False)
        o_ref[rs, :] = y.astype(o_ref.dtype)


def diff_attention_latent(proj, cache_k, cache_v, tables, lam_vec, subln_g, lam_init, tq=1024):
    nq = DEC_SEQ // tq
    row0 = TP // tq
    kv_row0 = TP // DEC_SEQ
    cos, sin_lo, sin_hi = tables
    full = pl.BlockSpec((DEC_SEQ, LANE), lambda b, h, i: (0, 0))
    qtab = pl.BlockSpec((tq, LANE), lambda b, h, i: (i, 0))
    cache = pl.BlockSpec((None, PAST_LEN, LANE), lambda b, h, i: (b, 0, h))
    return pl.pallas_call(
        functools.partial(_diff_lat_kernel, lam_init=lam_init),
        grid=(DEC_BATCH, HA, nq),
        in_specs=[pl.BlockSpec((tq, LANE), lambda b, h, i: (row0 + b * nq + i, h)),
                  pl.BlockSpec((DEC_SEQ, LANE), lambda b, h, i: (kv_row0 + b, HA + h)),
                  pl.BlockSpec((DEC_SEQ, LANE), lambda b, h, i: (kv_row0 + b, 2 * HA + h)),
                  cache, cache, full, full, full, qtab, qtab, qtab,
                  pl.BlockSpec((4, DIFF_HD), lambda b, h, i: (0, 0)),
                  pl.BlockSpec((1, DIFF_VD), lambda b, h, i: (0, 0))],
        out_specs=pl.BlockSpec((tq, LANE), lambda b, h, i: (b * nq + i, h)),
        out_shape=jax.ShapeDtypeStruct((TS, HA * LANE), BF16),
        scratch_shapes=[pltpu.VMEM((PAST_LEN + DEC_SEQ, LANE), BF16),
                        pltpu.VMEM((PAST_LEN + DEC_SEQ, LANE), BF16)],
        compiler_params=_params(("arbitrary", "arbitrary", "arbitrary")),
        name="diff_attention_latent",
    )(proj, proj, proj, cache_k, cache_v, cos, sin_lo, sin_hi, cos, sin_lo, sin_hi,
      lam_vec, subln_g.reshape(1, DIFF_VD))


def _gla_kernel(q_ref, k_ref, v_ref, r_ref, g_ref, s0_ref, gg_ref, y_ref, sfin_ref, of_scr, ob_scr, st_scr,
                *, seq, pairs):
    c = GLA_CHUNK
    sb = GLA_SUPER * c
    n_super = seq // sb
    rows = lax.broadcasted_iota(jnp.int32, (sb, sb), 0)
    cols = lax.broadcasted_iota(jnp.int32, (sb, sb), 1)
    same_chunk = (rows // c) == (cols // c)
    keep = (same_chunk & (rows >= cols), same_chunk & (cols >= rows))
    lane = lax.broadcasted_iota(jnp.int32, (sb, LANE), 1)
    own = (lane < GLA_DK, lane >= GLA_DK)
    chunk_of_row = lax.broadcasted_iota(jnp.int32, (sb, LANE), 0) // c
    st_rows = lax.broadcasted_iota(jnp.int32, (2 * GLA_DV, LANE), 0)
    st_lane = lax.broadcasted_iota(jnp.int32, (2 * GLA_DV, LANE), 1)
    st_own = (st_rows < GLA_DV) == (st_lane < GLA_DK)
    zpad = jnp.zeros((GLA_DK, GLA_DV), F32)

    def expand(x):
        return jnp.concatenate([jnp.where(chunk_of_row == ci, x, 0.0) for ci in range(GLA_SUPER)],
                               axis=1).astype(BF16)

    for p in range(pairs):
        for d in range(2):
            s0 = [s0_ref[d, 2 * p + hh] for hh in range(2)]
            st_scr[p, d] = jnp.concatenate([jnp.concatenate([s0[0], zpad], axis=0).T,
                                            jnp.concatenate([zpad, s0[1]], axis=0).T], axis=0)

    def rows_of(n, d):
        r0 = (n if d == 0 else n_super - 1 - n) * sb
        return pl.ds(r0, sb) if n_super == 1 else pl.ds(pl.multiple_of(r0, sb), sb)

    def chunk_totals(prefix):
        ends = [prefix[ci * c + c - 1:ci * c + c, :] for ci in range(GLA_SUPER)]
        return ends, jnp.concatenate([jnp.broadcast_to(e, (c, LANE)) for e in ends], axis=0)

    def log_decays(n, p):
        kl = slice(p * LANE, (p + 1) * LANE)
        g_b = g_ref[1, rows_of(n, 1), kl]
        prefix = jnp.concatenate([g_ref[0, rows_of(n, 0), kl], g_b], axis=1)
        pos = lax.broadcasted_iota(jnp.int32, prefix.shape, 0) % c
        step = 1
        while step < c:
            prefix = prefix + jnp.where(pos >= step, pltpu.roll(prefix, step, 0), 0.0)
            step *= 2
        ends_f, total_f = chunk_totals(prefix[:, :LANE])
        ends_b, total_b = chunk_totals(prefix[:, LANE:])
        return (prefix[:, :LANE], ends_f, total_f), (total_b - prefix[:, LANE:] + g_b, ends_b, total_b)

    def one_pair_direction(n, p, d, decays):
        kl = slice(p * LANE, (p + 1) * LANE)
        vl = slice(2 * p * GLA_DV, 2 * (p + 1) * GLA_DV)
        rs = rows_of(n, d)
        b, ends, total = decays
        q_in = (q_ref[rs, kl] * (GLA_DK ** -0.5)) * jnp.exp(b)
        kk = k_ref[rs, kl]
        k_in = (kk * jnp.exp(-b)).astype(BF16)
        k_end = kk * jnp.exp(total - b)
        v = v_ref[rs, vl]
        v_bf = v.astype(BF16)
        kv_all = _dot(v.T.astype(BF16), expand(k_end))
        q2 = jnp.concatenate([jnp.where(own[0], q_in, 0.0), jnp.where(own[1], q_in, 0.0)], axis=0)
        a2 = _dot_nt(q2.astype(BF16), k_in)
        o = jnp.concatenate(
            [_dot(jnp.where(keep[d], a2[hh * sb:(hh + 1) * sb], 0.0).astype(BF16),
                  v_bf[:, hh * GLA_DV:(hh + 1) * GLA_DV]) for hh in range(2)], axis=1)
        st = st_scr[p, d]
        entering = [None] * GLA_SUPER
        for ci in (range(GLA_SUPER) if d == 0 else range(GLA_SUPER - 1, -1, -1)):
            entering[ci] = st
            st = st * jnp.exp(ends[ci]) + jnp.where(st_own, kv_all[:, ci * LANE:(ci + 1) * LANE], 0.0)
        st_scr[p, d] = st
        o += _dot_nt(expand(q_in), jnp.concatenate(entering, axis=1).astype(BF16))
        if d == 0:
            of_scr[rs, vl] = o
        else:
            ob_scr[rs, vl] = o

    def super_block(n, carry):
        for p in range(pairs):
            decays = log_decays(n, p)
            for d in range(2):
                one_pair_direction(n, p, d, decays[d])
        return carry

    if n_super == 1:
        super_block(0, 0)
    else:
        lax.fori_loop(0, n_super, super_block, 0)

    for h in range(2 * pairs):
        vs = slice(h * GLA_DV, (h + 1) * GLA_DV)
        r = r_ref[:, vs]
        y = _rms(of_scr[:, vs] + ob_scr[:, vs], gg_ref[...]) * (r * jax.nn.sigmoid(r))
        y_ref[:, vs] = y.astype(y_ref.dtype)
        p, hh = divmod(h, 2)
        for d in range(2):
            sfin_ref[d, h] = st_scr[p, d, hh * GLA_DV:(hh + 1) * GLA_DV, :].T[hh * GLA_DK:(hh + 1) * GLA_DK, :]


def bidir_gla(proj, gates, s0, gla_g, n_batch, seq, row_blk0):
    shared_s0 = s0.shape[0] == 1
    pairs = HB // 2 if seq * HB * GLA_DV * 4 <= 1024 * 1024 else 2
    wqk = pairs * LANE
    wv = pairs * 2 * GLA_DV
    col_q = (2 * HA * 2 * DIFF_HD + HA * DIFF_VD) // wqk
    col_k = col_q + HB * GLA_DK // wqk
    col_v = (col_k * wqk + HB * GLA_DK) // wv
    col_r = col_v + HB * GLA_DV // wv
    n_tok = n_batch * seq
    return pl.pallas_call(
        functools.partial(_gla_kernel, seq=seq, pairs=pairs),
        grid=(n_batch, HB // 2 // pairs),
        in_specs=[pl.BlockSpec((seq, wqk), lambda b, p: (row_blk0 + b, col_q + p)),
                  pl.BlockSpec((seq, wqk), lambda b, p: (row_blk0 + b, col_k + p)),
                  pl.BlockSpec((seq, wv), lambda b, p: (row_blk0 + b, col_v + p)),
                  pl.BlockSpec((seq, wv), lambda b, p: (row_blk0 + b, col_r + p)),
                  pl.BlockSpec((2, seq, wqk), lambda b, p: (0, row_blk0 + b, p)),
                  pl.BlockSpec((None, 2, 2 * pairs, GLA_DK, GLA_DV),
                               lambda b, p: (0 if shared_s0 else b, 0, p, 0, 0)),
                  pl.BlockSpec((1, GLA_DV), lambda b, p: (0, 0))],
        out_specs=[pl.BlockSpec((seq, wv), lambda b, p: (b, p)),
                   pl.BlockSpec((None, 2, 2 * pairs, GLA_DK, GLA_DV), lambda b, p: (b, 0, p, 0, 0))],
        out_shape=[jax.ShapeDtypeStruct((n_tok, HB * GLA_DV), BF16),
                   jax.ShapeDtypeStruct((n_batch, 2, HB, GLA_DK, GLA_DV), F32)],
        scratch_shapes=[pltpu.VMEM((seq, wv), F32), pltpu.VMEM((seq, wv), F32),
                        pltpu.VMEM((pairs, 2, 2 * GLA_DV, LANE), F32)],
        compiler_params=_params(("arbitrary", "arbitrary")),
        name="bidir_gla",
    )(proj, proj, proj, proj, gates, s0, gla_g.reshape(1, GLA_DV))


def _head_slice(kh, gq):
    h = kh * Q_PER_KV + gq
    return slice(h * LANE, (h + 1) * LANE)


def _sink_column(sink_ref, kh, rows):
    return jnp.concatenate([jnp.full((rows, 1), sink_ref[kh * Q_PER_KV + gq], F32) for gq in range(Q_PER_KV)],
                           axis=0)


def _gqa_ctx_kernel(sink_ref, q_ref, k_ref, v_ref, o_ref, ko_ref, vo_ref):
    scale = HEAD_DIM ** -0.5
    n_rows = CTX_BATCHES_PER_STEP * SEQ
    for kh in range(KV_HEADS):
        ks = slice(kh * LANE, (kh + 1) * LANE)
        ko_ref[pl.ds(kh, n_rows, stride=KV_HEADS), :] = k_ref[:, ks]
        vo_ref[pl.ds(kh, n_rows, stride=KV_HEADS), :] = v_ref[:, ks]
    for bb in range(CTX_BATCHES_PER_STEP):
        rows = slice(bb * SEQ, (bb + 1) * SEQ)
        for kh in range(KV_HEADS):
            ks = slice(kh * LANE, (kh + 1) * LANE)
            k_bf = k_ref[rows, ks].astype(BF16)
            v1_bf = _with_ones(v_ref[rows, ks].astype(BF16))
            for gq in range(Q_PER_KV):
                hs = _head_slice(kh, gq)
                sink = sink_ref[kh * Q_PER_KV + gq]
                s = _dot_nt((q_ref[rows, hs] * scale).astype(BF16), k_bf)
                m = jnp.maximum(jnp.max(s, axis=-1, keepdims=True), sink)
                ov = _dot(jnp.exp(s - m).astype(BF16), v1_bf)
                o_ref[rows, hs] = (ov[:, :LANE] * (1.0 / (ov[:, LANE:] + jnp.exp(sink - m)))).astype(o_ref.dtype)


def gqa_context(proj, sinks):
    wq = N_HEADS * HEAD_DIM
    wkv = KV_HEADS * HEAD_DIM
    rows = CTX_BATCHES_PER_STEP * SEQ

    def kv(c):
        return pl.BlockSpec((rows, wkv), lambda b: (b, c))

    cache = pl.BlockSpec((rows * KV_HEADS, LANE), lambda b: (b, 0))
    return pl.pallas_call(
        _gqa_ctx_kernel,
        grid=(BATCH // CTX_BATCHES_PER_STEP,),
        in_specs=[pl.BlockSpec(memory_space=pltpu.SMEM),
                  pl.BlockSpec((rows, wq), lambda b: (b, 0)), kv(wq // wkv), kv(wq // wkv + 1)],
        out_specs=[pl.BlockSpec((rows, wq), lambda b: (b, 0)), cache, cache],
        out_shape=[jax.ShapeDtypeStruct((TP, wq), BF16), jax.ShapeDtypeStruct((TP * KV_HEADS, LANE), F32),
                   jax.ShapeDtypeStruct((TP * KV_HEADS, LANE), F32)],
        compiler_params=_params(("arbitrary",)),
        name="gqa_context",
    )(sinks, proj, proj, proj)


def _gqa_lat_kernel(sink_ref, q_ref, k_ref, v_ref, ck_ref, cv_ref, kc_ref, kl_ref, kh_ref,
                    qc_ref, ql_ref, qh_ref, o_ref, kw_scr, vw_scr, kc_scr, vc_scr):
    quarter = HEAD_DIM // 4
    w = WINDOW
    wkv = KV_HEADS * HEAD_DIM

    @pl.when(pl.program_id(1) == 0)
    def _():
        kw_scr[0:w, :] = jnp.zeros((w, wkv), BF16)
        kw_scr[w + DEC_SEQ:, :] = jnp.zeros((w, wkv), BF16)
        vw_scr[0:w, :] = jnp.zeros((w, 2 * wkv), BF16)
        vw_scr[w + DEC_SEQ:, :] = jnp.zeros((w, 2 * wkv), BF16)
        for kh in range(KV_HEADS):
            ks = slice(kh * LANE, (kh + 1) * LANE)
            vs = slice(2 * kh * LANE, 2 * (kh + 1) * LANE)
            kw_scr[w:w + DEC_SEQ, ks] = _rope(k_ref[:, ks], kc_ref[...], kl_ref[...], kh_ref[...],
                                              quarter).astype(BF16)
            vw_scr[w:w + DEC_SEQ, vs] = _with_ones(v_ref[:, ks].astype(BF16))
            vc_scr[:, vs] = _with_ones(cv_ref[:, ks].astype(BF16))
        kc_scr[...] = ck_ref[...].astype(BF16)

    qi = lax.broadcasted_iota(jnp.int32, (Q_PER_KV * w, 3 * w), 0) % w
    kj = lax.broadcasted_iota(jnp.int32, (Q_PER_KV * w, 3 * w), 1)
    scale = HEAD_DIM ** -0.5
    for blk in range(LAT_QBLOCKS_PER_STEP):
        n = pl.program_id(1) * LAT_QBLOCKS_PER_STEP + blk
        rows = slice(blk * w, (blk + 1) * w)
        win = pl.ds(pl.multiple_of(n * w, w), 3 * w)
        kpos = n * w - w + kj
        valid = (kj >= qi) & (kj <= qi + 2 * w) & (kpos >= 0) & (kpos < DEC_SEQ)
        for kh in range(KV_HEADS):
            ks = slice(kh * LANE, (kh + 1) * LANE)
            vs = slice(2 * kh * LANE, 2 * (kh + 1) * LANE)
            q = jnp.concatenate(
                [(_rope(q_ref[rows, _head_slice(kh, gq)], qc_ref[rows, :], ql_ref[rows, :], qh_ref[rows, :],
                        quarter) * scale).astype(BF16) for gq in range(Q_PER_KV)], axis=0)
            sink = _sink_column(sink_ref, kh, w)
            s_c = _dot_nt(q, kc_scr[:, ks])
            s_w = jnp.where(valid, _dot_nt(q, kw_scr[win, ks]), NEG_INF)
            m = jnp.maximum(jnp.maximum(jnp.max(s_c, axis=-1, keepdims=True),
                                        jnp.max(s_w, axis=-1, keepdims=True)), sink)
            ov = (_dot(jnp.exp(s_c - m).astype(BF16), vc_scr[:, vs])
                  + _dot(jnp.exp(s_w - m).astype(BF16), vw_scr[win, vs]))
            o = ov[:, :LANE] * (1.0 / (ov[:, LANE:] + jnp.exp(sink - m)))
            for gq in range(Q_PER_KV):
                o_ref[rows, _head_slice(kh, gq)] = o[gq * w:(gq + 1) * w].astype(o_ref.dtype)


def gqa_latent(proj, cache_k, cache_v, tables, sinks):
    wq = N_HEADS * HEAD_DIM
    wkv = KV_HEADS * HEAD_DIM
    tq = LAT_QBLOCKS_PER_STEP * WINDOW
    nq = DEC_SEQ // tq
    row0 = TP // tq
    kv_row0 = TP // DEC_SEQ
    cos, sin_lo, sin_hi = tables
    full = pl.BlockSpec((DEC_SEQ, LANE), lambda b, i: (0, 0))
    qtab = pl.BlockSpec((tq, LANE), lambda b, i: (i, 0))
    cache = pl.BlockSpec((None, PAST_LEN, wkv), lambda b, i: (b, 0, 0))
    return pl.pallas_call(
        _gqa_lat_kernel,
        grid=(DEC_BATCH, nq),
        in_specs=[pl.BlockSpec(memory_space=pltpu.SMEM),
                  pl.BlockSpec((tq, wq), lambda b, i: (row0 + b * nq + i, 0)),
                  pl.BlockSpec((DEC_SEQ, wkv), lambda b, i: (kv_row0 + b, wq // wkv)),
                  pl.BlockSpec((DEC_SEQ, wkv), lambda b, i: (kv_row0 + b, wq // wkv + 1)),
                  cache, cache, full, full, full, qtab, qtab, qtab],
        out_specs=pl.BlockSpec((tq, wq), lambda b, i: (b * nq + i, 0)),
        out_shape=jax.ShapeDtypeStruct((TS, wq), BF16),
        scratch_shapes=[pltpu.VMEM((DEC_SEQ + 2 * WINDOW, wkv), BF16),
                        pltpu.VMEM((DEC_SEQ + 2 * WINDOW, 2 * wkv), BF16),
                        pltpu.VMEM((PAST_LEN, wkv), BF16),
                        pltpu.VMEM((PAST_LEN, 2 * wkv), BF16)],
        compiler_params=_params(("arbitrary", "arbitrary")),
        name="gqa_latent",
    )(sinks, proj, proj, proj, cache_k, cache_v, cos, sin_lo, sin_hi, cos, sin_lo, sin_hi)


def _even_layer(xc, xs, mod, j, layer, norm_g, w_in_even, lam_vec, subln_g, w_gate1, w_gate2, b_gate, gla_norm_g,
                w_out_even, cache_a_k, cache_a_v, state_b, tables):
    lam_init = 0.8 - 0.6 * math.exp(-0.3 * layer)
    ng = HB * GLA_DK
    wg1 = jnp.zeros((D_MODEL, LANE), F32).at[:, :2 * GLA_GATE_RANK].set(
        jnp.concatenate([w_gate1[j, 0], w_gate1[j, 1]], axis=-1)).astype(BF16)
    wg2 = jnp.zeros((2, LANE, ng), F32)
    for e in range(2):
        wg2 = wg2.at[e, e * GLA_GATE_RANK:(e + 1) * GLA_GATE_RANK, :].set(w_gate2[j, e])
    h, gates = norm_mod_gate(xc, xs, norm_g[layer, 0], mod, 0, 1, wg1, wg2.astype(BF16),
                             b_gate[j].reshape(2, 1, ng))
    proj = ws_matmul(h, w_in_even, j)
    ya_c, new_k, new_v = diff_attention_context(proj, lam_vec, subln_g[j], lam_init)
    ya_s = diff_attention_latent(proj, cache_a_k[:, j].reshape(DEC_BATCH, PAST_LEN, HA * 2 * DIFF_HD),
                                 cache_a_v[:, j].reshape(DEC_BATCH, PAST_LEN, HA * DIFF_VD),
                                 tables, lam_vec, subln_g[j], lam_init)
    yb_c, s_fin = bidir_gla(proj, gates, jnp.zeros((1, 2, HB, GLA_DK, GLA_DV), F32), gla_norm_g[j],
                            BATCH, SEQ, 0)
    yb_s, _ = bidir_gla(proj, gates, state_b[:, j], gla_norm_g[j], DEC_BATCH, DEC_SEQ, TP // DEC_SEQ)
    x, h_ffn = ws_matmul_residual([(ya_c, ya_s), (yb_c, yb_s)], w_out_even, j, (xc, xs), mod, 2,
                                  tm=OUT_PROJ_ROWS, next_norm=(norm_g[layer, 1], 3, 4))
    return (x, h_ffn, new_k.reshape(BATCH, SEQ, HA, 2 * DIFF_HD), new_v.reshape(BATCH, SEQ, HA, DIFF_VD), s_fin)


def _odd_layer(x, mod, j, layer, norm_g, w_in_odd, sinks, w_out_odd, cache_c_k, cache_c_v, tables):
    wkv = KV_HEADS * HEAD_DIM
    proj = ws_matmul(norm_mod(x, norm_g[layer, 0], mod, 0, 1), w_in_odd, j)
    o_c, new_k, new_v = gqa_context(proj, sinks[j])
    o_s = gqa_latent(proj, cache_c_k[:, j].reshape(DEC_BATCH, PAST_LEN, wkv),
                     cache_c_v[:, j].reshape(DEC_BATCH, PAST_LEN, wkv), tables, sinks[j])
    x, h_ffn = ws_matmul_residual([(o_c, o_s)], w_out_odd, j, (x,), mod, 2,
                                  tm=OUT_PROJ_ROWS, next_norm=(norm_g[layer, 1], 3, 4))
    return (x, h_ffn, new_k.reshape(BATCH, SEQ, KV_HEADS, HEAD_DIM), new_v.reshape(BATCH, SEQ, KV_HEADS, HEAD_DIM))


def _ffn(x, h_ffn, mod, layer, w_ffn_in, w_ffn_out):
    act = ws_swiglu(h_ffn, w_ffn_in, layer)
    return ws_matmul_residual([(act,)], w_ffn_out, layer, (x,), mod, 5, tm=512, tn=512)


def kernel(x_prompt, x_sample, c, cache_a_k, cache_a_v, state_b, cache_c_k, cache_c_v, c_ctx, w_ada, b_ada,
           norm_g, w_in_even, lam_q1, lam_k1, lam_q2, lam_k2, subln_g, w_gate1, w_gate2, b_gate, gla_norm_g,
           w_out_even, w_in_odd, sinks, w_out_odd, w_ffn_in, w_ffn_out, final_norm_g):
    assert DEPTH == 2, "layer 0 reads the two input streams, layer 1 the fused token matrix"
    cvec = jnp.concatenate([c_ctx[None, :], c, jnp.zeros((MOD_ROWS - 1 - DEC_BATCH, D_MODEL), F32)], axis=0)
    mods = adaln(cvec, w_ada, b_ada).reshape(DEPTH, MOD_ROWS, 1, 6 * D_MODEL)
    tab_d = rope_tables(DEC_SEQ, DIFF_HD)
    tab_h = rope_tables(DEC_SEQ, HEAD_DIM)

    lam_vec = jnp.stack([lam_q1[0], lam_k1[0], lam_q2[0], lam_k2[0]], axis=0)
    x, h_ffn, ak, av, sb = _even_layer(x_prompt.reshape(TP, D_MODEL), x_sample.reshape(TS, D_MODEL), mods[0], 0, 0,
                                       norm_g, w_in_even, lam_vec, subln_g, w_gate1, w_gate2, b_gate, gla_norm_g,
                                       w_out_even, cache_a_k, cache_a_v, state_b, tab_d)
    x = _ffn(x, h_ffn, mods[0], 0, w_ffn_in, w_ffn_out)
    x, h_ffn, ck, cv = _odd_layer(x, mods[1], 0, 1, norm_g, w_in_odd, sinks, w_out_odd, cache_c_k, cache_c_v,
                                  tab_h)
    x = _ffn(x, h_ffn, mods[1], 1, w_ffn_in, w_ffn_out)

    y_prompt = final_norm(x, final_norm_g, 0, TP).reshape(BATCH, SEQ, D_MODEL)
    y_sample = final_norm(x, final_norm_g, TP, TS).reshape(DEC_BATCH, DEC_SEQ, D_MODEL)
    return (y_prompt, y_sample, ak[:, None], av[:, None], sb[:, None], ck[:, None], cv[:, None])
```

```python
import functools
import math

import jax
import jax.numpy as jnp
from jax import lax
from jax.experimental import pallas as pl
from jax.experimental.pallas import tpu as pltpu

D_MODEL = 2048
BATCH = 32
SEQ = 256
DEPTH = 2
DEC_BATCH = 2
DEC_SEQ = 2048
PAST_LEN = 512
GRID_W = 64
HEAD_DIM = 128
N_HEADS = D_MODEL // HEAD_DIM
HA = N_HEADS // 2
DIFF_HD = HEAD_DIM // 2
DIFF_VD = HEAD_DIM
HB = N_HEADS // 2
GLA_DK = HEAD_DIM // 2
GLA_DV = HEAD_DIM
GLA_GATE_RANK = 16
GLA_TAU = 16.0
GLA_CHUNK = 64
GLA_SUPER = 4
CTX_BATCHES_PER_STEP = 4
LAT_QBLOCKS_PER_STEP = 2
DIFF_LAT_CHAINS = 4
KV_HEADS = N_HEADS // 4
Q_PER_KV = N_HEADS // KV_HEADS
WINDOW = 128
D_FF = -(-8 * D_MODEL // (3 * 256)) * 256
ROPE_BASE = 10000.0
EPS = 1e-6
NEG_INF = -1e30
EVEN_IN = 2 * HA * 2 * DIFF_HD + HA * DIFF_VD + 2 * HB * GLA_DK + 2 * HB * GLA_DV
ODD_IN = (N_HEADS + 2 * KV_HEADS) * HEAD_DIM

TP = BATCH * SEQ
TS = DEC_BATCH * DEC_SEQ
T = TP + TS
LANE = 128
MOD_ROWS = 8
NORM_ROWS = 16
NORM_UNROLL = 8
OUT_PROJ_ROWS = 256
VMEM_LIMIT = 56 * 1024 * 1024

F32 = jnp.float32
BF16 = jnp.bfloat16


def _params(semantics, vmem=VMEM_LIMIT):
    return pltpu.CompilerParams(dimension_semantics=semantics, vmem_limit_bytes=vmem)


def _dot(a, b):
    return jnp.dot(a, b, preferred_element_type=F32)


def _dot_nt(a, b):
    return lax.dot_general(a, b, (((1,), (1,)), ((), ())), preferred_element_type=F32)


def _group_of_tile(i, tm):
    r = i * tm
    return jnp.where(r < TP, 0, 1 + (r - TP) // DEC_SEQ)


def _rms(x, g):
    return (x * lax.rsqrt(jnp.mean(x * x, axis=-1, keepdims=True) + EPS)) * g


def _adaln_kernel(c_ref, w_ref, b_ref, o_ref):
    c = c_ref[...]
    s = c * jax.nn.sigmoid(c)
    o_ref[...] = _dot(s.astype(BF16), w_ref[...].astype(BF16)) + b_ref[...]


def adaln(cvec, w_ada, b_ada, tn=2048):
    n = w_ada.shape[-1]
    return pl.pallas_call(
        _adaln_kernel,
        grid=(DEPTH, n // tn),
        in_specs=[
            pl.BlockSpec((MOD_ROWS, D_MODEL), lambda l, j: (0, 0)),
            pl.BlockSpec((None, D_MODEL, tn), lambda l, j: (l, 0, j)),
            pl.BlockSpec((None, 1, tn), lambda l, j: (l, 0, j)),
        ],
        out_specs=pl.BlockSpec((None, MOD_ROWS, tn), lambda l, j: (l, 0, j)),
        out_shape=jax.ShapeDtypeStruct((DEPTH, MOD_ROWS, n), F32),
        compiler_params=_params(("arbitrary", "arbitrary")),
        name="adaln",
    )(cvec, w_ada, b_ada.reshape(DEPTH, 1, n))


def _by_row_groups(x_ref, o_ref, fn):
    def body(r, carry):
        rs = pl.ds(pl.multiple_of(r * NORM_ROWS, NORM_ROWS), NORM_ROWS)
        o_ref[rs, :] = fn(x_ref[rs, :]).astype(o_ref.dtype)
        return carry

    lax.fori_loop(0, x_ref.shape[0] // NORM_ROWS, body, 0, unroll=NORM_UNROLL)


def _norm_mod_into(h_ref, x_ref, g_ref, shift_ref, scale_ref):
    _by_row_groups(x_ref, h_ref, lambda x: _rms(x, g_ref[...]) * (1 + scale_ref[...]) + shift_ref[...])


def _norm_mod_kernel(x_ref, g_ref, shift_ref, scale_ref, h_ref):
    _norm_mod_into(h_ref, x_ref, g_ref, shift_ref, scale_ref)


def _norm_mod_gate_kernel(xc_ref, xs_ref, g_ref, shift_ref, scale_ref, wg1_ref, wg2_ref, bg_ref,
                          h_ref, gate_ref, *, ctx_tiles):
    @pl.when(pl.program_id(0) < ctx_tiles)
    def _():
        _norm_mod_into(h_ref, xc_ref, g_ref, shift_ref, scale_ref)

    @pl.when(pl.program_id(0) >= ctx_tiles)
    def _():
        _norm_mod_into(h_ref, xs_ref, g_ref, shift_ref, scale_ref)

    low = _dot(h_ref[...], wg1_ref[...]).astype(BF16)
    for e in range(2):
        logit = _dot(low, wg2_ref[e]) + bg_ref[e]
        log_sig = jnp.minimum(logit, 0.0) - jnp.log(1.0 + jnp.exp(-jnp.abs(logit)))
        gate_ref[e] = log_sig / GLA_TAU


def _mod_specs(tm, shift_idx, scale_idx):
    def spec(idx):
        return pl.BlockSpec((None, 1, D_MODEL), lambda i: (_group_of_tile(i, tm), 0, idx))
    return [pl.BlockSpec((1, D_MODEL), lambda i: (0, 0)), spec(shift_idx), spec(scale_idx)]


def norm_mod(x, g, mod, shift_idx, scale_idx, tm=1024):
    return pl.pallas_call(
        _norm_mod_kernel,
        grid=(T // tm,),
        in_specs=[pl.BlockSpec((tm, D_MODEL), lambda i: (i, 0))] + _mod_specs(tm, shift_idx, scale_idx),
        out_specs=pl.BlockSpec((tm, D_MODEL), lambda i: (i, 0)),
        out_shape=jax.ShapeDtypeStruct((T, D_MODEL), BF16),
        compiler_params=_params(("arbitrary",)),
        name="norm_mod",
    )(x, g.reshape(1, D_MODEL), mod, mod)


def norm_mod_gate(xc, xs, g, mod, shift_idx, scale_idx, wg1, wg2, bg, tm=512):
    ng = HB * GLA_DK
    ctx_tiles = TP // tm
    return pl.pallas_call(
        functools.partial(_norm_mod_gate_kernel, ctx_tiles=ctx_tiles),
        grid=(T // tm,),
        in_specs=[pl.BlockSpec((tm, D_MODEL), lambda i: (jnp.minimum(i, ctx_tiles - 1), 0)),
                  pl.BlockSpec((tm, D_MODEL), lambda i: (jnp.maximum(i - ctx_tiles, 0), 0))]
        + _mod_specs(tm, shift_idx, scale_idx)
        + [pl.BlockSpec((D_MODEL, LANE), lambda i: (0, 0)),
           pl.BlockSpec((2, LANE, ng), lambda i: (0, 0, 0)),
           pl.BlockSpec((2, 1, ng), lambda i: (0, 0, 0))],
        out_specs=[pl.BlockSpec((tm, D_MODEL), lambda i: (i, 0)),
                   pl.BlockSpec((2, tm, ng), lambda i: (0, i, 0))],
        out_shape=[jax.ShapeDtypeStruct((T, D_MODEL), BF16), jax.ShapeDtypeStruct((2, T, ng), F32)],
        compiler_params=_params(("arbitrary",)),
        name="norm_mod_gate",
    )(xc, xs, g.reshape(1, D_MODEL), mod, mod, wg1, wg2, bg)


def _cast_weights(w_refs, w_scrs):
    @pl.when(pl.program_id(1) == 0)
    def _():
        for w_ref, w_scr in zip(w_refs, w_scrs):
            w_scr[...] = w_ref[...].astype(BF16)


def _ws_plain_kernel(h_ref, w_ref, o_ref, w_scr):
    _cast_weights([w_ref], [w_scr])
    o_ref[...] = _dot(h_ref[...], w_scr[...])


def _ws_swiglu_kernel(h_ref, wg_ref, wu_ref, o_ref, wg_scr, wu_scr):
    _cast_weights([wg_ref, wu_ref], [wg_scr, wu_scr])
    h = h_ref[...]
    gate = _dot(h, wg_scr[...])
    up = _dot(h, wu_scr[...])
    o_ref[...] = (gate * jax.nn.sigmoid(gate) * up).astype(o_ref.dtype)


def _ws_res_kernel(*refs, lhs_arity, x_arity, ctx_tiles, next_norm):
    n_l = sum(lhs_arity)
    w_ref = refs[n_l]
    x_refs = refs[n_l + 1:n_l + 1 + x_arity]
    rest = refs[n_l + 1 + x_arity:]
    if next_norm:
        gate_ref, g2_ref, shift2_ref, scale2_ref, o_ref, h_ref, w_scr = rest
    else:
        gate_ref, o_ref, w_scr = rest
    groups, at = [], 0
    for a in lhs_arity:
        groups.append(refs[at:at + a])
        at += a
    _cast_weights([w_ref], [w_scr])

    def emit(side):
        acc, k0 = None, 0
        for grp in groups:
            ref = grp[side] if len(grp) == 2 else grp[0]
            part = _dot(ref[...], w_scr[k0:k0 + ref.shape[1], :])
            acc = part if acc is None else acc + part
            k0 += ref.shape[1]
        x_ref = x_refs[side] if x_arity == 2 else x_refs[0]
        x_new = x_ref[...] + gate_ref[...] * acc
        o_ref[...] = x_new
        if next_norm:
            h_ref[...] = (_rms(x_new, g2_ref[...]) * (1 + scale2_ref[...]) + shift2_ref[...]).astype(BF16)

    if x_arity == 2 or 2 in lhs_arity:
        @pl.when(pl.program_id(1) < ctx_tiles)
        def _():
            emit(0)

        @pl.when(pl.program_id(1) >= ctx_tiles)
        def _():
            emit(1)
    else:
        emit(0)


def _rows_specs(arrays, tm, width, col):
    if len(arrays) == 1:
        return [pl.BlockSpec((tm, width), lambda j, i: (i, col(j)))]
    ctx_tiles = TP // tm
    return [pl.BlockSpec((tm, width), lambda j, i: (jnp.minimum(i, ctx_tiles - 1), col(j))),
            pl.BlockSpec((tm, width), lambda j, i: (jnp.maximum(i - ctx_tiles, 0), col(j)))]


def ws_matmul(h, w_stack, layer, tm=1024, tn=1024):
    kdim, n = w_stack.shape[1:]
    return pl.pallas_call(
        _ws_plain_kernel,
        grid=(n // tn, T // tm),
        in_specs=[pl.BlockSpec((tm, kdim), lambda j, i: (i, 0)),
                  pl.BlockSpec((None, kdim, tn), lambda j, i: (layer, 0, j))],
        out_specs=pl.BlockSpec((tm, tn), lambda j, i: (i, j)),
        out_shape=jax.ShapeDtypeStruct((T, n), F32),
        scratch_shapes=[pltpu.VMEM((kdim, tn), BF16)],
        compiler_params=_params(("arbitrary", "arbitrary")),
        name="ws_matmul",
    )(h, w_stack)


def ws_swiglu(h, w_stack, layer, tm=1024, tn=512):
    kdim = w_stack.shape[1]
    nf = D_FF // tn
    return pl.pallas_call(
        _ws_swiglu_kernel,
        grid=(nf, T // tm),
        in_specs=[pl.BlockSpec((tm, kdim), lambda j, i: (i, 0)),
                  pl.BlockSpec((None, kdim, tn), lambda j, i: (layer, 0, j)),
                  pl.BlockSpec((None, kdim, tn), lambda j, i: (layer, 0, j + nf))],
        out_specs=pl.BlockSpec((tm, tn), lambda j, i: (i, j)),
        out_shape=jax.ShapeDtypeStruct((T, D_FF), BF16),
        scratch_shapes=[pltpu.VMEM((kdim, tn), BF16), pltpu.VMEM((kdim, tn), BF16)],
        compiler_params=_params(("arbitrary", "arbitrary")),
        name="ws_swiglu",
    )(h, w_stack, w_stack)


def ws_matmul_residual(lhs_groups, w_stack, layer, x, mod, gate_idx, tm=512, tn=1024, next_norm=None):
    kdim, n = w_stack.shape[1:]
    if next_norm is not None:
        tn = n
    gate_blk = n // tn
    specs, args = [], []
    for grp in lhs_groups:
        specs += _rows_specs(grp, tm, grp[0].shape[1], lambda j: 0)
        args += list(grp)
    w_mode = {} if next_norm is None else dict(pipeline_mode=pl.Buffered(1))
    specs.append(pl.BlockSpec((None, kdim, tn), lambda j, i: (layer, 0, j), **w_mode))
    specs += _rows_specs(x, tm, tn, lambda j: j)
    specs.append(pl.BlockSpec((None, 1, tn), lambda j, i: (_group_of_tile(i, tm), 0, gate_idx * gate_blk + j)))
    tail = [mod]
    out_specs = [pl.BlockSpec((tm, tn), lambda j, i: (i, j))]
    out_shape = [jax.ShapeDtypeStruct((T, n), F32)]
    if next_norm is not None:
        gain, shift_idx, scale_idx = next_norm
        specs.append(pl.BlockSpec((1, n), lambda j, i: (0, 0)))
        specs += [pl.BlockSpec((None, 1, n), lambda j, i, idx=idx: (_group_of_tile(i, tm), 0, idx))
                  for idx in (shift_idx, scale_idx)]
        tail += [gain.reshape(1, n), mod, mod]
        out_specs.append(pl.BlockSpec((tm, tn), lambda j, i: (i, j)))
        out_shape.append(jax.ShapeDtypeStruct((T, n), BF16))
    outs = pl.pallas_call(
        functools.partial(_ws_res_kernel, lhs_arity=tuple(len(grp) for grp in lhs_groups), x_arity=len(x),
                          ctx_tiles=TP // tm, next_norm=next_norm is not None),
        grid=(n // tn, T // tm),
        in_specs=specs,
        out_specs=out_specs,
        out_shape=out_shape,
        scratch_shapes=[pltpu.VMEM((kdim, tn), BF16)],
        compiler_params=_params(("arbitrary", "arbitrary")),
        name="ws_matmul_residual",
    )(*args, w_stack, *x, *tail)
    return outs[0] if next_norm is None else tuple(outs)


def _final_norm_kernel(x_ref, g_ref, o_ref):
    o_ref[...] = _rms(x_ref[...], g_ref[...])


def final_norm(x, g, row0, n_rows, tm=1024):
    blk0 = row0 // tm
    return pl.pallas_call(
        _final_norm_kernel,
        grid=(n_rows // tm,),
        in_specs=[pl.BlockSpec((tm, D_MODEL), lambda i: (blk0 + i, 0)),
                  pl.BlockSpec((1, D_MODEL), lambda i: (0, 0))],
        out_specs=pl.BlockSpec((tm, D_MODEL), lambda i: (i, 0)),
        out_shape=jax.ShapeDtypeStruct((n_rows, D_MODEL), F32),
        compiler_params=_params(("arbitrary",)),
        name="final_norm",
    )(x, g.reshape(1, D_MODEL))


def _grid_angles(n_tok, rot_dim):
    t = jnp.arange((n_tok // GRID_W) * GRID_W)
    row = (t // GRID_W).astype(F32)
    col = (t % GRID_W).astype(F32)
    half = rot_dim // 2
    inv = ROPE_BASE ** (-jnp.arange(0, half, 2, dtype=F32) / half)
    return row[:, None] * inv[None], col[:, None] * inv[None]


def rope_tables(n_tok, rot_dim):
    ang_row, ang_col = _grid_angles(n_tok, rot_dim)
    zeros = jnp.zeros_like(ang_row)
    reps = LANE // rot_dim

    def lanes(r1, r2, c1, c2):
        return jnp.tile(jnp.concatenate([r1, r2, c1, c2], axis=-1), (1, reps))

    cr, sr, cc, sc = jnp.cos(ang_row), jnp.sin(ang_row), jnp.cos(ang_col), jnp.sin(ang_col)
    return lanes(cr, cr, cc, cc), lanes(-sr, zeros, -sc, zeros), lanes(zeros, sr, zeros, sc)


def _rope(x, cos, sin_lo, sin_hi, quarter):
    return (x * cos + pltpu.roll(x, LANE - quarter, 1) * sin_lo + pltpu.roll(x, quarter, 1) * sin_hi)


def _diff_lambda(lam_ref, lam_init):
    e1 = jnp.exp(jnp.sum(lam_ref[0:1, :] * lam_ref[1:2, :], axis=-1, keepdims=True))
    e2 = jnp.exp(jnp.sum(lam_ref[2:3, :] * lam_ref[3:4, :], axis=-1, keepdims=True))
    return e1 - e2 + lam_init


def _with_ones(v_bf):
    return jnp.concatenate([v_bf, jnp.ones(v_bf.shape, BF16)], axis=1)


def _softmax_av(q_bf, k_bf, v1_bf):
    s = _dot_nt(q_bf, k_bf)
    e = jnp.exp(s - jnp.max(s, axis=-1, keepdims=True))
    if v1_bf.shape[1] == LANE:
        return _dot(e.astype(BF16), v1_bf) * (1.0 / jnp.sum(e, axis=-1, keepdims=True))
    ov = _dot(e.astype(BF16), v1_bf)
    return ov[:, :LANE] * (1.0 / ov[:, LANE:])


def _diff_head(q, k_bf, v1_bf, lam, subln_g, lam_init, stack_maps):
    n = q.shape[0]
    first = lax.broadcasted_iota(jnp.int32, q.shape, 1) < DIFF_HD
    q0 = jnp.where(first, q, 0.0).astype(BF16)
    q1 = jnp.where(first, 0.0, q).astype(BF16)
    if stack_maps:
        r = _softmax_av(jnp.concatenate([q0, q1], axis=0), k_bf, v1_bf)
        o = r[:n] - lam * r[n:]
    else:
        o = _softmax_av(q0, k_bf, v1_bf) - lam * _softmax_av(q1, k_bf, v1_bf)
    return _rms(o, subln_g) * (1.0 - lam_init)


def _diff_ctx_kernel(q_ref, k_ref, v_ref, lam_ref, sg_ref, o_ref, ko_ref, vo_ref, *, lam_init):
    lam = _diff_lambda(lam_ref, lam_init)
    scale = DIFF_HD ** -0.5
    n_rows = CTX_BATCHES_PER_STEP * SEQ
    for h in range(HA):
        sl = slice(h * LANE, (h + 1) * LANE)
        ko_ref[pl.ds(h, n_rows, stride=HA), :] = k_ref[:, sl]
        vo_ref[pl.ds(h, n_rows, stride=HA), :] = v_ref[:, sl]
    for bb in range(CTX_BATCHES_PER_STEP):
        rows = slice(bb * SEQ, (bb + 1) * SEQ)
        for h in range(HA):
            sl = slice(h * LANE, (h + 1) * LANE)
            y = _diff_head(q_ref[rows, sl] * scale, k_ref[rows, sl].astype(BF16),
                           _with_ones(v_ref[rows, sl].astype(BF16)), lam, sg_ref[...], lam_init, stack_maps=True)
            o_ref[rows, sl] = y.astype(o_ref.dtype)


def diff_attention_context(proj, lam_vec, subln_g, lam_init):
    w = HA * LANE

    def blk(c):
        return pl.BlockSpec((CTX_BATCHES_PER_STEP * SEQ, w), lambda b: (b, c))

    cache = pl.BlockSpec((CTX_BATCHES_PER_STEP * SEQ * HA, LANE), lambda b: (b, 0))
    return pl.pallas_call(
        functools.partial(_diff_ctx_kernel, lam_init=lam_init),
        grid=(BATCH // CTX_BATCHES_PER_STEP,),
        in_specs=[blk(0), blk(1), blk(2),
                  pl.BlockSpec((4, DIFF_HD), lambda b: (0, 0)),
                  pl.BlockSpec((1, DIFF_VD), lambda b: (0, 0))],
        out_specs=[blk(0), cache, cache],
        out_shape=[jax.ShapeDtypeStruct((TP, w), BF16), jax.ShapeDtypeStruct((TP * HA, LANE), F32),
                   jax.ShapeDtypeStruct((TP * HA, LANE), F32)],
        compiler_params=_params(("arbitrary",)),
        name="diff_attention_context",
    )(proj, proj, proj, lam_vec, subln_g.reshape(1, DIFF_VD))


def _diff_lat_kernel(q_ref, k_ref, v_ref, ck_ref, cv_ref, kc_ref, kl_ref, kh_ref, qc_ref, ql_ref, qh_ref,
                     lam_ref, sg_ref, o_ref, k_scr, v_scr, *, lam_init):
    quarter = DIFF_HD // 4

    @pl.when(pl.program_id(2) == 0)
    def _():
        head_rows = pl.ds(pl.program_id(1), PAST_LEN, stride=HA)
        k_scr[0:PAST_LEN, :] = ck_ref[head_rows, :].astype(BF16)
        k_scr[PAST_LEN:, :] = _rope(k_ref[...], kc_ref[...], kl_ref[...], kh_ref[...], quarter).astype(BF16)
        v_scr[0:PAST_LEN, :] = cv_ref[head_rows, :].astype(BF16)
        v_scr[PAST_LEN:, :] = v_ref[...].astype(BF16)

    lam = _diff_lambda(lam_ref, lam_init)
    sub = q_ref.shape[0] // DIFF_LAT_CHAINS
    for c in range(DIFF_LAT_CHAINS):
        rs = slice(c * sub, (c + 1) * sub)
        q = _rope(q_ref[rs, :], qc_ref[rs, :], ql_ref[rs, :], qh_ref[rs, :], quarter) * (DIFF_HD ** -0.5)
        y = _diff_head(q, k_scr[...], v_scr[...], lam, sg_ref[...], lam_init, stack_maps=False)
        o_ref[rs, :] = y.astype(o_ref.dtype)


def diff_attention_latent(proj, cache_k, cache_v, tables, lam_vec, subln_g, lam_init, tq=1024):
    nq = DEC_SEQ // tq
    row0 = TP // tq
    kv_row0 = TP // DEC_SEQ
    cos, sin_lo, sin_hi = tables
    full = pl.BlockSpec((DEC_SEQ, LANE), lambda b, h, i: (0, 0))
    qtab = pl.BlockSpec((tq, LANE), lambda b, h, i: (i, 0))
    cache = pl.BlockSpec((None, PAST_LEN * HA, LANE), lambda b, h, i: (b, 0, 0))
    return pl.pallas_call(
        functools.partial(_diff_lat_kernel, lam_init=lam_init),
        grid=(DEC_BATCH, HA, nq),
        in_specs=[pl.BlockSpec((tq, LANE), lambda b, h, i: (row0 + b * nq + i, h)),
                  pl.BlockSpec((DEC_SEQ, LANE), lambda b, h, i: (kv_row0 + b, HA + h)),
                  pl.BlockSpec((DEC_SEQ, LANE), lambda b, h, i: (kv_row0 + b, 2 * HA + h)),
                  cache, cache, full, full, full, qtab, qtab, qtab,
                  pl.BlockSpec((4, DIFF_HD), lambda b, h, i: (0, 0)),
                  pl.BlockSpec((1, DIFF_VD), lambda b, h, i: (0, 0))],
        out_specs=pl.BlockSpec((tq, LANE), lambda b, h, i: (b * nq + i, h)),
        out_shape=jax.ShapeDtypeStruct((TS, HA * LANE), BF16),
        scratch_shapes=[pltpu.VMEM((PAST_LEN + DEC_SEQ, LANE), BF16),
                        pltpu.VMEM((PAST_LEN + DEC_SEQ, LANE), BF16)],
        compiler_params=_params(("arbitrary", "arbitrary", "arbitrary")),
        name="diff_attention_latent",
    )(proj, proj, proj, cache_k, cache_v, cos, sin_lo, sin_hi, cos, sin_lo, sin_hi,
      lam_vec, subln_g.reshape(1, DIFF_VD))


def _gla_kernel(q_ref, k_ref, v_ref, r_ref, g_ref, s0_ref, gg_ref, y_ref, sfin_ref, of_scr, ob_scr, st_scr,
                *, seq, pairs):
    c = GLA_CHUNK
    sb = GLA_SUPER * c
    n_super = seq // sb
    rows = lax.broadcasted_iota(jnp.int32, (sb, sb), 0)
    cols = lax.broadcasted_iota(jnp.int32, (sb, sb), 1)
    same_chunk = (rows // c) == (cols // c)
    keep = (same_chunk & (rows >= cols), same_chunk & (cols >= rows))
    lane = lax.broadcasted_iota(jnp.int32, (sb, LANE), 1)
    own = (lane < GLA_DK, lane >= GLA_DK)
    chunk_of_row = lax.broadcasted_iota(jnp.int32, (sb, LANE), 0) // c
    st_rows = lax.broadcasted_iota(jnp.int32, (2 * GLA_DV, LANE), 0)
    st_lane = lax.broadcasted_iota(jnp.int32, (2 * GLA_DV, LANE), 1)
    st_own = (st_rows < GLA_DV) == (st_lane < GLA_DK)
    zpad = jnp.zeros((GLA_DK, GLA_DV), F32)

    def expand(x):
        return jnp.concatenate([jnp.where(chunk_of_row == ci, x, 0.0) for ci in range(GLA_SUPER)],
                               axis=1).astype(BF16)

    for p in range(pairs):
        for d in range(2):
            s0 = [s0_ref[d, 2 * p + hh] for hh in range(2)]
            st_scr[p, d] = jnp.concatenate([jnp.concatenate([s0[0], zpad], axis=0).T,
                                            jnp.concatenate([zpad, s0[1]], axis=0).T], axis=0)

    def rows_of(n, d):
        r0 = (n if d == 0 else n_super - 1 - n) * sb
        return pl.ds(r0, sb) if n_super == 1 else pl.ds(pl.multiple_of(r0, sb), sb)

    def chunk_totals(prefix):
        ends = [prefix[ci * c + c - 1:ci * c + c, :] for ci in range(GLA_SUPER)]
        return ends, jnp.concatenate([jnp.broadcast_to(e, (c, LANE)) for e in ends], axis=0)

    def log_decays(n, p):
        kl = slice(p * LANE, (p + 1) * LANE)
        g_b = g_ref[1, rows_of(n, 1), kl]
        prefix = jnp.concatenate([g_ref[0, rows_of(n, 0), kl], g_b], axis=1)
        pos = lax.broadcasted_iota(jnp.int32, prefix.shape, 0) % c
        step = 1
        while step < c:
            prefix = prefix + jnp.where(pos >= step, pltpu.roll(prefix, step, 0), 0.0)
            step *= 2
        ends_f, total_f = chunk_totals(prefix[:, :LANE])
        ends_b, total_b = chunk_totals(prefix[:, LANE:])
        return (prefix[:, :LANE], ends_f, total_f), (total_b - prefix[:, LANE:] + g_b, ends_b, total_b)

    def one_pair_direction(n, p, d, decays):
        kl = slice(p * LANE, (p + 1) * LANE)
        vl = slice(2 * p * GLA_DV, 2 * (p + 1) * GLA_DV)
        rs = rows_of(n, d)
        b, ends, total = decays
        q_in = (q_ref[rs, kl] * (GLA_DK ** -0.5)) * jnp.exp(b)
        kk = k_ref[rs, kl]
        k_in = (kk * jnp.exp(-b)).astype(BF16)
        k_end = kk * jnp.exp(total - b)
        v = v_ref[rs, vl]
        v_bf = v.astype(BF16)
        kv_all = _dot(v.T.astype(BF16), expand(k_end))
        q2 = jnp.concatenate([jnp.where(own[0], q_in, 0.0), jnp.where(own[1], q_in, 0.0)], axis=0)
        a2 = _dot_nt(q2.astype(BF16), k_in)
        o = jnp.concatenate(
            [_dot(jnp.where(keep[d], a2[hh * sb:(hh + 1) * sb], 0.0).astype(BF16),
                  v_bf[:, hh * GLA_DV:(hh + 1) * GLA_DV]) for hh in range(2)], axis=1)
        st = st_scr[p, d]
        entering = [None] * GLA_SUPER
        for ci in (range(GLA_SUPER) if d == 0 else range(GLA_SUPER - 1, -1, -1)):
            entering[ci] = st
            st = st * jnp.exp(ends[ci]) + jnp.where(st_own, kv_all[:, ci * LANE:(ci + 1) * LANE], 0.0)
        st_scr[p, d] = st
        o += _dot_nt(expand(q_in), jnp.concatenate(entering, axis=1).astype(BF16))
        if d == 0:
            of_scr[rs, vl] = o
        else:
            ob_scr[rs, vl] = o

    def super_block(n, carry):
        for p in range(pairs):
            decays = log_decays(n, p)
            for d in range(2):
                one_pair_direction(n, p, d, decays[d])
        return carry

    if n_super == 1:
        super_block(0, 0)
    else:
        lax.fori_loop(0, n_super, super_block, 0)

    for h in range(2 * pairs):
        vs = slice(h * GLA_DV, (h + 1) * GLA_DV)
        r = r_ref[:, vs]
        y = _rms(of_scr[:, vs] + ob_scr[:, vs], gg_ref[...]) * (r * jax.nn.sigmoid(r))
        y_ref[:, vs] = y.astype(y_ref.dtype)
        p, hh = divmod(h, 2)
        for d in range(2):
            sfin_ref[d, h] = st_scr[p, d, hh * GLA_DV:(hh + 1) * GLA_DV, :].T[hh * GLA_DK:(hh + 1) * GLA_DK, :]


def bidir_gla(proj, gates, s0, gla_g, n_batch, seq, row_blk0):
    shared_s0 = s0.shape[0] == 1
    pairs = HB // 2 if seq * HB * GLA_DV * 4 <= 1024 * 1024 else 2
    wqk = pairs * LANE
    wv = pairs * 2 * GLA_DV
    col_q = (2 * HA * 2 * DIFF_HD + HA * DIFF_VD) // wqk
    col_k = col_q + HB * GLA_DK // wqk
    col_v = (col_k * wqk + HB * GLA_DK) // wv
    col_r = col_v + HB * GLA_DV // wv
    n_tok = n_batch * seq
    return pl.pallas_call(
        functools.partial(_gla_kernel, seq=seq, pairs=pairs),
        grid=(n_batch, HB // 2 // pairs),
        in_specs=[pl.BlockSpec((seq, wqk), lambda b, p: (row_blk0 + b, col_q + p)),
                  pl.BlockSpec((seq, wqk), lambda b, p: (row_blk0 + b, col_k + p)),
                  pl.BlockSpec((seq, wv), lambda b, p: (row_blk0 + b, col_v + p)),
                  pl.BlockSpec((seq, wv), lambda b, p: (row_blk0 + b, col_r + p)),
                  pl.BlockSpec((2, seq, wqk), lambda b, p: (0, row_blk0 + b, p)),
                  pl.BlockSpec((None, 2, 2 * pairs, GLA_DK, GLA_DV),
                               lambda b, p: (0 if shared_s0 else b, 0, p, 0, 0)),
                  pl.BlockSpec((1, GLA_DV), lambda b, p: (0, 0))],
        out_specs=[pl.BlockSpec((seq, wv), lambda b, p: (b, p)),
                   pl.BlockSpec((None, 2, 2 * pairs, GLA_DK, GLA_DV), lambda b, p: (b, 0, p, 0, 0))],
        out_shape=[jax.ShapeDtypeStruct((n_tok, HB * GLA_DV), BF16),
                   jax.ShapeDtypeStruct((n_batch, 2, HB, GLA_DK, GLA_DV), F32)],
        scratch_shapes=[pltpu.VMEM((seq, wv), F32), pltpu.VMEM((seq, wv), F32),
                        pltpu.VMEM((pairs, 2, 2 * GLA_DV, LANE), F32)],
        compiler_params=_params(("arbitrary", "arbitrary")),
        name="bidir_gla",
    )(proj, proj, proj, proj, gates, s0, gla_g.reshape(1, GLA_DV))


def _head_slice(kh, gq):
    h = kh * Q_PER_KV + gq
    return slice(h * LANE, (h + 1) * LANE)


def _sink_column(sink_ref, kh, rows):
    return jnp.concatenate([jnp.full((rows, 1), sink_ref[kh * Q_PER_KV + gq], F32) for gq in range(Q_PER_KV)],
                           axis=0)


def _gqa_ctx_kernel(sink_ref, q_ref, k_ref, v_ref, o_ref, ko_ref, vo_ref):
    scale = HEAD_DIM ** -0.5
    n_rows = CTX_BATCHES_PER_STEP * SEQ
    for kh in range(KV_HEADS):
        ks = slice(kh * LANE, (kh + 1) * LANE)
        ko_ref[pl.ds(kh, n_rows, stride=KV_HEADS), :] = k_ref[:, ks]
        vo_ref[pl.ds(kh, n_rows, stride=KV_HEADS), :] = v_ref[:, ks]
    for bb in range(CTX_BATCHES_PER_STEP):
        rows = slice(bb * SEQ, (bb + 1) * SEQ)
        for kh in range(KV_HEADS):
            ks = slice(kh * LANE, (kh + 1) * LANE)
            k_bf = k_ref[rows, ks].astype(BF16)
            v1_bf = _with_ones(v_ref[rows, ks].astype(BF16))
            for gq in range(Q_PER_KV):
                hs = _head_slice(kh, gq)
                sink = sink_ref[kh * Q_PER_KV + gq]
                s = _dot_nt((q_ref[rows, hs] * scale).astype(BF16), k_bf)
                m = jnp.maximum(jnp.max(s, axis=-1, keepdims=True), sink)
                ov = _dot(jnp.exp(s - m).astype(BF16), v1_bf)
                o_ref[rows, hs] = (ov[:, :LANE] * (1.0 / (ov[:, LANE:] + jnp.exp(sink - m)))).astype(o_ref.dtype)


def gqa_context(proj, sinks):
    wq = N_HEADS * HEAD_DIM
    wkv = KV_HEADS * HEAD_DIM
    rows = CTX_BATCHES_PER_STEP * SEQ

    def kv(c):
        return pl.BlockSpec((rows, wkv), lambda b: (b, c))

    cache = pl.BlockSpec((rows * KV_HEADS, LANE), lambda b: (b, 0))
    return pl.pallas_call(
        _gqa_ctx_kernel,
        grid=(BATCH // CTX_BATCHES_PER_STEP,),
        in_specs=[pl.BlockSpec(memory_space=pltpu.SMEM),
                  pl.BlockSpec((rows, wq), lambda b: (b, 0)), kv(wq // wkv), kv(wq // wkv + 1)],
        out_specs=[pl.BlockSpec((rows, wq), lambda b: (b, 0)), cache, cache],
        out_shape=[jax.ShapeDtypeStruct((TP, wq), BF16), jax.ShapeDtypeStruct((TP * KV_HEADS, LANE), F32),
                   jax.ShapeDtypeStruct((TP * KV_HEADS, LANE), F32)],
        compiler_params=_params(("arbitrary",)),
        name="gqa_context",
    )(sinks, proj, proj, proj)


def _gqa_lat_kernel(sink_ref, q_ref, k_ref, v_ref, ck_ref, cv_ref, kc_ref, kl_ref, kh_ref,
                    qc_ref, ql_ref, qh_ref, o_ref, kw_scr, vw_scr, kc_scr, vc_scr):
    quarter = HEAD_DIM // 4
    w = WINDOW
    wkv = KV_HEADS * HEAD_DIM

    @pl.when(pl.program_id(1) == 0)
    def _():
        kw_scr[0:w, :] = jnp.zeros((w, wkv), BF16)
        kw_scr[w + DEC_SEQ:, :] = jnp.zeros((w, wkv), BF16)
        vw_scr[0:w, :] = jnp.zeros((w, 2 * wkv), BF16)
        vw_scr[w + DEC_SEQ:, :] = jnp.zeros((w, 2 * wkv), BF16)
        for kh in range(KV_HEADS):
            ks = slice(kh * LANE, (kh + 1) * LANE)
            vs = slice(2 * kh * LANE, 2 * (kh + 1) * LANE)
            kw_scr[w:w + DEC_SEQ, ks] = _rope(k_ref[:, ks], kc_ref[...], kl_ref[...], kh_ref[...],
                                              quarter).astype(BF16)
            vw_scr[w:w + DEC_SEQ, vs] = _with_ones(v_ref[:, ks].astype(BF16))
            head_rows = pl.ds(kh, PAST_LEN, stride=KV_HEADS)
            vc_scr[:, vs] = _with_ones(cv_ref[head_rows, :].astype(BF16))
            kc_scr[:, ks] = ck_ref[head_rows, :].astype(BF16)

    qi = lax.broadcasted_iota(jnp.int32, (Q_PER_KV * w, 3 * w), 0) % w
    kj = lax.broadcasted_iota(jnp.int32, (Q_PER_KV * w, 3 * w), 1)
    scale = HEAD_DIM ** -0.5
    for blk in range(LAT_QBLOCKS_PER_STEP):
        n = pl.program_id(1) * LAT_QBLOCKS_PER_STEP + blk
        rows = slice(blk * w, (blk + 1) * w)
        win = pl.ds(pl.multiple_of(n * w, w), 3 * w)
        kpos = n * w - w + kj
        valid = (kj >= qi) & (kj <= qi + 2 * w) & (kpos >= 0) & (kpos < DEC_SEQ)
        for kh in range(KV_HEADS):
            ks = slice(kh * LANE, (kh + 1) * LANE)
            vs = slice(2 * kh * LANE, 2 * (kh + 1) * LANE)
            q = jnp.concatenate(
                [(_rope(q_ref[rows, _head_slice(kh, gq)], qc_ref[rows, :], ql_ref[rows, :], qh_ref[rows, :],
                        quarter) * scale).astype(BF16) for gq in range(Q_PER_KV)], axis=0)
            sink = _sink_column(sink_ref, kh, w)
            s_c = _dot_nt(q, kc_scr[:, ks])
            s_w = jnp.where(valid, _dot_nt(q, kw_scr[win, ks]), NEG_INF)
            m = jnp.maximum(jnp.maximum(jnp.max(s_c, axis=-1, keepdims=True),
                                        jnp.max(s_w, axis=-1, keepdims=True)), sink)
            ov = (_dot(jnp.exp(s_c - m).astype(BF16), vc_scr[:, vs])
                  + _dot(jnp.exp(s_w - m).astype(BF16), vw_scr[win, vs]))
            o = ov[:, :LANE] * (1.0 / (ov[:, LANE:] + jnp.exp(sink - m)))
            for gq in range(Q_PER_KV):
                o_ref[rows, _head_slice(kh, gq)] = o[gq * w:(gq + 1) * w].astype(o_ref.dtype)


def gqa_latent(proj, cache_k, cache_v, tables, sinks):
    wq = N_HEADS * HEAD_DIM
    wkv = KV_HEADS * HEAD_DIM
    tq = LAT_QBLOCKS_PER_STEP * WINDOW
    nq = DEC_SEQ // tq
    row0 = TP // tq
    kv_row0 = TP // DEC_SEQ
    cos, sin_lo, sin_hi = tables
    full = pl.BlockSpec((DEC_SEQ, LANE), lambda b, i: (0, 0))
    qtab = pl.BlockSpec((tq, LANE), lambda b, i: (i, 0))
    cache = pl.BlockSpec((None, PAST_LEN * KV_HEADS, LANE), lambda b, i: (b, 0, 0))
    return pl.pallas_call(
        _gqa_lat_kernel,
        grid=(DEC_BATCH, nq),
        in_specs=[pl.BlockSpec(memory_space=pltpu.SMEM),
                  pl.BlockSpec((tq, wq), lambda b, i: (row0 + b * nq + i, 0)),
                  pl.BlockSpec((DEC_SEQ, wkv), lambda b, i: (kv_row0 + b, wq // wkv)),
                  pl.BlockSpec((DEC_SEQ, wkv), lambda b, i: (kv_row0 + b, wq // wkv + 1)),
                  cache, cache, full, full, full, qtab, qtab, qtab],
        out_specs=pl.BlockSpec((tq, wq), lambda b, i: (b * nq + i, 0)),
        out_shape=jax.ShapeDtypeStruct((TS, wq), BF16),
        scratch_shapes=[pltpu.VMEM((DEC_SEQ + 2 * WINDOW, wkv), BF16),
                        pltpu.VMEM((DEC_SEQ + 2 * WINDOW, 2 * wkv), BF16),
                        pltpu.VMEM((PAST_LEN, wkv), BF16),
                        pltpu.VMEM((PAST_LEN, 2 * wkv), BF16)],
        compiler_params=_params(("arbitrary", "arbitrary")),
        name="gqa_latent",
    )(sinks, proj, proj, proj, cache_k, cache_v, cos, sin_lo, sin_hi, cos, sin_lo, sin_hi)


def _even_layer(xc, xs, mod, j, layer, norm_g, w_in_even, lam_vec, subln_g, w_gate1, w_gate2, b_gate, gla_norm_g,
                w_out_even, cache_a_k, cache_a_v, state_b, tables):
    lam_init = 0.8 - 0.6 * math.exp(-0.3 * layer)
    ng = HB * GLA_DK
    wg1 = jnp.zeros((D_MODEL, LANE), F32).at[:, :2 * GLA_GATE_RANK].set(
        jnp.concatenate([w_gate1[j, 0], w_gate1[j, 1]], axis=-1)).astype(BF16)
    wg2 = jnp.zeros((2, LANE, ng), F32)
    for e in range(2):
        wg2 = wg2.at[e, e * GLA_GATE_RANK:(e + 1) * GLA_GATE_RANK, :].set(w_gate2[j, e])
    h, gates = norm_mod_gate(xc, xs, norm_g[layer, 0], mod, 0, 1, wg1, wg2.astype(BF16),
                             b_gate[j].reshape(2, 1, ng))
    proj = ws_matmul(h, w_in_even, j)
    ya_c, new_k, new_v = diff_attention_context(proj, lam_vec, subln_g[j], lam_init)
    ya_s = diff_attention_latent(proj, cache_a_k[:, j].reshape(DEC_BATCH, PAST_LEN * HA, 2 * DIFF_HD),
                                 cache_a_v[:, j].reshape(DEC_BATCH, PAST_LEN * HA, DIFF_VD),
                                 tables, lam_vec, subln_g[j], lam_init)
    yb_c, s_fin = bidir_gla(proj, gates, jnp.zeros((1, 2, HB, GLA_DK, GLA_DV), F32), gla_norm_g[j],
                            BATCH, SEQ, 0)
    yb_s, _ = bidir_gla(proj, gates, state_b[:, j], gla_norm_g[j], DEC_BATCH, DEC_SEQ, TP // DEC_SEQ)
    x, h_ffn = ws_matmul_residual([(ya_c, ya_s), (yb_c, yb_s)], w_out_even, j, (xc, xs), mod, 2,
                                  tm=OUT_PROJ_ROWS, next_norm=(norm_g[layer, 1], 3, 4))
    return (x, h_ffn, new_k.reshape(BATCH, SEQ, HA, 2 * DIFF_HD), new_v.reshape(BATCH, SEQ, HA, DIFF_VD), s_fin)


def _odd_layer(x, mod, j, layer, norm_g, w_in_odd, sinks, w_out_odd, cache_c_k, cache_c_v, tables):
    proj = ws_matmul(norm_mod(x, norm_g[layer, 0], mod, 0, 1), w_in_odd, j)
    o_c, new_k, new_v = gqa_context(proj, sinks[j])
    o_s = gqa_latent(proj, cache_c_k[:, j].reshape(DEC_BATCH, PAST_LEN * KV_HEADS, HEAD_DIM),
                     cache_c_v[:, j].reshape(DEC_BATCH, PAST_LEN * KV_HEADS, HEAD_DIM), tables, sinks[j])
    x, h_ffn = ws_matmul_residual([(o_c, o_s)], w_out_odd, j, (x,), mod, 2,
                                  tm=OUT_PROJ_ROWS, next_norm=(norm_g[layer, 1], 3, 4))
    return (x, h_ffn, new_k.reshape(BATCH, SEQ, KV_HEADS, HEAD_DIM), new_v.reshape(BATCH, SEQ, KV_HEADS, HEAD_DIM))


def _ffn(x, h_ffn, mod, layer, w_ffn_in, w_ffn_out):
    act = ws_swiglu(h_ffn, w_ffn_in, layer)
    return ws_matmul_residual([(act,)], w_ffn_out, layer, (x,), mod, 5, tm=512, tn=512)


def kernel(x_prompt, x_sample, c, cache_a_k, cache_a_v, state_b, cache_c_k, cache_c_v, c_ctx, w_ada, b_ada,
           norm_g, w_in_even, lam_q1, lam_k1, lam_q2, lam_k2, subln_g, w_gate1, w_gate2, b_gate, gla_norm_g,
           w_out_even, w_in_odd, sinks, w_out_odd, w_ffn_in, w_ffn_out, final_norm_g):
    assert DEPTH == 2, "layer 0 reads the two input streams, layer 1 the fused token matrix"
    cvec = jnp.concatenate([c_ctx[None, :], c, jnp.zeros((MOD_ROWS - 1 - DEC_BATCH, D_MODEL), F32)], axis=0)
    mods = adaln(cvec, w_ada, b_ada).reshape(DEPTH, MOD_ROWS, 1, 6 * D_MODEL)
    tab_d = rope_tables(DEC_SEQ, DIFF_HD)
    tab_h = rope_tables(DEC_SEQ, HEAD_DIM)

    lam_vec = jnp.stack([lam_q1[0], lam_k1[0], lam_q2[0], lam_k2[0]], axis=0)
    x, h_ffn, ak, av, sb = _even_layer(x_prompt.reshape(TP, D_MODEL), x_sample.reshape(TS, D_MODEL), mods[0], 0, 0,
                                       norm_g, w_in_even, lam_vec, subln_g, w_gate1, w_gate2, b_gate, gla_norm_g,
                                       w_out_even, cache_a_k, cache_a_v, state_b, tab_d)
    x = _ffn(x, h_ffn, mods[0], 0, w_ffn_in, w_ffn_out)
    x, h_ffn, ck, cv = _odd_layer(x, mods[1], 0, 1, norm_g, w_in_odd, sinks, w_out_odd, cache_c_k, cache_c_v,
                                  tab_h)
    x = _ffn(x, h_ffn, mods[1], 1, w_ffn_in, w_ffn_out)

    y_prompt = final_norm(x, final_norm_g, 0, TP).reshape(BATCH, SEQ, D_MODEL)
    y_sample = final_norm(x, final_norm_g, TP, TS).reshape(DEC_BATCH, DEC_SEQ, D_MODEL)
    return (y_prompt, y_sample, ak[:, None], av[:, None], sb[:, None], ck[:, None], cv[:, None])
```

```python
import functools
import math

import jax
import jax.numpy as jnp
from jax import lax
from jax.experimental import pallas as pl
from jax.experimental.pallas import tpu as pltpu

D_MODEL = 2048
BATCH = 32
SEQ = 256
DEPTH = 2
DEC_BATCH = 2
DEC_SEQ = 2048
PAST_LEN = 512
GRID_W = 64
HEAD_DIM = 128
N_HEADS = D_MODEL // HEAD_DIM
HA = N_HEADS // 2
DIFF_HD = HEAD_DIM // 2
DIFF_VD = HEAD_DIM
HB = N_HEADS // 2
GLA_DK = HEAD_DIM // 2
GLA_DV = HEAD_DIM
GLA_GATE_RANK = 16
GLA_TAU = 16.0
GLA_CHUNK = 64
GLA_SUPER = 4
CTX_BATCHES_PER_STEP = 4
LAT_QBLOCKS_PER_STEP = 2
DIFF_LAT_CHAINS = 4
KV_HEADS = N_HEADS // 4
Q_PER_KV = N_HEADS // KV_HEADS
WINDOW = 128
D_FF = -(-8 * D_MODEL // (3 * 256)) * 256
ROPE_BASE = 10000.0
EPS = 1e-6
NEG_INF = -1e30
EVEN_IN = 2 * HA * 2 * DIFF_HD + HA * DIFF_VD + 2 * HB * GLA_DK + 2 * HB * GLA_DV
ODD_IN = (N_HEADS + 2 * KV_HEADS) * HEAD_DIM

TP = BATCH * SEQ
TS = DEC_BATCH * DEC_SEQ
T = TP + TS
LANE = 128
MOD_ROWS = 8
NORM_ROWS = 16
NORM_UNROLL = 8
OUT_PROJ_ROWS = 256
VMEM_LIMIT = 56 * 1024 * 1024

F32 = jnp.float32
BF16 = jnp.bfloat16


def _params(semantics, vmem=VMEM_LIMIT):
    return pltpu.CompilerParams(dimension_semantics=semantics, vmem_limit_bytes=vmem)


def _dot(a, b):
    return jnp.dot(a, b, preferred_element_type=F32)


def _dot_nt(a, b):
    return lax.dot_general(a, b, (((1,), (1,)), ((), ())), preferred_element_type=F32)


def _group_of_tile(i, tm):
    r = i * tm
    return jnp.where(r < TP, 0, 1 + (r - TP) // DEC_SEQ)


def _rms(x, g):
    return (x * lax.rsqrt(jnp.mean(x * x, axis=-1, keepdims=True) + EPS)) * g


def _adaln_kernel(c_ref, w_ref, b_ref, o_ref):
    c = c_ref[...]
    s = c * jax.nn.sigmoid(c)
    o_ref[...] = _dot(s.astype(BF16), w_ref[...].astype(BF16)) + b_ref[...]


def adaln(cvec, w_ada, b_ada, tn=2048):
    n = w_ada.shape[-1]
    return pl.pallas_call(
        _adaln_kernel,
        grid=(DEPTH, n // tn),
        in_specs=[
            pl.BlockSpec((MOD_ROWS, D_MODEL), lambda l, j: (0, 0)),
            pl.BlockSpec((None, D_MODEL, tn), lambda l, j: (l, 0, j)),
            pl.BlockSpec((None, 1, tn), lambda l, j: (l, 0, j)),
        ],
        out_specs=pl.BlockSpec((None, MOD_ROWS, tn), lambda l, j: (l, 0, j)),
        out_shape=jax.ShapeDtypeStruct((DEPTH, MOD_ROWS, n), F32),
        compiler_params=_params(("arbitrary", "arbitrary")),
        name="adaln",
    )(cvec, w_ada, b_ada.reshape(DEPTH, 1, n))


def _by_row_groups(x_ref, o_ref, fn):
    def body(r, carry):
        rs = pl.ds(pl.multiple_of(r * NORM_ROWS, NORM_ROWS), NORM_ROWS)
        o_ref[rs, :] = fn(x_ref[rs, :]).astype(o_ref.dtype)
        return carry

    lax.fori_loop(0, x_ref.shape[0] // NORM_ROWS, body, 0, unroll=NORM_UNROLL)


def _norm_mod_into(h_ref, x_ref, g_ref, shift_ref, scale_ref):
    _by_row_groups(x_ref, h_ref, lambda x: _rms(x, g_ref[...]) * (1 + scale_ref[...]) + shift_ref[...])


def _norm_mod_kernel(x_ref, g_ref, shift_ref, scale_ref, h_ref):
    _norm_mod_into(h_ref, x_ref, g_ref, shift_ref, scale_ref)


def _norm_mod_gate_kernel(xc_ref, xs_ref, g_ref, shift_ref, scale_ref, wg1_ref, wg2_ref, bg_ref,
                          h_ref, gate_ref, *, ctx_tiles):
    @pl.when(pl.program_id(0) < ctx_tiles)
    def _():
        _norm_mod_into(h_ref, xc_ref, g_ref, shift_ref, scale_ref)

    @pl.when(pl.program_id(0) >= ctx_tiles)
    def _():
        _norm_mod_into(h_ref, xs_ref, g_ref, shift_ref, scale_ref)

    low = _dot(h_ref[...], wg1_ref[...]).astype(BF16)
    for e in range(2):
        logit = _dot(low, wg2_ref[e]) + bg_ref[e]
        log_sig = jnp.minimum(logit, 0.0) - jnp.log(1.0 + jnp.exp(-jnp.abs(logit)))
        gate_ref[e] = log_sig / GLA_TAU


def _mod_specs(tm, shift_idx, scale_idx):
    def spec(idx):
        return pl.BlockSpec((None, 1, D_MODEL), lambda i: (_group_of_tile(i, tm), 0, idx))
    return [pl.BlockSpec((1, D_MODEL), lambda i: (0, 0)), spec(shift_idx), spec(scale_idx)]


def norm_mod(x, g, mod, shift_idx, scale_idx, tm=1024):
    return pl.pallas_call(
        _norm_mod_kernel,
        grid=(T // tm,),
        in_specs=[pl.BlockSpec((tm, D_MODEL), lambda i: (i, 0))] + _mod_specs(tm, shift_idx, scale_idx),
        out_specs=pl.BlockSpec((tm, D_MODEL), lambda i: (i, 0)),
        out_shape=jax.ShapeDtypeStruct((T, D_MODEL), BF16),
        compiler_params=_params(("arbitrary",)),
        name="norm_mod",
    )(x, g.reshape(1, D_MODEL), mod, mod)


def norm_mod_gate(xc, xs, g, mod, shift_idx, scale_idx, wg1, wg2, bg, tm=512):
    ng = HB * GLA_DK
    ctx_tiles = TP // tm
    return pl.pallas_call(
        functools.partial(_norm_mod_gate_kernel, ctx_tiles=ctx_tiles),
        grid=(T // tm,),
        in_specs=[pl.BlockSpec((tm, D_MODEL), lambda i: (jnp.minimum(i, ctx_tiles - 1), 0)),
                  pl.BlockSpec((tm, D_MODEL), lambda i: (jnp.maximum(i - ctx_tiles, 0), 0))]
        + _mod_specs(tm, shift_idx, scale_idx)
        + [pl.BlockSpec((D_MODEL, LANE), lambda i: (0, 0)),
           pl.BlockSpec((2, LANE, ng), lambda i: (0, 0, 0)),
           pl.BlockSpec((2, 1, ng), lambda i: (0, 0, 0))],
        out_specs=[pl.BlockSpec((tm, D_MODEL), lambda i: (i, 0)),
                   pl.BlockSpec((2, tm, ng), lambda i: (0, i, 0))],
        out_shape=[jax.ShapeDtypeStruct((T, D_MODEL), BF16), jax.ShapeDtypeStruct((2, T, ng), F32)],
        compiler_params=_params(("arbitrary",)),
        name="norm_mod_gate",
    )(xc, xs, g.reshape(1, D_MODEL), mod, mod, wg1, wg2, bg)


def _cast_weights(w_refs, w_scrs):
    @pl.when(pl.program_id(1) == 0)
    def _():
        for w_ref, w_scr in zip(w_refs, w_scrs):
            w_scr[...] = w_ref[...].astype(BF16)


def _ws_plain_kernel(h_ref, w_ref, o_ref, w_scr):
    _cast_weights([w_ref], [w_scr])
    o_ref[...] = _dot(h_ref[...], w_scr[...])


def _ws_swiglu_kernel(h_ref, wg_ref, wu_ref, o_ref, wg_scr, wu_scr):
    _cast_weights([wg_ref, wu_ref], [wg_scr, wu_scr])
    h = h_ref[...]
    gate = _dot(h, wg_scr[...])
    up = _dot(h, wu_scr[...])
    o_ref[...] = (gate * jax.nn.sigmoid(gate) * up).astype(o_ref.dtype)


def _ws_res_kernel(*refs, lhs_arity, x_arity, ctx_tiles, next_norm):
    n_l = sum(lhs_arity)
    w_ref = refs[n_l]
    x_refs = refs[n_l + 1:n_l + 1 + x_arity]
    rest = refs[n_l + 1 + x_arity:]
    if next_norm:
        gate_ref, g2_ref, shift2_ref, scale2_ref, o_ref, h_ref, w_scr = rest
    else:
        gate_ref, o_ref, w_scr = rest
    groups, at = [], 0
    for a in lhs_arity:
        groups.append(refs[at:at + a])
        at += a
    _cast_weights([w_ref], [w_scr])

    def emit(side):
        acc, k0 = None, 0
        for grp in groups:
            ref = grp[side] if len(grp) == 2 else grp[0]
            part = _dot(ref[...], w_scr[k0:k0 + ref.shape[1], :])
            acc = part if acc is None else acc + part
            k0 += ref.shape[1]
        x_ref = x_refs[side] if x_arity == 2 else x_refs[0]
        x_new = x_ref[...] + gate_ref[...] * acc
        o_ref[...] = x_new
        if next_norm:
            h_ref[...] = (_rms(x_new, g2_ref[...]) * (1 + scale2_ref[...]) + shift2_ref[...]).astype(BF16)

    if x_arity == 2 or 2 in lhs_arity:
        @pl.when(pl.program_id(1) < ctx_tiles)
        def _():
            emit(0)

        @pl.when(pl.program_id(1) >= ctx_tiles)
        def _():
            emit(1)
    else:
        emit(0)


def _rows_specs(arrays, tm, width, col):
    if len(arrays) == 1:
        return [pl.BlockSpec((tm, width), lambda j, i: (i, col(j)))]
    ctx_tiles = TP // tm
    return [pl.BlockSpec((tm, width), lambda j, i: (jnp.minimum(i, ctx_tiles - 1), col(j))),
            pl.BlockSpec((tm, width), lambda j, i: (jnp.maximum(i - ctx_tiles, 0), col(j)))]


def ws_matmul(h, w_stack, layer, tm=1024, tn=1024):
    kdim, n = w_stack.shape[1:]
    return pl.pallas_call(
        _ws_plain_kernel,
        grid=(n // tn, T // tm),
        in_specs=[pl.BlockSpec((tm, kdim), lambda j, i: (i, 0)),
                  pl.BlockSpec((None, kdim, tn), lambda j, i: (layer, 0, j))],
        out_specs=pl.BlockSpec((tm, tn), lambda j, i: (i, j)),
        out_shape=jax.ShapeDtypeStruct((T, n), F32),
        scratch_shapes=[pltpu.VMEM((kdim, tn), BF16)],
        compiler_params=_params(("arbitrary", "arbitrary")),
        name="ws_matmul",
    )(h, w_stack)


def ws_swiglu(h, w_stack, layer, tm=1024, tn=512):
    kdim = w_stack.shape[1]
    nf = D_FF // tn
    return pl.pallas_call(
        _ws_swiglu_kernel,
        grid=(nf, T // tm),
        in_specs=[pl.BlockSpec((tm, kdim), lambda j, i: (i, 0)),
                  pl.BlockSpec((None, kdim, tn), lambda j, i: (layer, 0, j)),
                  pl.BlockSpec((None, kdim, tn), lambda j, i: (layer, 0, j + nf))],
        out_specs=pl.BlockSpec((tm, tn), lambda j, i: (i, j)),
        out_shape=jax.ShapeDtypeStruct((T, D_FF), BF16),
        scratch_shapes=[pltpu.VMEM((kdim, tn), BF16), pltpu.VMEM((kdim, tn), BF16)],
        compiler_params=_params(("arbitrary", "arbitrary")),
        name="ws_swiglu",
    )(h, w_stack, w_stack)


def ws_matmul_residual(lhs_groups, w_stack, layer, x, mod, gate_idx, tm=512, tn=1024, next_norm=None):
    kdim, n = w_stack.shape[1:]
    if next_norm is not None:
        tn = n
    gate_blk = n // tn
    specs, args = [], []
    for grp in lhs_groups:
        specs += _rows_specs(grp, tm, grp[0].shape[1], lambda j: 0)
        args += list(grp)
    w_mode = {} if next_norm is None else dict(pipeline_mode=pl.Buffered(1))
    specs.append(pl.BlockSpec((None, kdim, tn), lambda j, i: (layer, 0, j), **w_mode))
    specs += _rows_specs(x, tm, tn, lambda j: j)
    specs.append(pl.BlockSpec((None, 1, tn), lambda j, i: (_group_of_tile(i, tm), 0, gate_idx * gate_blk + j)))
    tail = [mod]
    out_specs = [pl.BlockSpec((tm, tn), lambda j, i: (i, j))]
    out_shape = [jax.ShapeDtypeStruct((T, n), F32)]
    if next_norm is not None:
        gain, shift_idx, scale_idx = next_norm
        specs.append(pl.BlockSpec((1, n), lambda j, i: (0, 0)))
        specs += [pl.BlockSpec((None, 1, n), lambda j, i, idx=idx: (_group_of_tile(i, tm), 0, idx))
                  for idx in (shift_idx, scale_idx)]
        tail += [gain.reshape(1, n), mod, mod]
        out_specs.append(pl.BlockSpec((tm, tn), lambda j, i: (i, j)))
        out_shape.append(jax.ShapeDtypeStruct((T, n), BF16))
    outs = pl.pallas_call(
        functools.partial(_ws_res_kernel, lhs_arity=tuple(len(grp) for grp in lhs_groups), x_arity=len(x),
                          ctx_tiles=TP // tm, next_norm=next_norm is not None),
        grid=(n // tn, T // tm),
        in_specs=specs,
        out_specs=out_specs,
        out_shape=out_shape,
        scratch_shapes=[pltpu.VMEM((kdim, tn), BF16)],
        compiler_params=_params(("arbitrary", "arbitrary")),
        name="ws_matmul_residual",
    )(*args, w_stack, *x, *tail)
    return outs[0] if next_norm is None else tuple(outs)


def _final_norm_kernel(x_ref, g_ref, o_ref):
    o_ref[...] = _rms(x_ref[...], g_ref[...])


def final_norm(x, g, row0, n_rows, tm=1024):
    blk0 = row0 // tm
    return pl.pallas_call(
        _final_norm_kernel,
        grid=(n_rows // tm,),
        in_specs=[pl.BlockSpec((tm, D_MODEL), lambda i: (blk0 + i, 0)),
                  pl.BlockSpec((1, D_MODEL), lambda i: (0, 0))],
        out_specs=pl.BlockSpec((tm, D_MODEL), lambda i: (i, 0)),
        out_shape=jax.ShapeDtypeStruct((n_rows, D_MODEL), F32),
        compiler_params=_params(("arbitrary",)),
        name="final_norm",
    )(x, g.reshape(1, D_MODEL))


def _grid_angles(n_tok, rot_dim):
    t = jnp.arange((n_tok // GRID_W) * GRID_W)
    row = (t // GRID_W).astype(F32)
    col = (t % GRID_W).astype(F32)
    half = rot_dim // 2
    inv = ROPE_BASE ** (-jnp.arange(0, half, 2, dtype=F32) / half)
    return row[:, None] * inv[None], col[:, None] * inv[None]


def rope_tables(n_tok, rot_dim):
    ang_row, ang_col = _grid_angles(n_tok, rot_dim)
    zeros = jnp.zeros_like(ang_row)
    reps = LANE // rot_dim

    def lanes(r1, r2, c1, c2):
        return jnp.tile(jnp.concatenate([r1, r2, c1, c2], axis=-1), (1, reps))

    cr, sr, cc, sc = jnp.cos(ang_row), jnp.sin(ang_row), jnp.cos(ang_col), jnp.sin(ang_col)
    return lanes(cr, cr, cc, cc), lanes(-sr, zeros, -sc, zeros), lanes(zeros, sr, zeros, sc)


def _rope(x, cos, sin_lo, sin_hi, quarter):
    return (x * cos + pltpu.roll(x, LANE - quarter, 1) * sin_lo + pltpu.roll(x, quarter, 1) * sin_hi)


def _diff_lambda(lam_ref, lam_init):
    e1 = jnp.exp(jnp.sum(lam_ref[0:1, :] * lam_ref[1:2, :], axis=-1, keepdims=True))
    e2 = jnp.exp(jnp.sum(lam_ref[2:3, :] * lam_ref[3:4, :], axis=-1, keepdims=True))
    return e1 - e2 + lam_init


def _with_ones(v_bf):
    return jnp.concatenate([v_bf, jnp.ones(v_bf.shape, BF16)], axis=1)


def _softmax_av(q_bf, k_bf, v1_bf):
    s = _dot_nt(q_bf, k_bf)
    e = jnp.exp(s - jnp.max(s, axis=-1, keepdims=True))
    if v1_bf.shape[1] == LANE:
        return _dot(e.astype(BF16), v1_bf) * (1.0 / jnp.sum(e, axis=-1, keepdims=True))
    ov = _dot(e.astype(BF16), v1_bf)
    return ov[:, :LANE] * (1.0 / ov[:, LANE:])


def _diff_head(q, k_bf, v1_bf, lam, subln_g, lam_init, stack_maps):
    n = q.shape[0]
    first = lax.broadcasted_iota(jnp.int32, q.shape, 1) < DIFF_HD
    q0 = jnp.where(first, q, 0.0).astype(BF16)
    q1 = jnp.where(first, 0.0, q).astype(BF16)
    if stack_maps:
        r = _softmax_av(jnp.concatenate([q0, q1], axis=0), k_bf, v1_bf)
        o = r[:n] - lam * r[n:]
    else:
        o = _softmax_av(q0, k_bf, v1_bf) - lam * _softmax_av(q1, k_bf, v1_bf)
    return _rms(o, subln_g) * (1.0 - lam_init)


def _diff_ctx_kernel(q_ref, k_ref, v_ref, lam_ref, sg_ref, o_ref, ko_ref, vo_ref, *, lam_init):
    lam = _diff_lambda(lam_ref, lam_init)
    scale = DIFF_HD ** -0.5
    n_rows = CTX_BATCHES_PER_STEP * SEQ
    for h in range(HA):
        sl = slice(h * LANE, (h + 1) * LANE)
        ko_ref[pl.ds(h, n_rows, stride=HA), :] = k_ref[:, sl]
        vo_ref[pl.ds(h, n_rows, stride=HA), :] = v_ref[:, sl]
    for bb in range(CTX_BATCHES_PER_STEP):
        rows = slice(bb * SEQ, (bb + 1) * SEQ)
        for h in range(HA):
            sl = slice(h * LANE, (h + 1) * LANE)
            y = _diff_head(q_ref[rows, sl] * scale, k_ref[rows, sl].astype(BF16),
                           _with_ones(v_ref[rows, sl].astype(BF16)), lam, sg_ref[...], lam_init, stack_maps=True)
            o_ref[rows, sl] = y.astype(o_ref.dtype)


def diff_attention_context(proj, lam_vec, subln_g, lam_init):
    w = HA * LANE

    def blk(c):
        return pl.BlockSpec((CTX_BATCHES_PER_STEP * SEQ, w), lambda b: (b, c))

    cache = pl.BlockSpec((CTX_BATCHES_PER_STEP * SEQ * HA, LANE), lambda b: (b, 0))
    return pl.pallas_call(
        functools.partial(_diff_ctx_kernel, lam_init=lam_init),
        grid=(BATCH // CTX_BATCHES_PER_STEP,),
        in_specs=[blk(0), blk(1), blk(2),
                  pl.BlockSpec((4, DIFF_HD), lambda b: (0, 0)),
                  pl.BlockSpec((1, DIFF_VD), lambda b: (0, 0))],
        out_specs=[blk(0), cache, cache],
        out_shape=[jax.ShapeDtypeStruct((TP, w), BF16), jax.ShapeDtypeStruct((TP * HA, LANE), F32),
                   jax.ShapeDtypeStruct((TP * HA, LANE), F32)],
        compiler_params=_params(("arbitrary",)),
        name="diff_attention_context",
    )(proj, proj, proj, lam_vec, subln_g.reshape(1, DIFF_VD))


def _diff_lat_kernel(q_ref, k_ref, v_ref, ck_ref, cv_ref, kc_ref, kl_ref, kh_ref, qc_ref, ql_ref, qh_ref,
                     lam_ref, sg_ref, o_ref, k_scr, v_scr, *, lam_init):
    quarter = DIFF_HD // 4

    @pl.when(pl.program_id(2) == 0)
    def _():
        head_rows = pl.ds(pl.program_id(1), PAST_LEN, stride=HA)
        k_scr[0:PAST_LEN, :] = ck_ref[head_rows, :].astype(BF16)
        k_scr[PAST_LEN:, :] = _rope(k_ref[...], kc_ref[...], kl_ref[...], kh_ref[...], quarter).astype(BF16)
        v_scr[0:PAST_LEN, :] = cv_ref[head_rows, :].astype(BF16)
        v_scr[PAST_LEN:, :] = v_ref[...].astype(BF16)

    lam = _diff_lambda(lam_ref, lam_init)
    sub = q_ref.shape[0] // DIFF_LAT_CHAINS
    for c in range(DIFF_LAT_CHAINS):
        rs = slice(c * sub, (c + 1) * sub)
        q = _rope(q_ref[rs, :], qc_ref[rs, :], ql_ref[rs, :], qh_ref[rs, :], quarter) * (DIFF_HD ** -0.5)
        y = _diff_head(q, k_scr[...], v_scr[...], lam, sg_ref[...], lam_init, stack_maps=False)
        o_ref[rs, :] = y.astype(o_ref.dtype)


def diff_attention_latent(proj, cache_k, cache_v, tables, lam_vec, subln_g, lam_init, tq=1024):
    nq = DEC_SEQ // tq
    row0 = TP // tq
    kv_row0 = TP // DEC_SEQ
    cos, sin_lo, sin_hi = tables
    full = pl.BlockSpec((DEC_SEQ, LANE), lambda b, h, i: (0, 0))
    qtab = pl.BlockSpec((tq, LANE), lambda b, h, i: (i, 0))
    cache = pl.BlockSpec((None, PAST_LEN * HA, LANE), lambda b, h, i: (b, 0, 0))
    return pl.pallas_call(
        functools.partial(_diff_lat_kernel, lam_init=lam_init),
        grid=(DEC_BATCH, HA, nq),
        in_specs=[pl.BlockSpec((tq, LANE), lambda b, h, i: (row0 + b * nq + i, h)),
                  pl.BlockSpec((DEC_SEQ, LANE), lambda b, h, i: (kv_row0 + b, HA + h)),
                  pl.BlockSpec((DEC_SEQ, LANE), lambda b, h, i: (kv_row0 + b, 2 * HA + h)),
                  cache, cache, full, full, full, qtab, qtab, qtab,
                  pl.BlockSpec((4, DIFF_HD), lambda b, h, i: (0, 0)),
                  pl.BlockSpec((1, DIFF_VD), lambda b, h, i: (0, 0))],
        out_specs=pl.BlockSpec((tq, LANE), lambda b, h, i: (b * nq + i, h)),
        out_shape=jax.ShapeDtypeStruct((TS, HA * LANE), BF16),
        scratch_shapes=[pltpu.VMEM((PAST_LEN + DEC_SEQ, LANE), BF16),
                        pltpu.VMEM((PAST_LEN + DEC_SEQ, LANE), BF16)],
        compiler_params=_params(("arbitrary", "arbitrary", "arbitrary")),
        name="diff_attention_latent",
    )(proj, proj, proj, cache_k, cache_v, cos, sin_lo, sin_hi, cos, sin_lo, sin_hi,
      lam_vec, subln_g.reshape(1, DIFF_VD))


def _gla_kernel(q_ref, k_ref, v_ref, r_ref, g_ref, s0_ref, gg_ref, y_ref, sfin_ref, of_scr, ob_scr, st_scr,
                *, seq, pairs):
    c = GLA_CHUNK
    sb = GLA_SUPER * c
    n_super = seq // sb
    rows = lax.broadcasted_iota(jnp.int32, (sb, sb), 0)
    cols = lax.broadcasted_iota(jnp.int32, (sb, sb), 1)
    same_chunk = (rows // c) == (cols // c)
    keep = (same_chunk & (rows >= cols), same_chunk & (cols >= rows))
    lane = lax.broadcasted_iota(jnp.int32, (sb, LANE), 1)
    own = (lane < GLA_DK, lane >= GLA_DK)
    chunk_of_row = lax.broadcasted_iota(jnp.int32, (sb, LANE), 0) // c
    st_rows = lax.broadcasted_iota(jnp.int32, (2 * GLA_DV, LANE), 0)
    st_lane = lax.broadcasted_iota(jnp.int32, (2 * GLA_DV, LANE), 1)
    st_own = (st_rows < GLA_DV) == (st_lane < GLA_DK)
    zpad = jnp.zeros((GLA_DK, GLA_DV), F32)

    def expand(x):
        return jnp.concatenate([jnp.where(chunk_of_row == ci, x, 0.0) for ci in range(GLA_SUPER)],
                               axis=1).astype(BF16)

    for p in range(pairs):
        for d in range(2):
            s0 = [s0_ref[d, 2 * p + hh] for hh in range(2)]
            st_scr[p, d] = jnp.concatenate([jnp.concatenate([s0[0], zpad], axis=0).T,
                                            jnp.concatenate([zpad, s0[1]], axis=0).T], axis=0)

    def rows_of(n, d):
        r0 = (n if d == 0 else n_super - 1 - n) * sb
        return pl.ds(r0, sb) if n_super == 1 else pl.ds(pl.multiple_of(r0, sb), sb)

    def chunk_totals(prefix):
        ends = [prefix[ci * c + c - 1:ci * c + c, :] for ci in range(GLA_SUPER)]
        return ends, jnp.concatenate([jnp.broadcast_to(e, (c, LANE)) for e in ends], axis=0)

    def log_decays(n, p):
        kl = slice(p * LANE, (p + 1) * LANE)
        g_b = g_ref[1, rows_of(n, 1), kl]
        prefix = jnp.concatenate([g_ref[0, rows_of(n, 0), kl], g_b], axis=1)
        pos = lax.broadcasted_iota(jnp.int32, prefix.shape, 0) % c
        step = 1
        while step < c:
            prefix = prefix + jnp.where(pos >= step, pltpu.roll(prefix, step, 0), 0.0)
            step *= 2
        ends_f, total_f = chunk_totals(prefix[:, :LANE])
        ends_b, total_b = chunk_totals(prefix[:, LANE:])
        return (prefix[:, :LANE], ends_f, total_f), (total_b - prefix[:, LANE:] + g_b, ends_b, total_b)

    def one_pair_direction(n, p, d, decays):
        kl = slice(p * LANE, (p + 1) * LANE)
        vl = slice(2 * p * GLA_DV, 2 * (p + 1) * GLA_DV)
        rs = rows_of(n, d)
        b, ends, total = decays
        q_in = (q_ref[rs, kl] * (GLA_DK ** -0.5)) * jnp.exp(b)
        kk = k_ref[rs, kl]
        k_in = (kk * jnp.exp(-b)).astype(BF16)
        k_end = kk * jnp.exp(total - b)
        v = v_ref[rs, vl]
        v_bf = v.astype(BF16)
        kv_all = _dot(v.T.astype(BF16), expand(k_end))
        q2 = jnp.concatenate([jnp.where(own[0], q_in, 0.0), jnp.where(own[1], q_in, 0.0)], axis=0)
        a2 = _dot_nt(q2.astype(BF16), k_in)
        o = jnp.concatenate(
            [_dot(jnp.where(keep[d], a2[hh * sb:(hh + 1) * sb], 0.0).astype(BF16),
                  v_bf[:, hh * GLA_DV:(hh + 1) * GLA_DV]) for hh in range(2)], axis=1)
        st = st_scr[p, d]
        entering = [None] * GLA_SUPER
        for ci in (range(GLA_SUPER) if d == 0 else range(GLA_SUPER - 1, -1, -1)):
            entering[ci] = st
            st = st * jnp.exp(ends[ci]) + jnp.where(st_own, kv_all[:, ci * LANE:(ci + 1) * LANE], 0.0)
        st_scr[p, d] = st
        o += _dot_nt(expand(q_in), jnp.concatenate(entering, axis=1).astype(BF16))
        if d == 0:
            of_scr[rs, vl] = o
        else:
            ob_scr[rs, vl] = o

    def super_block(n, carry):
        for p in range(pairs):
            decays = log_decays(n, p)
            for d in range(2):
                one_pair_direction(n, p, d, decays[d])
        return carry

    if n_super == 1:
        super_block(0, 0)
    else:
        lax.fori_loop(0, n_super, super_block, 0)

    for h in range(2 * pairs):
        vs = slice(h * GLA_DV, (h + 1) * GLA_DV)
        r = r_ref[:, vs]
        y = _rms(of_scr[:, vs] + ob_scr[:, vs], gg_ref[...]) * (r * jax.nn.sigmoid(r))
        y_ref[:, vs] = y.astype(y_ref.dtype)
        p, hh = divmod(h, 2)
        for d in range(2):
            sfin_ref[d, h] = st_scr[p, d, hh * GLA_DV:(hh + 1) * GLA_DV, :].T[hh * GLA_DK:(hh + 1) * GLA_DK, :]


def bidir_gla(proj, gates, s0, gla_g, n_batch, seq, row_blk0):
    shared_s0 = s0.shape[0] == 1
    pairs = HB // 2 if seq * HB * GLA_DV * 4 <= 1024 * 1024 else 2
    wqk = pairs * LANE
    wv = pairs * 2 * GLA_DV
    col_q = (2 * HA * 2 * DIFF_HD + HA * DIFF_VD) // wqk
    col_k = col_q + HB * GLA_DK // wqk
    col_v = (col_k * wqk + HB * GLA_DK) // wv
    col_r = col_v + HB * GLA_DV // wv
    n_tok = n_batch * seq
    return pl.pallas_call(
        functools.partial(_gla_kernel, seq=seq, pairs=pairs),
        grid=(n_batch, HB // 2 // pairs),
        in_specs=[pl.BlockSpec((seq, wqk), lambda b, p: (row_blk0 + b, col_q + p)),
                  pl.BlockSpec((seq, wqk), lambda b, p: (row_blk0 + b, col_k + p)),
                  pl.BlockSpec((seq, wv), lambda b, p: (row_blk0 + b, col_v + p)),
                  pl.BlockSpec((seq, wv), lambda b, p: (row_blk0 + b, col_r + p)),
                  pl.BlockSpec((2, seq, wqk), lambda b, p: (0, row_blk0 + b, p)),
                  pl.BlockSpec((None, 2, 2 * pairs, GLA_DK, GLA_DV),
                               lambda b, p: (0 if shared_s0 else b, 0, p, 0, 0)),
                  pl.BlockSpec((1, GLA_DV), lambda b, p: (0, 0))],
        out_specs=[pl.BlockSpec((seq, wv), lambda b, p: (b, p)),
                   pl.BlockSpec((None, 2, 2 * pairs, GLA_DK, GLA_DV), lambda b, p: (b, 0, p, 0, 0))],
        out_shape=[jax.ShapeDtypeStruct((n_tok, HB * GLA_DV), BF16),
                   jax.ShapeDtypeStruct((n_batch, 2, HB, GLA_DK, GLA_DV), F32)],
        scratch_shapes=[pltpu.VMEM((seq, wv), F32), pltpu.VMEM((seq, wv), F32),
                        pltpu.VMEM((pairs, 2, 2 * GLA_DV, LANE), F32)],
        compiler_params=_params(("arbitrary", "arbitrary")),
        name="bidir_gla",
    )(proj, proj, proj, proj, gates, s0, gla_g.reshape(1, GLA_DV))


def _head_slice(kh, gq):
    h = kh * Q_PER_KV + gq
    return slice(h * LANE, (h + 1) * LANE)


def _sink_column(sink_ref, kh, rows):
    return jnp.concatenate([jnp.full((rows, 1), sink_ref[kh * Q_PER_KV + gq], F32) for gq in range(Q_PER_KV)],
                           axis=0)


def _gqa_ctx_kernel(sink_ref, q_ref, k_ref, v_ref, o_ref, ko_ref, vo_ref):
    scale = HEAD_DIM ** -0.5
    n_rows = CTX_BATCHES_PER_STEP * SEQ
    for kh in range(KV_HEADS):
        ks = slice(kh * LANE, (kh + 1) * LANE)
        ko_ref[pl.ds(kh, n_rows, stride=KV_HEADS), :] = k_ref[:, ks]
        vo_ref[pl.ds(kh, n_rows, stride=KV_HEADS), :] = v_ref[:, ks]
    for bb in range(CTX_BATCHES_PER_STEP):
        rows = slice(bb * SEQ, (bb + 1) * SEQ)
        for kh in range(KV_HEADS):
            ks = slice(kh * LANE, (kh + 1) * LANE)
            k_bf = k_ref[rows, ks].astype(BF16)
            v1_bf = _with_ones(v_ref[rows, ks].astype(BF16))
            for gq in range(Q_PER_KV):
                hs = _head_slice(kh, gq)
                sink = sink_ref[kh * Q_PER_KV + gq]
                s = _dot_nt((q_ref[rows, hs] * scale).astype(BF16), k_bf)
                m = jnp.maximum(jnp.max(s, axis=-1, keepdims=True), sink)
                ov = _dot(jnp.exp(s - m).astype(BF16), v1_bf)
                o_ref[rows, hs] = (ov[:, :LANE] * (1.0 / (ov[:, LANE:] + jnp.exp(sink - m)))).astype(o_ref.dtype)


def gqa_context(proj, sinks):
    wq = N_HEADS * HEAD_DIM
    wkv = KV_HEADS * HEAD_DIM
    rows = CTX_BATCHES_PER_STEP * SEQ

    def kv(c):
        return pl.BlockSpec((rows, wkv), lambda b: (b, c))

    cache = pl.BlockSpec((rows * KV_HEADS, LANE), lambda b: (b, 0))
    return pl.pallas_call(
        _gqa_ctx_kernel,
        grid=(BATCH // CTX_BATCHES_PER_STEP,),
        in_specs=[pl.BlockSpec(memory_space=pltpu.SMEM),
                  pl.BlockSpec((rows, wq), lambda b: (b, 0)), kv(wq // wkv), kv(wq // wkv + 1)],
        out_specs=[pl.BlockSpec((rows, wq), lambda b: (b, 0)), cache, cache],
        out_shape=[jax.ShapeDtypeStruct((TP, wq), BF16), jax.ShapeDtypeStruct((TP * KV_HEADS, LANE), F32),
                   jax.ShapeDtypeStruct((TP * KV_HEADS, LANE), F32)],
        compiler_params=_params(("arbitrary",)),
        name="gqa_context",
    )(sinks, proj, proj, proj)


def _gqa_lat_kernel(sink_ref, q_ref, k_ref, v_ref, ck_ref, cv_ref, kc_ref, kl_ref, kh_ref,
                    qc_ref, ql_ref, qh_ref, o_ref, kw_scr, vw_scr, kc_scr, vc_scr):
    quarter = HEAD_DIM // 4
    w = WINDOW
    wkv = KV_HEADS * HEAD_DIM

    @pl.when(pl.program_id(1) == 0)
    def _():
        kw_scr[0:w, :] = jnp.zeros((w, wkv), BF16)
        kw_scr[w + DEC_SEQ:, :] = jnp.zeros((w, wkv), BF16)
        vw_scr[0:w, :] = jnp.zeros((w, 2 * wkv), BF16)
        vw_scr[w + DEC_SEQ:, :] = jnp.zeros((w, 2 * wkv), BF16)
        for kh in range(KV_HEADS):
            ks = slice(kh * LANE, (kh + 1) * LANE)
            vs = slice(2 * kh * LANE, 2 * (kh + 1) * LANE)
            kw_scr[w:w + DEC_SEQ, ks] = _rope(k_ref[:, ks], kc_ref[...], kl_ref[...], kh_ref[...],
                                              quarter).astype(BF16)
            vw_scr[w:w + DEC_SEQ, vs] = _with_ones(v_ref[:, ks].astype(BF16))
            head_rows = pl.ds(kh, PAST_LEN, stride=KV_HEADS)
            vc_scr[:, vs] = _with_ones(cv_ref[head_rows, :].astype(BF16))
            kc_scr[:, ks] = ck_ref[head_rows, :].astype(BF16)

    qi = lax.broadcasted_iota(jnp.int32, (Q_PER_KV * w, 3 * w), 0) % w
    kj = lax.broadcasted_iota(jnp.int32, (Q_PER_KV * w, 3 * w), 1)
    scale = HEAD_DIM ** -0.5
    for blk in range(LAT_QBLOCKS_PER_STEP):
        n = pl.program_id(1) * LAT_QBLOCKS_PER_STEP + blk
        rows = slice(blk * w, (blk + 1) * w)
        win = pl.ds(pl.multiple_of(n * w, w), 3 * w)
        kpos = n * w - w + kj
        valid = (kj >= qi) & (kj <= qi + 2 * w) & (kpos >= 0) & (kpos < DEC_SEQ)
        for kh in range(KV_HEADS):
            ks = slice(kh * LANE, (kh + 1) * LANE)
            vs = slice(2 * kh * LANE, 2 * (kh + 1) * LANE)
            q = jnp.concatenate(
                [(_rope(q_ref[rows, _head_slice(kh, gq)], qc_ref[rows, :], ql_ref[rows, :], qh_ref[rows, :],
                        quarter) * scale).astype(BF16) for gq in range(Q_PER_KV)], axis=0)
            sink = _sink_column(sink_ref, kh, w)
            s_c = _dot_nt(q, kc_scr[:, ks])
            s_w = jnp.where(valid, _dot_nt(q, kw_scr[win, ks]), NEG_INF)
            m = jnp.maximum(jnp.maximum(jnp.max(s_c, axis=-1, keepdims=True),
                                        jnp.max(s_w, axis=-1, keepdims=True)), sink)
            ov = (_dot(jnp.exp(s_c - m).astype(BF16), vc_scr[:, vs])
                  + _dot(jnp.exp(s_w - m).astype(BF16), vw_scr[win, vs]))
            o = ov[:, :LANE] * (1.0 / (ov[:, LANE:] + jnp.exp(sink - m)))
            for gq in range(Q_PER_KV):
                o_ref[rows, _head_slice(kh, gq)] = o[gq * w:(gq + 1) * w].astype(o_ref.dtype)


def gqa_latent(proj, cache_k, cache_v, tables, sinks):
    wq = N_HEADS * HEAD_DIM
    wkv = KV_HEADS * HEAD_DIM
    tq = LAT_QBLOCKS_PER_STEP * WINDOW
    nq = DEC_SEQ // tq
    row0 = TP // tq
    kv_row0 = TP // DEC_SEQ
    cos, sin_lo, sin_hi = tables
    full = pl.BlockSpec((DEC_SEQ, LANE), lambda b, i: (0, 0))
    qtab = pl.BlockSpec((tq, LANE), lambda b, i: (i, 0))
    cache = pl.BlockSpec((None, PAST_LEN * KV_HEADS, LANE), lambda b, i: (b, 0, 0))
    return pl.pallas_call(
        _gqa_lat_kernel,
        grid=(DEC_BATCH, nq),
        in_specs=[pl.BlockSpec(memory_space=pltpu.SMEM),
                  pl.BlockSpec((tq, wq), lambda b, i: (row0 + b * nq + i, 0)),
                  pl.BlockSpec((DEC_SEQ, wkv), lambda b, i: (kv_row0 + b, wq // wkv)),
                  pl.BlockSpec((DEC_SEQ, wkv), lambda b, i: (kv_row0 + b, wq // wkv + 1)),
                  cache, cache, full, full, full, qtab, qtab, qtab],
        out_specs=pl.BlockSpec((tq, wq), lambda b, i: (b * nq + i, 0)),
        out_shape=jax.ShapeDtypeStruct((TS, wq), BF16),
        scratch_shapes=[pltpu.VMEM((DEC_SEQ + 2 * WINDOW, wkv), BF16),
                        pltpu.VMEM((DEC_SEQ + 2 * WINDOW, 2 * wkv), BF16),
                        pltpu.VMEM((PAST_LEN, wkv), BF16),
                        pltpu.VMEM((PAST_LEN, 2 * wkv), BF16)],
        compiler_params=_params(("arbitrary", "arbitrary")),
        name="gqa_latent",
    )(sinks, proj, proj, proj, cache_k, cache_v, cos, sin_lo, sin_hi, cos, sin_lo, sin_hi)


def _even_layer(xc, xs, mod, j, layer, norm_g, w_in_even, lam_vec, subln_g, w_gate1, w_gate2, b_gate, gla_norm_g,
                w_out_even, cache_a_k, cache_a_v, state_b, tables):
    lam_init = 0.8 - 0.6 * math.exp(-0.3 * layer)
    ng = HB * GLA_DK
    wg1 = jnp.zeros((D_MODEL, LANE), F32).at[:, :2 * GLA_GATE_RANK].set(
        jnp.concatenate([w_gate1[j, 0], w_gate1[j, 1]], axis=-1)).astype(BF16)
    wg2 = jnp.zeros((2, LANE, ng), F32)
    for e in range(2):
        wg2 = wg2.at[e, e * GLA_GATE_RANK:(e + 1) * GLA_GATE_RANK, :].set(w_gate2[j, e])
    h, gates = norm_mod_gate(xc, xs, norm_g[layer, 0], mod, 0, 1, wg1, wg2.astype(BF16),
                             b_gate[j].reshape(2, 1, ng))
    proj = ws_matmul(h, w_in_even, j)
    ya_c, new_k, new_v = diff_attention_context(proj, lam_vec, subln_g[j], lam_init)
    ya_s = diff_attention_latent(proj, cache_a_k[:, j].reshape(DEC_BATCH, PAST_LEN * HA, 2 * DIFF_HD),
                                 cache_a_v[:, j].reshape(DEC_BATCH, PAST_LEN * HA, DIFF_VD),
                                 tables, lam_vec, subln_g[j], lam_init)
    yb_c, s_fin = bidir_gla(proj, gates, jnp.zeros((1, 2, HB, GLA_DK, GLA_DV), F32), gla_norm_g[j],
                            BATCH, SEQ, 0)
    yb_s, _ = bidir_gla(proj, gates, state_b[:, j], gla_norm_g[j], DEC_BATCH, DEC_SEQ, TP // DEC_SEQ)
    x, h_ffn = ws_matmul_residual([(ya_c, ya_s), (yb_c, yb_s)], w_out_even, j, (xc, xs), mod, 2,
                                  tm=OUT_PROJ_ROWS, next_norm=(norm_g[layer, 1], 3, 4))
    return (x, h_ffn, new_k.reshape(BATCH, SEQ, HA, 2 * DIFF_HD), new_v.reshape(BATCH, SEQ, HA, DIFF_VD), s_fin)


def _odd_layer(x, mod, j, layer, norm_g, w_in_odd, sinks, w_out_odd, cache_c_k, cache_c_v, tables):
    proj = ws_matmul(norm_mod(x, norm_g[layer, 0], mod, 0, 1), w_in_odd, j)
    o_c, new_k, new_v = gqa_context(proj, sinks[j])
    o_s = gqa_latent(proj, cache_c_k[:, j].reshape(DEC_BATCH, PAST_LEN * KV_HEADS, HEAD_DIM),
                     cache_c_v[:, j].reshape(DEC_BATCH, PAST_LEN * KV_HEADS, HEAD_DIM), tables, sinks[j])
    x, h_ffn = ws_matmul_residual([(o_c, o_s)], w_out_odd, j, (x,), mod, 2,
                                  tm=2 * OUT_PROJ_ROWS, next_norm=(norm_g[layer, 1], 3, 4))
    return (x, h_ffn, new_k.reshape(BATCH, SEQ, KV_HEADS, HEAD_DIM), new_v.reshape(BATCH, SEQ, KV_HEADS, HEAD_DIM))


def _ffn(x, h_ffn, mod, layer, w_ffn_in, w_ffn_out):
    act = ws_swiglu(h_ffn, w_ffn_in, layer)
    return ws_matmul_residual([(act,)], w_ffn_out, layer, (x,), mod, 5, tm=512, tn=512)


def kernel(x_prompt, x_sample, c, cache_a_k, cache_a_v, state_b, cache_c_k, cache_c_v, c_ctx, w_ada, b_ada,
           norm_g, w_in_even, lam_q1, lam_k1, lam_q2, lam_k2, subln_g, w_gate1, w_gate2, b_gate, gla_norm_g,
           w_out_even, w_in_odd, sinks, w_out_odd, w_ffn_in, w_ffn_out, final_norm_g):
    assert DEPTH == 2, "layer 0 reads the two input streams, layer 1 the fused token matrix"
    cvec = jnp.concatenate([c_ctx[None, :], c, jnp.zeros((MOD_ROWS - 1 - DEC_BATCH, D_MODEL), F32)], axis=0)
    mods = adaln(cvec, w_ada, b_ada).reshape(DEPTH, MOD_ROWS, 1, 6 * D_MODEL)
    tab_d = rope_tables(DEC_SEQ, DIFF_HD)
    tab_h = rope_tables(DEC_SEQ, HEAD_DIM)

    lam_vec = jnp.stack([lam_q1[0], lam_k1[0], lam_q2[0], lam_k2[0]], axis=0)
    x, h_ffn, ak, av, sb = _even_layer(x_prompt.reshape(TP, D_MODEL), x_sample.reshape(TS, D_MODEL), mods[0], 0, 0,
                                       norm_g, w_in_even, lam_vec, subln_g, w_gate1, w_gate2, b_gate, gla_norm_g,
                                       w_out_even, cache_a_k, cache_a_v, state_b, tab_d)
    x = _ffn(x, h_ffn, mods[0], 0, w_ffn_in, w_ffn_out)
    x, h_ffn, ck, cv = _odd_layer(x, mods[1], 0, 1, norm_g, w_in_odd, sinks, w_out_odd, cache_c_k, cache_c_v,
                                  tab_h)
    x = _ffn(x, h_ffn, mods[1], 1, w_ffn_in, w_ffn_out)

    y_prompt = final_norm(x, final_norm_g, 0, TP).reshape(BATCH, SEQ, D_MODEL)
    y_sample = final_norm(x, final_norm_g, TP, TS).reshape(DEC_BATCH, DEC_SEQ, D_MODEL)
    return (y_prompt, y_sample, ak[:, None], av[:, None], sb[:, None], ck[:, None], cv[:, None])
```

```python
import functools
import math

import jax
import jax.numpy as jnp
from jax import lax
from jax.experimental import pallas as pl
from jax.experimental.pallas import tpu as pltpu

D_MODEL = 2048
BATCH = 32
SEQ = 256
DEPTH = 2
DEC_BATCH = 2
DEC_SEQ = 2048
PAST_LEN = 512
GRID_W = 64
HEAD_DIM = 128
N_HEADS = D_MODEL // HEAD_DIM
HA = N_HEADS // 2
DIFF_HD = HEAD_DIM // 2
DIFF_VD = HEAD_DIM
HB = N_HEADS // 2
GLA_DK = HEAD_DIM // 2
GLA_DV = HEAD_DIM
GLA_GATE_RANK = 16
GLA_TAU = 16.0
GLA_CHUNK = 64
GLA_SUPER = 4
CTX_BATCHES_PER_STEP = 4
LAT_QBLOCKS_PER_STEP = 2
DIFF_LAT_CHAINS = 4
KV_HEADS = N_HEADS // 4
Q_PER_KV = N_HEADS // KV_HEADS
WINDOW = 128
D_FF = -(-8 * D_MODEL // (3 * 256)) * 256
ROPE_BASE = 10000.0
EPS = 1e-6
NEG_INF = -1e30
EVEN_IN = 2 * HA * 2 * DIFF_HD + HA * DIFF_VD + 2 * HB * GLA_DK + 2 * HB * GLA_DV
ODD_IN = (N_HEADS + 2 * KV_HEADS) * HEAD_DIM

TP = BATCH * SEQ
TS = DEC_BATCH * DEC_SEQ
T = TP + TS
LANE = 128
MOD_ROWS = 8
NORM_ROWS = 16
NORM_UNROLL = 8
OUT_PROJ_ROWS = 256
VMEM_LIMIT = 56 * 1024 * 1024

F32 = jnp.float32
BF16 = jnp.bfloat16


def _params(semantics, vmem=VMEM_LIMIT):
    return pltpu.CompilerParams(dimension_semantics=semantics, vmem_limit_bytes=vmem)


def _dot(a, b):
    return jnp.dot(a, b, preferred_element_type=F32)


def _dot_nt(a, b):
    return lax.dot_general(a, b, (((1,), (1,)), ((), ())), preferred_element_type=F32)


def _group_of_tile(i, tm):
    r = i * tm
    return jnp.where(r < TP, 0, 1 + (r - TP) // DEC_SEQ)


def _rms(x, g):
    return (x * lax.rsqrt(jnp.mean(x * x, axis=-1, keepdims=True) + EPS)) * g


def _adaln_kernel(c_ref, w_ref, b_ref, o_ref):
    c = c_ref[...]
    s = c * jax.nn.sigmoid(c)
    o_ref[...] = _dot(s.astype(BF16), w_ref[...].astype(BF16)) + b_ref[...]


def adaln(cvec, w_ada, b_ada, tn=2048):
    n = w_ada.shape[-1]
    return pl.pallas_call(
        _adaln_kernel,
        grid=(DEPTH, n // tn),
        in_specs=[
            pl.BlockSpec((MOD_ROWS, D_MODEL), lambda l, j: (0, 0)),
            pl.BlockSpec((None, D_MODEL, tn), lambda l, j: (l, 0, j)),
            pl.BlockSpec((None, 1, tn), lambda l, j: (l, 0, j)),
        ],
        out_specs=pl.BlockSpec((None, MOD_ROWS, tn), lambda l, j: (l, 0, j)),
        out_shape=jax.ShapeDtypeStruct((DEPTH, MOD_ROWS, n), F32),
        compiler_params=_params(("arbitrary", "arbitrary")),
        name="adaln",
    )(cvec, w_ada, b_ada.reshape(DEPTH, 1, n))


def _by_row_groups(x_ref, o_ref, fn):
    def body(r, carry):
        rs = pl.ds(pl.multiple_of(r * NORM_ROWS, NORM_ROWS), NORM_ROWS)
        o_ref[rs, :] = fn(x_ref[rs, :]).astype(o_ref.dtype)
        return carry

    lax.fori_loop(0, x_ref.shape[0] // NORM_ROWS, body, 0, unroll=NORM_UNROLL)


def _norm_mod_into(h_ref, x_ref, g_ref, shift_ref, scale_ref):
    _by_row_groups(x_ref, h_ref, lambda x: _rms(x, g_ref[...]) * (1 + scale_ref[...]) + shift_ref[...])


def _norm_mod_kernel(x_ref, g_ref, shift_ref, scale_ref, h_ref):
    _norm_mod_into(h_ref, x_ref, g_ref, shift_ref, scale_ref)


def _norm_mod_gate_kernel(xc_ref, xs_ref, g_ref, shift_ref, scale_ref, wg1_ref, wg2_ref, bg_ref,
                          h_ref, gate_ref, *, ctx_tiles):
    @pl.when(pl.program_id(0) < ctx_tiles)
    def _():
        _norm_mod_into(h_ref, xc_ref, g_ref, shift_ref, scale_ref)

    @pl.when(pl.program_id(0) >= ctx_tiles)
    def _():
        _norm_mod_into(h_ref, xs_ref, g_ref, shift_ref, scale_ref)

    low = _dot(h_ref[...], wg1_ref[...]).astype(BF16)
    for e in range(2):
        logit = _dot(low, wg2_ref[e]) + bg_ref[e]
        log_sig = jnp.minimum(logit, 0.0) - jnp.log(1.0 + jnp.exp(-jnp.abs(logit)))
        gate_ref[e] = log_sig / GLA_TAU


def _mod_specs(tm, shift_idx, scale_idx):
    def spec(idx):
        return pl.BlockSpec((None, 1, D_MODEL), lambda i: (_group_of_tile(i, tm), 0, idx))
    return [pl.BlockSpec((1, D_MODEL), lambda i: (0, 0)), spec(shift_idx), spec(scale_idx)]


def norm_mod(x, g, mod, shift_idx, scale_idx, tm=1024):
    return pl.pallas_call(
        _norm_mod_kernel,
        grid=(T // tm,),
        in_specs=[pl.BlockSpec((tm, D_MODEL), lambda i: (i, 0))] + _mod_specs(tm, shift_idx, scale_idx),
        out_specs=pl.BlockSpec((tm, D_MODEL), lambda i: (i, 0)),
        out_shape=jax.ShapeDtypeStruct((T, D_MODEL), BF16),
        compiler_params=_params(("arbitrary",)),
        name="norm_mod",
    )(x, g.reshape(1, D_MODEL), mod, mod)


def norm_mod_gate(xc, xs, g, mod, shift_idx, scale_idx, wg1, wg2, bg, tm=512):
    ng = HB * GLA_DK
    ctx_tiles = TP // tm
    return pl.pallas_call(
        functools.partial(_norm_mod_gate_kernel, ctx_tiles=ctx_tiles),
        grid=(T // tm,),
        in_specs=[pl.BlockSpec((tm, D_MODEL), lambda i: (jnp.minimum(i, ctx_tiles - 1), 0)),
                  pl.BlockSpec((tm, D_MODEL), lambda i: (jnp.maximum(i - ctx_tiles, 0), 0))]
        + _mod_specs(tm, shift_idx, scale_idx)
        + [pl.BlockSpec((D_MODEL, LANE), lambda i: (0, 0)),
           pl.BlockSpec((2, LANE, ng), lambda i: (0, 0, 0)),
           pl.BlockSpec((2, 1, ng), lambda i: (0, 0, 0))],
        out_specs=[pl.BlockSpec((tm, D_MODEL), lambda i: (i, 0)),
                   pl.BlockSpec((2, tm, ng), lambda i: (0, i, 0))],
        out_shape=[jax.ShapeDtypeStruct((T, D_MODEL), BF16), jax.ShapeDtypeStruct((2, T, ng), F32)],
        compiler_params=_params(("arbitrary",)),
        name="norm_mod_gate",
    )(xc, xs, g.reshape(1, D_MODEL), mod, mod, wg1, wg2, bg)


def _cast_weights(w_refs, w_scrs):
    @pl.when(pl.program_id(1) == 0)
    def _():
        for w_ref, w_scr in zip(w_refs, w_scrs):
            w_scr[...] = w_ref[...].astype(BF16)


def _ws_plain_kernel(h_ref, w_ref, o_ref, w_scr):
    _cast_weights([w_ref], [w_scr])
    o_ref[...] = _dot(h_ref[...], w_scr[...])


def _ws_swiglu_kernel(h_ref, wg_ref, wu_ref, o_ref, wg_scr, wu_scr):
    _cast_weights([wg_ref, wu_ref], [wg_scr, wu_scr])
    h = h_ref[...]
    gate = _dot(h, wg_scr[...])
    up = _dot(h, wu_scr[...])
    o_ref[...] = (gate * jax.nn.sigmoid(gate) * up).astype(o_ref.dtype)


def _ws_res_kernel(*refs, lhs_arity, x_arity, ctx_tiles, next_norm):
    n_l = sum(lhs_arity)
    w_ref = refs[n_l]
    x_refs = refs[n_l + 1:n_l + 1 + x_arity]
    rest = refs[n_l + 1 + x_arity:]
    if next_norm:
        gate_ref, g2_ref, shift2_ref, scale2_ref, o_ref, h_ref, w_scr = rest
    else:
        gate_ref, o_ref, w_scr = rest
    groups, at = [], 0
    for a in lhs_arity:
        groups.append(refs[at:at + a])
        at += a
    _cast_weights([w_ref], [w_scr])

    def emit(side):
        acc, k0 = None, 0
        for grp in groups:
            ref = grp[side] if len(grp) == 2 else grp[0]
            part = _dot(ref[...], w_scr[k0:k0 + ref.shape[1], :])
            acc = part if acc is None else acc + part
            k0 += ref.shape[1]
        x_ref = x_refs[side] if x_arity == 2 else x_refs[0]
        x_new = x_ref[...] + gate_ref[...] * acc
        o_ref[...] = x_new
        if next_norm:
            h_ref[...] = (_rms(x_new, g2_ref[...]) * (1 + scale2_ref[...]) + shift2_ref[...]).astype(BF16)

    if x_arity == 2 or 2 in lhs_arity:
        @pl.when(pl.program_id(1) < ctx_tiles)
        def _():
            emit(0)

        @pl.when(pl.program_id(1) >= ctx_tiles)
        def _():
            emit(1)
    else:
        emit(0)


def _rows_specs(arrays, tm, width, col):
    if len(arrays) == 1:
        return [pl.BlockSpec((tm, width), lambda j, i: (i, col(j)))]
    ctx_tiles = TP // tm
    return [pl.BlockSpec((tm, width), lambda j, i: (jnp.minimum(i, ctx_tiles - 1), col(j))),
            pl.BlockSpec((tm, width), lambda j, i: (jnp.maximum(i - ctx_tiles, 0), col(j)))]


def ws_matmul(h, w_stack, layer, tm=1024, tn=1024):
    kdim, n = w_stack.shape[1:]
    return pl.pallas_call(
        _ws_plain_kernel,
        grid=(n // tn, T // tm),
        in_specs=[pl.BlockSpec((tm, kdim), lambda j, i: (i, 0)),
                  pl.BlockSpec((None, kdim, tn), lambda j, i: (layer, 0, j))],
        out_specs=pl.BlockSpec((tm, tn), lambda j, i: (i, j)),
        out_shape=jax.ShapeDtypeStruct((T, n), F32),
        scratch_shapes=[pltpu.VMEM((kdim, tn), BF16)],
        compiler_params=_params(("arbitrary", "arbitrary")),
        name="ws_matmul",
    )(h, w_stack)


def ws_swiglu(h, w_stack, layer, tm=1024, tn=512):
    kdim = w_stack.shape[1]
    nf = D_FF // tn
    return pl.pallas_call(
        _ws_swiglu_kernel,
        grid=(nf, T // tm),
        in_specs=[pl.BlockSpec((tm, kdim), lambda j, i: (i, 0)),
                  pl.BlockSpec((None, kdim, tn), lambda j, i: (layer, 0, j)),
                  pl.BlockSpec((None, kdim, tn), lambda j, i: (layer, 0, j + nf))],
        out_specs=pl.BlockSpec((tm, tn), lambda j, i: (i, j)),
        out_shape=jax.ShapeDtypeStruct((T, D_FF), BF16),
        scratch_shapes=[pltpu.VMEM((kdim, tn), BF16), pltpu.VMEM((kdim, tn), BF16)],
        compiler_params=_params(("arbitrary", "arbitrary")),
        name="ws_swiglu",
    )(h, w_stack, w_stack)


def ws_matmul_residual(lhs_groups, w_stack, layer, x, mod, gate_idx, tm=512, tn=1024, next_norm=None,
                       single_buffer_w=False):
    kdim, n = w_stack.shape[1:]
    if next_norm is not None:
        tn = n
    gate_blk = n // tn
    specs, args = [], []
    for grp in lhs_groups:
        specs += _rows_specs(grp, tm, grp[0].shape[1], lambda j: 0)
        args += list(grp)
    w_mode = dict(pipeline_mode=pl.Buffered(1)) if (next_norm is not None or single_buffer_w) else {}
    specs.append(pl.BlockSpec((None, kdim, tn), lambda j, i: (layer, 0, j), **w_mode))
    specs += _rows_specs(x, tm, tn, lambda j: j)
    specs.append(pl.BlockSpec((None, 1, tn), lambda j, i: (_group_of_tile(i, tm), 0, gate_idx * gate_blk + j)))
    tail = [mod]
    out_specs = [pl.BlockSpec((tm, tn), lambda j, i: (i, j))]
    out_shape = [jax.ShapeDtypeStruct((T, n), F32)]
    if next_norm is not None:
        gain, shift_idx, scale_idx = next_norm
        specs.append(pl.BlockSpec((1, n), lambda j, i: (0, 0)))
        specs += [pl.BlockSpec((None, 1, n), lambda j, i, idx=idx: (_group_of_tile(i, tm), 0, idx))
                  for idx in (shift_idx, scale_idx)]
        tail += [gain.reshape(1, n), mod, mod]
        out_specs.append(pl.BlockSpec((tm, tn), lambda j, i: (i, j)))
        out_shape.append(jax.ShapeDtypeStruct((T, n), BF16))
    outs = pl.pallas_call(
        functools.partial(_ws_res_kernel, lhs_arity=tuple(len(grp) for grp in lhs_groups), x_arity=len(x),
                          ctx_tiles=TP // tm, next_norm=next_norm is not None),
        grid=(n // tn, T // tm),
        in_specs=specs,
        out_specs=out_specs,
        out_shape=out_shape,
        scratch_shapes=[pltpu.VMEM((kdim, tn), BF16)],
        compiler_params=_params(("arbitrary", "arbitrary")),
        name="ws_matmul_residual",
    )(*args, w_stack, *x, *tail)
    return outs[0] if next_norm is None else tuple(outs)


def _final_norm_kernel(x_ref, g_ref, o_ref):
    o_ref[...] = _rms(x_ref[...], g_ref[...])


def final_norm(x, g, row0, n_rows, tm=1024):
    blk0 = row0 // tm
    return pl.pallas_call(
        _final_norm_kernel,
        grid=(n_rows // tm,),
        in_specs=[pl.BlockSpec((tm, D_MODEL), lambda i: (blk0 + i, 0)),
                  pl.BlockSpec((1, D_MODEL), lambda i: (0, 0))],
        out_specs=pl.BlockSpec((tm, D_MODEL), lambda i: (i, 0)),
        out_shape=jax.ShapeDtypeStruct((n_rows, D_MODEL), F32),
        compiler_params=_params(("arbitrary",)),
        name="final_norm",
    )(x, g.reshape(1, D_MODEL))


def _grid_angles(n_tok, rot_dim):
    t = jnp.arange((n_tok // GRID_W) * GRID_W)
    row = (t // GRID_W).astype(F32)
    col = (t % GRID_W).astype(F32)
    half = rot_dim // 2
    inv = ROPE_BASE ** (-jnp.arange(0, half, 2, dtype=F32) / half)
    return row[:, None] * inv[None], col[:, None] * inv[None]


def rope_tables(n_tok, rot_dim):
    ang_row, ang_col = _grid_angles(n_tok, rot_dim)
    zeros = jnp.zeros_like(ang_row)
    reps = LANE // rot_dim

    def lanes(r1, r2, c1, c2):
        return jnp.tile(jnp.concatenate([r1, r2, c1, c2], axis=-1), (1, reps))

    cr, sr, cc, sc = jnp.cos(ang_row), jnp.sin(ang_row), jnp.cos(ang_col), jnp.sin(ang_col)
    return lanes(cr, cr, cc, cc), lanes(-sr, zeros, -sc, zeros), lanes(zeros, sr, zeros, sc)


def _rope(x, cos, sin_lo, sin_hi, quarter):
    return (x * cos + pltpu.roll(x, LANE - quarter, 1) * sin_lo + pltpu.roll(x, quarter, 1) * sin_hi)


def _diff_lambda(lam_ref, lam_init):
    e1 = jnp.exp(jnp.sum(lam_ref[0:1, :] * lam_ref[1:2, :], axis=-1, keepdims=True))
    e2 = jnp.exp(jnp.sum(lam_ref[2:3, :] * lam_ref[3:4, :], axis=-1, keepdims=True))
    return e1 - e2 + lam_init


def _with_ones(v_bf):
    return jnp.concatenate([v_bf, jnp.ones(v_bf.shape, BF16)], axis=1)


def _softmax_av(q_bf, k_bf, v1_bf):
    s = _dot_nt(q_bf, k_bf)
    e = jnp.exp(s - jnp.max(s, axis=-1, keepdims=True))
    if v1_bf.shape[1] == LANE:
        return _dot(e.astype(BF16), v1_bf) * (1.0 / jnp.sum(e, axis=-1, keepdims=True))
    ov = _dot(e.astype(BF16), v1_bf)
    return ov[:, :LANE] * (1.0 / ov[:, LANE:])


def _diff_head(q, k_bf, v1_bf, lam, subln_g, lam_init, stack_maps):
    n = q.shape[0]
    first = lax.broadcasted_iota(jnp.int32, q.shape, 1) < DIFF_HD
    q0 = jnp.where(first, q, 0.0).astype(BF16)
    q1 = jnp.where(first, 0.0, q).astype(BF16)
    if stack_maps:
        r = _softmax_av(jnp.concatenate([q0, q1], axis=0), k_bf, v1_bf)
        o = r[:n] - lam * r[n:]
    else:
        o = _softmax_av(q0, k_bf, v1_bf) - lam * _softmax_av(q1, k_bf, v1_bf)
    return _rms(o, subln_g) * (1.0 - lam_init)


def _diff_ctx_kernel(q_ref, k_ref, v_ref, lam_ref, sg_ref, o_ref, ko_ref, vo_ref, *, lam_init):
    lam = _diff_lambda(lam_ref, lam_init)
    scale = DIFF_HD ** -0.5
    n_rows = CTX_BATCHES_PER_STEP * SEQ
    for h in range(HA):
        sl = slice(h * LANE, (h + 1) * LANE)
        ko_ref[pl.ds(h, n_rows, stride=HA), :] = k_ref[:, sl]
        vo_ref[pl.ds(h, n_rows, stride=HA), :] = v_ref[:, sl]
    for bb in range(CTX_BATCHES_PER_STEP):
        rows = slice(bb * SEQ, (bb + 1) * SEQ)
        for h in range(HA):
            sl = slice(h * LANE, (h + 1) * LANE)
            y = _diff_head(q_ref[rows, sl] * scale, k_ref[rows, sl].astype(BF16),
                           _with_ones(v_ref[rows, sl].astype(BF16)), lam, sg_ref[...], lam_init, stack_maps=True)
            o_ref[rows, sl] = y.astype(o_ref.dtype)


def diff_attention_context(proj, lam_vec, subln_g, lam_init):
    w = HA * LANE

    def blk(c):
        return pl.BlockSpec((CTX_BATCHES_PER_STEP * SEQ, w), lambda b: (b, c))

    cache = pl.BlockSpec((CTX_BATCHES_PER_STEP * SEQ * HA, LANE), lambda b: (b, 0))
    return pl.pallas_call(
        functools.partial(_diff_ctx_kernel, lam_init=lam_init),
        grid=(BATCH // CTX_BATCHES_PER_STEP,),
        in_specs=[blk(0), blk(1), blk(2),
                  pl.BlockSpec((4, DIFF_HD), lambda b: (0, 0)),
                  pl.BlockSpec((1, DIFF_VD), lambda b: (0, 0))],
        out_specs=[blk(0), cache, cache],
        out_shape=[jax.ShapeDtypeStruct((TP, w), BF16), jax.ShapeDtypeStruct((TP * HA, LANE), F32),
                   jax.ShapeDtypeStruct((TP * HA, LANE), F32)],
        compiler_params=_params(("arbitrary",)),
        name="diff_attention_context",
    )(proj, proj, proj, lam_vec, subln_g.reshape(1, DIFF_VD))


def _diff_lat_kernel(q_ref, k_ref, v_ref, ck_ref, cv_ref, kc_ref, kl_ref, kh_ref, qc_ref, ql_ref, qh_ref,
                     lam_ref, sg_ref, o_ref, k_scr, v_scr, *, lam_init):
    quarter = DIFF_HD // 4

    @pl.when(pl.program_id(2) == 0)
    def _():
        head_rows = pl.ds(pl.program_id(1), PAST_LEN, stride=HA)
        k_scr[0:PAST_LEN, :] = ck_ref[head_rows, :].astype(BF16)
        k_scr[PAST_LEN:, :] = _rope(k_ref[...], kc_ref[...], kl_ref[...], kh_ref[...], quarter).astype(BF16)
        v_scr[0:PAST_LEN, :] = cv_ref[head_rows, :].astype(BF16)
        v_scr[PAST_LEN:, :] = v_ref[...].astype(BF16)

    lam = _diff_lambda(lam_ref, lam_init)
    sub = q_ref.shape[0] // DIFF_LAT_CHAINS
    for c in range(DIFF_LAT_CHAINS):
        rs = slice(c * sub, (c + 1) * sub)
        q = _rope(q_ref[rs, :], qc_ref[rs, :], ql_ref[rs, :], qh_ref[rs, :], quarter) * (DIFF_HD ** -0.5)
        y = _diff_head(q, k_scr[...], v_scr[...], lam, sg_ref[...], lam_init, stack_maps=False)
        o_ref[rs, :] = y.astype(o_ref.dtype)


def diff_attention_latent(proj, cache_k, cache_v, tables, lam_vec, subln_g, lam_init, tq=1024):
    nq = DEC_SEQ // tq
    row0 = TP // tq
    kv_row0 = TP // DEC_SEQ
    cos, sin_lo, sin_hi = tables
    full = pl.BlockSpec((DEC_SEQ, LANE), lambda b, h, i: (0, 0))
    qtab = pl.BlockSpec((tq, LANE), lambda b, h, i: (i, 0))
    cache = pl.BlockSpec((None, PAST_LEN * HA, LANE), lambda b, h, i: (b, 0, 0))
    return pl.pallas_call(
        functools.partial(_diff_lat_kernel, lam_init=lam_init),
        grid=(DEC_BATCH, HA, nq),
        in_specs=[pl.BlockSpec((tq, LANE), lambda b, h, i: (row0 + b * nq + i, h)),
                  pl.BlockSpec((DEC_SEQ, LANE), lambda b, h, i: (kv_row0 + b, HA + h)),
                  pl.BlockSpec((DEC_SEQ, LANE), lambda b, h, i: (kv_row0 + b, 2 * HA + h)),
                  cache, cache, full, full, full, qtab, qtab, qtab,
                  pl.BlockSpec((4, DIFF_HD), lambda b, h, i: (0, 0)),
                  pl.BlockSpec((1, DIFF_VD), lambda b, h, i: (0, 0))],
        out_specs=pl.BlockSpec((tq, LANE), lambda b, h, i: (b * nq + i, h)),
        out_shape=jax.ShapeDtypeStruct((TS, HA * LANE), BF16),
        scratch_shapes=[pltpu.VMEM((PAST_LEN + DEC_SEQ, LANE), BF16),
                        pltpu.VMEM((PAST_LEN + DEC_SEQ, LANE), BF16)],
        compiler_params=_params(("arbitrary", "arbitrary", "arbitrary")),
        name="diff_attention_latent",
    )(proj, proj, proj, cache_k, cache_v, cos, sin_lo, sin_hi, cos, sin_lo, sin_hi,
      lam_vec, subln_g.reshape(1, DIFF_VD))


def _gla_kernel(q_ref, k_ref, v_ref, r_ref, g_ref, s0_ref, gg_ref, y_ref, sfin_ref, of_scr, ob_scr, st_scr,
                *, seq, pairs):
    c = GLA_CHUNK
    sb = GLA_SUPER * c
    n_super = seq // sb
    rows = lax.broadcasted_iota(jnp.int32, (sb, sb), 0)
    cols = lax.broadcasted_iota(jnp.int32, (sb, sb), 1)
    same_chunk = (rows // c) == (cols // c)
    keep = (same_chunk & (rows >= cols), same_chunk & (cols >= rows))
    lane = lax.broadcasted_iota(jnp.int32, (sb, LANE), 1)
    own = (lane < GLA_DK, lane >= GLA_DK)
    chunk_of_row = lax.broadcasted_iota(jnp.int32, (sb, LANE), 0) // c
    st_rows = lax.broadcasted_iota(jnp.int32, (2 * GLA_DV, LANE), 0)
    st_lane = lax.broadcasted_iota(jnp.int32, (2 * GLA_DV, LANE), 1)
    st_own = (st_rows < GLA_DV) == (st_lane < GLA_DK)
    zpad = jnp.zeros((GLA_DK, GLA_DV), F32)

    def expand(x):
        return jnp.concatenate([jnp.where(chunk_of_row == ci, x, 0.0) for ci in range(GLA_SUPER)],
                               axis=1).astype(BF16)

    for p in range(pairs):
        for d in range(2):
            s0 = [s0_ref[d, 2 * p + hh] for hh in range(2)]
            st_scr[p, d] = jnp.concatenate([jnp.concatenate([s0[0], zpad], axis=0).T,
                                            jnp.concatenate([zpad, s0[1]], axis=0).T], axis=0)

    def rows_of(n, d):
        r0 = (n if d == 0 else n_super - 1 - n) * sb
        return pl.ds(r0, sb) if n_super == 1 else pl.ds(pl.multiple_of(r0, sb), sb)

    def chunk_totals(prefix):
        ends = [prefix[ci * c + c - 1:ci * c + c, :] for ci in range(GLA_SUPER)]
        return ends, jnp.concatenate([jnp.broadcast_to(e, (c, LANE)) for e in ends], axis=0)

    def log_decays(n, p):
        kl = slice(p * LANE, (p + 1) * LANE)
        g_b = g_ref[1, rows_of(n, 1), kl]
        prefix = jnp.concatenate([g_ref[0, rows_of(n, 0), kl], g_b], axis=1)
        pos = lax.broadcasted_iota(jnp.int32, prefix.shape, 0) % c
        step = 1
        while step < c:
            prefix = prefix + jnp.where(pos >= step, pltpu.roll(prefix, step, 0), 0.0)
            step *= 2
        ends_f, total_f = chunk_totals(prefix[:, :LANE])
        ends_b, total_b = chunk_totals(prefix[:, LANE:])
        return (prefix[:, :LANE], ends_f, total_f), (total_b - prefix[:, LANE:] + g_b, ends_b, total_b)

    def one_pair_direction(n, p, d, decays):
        kl = slice(p * LANE, (p + 1) * LANE)
        vl = slice(2 * p * GLA_DV, 2 * (p + 1) * GLA_DV)
        rs = rows_of(n, d)
        b, ends, total = decays
        q_in = (q_ref[rs, kl] * (GLA_DK ** -0.5)) * jnp.exp(b)
        kk = k_ref[rs, kl]
        k_in = (kk * jnp.exp(-b)).astype(BF16)
        k_end = kk * jnp.exp(total - b)
        v = v_ref[rs, vl]
        v_bf = v.astype(BF16)
        kv_all = _dot(v.T.astype(BF16), expand(k_end))
        q2 = jnp.concatenate([jnp.where(own[0], q_in, 0.0), jnp.where(own[1], q_in, 0.0)], axis=0)
        a2 = _dot_nt(q2.astype(BF16), k_in)
        o = jnp.concatenate(
            [_dot(jnp.where(keep[d], a2[hh * sb:(hh + 1) * sb], 0.0).astype(BF16),
                  v_bf[:, hh * GLA_DV:(hh + 1) * GLA_DV]) for hh in range(2)], axis=1)
        st = st_scr[p, d]
        entering = [None] * GLA_SUPER
        for ci in (range(GLA_SUPER) if d == 0 else range(GLA_SUPER - 1, -1, -1)):
            entering[ci] = st
            st = st * jnp.exp(ends[ci]) + jnp.where(st_own, kv_all[:, ci * LANE:(ci + 1) * LANE], 0.0)
        st_scr[p, d] = st
        o += _dot_nt(expand(q_in), jnp.concatenate(entering, axis=1).astype(BF16))
        if d == 0:
            of_scr[rs, vl] = o
        else:
            ob_scr[rs, vl] = o

    def super_block(n, carry):
        for p in range(pairs):
            decays = log_decays(n, p)
            for d in range(2):
                one_pair_direction(n, p, d, decays[d])
        return carry

    if n_super == 1:
        super_block(0, 0)
    else:
        lax.fori_loop(0, n_super, super_block, 0)

    for h in range(2 * pairs):
        vs = slice(h * GLA_DV, (h + 1) * GLA_DV)
        r = r_ref[:, vs]
        y = _rms(of_scr[:, vs] + ob_scr[:, vs], gg_ref[...]) * (r * jax.nn.sigmoid(r))
        y_ref[:, vs] = y.astype(y_ref.dtype)
        p, hh = divmod(h, 2)
        for d in range(2):
            sfin_ref[d, h] = st_scr[p, d, hh * GLA_DV:(hh + 1) * GLA_DV, :].T[hh * GLA_DK:(hh + 1) * GLA_DK, :]


def bidir_gla(proj, gates, s0, gla_g, n_batch, seq, row_blk0):
    shared_s0 = s0.shape[0] == 1
    pairs = HB // 2 if seq * HB * GLA_DV * 4 <= 1024 * 1024 else 2
    wqk = pairs * LANE
    wv = pairs * 2 * GLA_DV
    col_q = (2 * HA * 2 * DIFF_HD + HA * DIFF_VD) // wqk
    col_k = col_q + HB * GLA_DK // wqk
    col_v = (col_k * wqk + HB * GLA_DK) // wv
    col_r = col_v + HB * GLA_DV // wv
    n_tok = n_batch * seq
    return pl.pallas_call(
        functools.partial(_gla_kernel, seq=seq, pairs=pairs),
        grid=(n_batch, HB // 2 // pairs),
        in_specs=[pl.BlockSpec((seq, wqk), lambda b, p: (row_blk0 + b, col_q + p)),
                  pl.BlockSpec((seq, wqk), lambda b, p: (row_blk0 + b, col_k + p)),
                  pl.BlockSpec((seq, wv), lambda b, p: (row_blk0 + b, col_v + p)),
                  pl.BlockSpec((seq, wv), lambda b, p: (row_blk0 + b, col_r + p)),
                  pl.BlockSpec((2, seq, wqk), lambda b, p: (0, row_blk0 + b, p)),
                  pl.BlockSpec((None, 2, 2 * pairs, GLA_DK, GLA_DV),
                               lambda b, p: (0 if shared_s0 else b, 0, p, 0, 0)),
                  pl.BlockSpec((1, GLA_DV), lambda b, p: (0, 0))],
        out_specs=[pl.BlockSpec((seq, wv), lambda b, p: (b, p)),
                   pl.BlockSpec((None, 2, 2 * pairs, GLA_DK, GLA_DV), lambda b, p: (b, 0, p, 0, 0))],
        out_shape=[jax.ShapeDtypeStruct((n_tok, HB * GLA_DV), BF16),
                   jax.ShapeDtypeStruct((n_batch, 2, HB, GLA_DK, GLA_DV), F32)],
        scratch_shapes=[pltpu.VMEM((seq, wv), F32), pltpu.VMEM((seq, wv), F32),
                        pltpu.VMEM((pairs, 2, 2 * GLA_DV, LANE), F32)],
        compiler_params=_params(("arbitrary", "arbitrary")),
        name="bidir_gla",
    )(proj, proj, proj, proj, gates, s0, gla_g.reshape(1, GLA_DV))


def _head_slice(kh, gq):
    h = kh * Q_PER_KV + gq
    return slice(h * LANE, (h + 1) * LANE)


def _sink_column(sink_ref, kh, rows):
    return jnp.concatenate([jnp.full((rows, 1), sink_ref[kh * Q_PER_KV + gq], F32) for gq in range(Q_PER_KV)],
                           axis=0)


def _gqa_ctx_kernel(sink_ref, q_ref, k_ref, v_ref, o_ref, ko_ref, vo_ref):
    scale = HEAD_DIM ** -0.5
    n_rows = CTX_BATCHES_PER_STEP * SEQ
    for kh in range(KV_HEADS):
        ks = slice(kh * LANE, (kh + 1) * LANE)
        ko_ref[pl.ds(kh, n_rows, stride=KV_HEADS), :] = k_ref[:, ks]
        vo_ref[pl.ds(kh, n_rows, stride=KV_HEADS), :] = v_ref[:, ks]
    for bb in range(CTX_BATCHES_PER_STEP):
        rows = slice(bb * SEQ, (bb + 1) * SEQ)
        for kh in range(KV_HEADS):
            ks = slice(kh * LANE, (kh + 1) * LANE)
            k_bf = k_ref[rows, ks].astype(BF16)
            v1_bf = _with_ones(v_ref[rows, ks].astype(BF16))
            for gq in range(Q_PER_KV):
                hs = _head_slice(kh, gq)
                sink = sink_ref[kh * Q_PER_KV + gq]
                s = _dot_nt((q_ref[rows, hs] * scale).astype(BF16), k_bf)
                m = jnp.maximum(jnp.max(s, axis=-1, keepdims=True), sink)
                ov = _dot(jnp.exp(s - m).astype(BF16), v1_bf)
                o_ref[rows, hs] = (ov[:, :LANE] * (1.0 / (ov[:, LANE:] + jnp.exp(sink - m)))).astype(o_ref.dtype)


def gqa_context(proj, sinks):
    wq = N_HEADS * HEAD_DIM
    wkv = KV_HEADS * HEAD_DIM
    rows = CTX_BATCHES_PER_STEP * SEQ

    def kv(c):
        return pl.BlockSpec((rows, wkv), lambda b: (b, c))

    cache = pl.BlockSpec((rows * KV_HEADS, LANE), lambda b: (b, 0))
    return pl.pallas_call(
        _gqa_ctx_kernel,
        grid=(BATCH // CTX_BATCHES_PER_STEP,),
        in_specs=[pl.BlockSpec(memory_space=pltpu.SMEM),
                  pl.BlockSpec((rows, wq), lambda b: (b, 0)), kv(wq // wkv), kv(wq // wkv + 1)],
        out_specs=[pl.BlockSpec((rows, wq), lambda b: (b, 0)), cache, cache],
        out_shape=[jax.ShapeDtypeStruct((TP, wq), BF16), jax.ShapeDtypeStruct((TP * KV_HEADS, LANE), F32),
                   jax.ShapeDtypeStruct((TP * KV_HEADS, LANE), F32)],
        compiler_params=_params(("arbitrary",)),
        name="gqa_context",
    )(sinks, proj, proj, proj)


def _gqa_lat_kernel(sink_ref, q_ref, k_ref, v_ref, ck_ref, cv_ref, kc_ref, kl_ref, kh_ref,
                    qc_ref, ql_ref, qh_ref, o_ref, kw_scr, vw_scr, kc_scr, vc_scr):
    quarter = HEAD_DIM // 4
    w = WINDOW
    wkv = KV_HEADS * HEAD_DIM

    @pl.when(pl.program_id(1) == 0)
    def _():
        kw_scr[0:w, :] = jnp.zeros((w, wkv), BF16)
        kw_scr[w + DEC_SEQ:, :] = jnp.zeros((w, wkv), BF16)
        vw_scr[0:w, :] = jnp.zeros((w, 2 * wkv), BF16)
        vw_scr[w + DEC_SEQ:, :] = jnp.zeros((w, 2 * wkv), BF16)
        for kh in range(KV_HEADS):
            ks = slice(kh * LANE, (kh + 1) * LANE)
            vs = slice(2 * kh * LANE, 2 * (kh + 1) * LANE)
            kw_scr[w:w + DEC_SEQ, ks] = _rope(k_ref[:, ks], kc_ref[...], kl_ref[...], kh_ref[...],
                                              quarter).astype(BF16)
            vw_scr[w:w + DEC_SEQ, vs] = _with_ones(v_ref[:, ks].astype(BF16))
            head_rows = pl.ds(kh, PAST_LEN, stride=KV_HEADS)
            vc_scr[:, vs] = _with_ones(cv_ref[head_rows, :].astype(BF16))
            kc_scr[:, ks] = ck_ref[head_rows, :].astype(BF16)

    qi = lax.broadcasted_iota(jnp.int32, (Q_PER_KV * w, 3 * w), 0) % w
    kj = lax.broadcasted_iota(jnp.int32, (Q_PER_KV * w, 3 * w), 1)
    scale = HEAD_DIM ** -0.5
    for blk in range(LAT_QBLOCKS_PER_STEP):
        n = pl.program_id(1) * LAT_QBLOCKS_PER_STEP + blk
        rows = slice(blk * w, (blk + 1) * w)
        win = pl.ds(pl.multiple_of(n * w, w), 3 * w)
        kpos = n * w - w + kj
        valid = (kj >= qi) & (kj <= qi + 2 * w) & (kpos >= 0) & (kpos < DEC_SEQ)
        for kh in range(KV_HEADS):
            ks = slice(kh * LANE, (kh + 1) * LANE)
            vs = slice(2 * kh * LANE, 2 * (kh + 1) * LANE)
            q = jnp.concatenate(
                [(_rope(q_ref[rows, _head_slice(kh, gq)], qc_ref[rows, :], ql_ref[rows, :], qh_ref[rows, :],
                        quarter) * scale).astype(BF16) for gq in range(Q_PER_KV)], axis=0)
            sink = _sink_column(sink_ref, kh, w)
            s_c = _dot_nt(q, kc_scr[:, ks])
            s_w = jnp.where(valid, _dot_nt(q, kw_scr[win, ks]), NEG_INF)
            m = jnp.maximum(jnp.maximum(jnp.max(s_c, axis=-1, keepdims=True),
                                        jnp.max(s_w, axis=-1, keepdims=True)), sink)
            ov = (_dot(jnp.exp(s_c - m).astype(BF16), vc_scr[:, vs])
                  + _dot(jnp.exp(s_w - m).astype(BF16), vw_scr[win, vs]))
            o = ov[:, :LANE] * (1.0 / (ov[:, LANE:] + jnp.exp(sink - m)))
            for gq in range(Q_PER_KV):
                o_ref[rows, _head_slice(kh, gq)] = o[gq * w:(gq + 1) * w].astype(o_ref.dtype)


def gqa_latent(proj, cache_k, cache_v, tables, sinks):
    wq = N_HEADS * HEAD_DIM
    wkv = KV_HEADS * HEAD_DIM
    tq = LAT_QBLOCKS_PER_STEP * WINDOW
    nq = DEC_SEQ // tq
    row0 = TP // tq
    kv_row0 = TP // DEC_SEQ
    cos, sin_lo, sin_hi = tables
    full = pl.BlockSpec((DEC_SEQ, LANE), lambda b, i: (0, 0))
    qtab = pl.BlockSpec((tq, LANE), lambda b, i: (i, 0))
    cache = pl.BlockSpec((None, PAST_LEN * KV_HEADS, LANE), lambda b, i: (b, 0, 0))
    return pl.pallas_call(
        _gqa_lat_kernel,
        grid=(DEC_BATCH, nq),
        in_specs=[pl.BlockSpec(memory_space=pltpu.SMEM),
                  pl.BlockSpec((tq, wq), lambda b, i: (row0 + b * nq + i, 0)),
                  pl.BlockSpec((DEC_SEQ, wkv), lambda b, i: (kv_row0 + b, wq // wkv)),
                  pl.BlockSpec((DEC_SEQ, wkv), lambda b, i: (kv_row0 + b, wq // wkv + 1)),
                  cache, cache, full, full, full, qtab, qtab, qtab],
        out_specs=pl.BlockSpec((tq, wq), lambda b, i: (b * nq + i, 0)),
        out_shape=jax.ShapeDtypeStruct((TS, wq), BF16),
        scratch_shapes=[pltpu.VMEM((DEC_SEQ + 2 * WINDOW, wkv), BF16),
                        pltpu.VMEM((DEC_SEQ + 2 * WINDOW, 2 * wkv), BF16),
                        pltpu.VMEM((PAST_LEN, wkv), BF16),
                        pltpu.VMEM((PAST_LEN, 2 * wkv), BF16)],
        compiler_params=_params(("arbitrary", "arbitrary")),
        name="gqa_latent",
    )(sinks, proj, proj, proj, cache_k, cache_v, cos, sin_lo, sin_hi, cos, sin_lo, sin_hi)


def _even_layer(xc, xs, mod, j, layer, norm_g, w_in_even, lam_vec, subln_g, w_gate1, w_gate2, b_gate, gla_norm_g,
                w_out_even, cache_a_k, cache_a_v, state_b, tables):
    lam_init = 0.8 - 0.6 * math.exp(-0.3 * layer)
    ng = HB * GLA_DK
    wg1 = jnp.zeros((D_MODEL, LANE), F32).at[:, :2 * GLA_GATE_RANK].set(
        jnp.concatenate([w_gate1[j, 0], w_gate1[j, 1]], axis=-1)).astype(BF16)
    wg2 = jnp.zeros((2, LANE, ng), F32)
    for e in range(2):
        wg2 = wg2.at[e, e * GLA_GATE_RANK:(e + 1) * GLA_GATE_RANK, :].set(w_gate2[j, e])
    h, gates = norm_mod_gate(xc, xs, norm_g[layer, 0], mod, 0, 1, wg1, wg2.astype(BF16),
                             b_gate[j].reshape(2, 1, ng))
    proj = ws_matmul(h, w_in_even, j)
    ya_c, new_k, new_v = diff_attention_context(proj, lam_vec, subln_g[j], lam_init)
    ya_s = diff_attention_latent(proj, cache_a_k[:, j].reshape(DEC_BATCH, PAST_LEN * HA, 2 * DIFF_HD),
                                 cache_a_v[:, j].reshape(DEC_BATCH, PAST_LEN * HA, DIFF_VD),
                                 tables, lam_vec, subln_g[j], lam_init)
    yb_c, s_fin = bidir_gla(proj, gates, jnp.zeros((1, 2, HB, GLA_DK, GLA_DV), F32), gla_norm_g[j],
                            BATCH, SEQ, 0)
    yb_s, _ = bidir_gla(proj, gates, state_b[:, j], gla_norm_g[j], DEC_BATCH, DEC_SEQ, TP // DEC_SEQ)
    x, h_ffn = ws_matmul_residual([(ya_c, ya_s), (yb_c, yb_s)], w_out_even, j, (xc, xs), mod, 2,
                                  tm=OUT_PROJ_ROWS, next_norm=(norm_g[layer, 1], 3, 4))
    return (x, h_ffn, new_k.reshape(BATCH, SEQ, HA, 2 * DIFF_HD), new_v.reshape(BATCH, SEQ, HA, DIFF_VD), s_fin)


def _odd_layer(x, mod, j, layer, norm_g, w_in_odd, sinks, w_out_odd, cache_c_k, cache_c_v, tables):
    proj = ws_matmul(norm_mod(x, norm_g[layer, 0], mod, 0, 1), w_in_odd, j)
    o_c, new_k, new_v = gqa_context(proj, sinks[j])
    o_s = gqa_latent(proj, cache_c_k[:, j].reshape(DEC_BATCH, PAST_LEN * KV_HEADS, HEAD_DIM),
                     cache_c_v[:, j].reshape(DEC_BATCH, PAST_LEN * KV_HEADS, HEAD_DIM), tables, sinks[j])
    x, h_ffn = ws_matmul_residual([(o_c, o_s)], w_out_odd, j, (x,), mod, 2,
                                  tm=2 * OUT_PROJ_ROWS, next_norm=(norm_g[layer, 1], 3, 4))
    return (x, h_ffn, new_k.reshape(BATCH, SEQ, KV_HEADS, HEAD_DIM), new_v.reshape(BATCH, SEQ, KV_HEADS, HEAD_DIM))


def _ffn(x, h_ffn, mod, layer, w_ffn_in, w_ffn_out):
    act = ws_swiglu(h_ffn, w_ffn_in, layer)
    return ws_matmul_residual([(act,)], w_ffn_out, layer, (x,), mod, 5, tm=1024, tn=512, single_buffer_w=True)


def kernel(x_prompt, x_sample, c, cache_a_k, cache_a_v, state_b, cache_c_k, cache_c_v, c_ctx, w_ada, b_ada,
           norm_g, w_in_even, lam_q1, lam_k1, lam_q2, lam_k2, subln_g, w_gate1, w_gate2, b_gate, gla_norm_g,
           w_out_even, w_in_odd, sinks, w_out_odd, w_ffn_in, w_ffn_out, final_norm_g):
    assert DEPTH == 2, "layer 0 reads the two input streams, layer 1 the fused token matrix"
    cvec = jnp.concatenate([c_ctx[None, :], c, jnp.zeros((MOD_ROWS - 1 - DEC_BATCH, D_MODEL), F32)], axis=0)
    mods = adaln(cvec, w_ada, b_ada).reshape(DEPTH, MOD_ROWS, 1, 6 * D_MODEL)
    tab_d = rope_tables(DEC_SEQ, DIFF_HD)
    tab_h = rope_tables(DEC_SEQ, HEAD_DIM)

    lam_vec = jnp.stack([lam_q1[0], lam_k1[0], lam_q2[0], lam_k2[0]], axis=0)
    x, h_ffn, ak, av, sb = _even_layer(x_prompt.reshape(TP, D_MODEL), x_sample.reshape(TS, D_MODEL), mods[0], 0, 0,
                                       norm_g, w_in_even, lam_vec, subln_g, w_gate1, w_gate2, b_gate, gla_norm_g,
                                       w_out_even, cache_a_k, cache_a_v, state_b, tab_d)
    x = _ffn(x, h_ffn, mods[0], 0, w_ffn_in, w_ffn_out)
    x, h_ffn, ck, cv = _odd_layer(x, mods[1], 0, 1, norm_g, w_in_odd, sinks, w_out_odd, cache_c_k, cache_c_v,
                                  tab_h)
    x = _ffn(x, h_ffn, mods[1], 1, w_ffn_in, w_ffn_out)

    y_prompt = final_norm(x, final_norm_g, 0, TP).reshape(BATCH, SEQ, D_MODEL)
    y_sample = final_norm(x, final_norm_g, TP, TS).reshape(DEC_BATCH, DEC_SEQ, D_MODEL)
    return (y_prompt, y_sample, ak[:, None], av[:, None], sb[:, None], ck[:, None], cv[:, None])
```

```python
import functools
import math

import jax
import jax.numpy as jnp
from jax import lax
from jax.experimental import pallas as pl
from jax.experimental.pallas import tpu as pltpu

D_MODEL = 2048
BATCH = 32
SEQ = 256
DEPTH = 2
DEC_BATCH = 2
DEC_SEQ = 2048
PAST_LEN = 512
GRID_W = 64
HEAD_DIM = 128
N_HEADS = D_MODEL // HEAD_DIM
HA = N_HEADS // 2
DIFF_HD = HEAD_DIM // 2
DIFF_VD = HEAD_DIM
HB = N_HEADS // 2
GLA_DK = HEAD_DIM // 2
GLA_DV = HEAD_DIM
GLA_GATE_RANK = 16
GLA_TAU = 16.0
GLA_CHUNK = 64
GLA_SUPER = 4
CTX_BATCHES_PER_STEP = 4
LAT_QBLOCKS_PER_STEP = 2
DIFF_LAT_CHAINS = 8
KV_HEADS = N_HEADS // 4
Q_PER_KV = N_HEADS // KV_HEADS
WINDOW = 128
D_FF = -(-8 * D_MODEL // (3 * 256)) * 256
ROPE_BASE = 10000.0
EPS = 1e-6
NEG_INF = -1e30
EVEN_IN = 2 * HA * 2 * DIFF_HD + HA * DIFF_VD + 2 * HB * GLA_DK + 2 * HB * GLA_DV
ODD_IN = (N_HEADS + 2 * KV_HEADS) * HEAD_DIM

TP = BATCH * SEQ
TS = DEC_BATCH * DEC_SEQ
T = TP + TS
LANE = 128
MOD_ROWS = 8
NORM_ROWS = 16
NORM_UNROLL = 8
OUT_PROJ_ROWS = 256
VMEM_LIMIT = 56 * 1024 * 1024

F32 = jnp.float32
BF16 = jnp.bfloat16


def _params(semantics, vmem=VMEM_LIMIT):
    return pltpu.CompilerParams(dimension_semantics=semantics, vmem_limit_bytes=vmem)


def _dot(a, b):
    return jnp.dot(a, b, preferred_element_type=F32)


def _dot_nt(a, b):
    return lax.dot_general(a, b, (((1,), (1,)), ((), ())), preferred_element_type=F32)


def _group_of_tile(i, tm):
    r = i * tm
    return jnp.where(r < TP, 0, 1 + (r - TP) // DEC_SEQ)


def _rms(x, g):
    return (x * lax.rsqrt(jnp.mean(x * x, axis=-1, keepdims=True) + EPS)) * g


def _adaln_kernel(c_ref, w_ref, b_ref, o_ref):
    c = c_ref[...]
    s = c * jax.nn.sigmoid(c)
    o_ref[...] = _dot(s.astype(BF16), w_ref[...].astype(BF16)) + b_ref[...]


def adaln(cvec, w_ada, b_ada, tn=2048):
    n = w_ada.shape[-1]
    return pl.pallas_call(
        _adaln_kernel,
        grid=(DEPTH, n // tn),
        in_specs=[
            pl.BlockSpec((MOD_ROWS, D_MODEL), lambda l, j: (0, 0)),
            pl.BlockSpec((None, D_MODEL, tn), lambda l, j: (l, 0, j)),
            pl.BlockSpec((None, 1, tn), lambda l, j: (l, 0, j)),
        ],
        out_specs=pl.BlockSpec((None, MOD_ROWS, tn), lambda l, j: (l, 0, j)),
        out_shape=jax.ShapeDtypeStruct((DEPTH, MOD_ROWS, n), F32),
        compiler_params=_params(("arbitrary", "arbitrary")),
        name="adaln",
    )(cvec, w_ada, b_ada.reshape(DEPTH, 1, n))


def _by_row_groups(x_ref, o_ref, fn):
    def body(r, carry):
        rs = pl.ds(pl.multiple_of(r * NORM_ROWS, NORM_ROWS), NORM_ROWS)
        o_ref[rs, :] = fn(x_ref[rs, :]).astype(o_ref.dtype)
        return carry

    lax.fori_loop(0, x_ref.shape[0] // NORM_ROWS, body, 0, unroll=NORM_UNROLL)


def _norm_mod_into(h_ref, x_ref, g_ref, shift_ref, scale_ref):
    _by_row_groups(x_ref, h_ref, lambda x: _rms(x, g_ref[...]) * (1 + scale_ref[...]) + shift_ref[...])


def _norm_mod_kernel(x_ref, g_ref, shift_ref, scale_ref, h_ref):
    _norm_mod_into(h_ref, x_ref, g_ref, shift_ref, scale_ref)


def _norm_mod_gate_kernel(xc_ref, xs_ref, g_ref, shift_ref, scale_ref, wg1_ref, wg2_ref, bg_ref,
                          h_ref, gate_ref, *, ctx_tiles):
    @pl.when(pl.program_id(0) < ctx_tiles)
    def _():
        _norm_mod_into(h_ref, xc_ref, g_ref, shift_ref, scale_ref)

    @pl.when(pl.program_id(0) >= ctx_tiles)
    def _():
        _norm_mod_into(h_ref, xs_ref, g_ref, shift_ref, scale_ref)

    low = _dot(h_ref[...], wg1_ref[...]).astype(BF16)
    for e in range(2):
        logit = _dot(low, wg2_ref[e]) + bg_ref[e]
        log_sig = jnp.minimum(logit, 0.0) - jnp.log(1.0 + jnp.exp(-jnp.abs(logit)))
        gate_ref[e] = log_sig / GLA_TAU


def _mod_specs(tm, shift_idx, scale_idx):
    def spec(idx):
        return pl.BlockSpec((None, 1, D_MODEL), lambda i: (_group_of_tile(i, tm), 0, idx))
    return [pl.BlockSpec((1, D_MODEL), lambda i: (0, 0)), spec(shift_idx), spec(scale_idx)]


def norm_mod(x, g, mod, shift_idx, scale_idx, tm=1024):
    return pl.pallas_call(
        _norm_mod_kernel,
        grid=(T // tm,),
        in_specs=[pl.BlockSpec((tm, D_MODEL), lambda i: (i, 0))] + _mod_specs(tm, shift_idx, scale_idx),
        out_specs=pl.BlockSpec((tm, D_MODEL), lambda i: (i, 0)),
        out_shape=jax.ShapeDtypeStruct((T, D_MODEL), BF16),
        compiler_params=_params(("arbitrary",)),
        name="norm_mod",
    )(x, g.reshape(1, D_MODEL), mod, mod)


def norm_mod_gate(xc, xs, g, mod, shift_idx, scale_idx, wg1, wg2, bg, tm=512):
    ng = HB * GLA_DK
    ctx_tiles = TP // tm
    return pl.pallas_call(
        functools.partial(_norm_mod_gate_kernel, ctx_tiles=ctx_tiles),
        grid=(T // tm,),
        in_specs=[pl.BlockSpec((tm, D_MODEL), lambda i: (jnp.minimum(i, ctx_tiles - 1), 0)),
                  pl.BlockSpec((tm, D_MODEL), lambda i: (jnp.maximum(i - ctx_tiles, 0), 0))]
        + _mod_specs(tm, shift_idx, scale_idx)
        + [pl.BlockSpec((D_MODEL, LANE), lambda i: (0, 0)),
           pl.BlockSpec((2, LANE, ng), lambda i: (0, 0, 0)),
           pl.BlockSpec((2, 1, ng), lambda i: (0, 0, 0))],
        out_specs=[pl.BlockSpec((tm, D_MODEL), lambda i: (i, 0)),
                   pl.BlockSpec((2, tm, ng), lambda i: (0, i, 0))],
        out_shape=[jax.ShapeDtypeStruct((T, D_MODEL), BF16), jax.ShapeDtypeStruct((2, T, ng), F32)],
        compiler_params=_params(("arbitrary",)),
        name="norm_mod_gate",
    )(xc, xs, g.reshape(1, D_MODEL), mod, mod, wg1, wg2, bg)


def _cast_weights(w_refs, w_scrs):
    @pl.when(pl.program_id(1) == 0)
    def _():
        for w_ref, w_scr in zip(w_refs, w_scrs):
            w_scr[...] = w_ref[...].astype(BF16)


def _ws_plain_kernel(h_ref, w_ref, o_ref, w_scr):
    _cast_weights([w_ref], [w_scr])
    o_ref[...] = _dot(h_ref[...], w_scr[...])


def _ws_swiglu_kernel(h_ref, wg_ref, wu_ref, o_ref, wg_scr, wu_scr):
    _cast_weights([wg_ref, wu_ref], [wg_scr, wu_scr])
    h = h_ref[...]
    gate = _dot(h, wg_scr[...])
    up = _dot(h, wu_scr[...])
    o_ref[...] = (gate * jax.nn.sigmoid(gate) * up).astype(o_ref.dtype)


def _ws_res_kernel(*refs, lhs_arity, x_arity, ctx_tiles, next_norm):
    n_l = sum(lhs_arity)
    w_ref = refs[n_l]
    x_refs = refs[n_l + 1:n_l + 1 + x_arity]
    rest = refs[n_l + 1 + x_arity:]
    if next_norm:
        gate_ref, g2_ref, shift2_ref, scale2_ref, o_ref, h_ref, w_scr = rest
    else:
        gate_ref, o_ref, w_scr = rest
    groups, at = [], 0
    for a in lhs_arity:
        groups.append(refs[at:at + a])
        at += a
    _cast_weights([w_ref], [w_scr])

    def emit(side):
        acc, k0 = None, 0
        for grp in groups:
            ref = grp[side] if len(grp) == 2 else grp[0]
            part = _dot(ref[...], w_scr[k0:k0 + ref.shape[1], :])
            acc = part if acc is None else acc + part
            k0 += ref.shape[1]
        x_ref = x_refs[side] if x_arity == 2 else x_refs[0]
        x_new = x_ref[...] + gate_ref[...] * acc
        o_ref[...] = x_new
        if next_norm:
            h_ref[...] = (_rms(x_new, g2_ref[...]) * (1 + scale2_ref[...]) + shift2_ref[...]).astype(BF16)

    if x_arity == 2 or 2 in lhs_arity:
        @pl.when(pl.program_id(1) < ctx_tiles)
        def _():
            emit(0)

        @pl.when(pl.program_id(1) >= ctx_tiles)
        def _():
            emit(1)
    else:
        emit(0)


def _rows_specs(arrays, tm, width, col):
    if len(arrays) == 1:
        return [pl.BlockSpec((tm, width), lambda j, i: (i, col(j)))]
    ctx_tiles = TP // tm
    return [pl.BlockSpec((tm, width), lambda j, i: (jnp.minimum(i, ctx_tiles - 1), col(j))),
            pl.BlockSpec((tm, width), lambda j, i: (jnp.maximum(i - ctx_tiles, 0), col(j)))]


def ws_matmul(h, w_stack, layer, tm=1024, tn=1024):
    kdim, n = w_stack.shape[1:]
    return pl.pallas_call(
        _ws_plain_kernel,
        grid=(n // tn, T // tm),
        in_specs=[pl.BlockSpec((tm, kdim), lambda j, i: (i, 0)),
                  pl.BlockSpec((None, kdim, tn), lambda j, i: (layer, 0, j))],
        out_specs=pl.BlockSpec((tm, tn), lambda j, i: (i, j)),
        out_shape=jax.ShapeDtypeStruct((T, n), F32),
        scratch_shapes=[pltpu.VMEM((kdim, tn), BF16)],
        compiler_params=_params(("arbitrary", "arbitrary")),
        name="ws_matmul",
    )(h, w_stack)


def ws_swiglu(h, w_stack, layer, tm=1024, tn=512):
    kdim = w_stack.shape[1]
    nf = D_FF // tn
    return pl.pallas_call(
        _ws_swiglu_kernel,
        grid=(nf, T // tm),
        in_specs=[pl.BlockSpec((tm, kdim), lambda j, i: (i, 0)),
                  pl.BlockSpec((None, kdim, tn), lambda j, i: (layer, 0, j)),
                  pl.BlockSpec((None, kdim, tn), lambda j, i: (layer, 0, j + nf))],
        out_specs=pl.BlockSpec((tm, tn), lambda j, i: (i, j)),
        out_shape=jax.ShapeDtypeStruct((T, D_FF), BF16),
        scratch_shapes=[pltpu.VMEM((kdim, tn), BF16), pltpu.VMEM((kdim, tn), BF16)],
        compiler_params=_params(("arbitrary", "arbitrary")),
        name="ws_swiglu",
    )(h, w_stack, w_stack)


def ws_matmul_residual(lhs_groups, w_stack, layer, x, mod, gate_idx, tm=512, tn=1024, next_norm=None):
    kdim, n = w_stack.shape[1:]
    if next_norm is not None:
        tn = n
    gate_blk = n // tn
    specs, args = [], []
    for grp in lhs_groups:
        specs += _rows_specs(grp, tm, grp[0].shape[1], lambda j: 0)
        args += list(grp)
    w_mode = {} if next_norm is None else dict(pipeline_mode=pl.Buffered(1))
    specs.append(pl.BlockSpec((None, kdim, tn), lambda j, i: (layer, 0, j), **w_mode))
    specs += _rows_specs(x, tm, tn, lambda j: j)
    specs.append(pl.BlockSpec((None, 1, tn), lambda j, i: (_group_of_tile(i, tm), 0, gate_idx * gate_blk + j)))
    tail = [mod]
    out_specs = [pl.BlockSpec((tm, tn), lambda j, i: (i, j))]
    out_shape = [jax.ShapeDtypeStruct((T, n), F32)]
    if next_norm is not None:
        gain, shift_idx, scale_idx = next_norm
        specs.append(pl.BlockSpec((1, n), lambda j, i: (0, 0)))
        specs += [pl.BlockSpec((None, 1, n), lambda j, i, idx=idx: (_group_of_tile(i, tm), 0, idx))
                  for idx in (shift_idx, scale_idx)]
        tail += [gain.reshape(1, n), mod, mod]
        out_specs.append(pl.BlockSpec((tm, tn), lambda j, i: (i, j)))
        out_shape.append(jax.ShapeDtypeStruct((T, n), BF16))
    outs = pl.pallas_call(
        functools.partial(_ws_res_kernel, lhs_arity=tuple(len(grp) for grp in lhs_groups), x_arity=len(x),
                          ctx_tiles=TP // tm, next_norm=next_norm is not None),
        grid=(n // tn, T // tm),
        in_specs=specs,
        out_specs=out_specs,
        out_shape=out_shape,
        scratch_shapes=[pltpu.VMEM((kdim, tn), BF16)],
        compiler_params=_params(("arbitrary", "arbitrary")),
        name="ws_matmul_residual",
    )(*args, w_stack, *x, *tail)
    return outs[0] if next_norm is None else tuple(outs)


def _final_norm_kernel(x_ref, g_ref, o_ref):
    o_ref[...] = _rms(x_ref[...], g_ref[...])


def final_norm(x, g, row0, n_rows, tm=1024):
    blk0 = row0 // tm
    return pl.pallas_call(
        _final_norm_kernel,
        grid=(n_rows // tm,),
        in_specs=[pl.BlockSpec((tm, D_MODEL), lambda i: (blk0 + i, 0)),
                  pl.BlockSpec((1, D_MODEL), lambda i: (0, 0))],
        out_specs=pl.BlockSpec((tm, D_MODEL), lambda i: (i, 0)),
        out_shape=jax.ShapeDtypeStruct((n_rows, D_MODEL), F32),
        compiler_params=_params(("arbitrary",)),
        name="final_norm",
    )(x, g.reshape(1, D_MODEL))


def _grid_angles(n_tok, rot_dim):
    t = jnp.arange((n_tok // GRID_W) * GRID_W)
    row = (t // GRID_W).astype(F32)
    col = (t % GRID_W).astype(F32)
    half = rot_dim // 2
    inv = ROPE_BASE ** (-jnp.arange(0, half, 2, dtype=F32) / half)
    return row[:, None] * inv[None], col[:, None] * inv[None]


def rope_tables(n_tok, rot_dim):
    ang_row, ang_col = _grid_angles(n_tok, rot_dim)
    zeros = jnp.zeros_like(ang_row)
    reps = LANE // rot_dim

    def lanes(r1, r2, c1, c2):
        return jnp.tile(jnp.concatenate([r1, r2, c1, c2], axis=-1), (1, reps))

    cr, sr, cc, sc = jnp.cos(ang_row), jnp.sin(ang_row), jnp.cos(ang_col), jnp.sin(ang_col)
    return lanes(cr, cr, cc, cc), lanes(-sr, zeros, -sc, zeros), lanes(zeros, sr, zeros, sc)


def _rope(x, cos, sin_lo, sin_hi, quarter):
    return (x * cos + pltpu.roll(x, LANE - quarter, 1) * sin_lo + pltpu.roll(x, quarter, 1) * sin_hi)


def _diff_lambda(lam_ref, lam_init):
    e1 = jnp.exp(jnp.sum(lam_ref[0:1, :] * lam_ref[1:2, :], axis=-1, keepdims=True))
    e2 = jnp.exp(jnp.sum(lam_ref[2:3, :] * lam_ref[3:4, :], axis=-1, keepdims=True))
    return e1 - e2 + lam_init


def _with_ones(v_bf):
    return jnp.concatenate([v_bf, jnp.ones(v_bf.shape, BF16)], axis=1)


def _softmax_av(q_bf, k_bf, v1_bf):
    s = _dot_nt(q_bf, k_bf)
    e = jnp.exp(s - jnp.max(s, axis=-1, keepdims=True))
    if v1_bf.shape[1] == LANE:
        return _dot(e.astype(BF16), v1_bf) * (1.0 / jnp.sum(e, axis=-1, keepdims=True))
    ov = _dot(e.astype(BF16), v1_bf)
    return ov[:, :LANE] * (1.0 / ov[:, LANE:])


def _diff_head(q, k_bf, v1_bf, lam, subln_g, lam_init, stack_maps):
    n = q.shape[0]
    first = lax.broadcasted_iota(jnp.int32, q.shape, 1) < DIFF_HD
    q0 = jnp.where(first, q, 0.0).astype(BF16)
    q1 = jnp.where(first, 0.0, q).astype(BF16)
    if stack_maps:
        r = _softmax_av(jnp.concatenate([q0, q1], axis=0), k_bf, v1_bf)
        o = r[:n] - lam * r[n:]
    else:
        o = _softmax_av(q0, k_bf, v1_bf) - lam * _softmax_av(q1, k_bf, v1_bf)
    return _rms(o, subln_g) * (1.0 - lam_init)


def _diff_ctx_kernel(q_ref, k_ref, v_ref, lam_ref, sg_ref, o_ref, ko_ref, vo_ref, *, lam_init):
    lam = _diff_lambda(lam_ref, lam_init)
    scale = DIFF_HD ** -0.5
    n_rows = CTX_BATCHES_PER_STEP * SEQ
    for h in range(HA):
        sl = slice(h * LANE, (h + 1) * LANE)
        ko_ref[pl.ds(h, n_rows, stride=HA), :] = k_ref[:, sl]
        vo_ref[pl.ds(h, n_rows, stride=HA), :] = v_ref[:, sl]
    for bb in range(CTX_BATCHES_PER_STEP):
        rows = slice(bb * SEQ, (bb + 1) * SEQ)
        for h in range(HA):
            sl = slice(h * LANE, (h + 1) * LANE)
            y = _diff_head(q_ref[rows, sl] * scale, k_ref[rows, sl].astype(BF16),
                           _with_ones(v_ref[rows, sl].astype(BF16)), lam, sg_ref[...], lam_init, stack_maps=True)
            o_ref[rows, sl] = y.astype(o_ref.dtype)


def diff_attention_context(proj, lam_vec, subln_g, lam_init):
    w = HA * LANE

    def blk(c):
        return pl.BlockSpec((CTX_BATCHES_PER_STEP * SEQ, w), lambda b: (b, c))

    cache = pl.BlockSpec((CTX_BATCHES_PER_STEP * SEQ * HA, LANE), lambda b: (b, 0))
    return pl.pallas_call(
        functools.partial(_diff_ctx_kernel, lam_init=lam_init),
        grid=(BATCH // CTX_BATCHES_PER_STEP,),
        in_specs=[blk(0), blk(1), blk(2),
                  pl.BlockSpec((4, DIFF_HD), lambda b: (0, 0)),
                  pl.BlockSpec((1, DIFF_VD), lambda b: (0, 0))],
        out_specs=[blk(0), cache, cache],
        out_shape=[jax.ShapeDtypeStruct((TP, w), BF16), jax.ShapeDtypeStruct((TP * HA, LANE), F32),
                   jax.ShapeDtypeStruct((TP * HA, LANE), F32)],
        compiler_params=_params(("arbitrary",)),
        name="diff_attention_context",
    )(proj, proj, proj, lam_vec, subln_g.reshape(1, DIFF_VD))


def _diff_lat_kernel(q_ref, k_ref, v_ref, ck_ref, cv_ref, kc_ref, kl_ref, kh_ref, qc_ref, ql_ref, qh_ref,
                     lam_ref, sg_ref, o_ref, k_scr, v_scr, *, lam_init):
    quarter = DIFF_HD // 4

    @pl.when(pl.program_id(2) == 0)
    def _():
        head_rows = pl.ds(pl.program_id(1), PAST_LEN, stride=HA)
        k_scr[0:PAST_LEN, :] = ck_ref[head_rows, :].astype(BF16)
        k_scr[PAST_LEN:, :] = _rope(k_ref[...], kc_ref[...], kl_ref[...], kh_ref[...], quarter).astype(BF16)
        v_scr[0:PAST_LEN, :] = cv_ref[head_rows, :].astype(BF16)
        v_scr[PAST_LEN:, :] = v_ref[...].astype(BF16)

    lam = _diff_lambda(lam_ref, lam_init)
    sub = q_ref.shape[0] // DIFF_LAT_CHAINS
    for c in range(DIFF_LAT_CHAINS):
        rs = slice(c * sub, (c + 1) * sub)
        q = _rope(q_ref[rs, :], qc_ref[rs, :], ql_ref[rs, :], qh_ref[rs, :], quarter) * (DIFF_HD ** -0.5)
        y = _diff_head(q, k_scr[...], v_scr[...], lam, sg_ref[...], lam_init, stack_maps=False)
        o_ref[rs, :] = y.astype(o_ref.dtype)


def diff_attention_latent(proj, cache_k, cache_v, tables, lam_vec, subln_g, lam_init, tq=2048):
    nq = DEC_SEQ // tq
    row0 = TP // tq
    kv_row0 = TP // DEC_SEQ
    cos, sin_lo, sin_hi = tables
    full = pl.BlockSpec((DEC_SEQ, LANE), lambda b, h, i: (0, 0))
    qtab = pl.BlockSpec((tq, LANE), lambda b, h, i: (i, 0))
    cache = pl.BlockSpec((None, PAST_LEN * HA, LANE), lambda b, h, i: (b, 0, 0))
    return pl.pallas_call(
        functools.partial(_diff_lat_kernel, lam_init=lam_init),
        grid=(DEC_BATCH, HA, nq),
        in_specs=[pl.BlockSpec((tq, LANE), lambda b, h, i: (row0 + b * nq + i, h)),
                  pl.BlockSpec((DEC_SEQ, LANE), lambda b, h, i: (kv_row0 + b, HA + h)),
                  pl.BlockSpec((DEC_SEQ, LANE), lambda b, h, i: (kv_row0 + b, 2 * HA + h)),
                  cache, cache, full, full, full, qtab, qtab, qtab,
                  pl.BlockSpec((4, DIFF_HD), lambda b, h, i: (0, 0)),
                  pl.BlockSpec((1, DIFF_VD), lambda b, h, i: (0, 0))],
        out_specs=pl.BlockSpec((tq, LANE), lambda b, h, i: (b * nq + i, h)),
        out_shape=jax.ShapeDtypeStruct((TS, HA * LANE), BF16),
        scratch_shapes=[pltpu.VMEM((PAST_LEN + DEC_SEQ, LANE), BF16),
                        pltpu.VMEM((PAST_LEN + DEC_SEQ, LANE), BF16)],
        compiler_params=_params(("arbitrary", "arbitrary", "arbitrary")),
        name="diff_attention_latent",
    )(proj, proj, proj, cache_k, cache_v, cos, sin_lo, sin_hi, cos, sin_lo, sin_hi,
      lam_vec, subln_g.reshape(1, DIFF_VD))


def _gla_kernel(q_ref, k_ref, v_ref, r_ref, g_ref, s0_ref, gg_ref, y_ref, sfin_ref, of_scr, ob_scr, st_scr,
                *, seq, pairs):
    c = GLA_CHUNK
    sb = GLA_SUPER * c
    n_super = seq // sb
    rows = lax.broadcasted_iota(jnp.int32, (sb, sb), 0)
    cols = lax.broadcasted_iota(jnp.int32, (sb, sb), 1)
    same_chunk = (rows // c) == (cols // c)
    keep = (same_chunk & (rows >= cols), same_chunk & (cols >= rows))
    lane = lax.broadcasted_iota(jnp.int32, (sb, LANE), 1)
    own = (lane < GLA_DK, lane >= GLA_DK)
    chunk_of_row = lax.broadcasted_iota(jnp.int32, (sb, LANE), 0) // c
    st_rows = lax.broadcasted_iota(jnp.int32, (2 * GLA_DV, LANE), 0)
    st_lane = lax.broadcasted_iota(jnp.int32, (2 * GLA_DV, LANE), 1)
    st_own = (st_rows < GLA_DV) == (st_lane < GLA_DK)
    zpad = jnp.zeros((GLA_DK, GLA_DV), F32)

    def expand(x):
        return jnp.concatenate([jnp.where(chunk_of_row == ci, x, 0.0) for ci in range(GLA_SUPER)],
                               axis=1).astype(BF16)

    for p in range(pairs):
        for d in range(2):
            s0 = [s0_ref[d, 2 * p + hh] for hh in range(2)]
            st_scr[p, d] = jnp.concatenate([jnp.concatenate([s0[0], zpad], axis=0).T,
                                            jnp.concatenate([zpad, s0[1]], axis=0).T], axis=0)

    def rows_of(n, d):
        r0 = (n if d == 0 else n_super - 1 - n) * sb
        return pl.ds(r0, sb) if n_super == 1 else pl.ds(pl.multiple_of(r0, sb), sb)

    def chunk_totals(prefix):
        ends = [prefix[ci * c + c - 1:ci * c + c, :] for ci in range(GLA_SUPER)]
        return ends, jnp.concatenate([jnp.broadcast_to(e, (c, LANE)) for e in ends], axis=0)

    def log_decays(n, p):
        kl = slice(p * LANE, (p + 1) * LANE)
        g_b = g_ref[1, rows_of(n, 1), kl]
        prefix = jnp.concatenate([g_ref[0, rows_of(n, 0), kl], g_b], axis=1)
        pos = lax.broadcasted_iota(jnp.int32, prefix.shape, 0) % c
        step = 1
        while step < c:
            prefix = prefix + jnp.where(pos >= step, pltpu.roll(prefix, step, 0), 0.0)
            step *= 2
        ends_f, total_f = chunk_totals(prefix[:, :LANE])
        ends_b, total_b = chunk_totals(prefix[:, LANE:])
        return (prefix[:, :LANE], ends_f, total_f), (total_b - prefix[:, LANE:] + g_b, ends_b, total_b)

    def one_pair_direction(n, p, d, decays):
        kl = slice(p * LANE, (p + 1) * LANE)
        vl = slice(2 * p * GLA_DV, 2 * (p + 1) * GLA_DV)
        rs = rows_of(n, d)
        b, ends, total = decays
        q_in = (q_ref[rs, kl] * (GLA_DK ** -0.5)) * jnp.exp(b)
        kk = k_ref[rs, kl]
        k_in = (kk * jnp.exp(-b)).astype(BF16)
        k_end = kk * jnp.exp(total - b)
        v = v_ref[rs, vl]
        v_bf = v.astype(BF16)
        kv_all = _dot(v.T.astype(BF16), expand(k_end))
        q2 = jnp.concatenate([jnp.where(own[0], q_in, 0.0), jnp.where(own[1], q_in, 0.0)], axis=0)
        a2 = _dot_nt(q2.astype(BF16), k_in)
        o = jnp.concatenate(
            [_dot(jnp.where(keep[d], a2[hh * sb:(hh + 1) * sb], 0.0).astype(BF16),
                  v_bf[:, hh * GLA_DV:(hh + 1) * GLA_DV]) for hh in range(2)], axis=1)
        st = st_scr[p, d]
        entering = [None] * GLA_SUPER
        for ci in (range(GLA_SUPER) if d == 0 else range(GLA_SUPER - 1, -1, -1)):
            entering[ci] = st
            st = st * jnp.exp(ends[ci]) + jnp.where(st_own, kv_all[:, ci * LANE:(ci + 1) * LANE], 0.0)
        st_scr[p, d] = st
        o += _dot_nt(expand(q_in), jnp.concatenate(entering, axis=1).astype(BF16))
        if d == 0:
            of_scr[rs, vl] = o
        else:
            ob_scr[rs, vl] = o

    def super_block(n, carry):
        for p in range(pairs):
            decays = log_decays(n, p)
            for d in range(2):
                one_pair_direction(n, p, d, decays[d])
        return carry

    if n_super == 1:
        super_block(0, 0)
    else:
        lax.fori_loop(0, n_super, super_block, 0)

    for h in range(2 * pairs):
        vs = slice(h * GLA_DV, (h + 1) * GLA_DV)
        r = r_ref[:, vs]
        y = _rms(of_scr[:, vs] + ob_scr[:, vs], gg_ref[...]) * (r * jax.nn.sigmoid(r))
        y_ref[:, vs] = y.astype(y_ref.dtype)
        p, hh = divmod(h, 2)
        for d in range(2):
            sfin_ref[d, h] = st_scr[p, d, hh * GLA_DV:(hh + 1) * GLA_DV, :].T[hh * GLA_DK:(hh + 1) * GLA_DK, :]


def bidir_gla(proj, gates, s0, gla_g, n_batch, seq, row_blk0):
    shared_s0 = s0.shape[0] == 1
    pairs = HB // 2 if seq * HB * GLA_DV * 4 <= 1024 * 1024 else 2
    wqk = pairs * LANE
    wv = pairs * 2 * GLA_DV
    col_q = (2 * HA * 2 * DIFF_HD + HA * DIFF_VD) // wqk
    col_k = col_q + HB * GLA_DK // wqk
    col_v = (col_k * wqk + HB * GLA_DK) // wv
    col_r = col_v + HB * GLA_DV // wv
    n_tok = n_batch * seq
    return pl.pallas_call(
        functools.partial(_gla_kernel, seq=seq, pairs=pairs),
        grid=(n_batch, HB // 2 // pairs),
        in_specs=[pl.BlockSpec((seq, wqk), lambda b, p: (row_blk0 + b, col_q + p)),
                  pl.BlockSpec((seq, wqk), lambda b, p: (row_blk0 + b, col_k + p)),
                  pl.BlockSpec((seq, wv), lambda b, p: (row_blk0 + b, col_v + p)),
                  pl.BlockSpec((seq, wv), lambda b, p: (row_blk0 + b, col_r + p)),
                  pl.BlockSpec((2, seq, wqk), lambda b, p: (0, row_blk0 + b, p)),
                  pl.BlockSpec((None, 2, 2 * pairs, GLA_DK, GLA_DV),
                               lambda b, p: (0 if shared_s0 else b, 0, p, 0, 0)),
                  pl.BlockSpec((1, GLA_DV), lambda b, p: (0, 0))],
        out_specs=[pl.BlockSpec((seq, wv), lambda b, p: (b, p)),
                   pl.BlockSpec((None, 2, 2 * pairs, GLA_DK, GLA_DV), lambda b, p: (b, 0, p, 0, 0))],
        out_shape=[jax.ShapeDtypeStruct((n_tok, HB * GLA_DV), BF16),
                   jax.ShapeDtypeStruct((n_batch, 2, HB, GLA_DK, GLA_DV), F32)],
        scratch_shapes=[pltpu.VMEM((seq, wv), F32), pltpu.VMEM((seq, wv), F32),
                        pltpu.VMEM((pairs, 2, 2 * GLA_DV, LANE), F32)],
        compiler_params=_params(("arbitrary", "arbitrary")),
        name="bidir_gla",
    )(proj, proj, proj, proj, gates, s0, gla_g.reshape(1, GLA_DV))


def _head_slice(kh, gq):
    h = kh * Q_PER_KV + gq
    return slice(h * LANE, (h + 1) * LANE)


def _sink_column(sink_ref, kh, rows):
    return jnp.concatenate([jnp.full((rows, 1), sink_ref[kh * Q_PER_KV + gq], F32) for gq in range(Q_PER_KV)],
                           axis=0)


def _gqa_ctx_kernel(sink_ref, q_ref, k_ref, v_ref, o_ref, ko_ref, vo_ref):
    scale = HEAD_DIM ** -0.5
    n_rows = CTX_BATCHES_PER_STEP * SEQ
    for kh in range(KV_HEADS):
        ks = slice(kh * LANE, (kh + 1) * LANE)
        ko_ref[pl.ds(kh, n_rows, stride=KV_HEADS), :] = k_ref[:, ks]
        vo_ref[pl.ds(kh, n_rows, stride=KV_HEADS), :] = v_ref[:, ks]
    for bb in range(CTX_BATCHES_PER_STEP):
        rows = slice(bb * SEQ, (bb + 1) * SEQ)
        for kh in range(KV_HEADS):
            ks = slice(kh * LANE, (kh + 1) * LANE)
            k_bf = k_ref[rows, ks].astype(BF16)
            v1_bf = _with_ones(v_ref[rows, ks].astype(BF16))
            for gq in range(Q_PER_KV):
                hs = _head_slice(kh, gq)
                sink = sink_ref[kh * Q_PER_KV + gq]
                s = _dot_nt((q_ref[rows, hs] * scale).astype(BF16), k_bf)
                m = jnp.maximum(jnp.max(s, axis=-1, keepdims=True), sink)
                ov = _dot(jnp.exp(s - m).astype(BF16), v1_bf)
                o_ref[rows, hs] = (ov[:, :LANE] * (1.0 / (ov[:, LANE:] + jnp.exp(sink - m)))).astype(o_ref.dtype)


def gqa_context(proj, sinks):
    wq = N_HEADS * HEAD_DIM
    wkv = KV_HEADS * HEAD_DIM
    rows = CTX_BATCHES_PER_STEP * SEQ

    def kv(c):
        return pl.BlockSpec((rows, wkv), lambda b: (b, c))

    cache = pl.BlockSpec((rows * KV_HEADS, LANE), lambda b: (b, 0))
    return pl.pallas_call(
        _gqa_ctx_kernel,
        grid=(BATCH // CTX_BATCHES_PER_STEP,),
        in_specs=[pl.BlockSpec(memory_space=pltpu.SMEM),
                  pl.BlockSpec((rows, wq), lambda b: (b, 0)), kv(wq // wkv), kv(wq // wkv + 1)],
        out_specs=[pl.BlockSpec((rows, wq), lambda b: (b, 0)), cache, cache],
        out_shape=[jax.ShapeDtypeStruct((TP, wq), BF16), jax.ShapeDtypeStruct((TP * KV_HEADS, LANE), F32),
                   jax.ShapeDtypeStruct((TP * KV_HEADS, LANE), F32)],
        compiler_params=_params(("arbitrary",)),
        name="gqa_context",
    )(sinks, proj, proj, proj)


def _gqa_lat_kernel(sink_ref, q_ref, k_ref, v_ref, ck_ref, cv_ref, kc_ref, kl_ref, kh_ref,
                    qc_ref, ql_ref, qh_ref, o_ref, kw_scr, vw_scr, kc_scr, vc_scr):
    quarter = HEAD_DIM // 4
    w = WINDOW
    wkv = KV_HEADS * HEAD_DIM

    @pl.when(pl.program_id(1) == 0)
    def _():
        kw_scr[0:w, :] = jnp.zeros((w, wkv), BF16)
        kw_scr[w + DEC_SEQ:, :] = jnp.zeros((w, wkv), BF16)
        vw_scr[0:w, :] = jnp.zeros((w, 2 * wkv), BF16)
        vw_scr[w + DEC_SEQ:, :] = jnp.zeros((w, 2 * wkv), BF16)
        for kh in range(KV_HEADS):
            ks = slice(kh * LANE, (kh + 1) * LANE)
            vs = slice(2 * kh * LANE, 2 * (kh + 1) * LANE)
            kw_scr[w:w + DEC_SEQ, ks] = _rope(k_ref[:, ks], kc_ref[...], kl_ref[...], kh_ref[...],
                                              quarter).astype(BF16)
            vw_scr[w:w + DEC_SEQ, vs] = _with_ones(v_ref[:, ks].astype(BF16))
            head_rows = pl.ds(kh, PAST_LEN, stride=KV_HEADS)
            vc_scr[:, vs] = _with_ones(cv_ref[head_rows, :].astype(BF16))
            kc_scr[:, ks] = ck_ref[head_rows, :].astype(BF16)

    qi = lax.broadcasted_iota(jnp.int32, (Q_PER_KV * w, 3 * w), 0) % w
    kj = lax.broadcasted_iota(jnp.int32, (Q_PER_KV * w, 3 * w), 1)
    scale = HEAD_DIM ** -0.5
    for blk in range(LAT_QBLOCKS_PER_STEP):
        n = pl.program_id(1) * LAT_QBLOCKS_PER_STEP + blk
        rows = slice(blk * w, (blk + 1) * w)
        win = pl.ds(pl.multiple_of(n * w, w), 3 * w)
        kpos = n * w - w + kj
        valid = (kj >= qi) & (kj <= qi + 2 * w) & (kpos >= 0) & (kpos < DEC_SEQ)
        for kh in range(KV_HEADS):
            ks = slice(kh * LANE, (kh + 1) * LANE)
            vs = slice(2 * kh * LANE, 2 * (kh + 1) * LANE)
            q = jnp.concatenate(
                [(_rope(q_ref[rows, _head_slice(kh, gq)], qc_ref[rows, :], ql_ref[rows, :], qh_ref[rows, :],
                        quarter) * scale).astype(BF16) for gq in range(Q_PER_KV)], axis=0)
            sink = _sink_column(sink_ref, kh, w)
            s_c = _dot_nt(q, kc_scr[:, ks])
            s_w = jnp.where(valid, _dot_nt(q, kw_scr[win, ks]), NEG_INF)
            m = jnp.maximum(jnp.maximum(jnp.max(s_c, axis=-1, keepdims=True),
                                        jnp.max(s_w, axis=-1, keepdims=True)), sink)
            ov = (_dot(jnp.exp(s_c - m).astype(BF16), vc_scr[:, vs])
                  + _dot(jnp.exp(s_w - m).astype(BF16), vw_scr[win, vs]))
            o = ov[:, :LANE] * (1.0 / (ov[:, LANE:] + jnp.exp(sink - m)))
            for gq in range(Q_PER_KV):
                o_ref[rows, _head_slice(kh, gq)] = o[gq * w:(gq + 1) * w].astype(o_ref.dtype)


def gqa_latent(proj, cache_k, cache_v, tables, sinks):
    wq = N_HEADS * HEAD_DIM
    wkv = KV_HEADS * HEAD_DIM
    tq = LAT_QBLOCKS_PER_STEP * WINDOW
    nq = DEC_SEQ // tq
    row0 = TP // tq
    kv_row0 = TP // DEC_SEQ
    cos, sin_lo, sin_hi = tables
    full = pl.BlockSpec((DEC_SEQ, LANE), lambda b, i: (0, 0))
    qtab = pl.BlockSpec((tq, LANE), lambda b, i: (i, 0))
    cache = pl.BlockSpec((None, PAST_LEN * KV_HEADS, LANE), lambda b, i: (b, 0, 0))
    return pl.pallas_call(
        _gqa_lat_kernel,
        grid=(DEC_BATCH, nq),
        in_specs=[pl.BlockSpec(memory_space=pltpu.SMEM),
                  pl.BlockSpec((tq, wq), lambda b, i: (row0 + b * nq + i, 0)),
                  pl.BlockSpec((DEC_SEQ, wkv), lambda b, i: (kv_row0 + b, wq // wkv)),
                  pl.BlockSpec((DEC_SEQ, wkv), lambda b, i: (kv_row0 + b, wq // wkv + 1)),
                  cache, cache, full, full, full, qtab, qtab, qtab],
        out_specs=pl.BlockSpec((tq, wq), lambda b, i: (b * nq + i, 0)),
        out_shape=jax.ShapeDtypeStruct((TS, wq), BF16),
        scratch_shapes=[pltpu.VMEM((DEC_SEQ + 2 * WINDOW, wkv), BF16),
                        pltpu.VMEM((DEC_SEQ + 2 * WINDOW, 2 * wkv), BF16),
                        pltpu.VMEM((PAST_LEN, wkv), BF16),
                        pltpu.VMEM((PAST_LEN, 2 * wkv), BF16)],
        compiler_params=_params(("arbitrary", "arbitrary")),
        name="gqa_latent",
    )(sinks, proj, proj, proj, cache_k, cache_v, cos, sin_lo, sin_hi, cos, sin_lo, sin_hi)


def _even_layer(xc, xs, mod, j, layer, norm_g, w_in_even, lam_vec, subln_g, w_gate1, w_gate2, b_gate, gla_norm_g,
                w_out_even, cache_a_k, cache_a_v, state_b, tables):
    lam_init = 0.8 - 0.6 * math.exp(-0.3 * layer)
    ng = HB * GLA_DK
    wg1 = jnp.zeros((D_MODEL, LANE), F32).at[:, :2 * GLA_GATE_RANK].set(
        jnp.concatenate([w_gate1[j, 0], w_gate1[j, 1]], axis=-1)).astype(BF16)
    wg2 = jnp.zeros((2, LANE, ng), F32)
    for e in range(2):
        wg2 = wg2.at[e, e * GLA_GATE_RANK:(e + 1) * GLA_GATE_RANK, :].set(w_gate2[j, e])
    h, gates = norm_mod_gate(xc, xs, norm_g[layer, 0], mod, 0, 1, wg1, wg2.astype(BF16),
                             b_gate[j].reshape(2, 1, ng))
    proj = ws_matmul(h, w_in_even, j)
    ya_c, new_k, new_v = diff_attention_context(proj, lam_vec, subln_g[j], lam_init)
    ya_s = diff_attention_latent(proj, cache_a_k[:, j].reshape(DEC_BATCH, PAST_LEN * HA, 2 * DIFF_HD),
                                 cache_a_v[:, j].reshape(DEC_BATCH, PAST_LEN * HA, DIFF_VD),
                                 tables, lam_vec, subln_g[j], lam_init)
    yb_c, s_fin = bidir_gla(proj, gates, jnp.zeros((1, 2, HB, GLA_DK, GLA_DV), F32), gla_norm_g[j],
                            BATCH, SEQ, 0)
    yb_s, _ = bidir_gla(proj, gates, state_b[:, j], gla_norm_g[j], DEC_BATCH, DEC_SEQ, TP // DEC_SEQ)
    x, h_ffn = ws_matmul_residual([(ya_c, ya_s), (yb_c, yb_s)], w_out_even, j, (xc, xs), mod, 2,
                                  tm=OUT_PROJ_ROWS, next_norm=(norm_g[layer, 1], 3, 4))
    return (x, h_ffn, new_k.reshape(BATCH, SEQ, HA, 2 * DIFF_HD), new_v.reshape(BATCH, SEQ, HA, DIFF_VD), s_fin)


def _odd_layer(x, mod, j, layer, norm_g, w_in_odd, sinks, w_out_odd, cache_c_k, cache_c_v, tables):
    proj = ws_matmul(norm_mod(x, norm_g[layer, 0], mod, 0, 1), w_in_odd, j)
    o_c, new_k, new_v = gqa_context(proj, sinks[j])
    o_s = gqa_latent(proj, cache_c_k[:, j].reshape(DEC_BATCH, PAST_LEN * KV_HEADS, HEAD_DIM),
                     cache_c_v[:, j].reshape(DEC_BATCH, PAST_LEN * KV_HEADS, HEAD_DIM), tables, sinks[j])
    x, h_ffn = ws_matmul_residual([(o_c, o_s)], w_out_odd, j, (x,), mod, 2,
                                  tm=2 * OUT_PROJ_ROWS, next_norm=(norm_g[layer, 1], 3, 4))
    return (x, h_ffn, new_k.reshape(BATCH, SEQ, KV_HEADS, HEAD_DIM), new_v.reshape(BATCH, SEQ, KV_HEADS, HEAD_DIM))


def _ffn(x, h_ffn, mod, layer, w_ffn_in, w_ffn_out):
    act = ws_swiglu(h_ffn, w_ffn_in, layer)
    return ws_matmul_residual([(act,)], w_ffn_out, layer, (x,), mod, 5, tm=512, tn=512)


def kernel(x_prompt, x_sample, c, cache_a_k, cache_a_v, state_b, cache_c_k, cache_c_v, c_ctx, w_ada, b_ada,
           norm_g, w_in_even, lam_q1, lam_k1, lam_q2, lam_k2, subln_g, w_gate1, w_gate2, b_gate, gla_norm_g,
           w_out_even, w_in_odd, sinks, w_out_odd, w_ffn_in, w_ffn_out, final_norm_g):
    assert DEPTH == 2, "layer 0 reads the two input streams, layer 1 the fused token matrix"
    cvec = jnp.concatenate([c_ctx[None, :], c, jnp.zeros((MOD_ROWS - 1 - DEC_BATCH, D_MODEL), F32)], axis=0)
    mods = adaln(cvec, w_ada, b_ada).reshape(DEPTH, MOD_ROWS, 1, 6 * D_MODEL)
    tab_d = rope_tables(DEC_SEQ, DIFF_HD)
    tab_h = rope_tables(DEC_SEQ, HEAD_DIM)

    lam_vec = jnp.stack([lam_q1[0], lam_k1[0], lam_q2[0], lam_k2[0]], axis=0)
    x, h_ffn, ak, av, sb = _even_layer(x_prompt.reshape(TP, D_MODEL), x_sample.reshape(TS, D_MODEL), mods[0], 0, 0,
                                       norm_g, w_in_even, lam_vec, subln_g, w_gate1, w_gate2, b_gate, gla_norm_g,
                                       w_out_even, cache_a_k, cache_a_v, state_b, tab_d)
    x = _ffn(x, h_ffn, mods[0], 0, w_ffn_in, w_ffn_out)
    x, h_ffn, ck, cv = _odd_layer(x, mods[1], 0, 1, norm_g, w_in_odd, sinks, w_out_odd, cache_c_k, cache_c_v,
                                  tab_h)
    x = _ffn(x, h_ffn, mods[1], 1, w_ffn_in, w_ffn_out)

    y_prompt = final_norm(x, final_norm_g, 0, TP).reshape(BATCH, SEQ, D_MODEL)
    y_sample = final_norm(x, final_norm_g, TP, TS).reshape(DEC_BATCH, DEC_SEQ, D_MODEL)
    return (y_prompt, y_sample, ak[:, None], av[:, None], sb[:, None], ck[:, None], cv[:, None])
```

```python
import functools
import math

import jax
import jax.numpy as jnp
from jax import lax
from jax.experimental import pallas as pl
from jax.experimental.pallas import tpu as pltpu

D_MODEL = 2048
BATCH = 32
SEQ = 256
DEPTH = 2
DEC_BATCH = 2
DEC_SEQ = 2048
PAST_LEN = 512
GRID_W = 64
HEAD_DIM = 128
N_HEADS = D_MODEL // HEAD_DIM
HA = N_HEADS // 2
DIFF_HD = HEAD_DIM // 2
DIFF_VD = HEAD_DIM
HB = N_HEADS // 2
GLA_DK = HEAD_DIM // 2
GLA_DV = HEAD_DIM
GLA_GATE_RANK = 16
GLA_TAU = 16.0
GLA_CHUNK = 64
GLA_SUPER = 4
CTX_BATCHES_PER_STEP = 4
LAT_QBLOCKS_PER_STEP = 2
DIFF_LAT_CHAINS = 8
KV_HEADS = N_HEADS // 4
Q_PER_KV = N_HEADS // KV_HEADS
WINDOW = 128
D_FF = -(-8 * D_MODEL // (3 * 256)) * 256
ROPE_BASE = 10000.0
EPS = 1e-6
NEG_INF = -1e30
EVEN_IN = 2 * HA * 2 * DIFF_HD + HA * DIFF_VD + 2 * HB * GLA_DK + 2 * HB * GLA_DV
ODD_IN = (N_HEADS + 2 * KV_HEADS) * HEAD_DIM

TP = BATCH * SEQ
TS = DEC_BATCH * DEC_SEQ
T = TP + TS
LANE = 128
MOD_ROWS = 8
NORM_ROWS = 16
NORM_UNROLL = 8
FFN_W_CHUNK = 512
OUT_PROJ_ROWS = 256
VMEM_LIMIT = 56 * 1024 * 1024

F32 = jnp.float32
BF16 = jnp.bfloat16


def _params(semantics, vmem=VMEM_LIMIT):
    return pltpu.CompilerParams(dimension_semantics=semantics, vmem_limit_bytes=vmem)


def _dot(a, b):
    return jnp.dot(a, b, preferred_element_type=F32)


def _dot_nt(a, b):
    return lax.dot_general(a, b, (((1,), (1,)), ((), ())), preferred_element_type=F32)


def _group_of_tile(i, tm):
    r = i * tm
    return jnp.where(r < TP, 0, 1 + (r - TP) // DEC_SEQ)


def _rms(x, g):
    return (x * lax.rsqrt(jnp.mean(x * x, axis=-1, keepdims=True) + EPS)) * g


def _adaln_kernel(c_ref, w_ref, b_ref, o_ref):
    c = c_ref[...]
    s = c * jax.nn.sigmoid(c)
    o_ref[...] = _dot(s.astype(BF16), w_ref[...].astype(BF16)) + b_ref[...]


def adaln(cvec, w_ada, b_ada, tn=2048):
    n = w_ada.shape[-1]
    return pl.pallas_call(
        _adaln_kernel,
        grid=(DEPTH, n // tn),
        in_specs=[
            pl.BlockSpec((MOD_ROWS, D_MODEL), lambda l, j: (0, 0)),
            pl.BlockSpec((None, D_MODEL, tn), lambda l, j: (l, 0, j)),
            pl.BlockSpec((None, 1, tn), lambda l, j: (l, 0, j)),
        ],
        out_specs=pl.BlockSpec((None, MOD_ROWS, tn), lambda l, j: (l, 0, j)),
        out_shape=jax.ShapeDtypeStruct((DEPTH, MOD_ROWS, n), F32),
        compiler_params=_params(("arbitrary", "arbitrary")),
        name="adaln",
    )(cvec, w_ada, b_ada.reshape(DEPTH, 1, n))


def _by_row_groups(x_ref, o_ref, fn):
    def body(r, carry):
        rs = pl.ds(pl.multiple_of(r * NORM_ROWS, NORM_ROWS), NORM_ROWS)
        o_ref[rs, :] = fn(x_ref[rs, :]).astype(o_ref.dtype)
        return carry

    lax.fori_loop(0, x_ref.shape[0] // NORM_ROWS, body, 0, unroll=NORM_UNROLL)


def _norm_mod_into(h_ref, x_ref, g_ref, shift_ref, scale_ref):
    _by_row_groups(x_ref, h_ref, lambda x: _rms(x, g_ref[...]) * (1 + scale_ref[...]) + shift_ref[...])


def _norm_mod_kernel(x_ref, g_ref, shift_ref, scale_ref, h_ref):
    _norm_mod_into(h_ref, x_ref, g_ref, shift_ref, scale_ref)


def _norm_mod_gate_kernel(xc_ref, xs_ref, g_ref, shift_ref, scale_ref, wg1_ref, wg2_ref, bg_ref,
                          h_ref, gate_ref, *, ctx_tiles):
    @pl.when(pl.program_id(0) < ctx_tiles)
    def _():
        _norm_mod_into(h_ref, xc_ref, g_ref, shift_ref, scale_ref)

    @pl.when(pl.program_id(0) >= ctx_tiles)
    def _():
        _norm_mod_into(h_ref, xs_ref, g_ref, shift_ref, scale_ref)

    low = _dot(h_ref[...], wg1_ref[...]).astype(BF16)
    for e in range(2):
        logit = _dot(low, wg2_ref[e]) + bg_ref[e]
        log_sig = jnp.minimum(logit, 0.0) - jnp.log(1.0 + jnp.exp(-jnp.abs(logit)))
        gate_ref[e] = log_sig / GLA_TAU


def _mod_specs(tm, shift_idx, scale_idx):
    def spec(idx):
        return pl.BlockSpec((None, 1, D_MODEL), lambda i: (_group_of_tile(i, tm), 0, idx))
    return [pl.BlockSpec((1, D_MODEL), lambda i: (0, 0)), spec(shift_idx), spec(scale_idx)]


def norm_mod(x, g, mod, shift_idx, scale_idx, tm=1024):
    return pl.pallas_call(
        _norm_mod_kernel,
        grid=(T // tm,),
        in_specs=[pl.BlockSpec((tm, D_MODEL), lambda i: (i, 0))] + _mod_specs(tm, shift_idx, scale_idx),
        out_specs=pl.BlockSpec((tm, D_MODEL), lambda i: (i, 0)),
        out_shape=jax.ShapeDtypeStruct((T, D_MODEL), BF16),
        compiler_params=_params(("arbitrary",)),
        name="norm_mod",
    )(x, g.reshape(1, D_MODEL), mod, mod)


def norm_mod_gate(xc, xs, g, mod, shift_idx, scale_idx, wg1, wg2, bg, tm=512):
    ng = HB * GLA_DK
    ctx_tiles = TP // tm
    return pl.pallas_call(
        functools.partial(_norm_mod_gate_kernel, ctx_tiles=ctx_tiles),
        grid=(T // tm,),
        in_specs=[pl.BlockSpec((tm, D_MODEL), lambda i: (jnp.minimum(i, ctx_tiles - 1), 0)),
                  pl.BlockSpec((tm, D_MODEL), lambda i: (jnp.maximum(i - ctx_tiles, 0), 0))]
        + _mod_specs(tm, shift_idx, scale_idx)
        + [pl.BlockSpec((D_MODEL, LANE), lambda i: (0, 0)),
           pl.BlockSpec((2, LANE, ng), lambda i: (0, 0, 0)),
           pl.BlockSpec((2, 1, ng), lambda i: (0, 0, 0))],
        out_specs=[pl.BlockSpec((tm, D_MODEL), lambda i: (i, 0)),
                   pl.BlockSpec((2, tm, ng), lambda i: (0, i, 0))],
        out_shape=[jax.ShapeDtypeStruct((T, D_MODEL), BF16), jax.ShapeDtypeStruct((2, T, ng), F32)],
        compiler_params=_params(("arbitrary",)),
        name="norm_mod_gate",
    )(xc, xs, g.reshape(1, D_MODEL), mod, mod, wg1, wg2, bg)


def _cast_weights(w_refs, w_scrs):
    @pl.when(pl.program_id(1) == 0)
    def _():
        for w_ref, w_scr in zip(w_refs, w_scrs):
            w_scr[...] = w_ref[...].astype(BF16)


def _ws_plain_kernel(h_ref, w_ref, o_ref, w_scr):
    _cast_weights([w_ref], [w_scr])
    o_ref[...] = _dot(h_ref[...], w_scr[...])


def _ws_swiglu_kernel(h_ref, wg_ref, wu_ref, o_ref, wg_scr, wu_scr):
    _cast_weights([wg_ref, wu_ref], [wg_scr, wu_scr])
    h = h_ref[...]
    gate = _dot(h, wg_scr[...])
    up = _dot(h, wu_scr[...])
    o_ref[...] = (gate * jax.nn.sigmoid(gate) * up).astype(o_ref.dtype)


def _ws_res_kernel(*refs, lhs_arity, x_arity, ctx_tiles, next_norm):
    n_l = sum(lhs_arity)
    w_ref = refs[n_l]
    x_refs = refs[n_l + 1:n_l + 1 + x_arity]
    rest = refs[n_l + 1 + x_arity:]
    if next_norm:
        gate_ref, g2_ref, shift2_ref, scale2_ref, o_ref, h_ref, w_scr = rest
    else:
        gate_ref, o_ref, w_scr = rest
    groups, at = [], 0
    for a in lhs_arity:
        groups.append(refs[at:at + a])
        at += a
    _cast_weights([w_ref], [w_scr])

    def emit(side):
        acc, k0 = None, 0
        for grp in groups:
            ref = grp[side] if len(grp) == 2 else grp[0]
            part = _dot(ref[...], w_scr[k0:k0 + ref.shape[1], :])
            acc = part if acc is None else acc + part
            k0 += ref.shape[1]
        x_ref = x_refs[side] if x_arity == 2 else x_refs[0]
        x_new = x_ref[...] + gate_ref[...] * acc
        o_ref[...] = x_new
        if next_norm:
            h_ref[...] = (_rms(x_new, g2_ref[...]) * (1 + scale2_ref[...]) + shift2_ref[...]).astype(BF16)

    if x_arity == 2 or 2 in lhs_arity:
        @pl.when(pl.program_id(1) < ctx_tiles)
        def _():
            emit(0)

        @pl.when(pl.program_id(1) >= ctx_tiles)
        def _():
            emit(1)
    else:
        emit(0)


def _rows_specs(arrays, tm, width, col):
    if len(arrays) == 1:
        return [pl.BlockSpec((tm, width), lambda j, i: (i, col(j)))]
    ctx_tiles = TP // tm
    return [pl.BlockSpec((tm, width), lambda j, i: (jnp.minimum(i, ctx_tiles - 1), col(j))),
            pl.BlockSpec((tm, width), lambda j, i: (jnp.maximum(i - ctx_tiles, 0), col(j)))]


def ws_matmul(h, w_stack, layer, tm=1024, tn=1024):
    kdim, n = w_stack.shape[1:]
    return pl.pallas_call(
        _ws_plain_kernel,
        grid=(n // tn, T // tm),
        in_specs=[pl.BlockSpec((tm, kdim), lambda j, i: (i, 0)),
                  pl.BlockSpec((None, kdim, tn), lambda j, i: (layer, 0, j))],
        out_specs=pl.BlockSpec((tm, tn), lambda j, i: (i, j)),
        out_shape=jax.ShapeDtypeStruct((T, n), F32),
        scratch_shapes=[pltpu.VMEM((kdim, tn), BF16)],
        compiler_params=_params(("arbitrary", "arbitrary")),
        name="ws_matmul",
    )(h, w_stack)


def ws_swiglu(h, w_stack, layer, tm=1024, tn=512):
    kdim = w_stack.shape[1]
    nf = D_FF // tn
    return pl.pallas_call(
        _ws_swiglu_kernel,
        grid=(nf, T // tm),
        in_specs=[pl.BlockSpec((tm, kdim), lambda j, i: (i, 0)),
                  pl.BlockSpec((None, kdim, tn), lambda j, i: (layer, 0, j)),
                  pl.BlockSpec((None, kdim, tn), lambda j, i: (layer, 0, j + nf))],
        out_specs=pl.BlockSpec((tm, tn), lambda j, i: (i, j)),
        out_shape=jax.ShapeDtypeStruct((T, D_FF), BF16),
        scratch_shapes=[pltpu.VMEM((kdim, tn), BF16), pltpu.VMEM((kdim, tn), BF16)],
        compiler_params=_params(("arbitrary", "arbitrary")),
        name="ws_swiglu",
    )(h, w_stack, w_stack)


def ws_matmul_residual(lhs_groups, w_stack, layer, x, mod, gate_idx, tm=512, tn=1024, next_norm=None):
    kdim, n = w_stack.shape[1:]
    if next_norm is not None:
        tn = n
    gate_blk = n // tn
    specs, args = [], []
    for grp in lhs_groups:
        specs += _rows_specs(grp, tm, grp[0].shape[1], lambda j: 0)
        args += list(grp)
    w_mode = {} if next_norm is None else dict(pipeline_mode=pl.Buffered(1))
    specs.append(pl.BlockSpec((None, kdim, tn), lambda j, i: (layer, 0, j), **w_mode))
    specs += _rows_specs(x, tm, tn, lambda j: j)
    specs.append(pl.BlockSpec((None, 1, tn), lambda j, i: (_group_of_tile(i, tm), 0, gate_idx * gate_blk + j)))
    tail = [mod]
    out_specs = [pl.BlockSpec((tm, tn), lambda j, i: (i, j))]
    out_shape = [jax.ShapeDtypeStruct((T, n), F32)]
    if next_norm is not None:
        gain, shift_idx, scale_idx = next_norm
        specs.append(pl.BlockSpec((1, n), lambda j, i: (0, 0)))
        specs += [pl.BlockSpec((None, 1, n), lambda j, i, idx=idx: (_group_of_tile(i, tm), 0, idx))
                  for idx in (shift_idx, scale_idx)]
        tail += [gain.reshape(1, n), mod, mod]
        out_specs.append(pl.BlockSpec((tm, tn), lambda j, i: (i, j)))
        out_shape.append(jax.ShapeDtypeStruct((T, n), BF16))
    outs = pl.pallas_call(
        functools.partial(_ws_res_kernel, lhs_arity=tuple(len(grp) for grp in lhs_groups), x_arity=len(x),
                          ctx_tiles=TP // tm, next_norm=next_norm is not None),
        grid=(n // tn, T // tm),
        in_specs=specs,
        out_specs=out_specs,
        out_shape=out_shape,
        scratch_shapes=[pltpu.VMEM((kdim, tn), BF16)],
        compiler_params=_params(("arbitrary", "arbitrary")),
        name="ws_matmul_residual",
    )(*args, w_stack, *x, *tail)
    return outs[0] if next_norm is None else tuple(outs)


def _ffn_out_kernel(act_ref, w_hbm, x_ref, gate_ref, o_ref, w_scr, stage, sem, *, layer, tn):
    @pl.when(pl.program_id(1) == 0)
    def _():
        n_chunks = w_scr.shape[0] // FFN_W_CHUNK
        col0 = pl.multiple_of(pl.program_id(0) * tn, tn)

        def chunk_copy(c):
            return pltpu.make_async_copy(w_hbm.at[layer, pl.ds(c * FFN_W_CHUNK, FFN_W_CHUNK), pl.ds(col0, tn)],
                                         stage.at[c % 2], sem.at[c % 2])

        chunk_copy(0).start()
        chunk_copy(1).start()
        for c in range(n_chunks):
            chunk_copy(c).wait()
            w_scr[c * FFN_W_CHUNK:(c + 1) * FFN_W_CHUNK, :] = stage[c % 2].astype(BF16)
            if c + 2 < n_chunks:
                chunk_copy(c + 2).start()

    o_ref[...] = x_ref[...] + gate_ref[...] * _dot(act_ref[...], w_scr[...])


def ffn_out(act, w_stack, layer, x, mod, gate_idx, tm=512, tn=1024):
    kdim, n = w_stack.shape[1:]
    gate_blk = n // tn
    return pl.pallas_call(
        functools.partial(_ffn_out_kernel, layer=layer, tn=tn),
        grid=(n // tn, T // tm),
        in_specs=[pl.BlockSpec((tm, kdim), lambda j, i: (i, 0)),
                  pl.BlockSpec(memory_space=pl.ANY),
                  pl.BlockSpec((tm, tn), lambda j, i: (i, j)),
                  pl.BlockSpec((None, 1, tn), lambda j, i: (_group_of_tile(i, tm), 0, gate_idx * gate_blk + j))],
        out_specs=pl.BlockSpec((tm, tn), lambda j, i: (i, j)),
        out_shape=jax.ShapeDtypeStruct((T, n), F32),
        scratch_shapes=[pltpu.VMEM((kdim, tn), BF16), pltpu.VMEM((2, FFN_W_CHUNK, tn), F32),
                        pltpu.SemaphoreType.DMA((2,))],
        compiler_params=_params(("arbitrary", "arbitrary")),
        name="ffn_out",
    )(act, w_stack, x, mod)


def _final_norm_kernel(x_ref, g_ref, o_ref):
    o_ref[...] = _rms(x_ref[...], g_ref[...])


def final_norm(x, g, row0, n_rows, tm=1024):
    blk0 = row0 // tm
    return pl.pallas_call(
        _final_norm_kernel,
        grid=(n_rows // tm,),
        in_specs=[pl.BlockSpec((tm, D_MODEL), lambda i: (blk0 + i, 0)),
                  pl.BlockSpec((1, D_MODEL), lambda i: (0, 0))],
        out_specs=pl.BlockSpec((tm, D_MODEL), lambda i: (i, 0)),
        out_shape=jax.ShapeDtypeStruct((n_rows, D_MODEL), F32),
        compiler_params=_params(("arbitrary",)),
        name="final_norm",
    )(x, g.reshape(1, D_MODEL))


def _grid_angles(n_tok, rot_dim):
    t = jnp.arange((n_tok // GRID_W) * GRID_W)
    row = (t // GRID_W).astype(F32)
    col = (t % GRID_W).astype(F32)
    half = rot_dim // 2
    inv = ROPE_BASE ** (-jnp.arange(0, half, 2, dtype=F32) / half)
    return row[:, None] * inv[None], col[:, None] * inv[None]


def rope_tables(n_tok, rot_dim):
    ang_row, ang_col = _grid_angles(n_tok, rot_dim)
    zeros = jnp.zeros_like(ang_row)
    reps = LANE // rot_dim

    def lanes(r1, r2, c1, c2):
        return jnp.tile(jnp.concatenate([r1, r2, c1, c2], axis=-1), (1, reps))

    cr, sr, cc, sc = jnp.cos(ang_row), jnp.sin(ang_row), jnp.cos(ang_col), jnp.sin(ang_col)
    return lanes(cr, cr, cc, cc), lanes(-sr, zeros, -sc, zeros), lanes(zeros, sr, zeros, sc)


def _rope(x, cos, sin_lo, sin_hi, quarter):
    return (x * cos + pltpu.roll(x, LANE - quarter, 1) * sin_lo + pltpu.roll(x, quarter, 1) * sin_hi)


def _diff_lambda(lam_ref, lam_init):
    e1 = jnp.exp(jnp.sum(lam_ref[0:1, :] * lam_ref[1:2, :], axis=-1, keepdims=True))
    e2 = jnp.exp(jnp.sum(lam_ref[2:3, :] * lam_ref[3:4, :], axis=-1, keepdims=True))
    return e1 - e2 + lam_init


def _with_ones(v_bf):
    return jnp.concatenate([v_bf, jnp.ones(v_bf.shape, BF16)], axis=1)


def _softmax_av(q_bf, k_bf, v1_bf):
    s = _dot_nt(q_bf, k_bf)
    e = jnp.exp(s - jnp.max(s, axis=-1, keepdims=True))
    if v1_bf.shape[1] == LANE:
        return _dot(e.astype(BF16), v1_bf) * (1.0 / jnp.sum(e, axis=-1, keepdims=True))
    ov = _dot(e.astype(BF16), v1_bf)
    return ov[:, :LANE] * (1.0 / ov[:, LANE:])


def _diff_head(q, k_bf, v1_bf, lam, subln_g, lam_init, stack_maps):
    n = q.shape[0]
    first = lax.broadcasted_iota(jnp.int32, q.shape, 1) < DIFF_HD
    q0 = jnp.where(first, q, 0.0).astype(BF16)
    q1 = jnp.where(first, 0.0, q).astype(BF16)
    if stack_maps:
        r = _softmax_av(jnp.concatenate([q0, q1], axis=0), k_bf, v1_bf)
        o = r[:n] - lam * r[n:]
    else:
        o = _softmax_av(q0, k_bf, v1_bf) - lam * _softmax_av(q1, k_bf, v1_bf)
    return _rms(o, subln_g) * (1.0 - lam_init)


def _diff_ctx_kernel(q_ref, k_ref, v_ref, lam_ref, sg_ref, o_ref, ko_ref, vo_ref, *, lam_init):
    lam = _diff_lambda(lam_ref, lam_init)
    scale = DIFF_HD ** -0.5
    n_rows = CTX_BATCHES_PER_STEP * SEQ
    for h in range(HA):
        sl = slice(h * LANE, (h + 1) * LANE)
        ko_ref[pl.ds(h, n_rows, stride=HA), :] = k_ref[:, sl]
        vo_ref[pl.ds(h, n_rows, stride=HA), :] = v_ref[:, sl]
    for bb in range(CTX_BATCHES_PER_STEP):
        rows = slice(bb * SEQ, (bb + 1) * SEQ)
        for h in range(HA):
            sl = slice(h * LANE, (h + 1) * LANE)
            y = _diff_head(q_ref[rows, sl] * scale, k_ref[rows, sl].astype(BF16),
                           _with_ones(v_ref[rows, sl].astype(BF16)), lam, sg_ref[...], lam_init, stack_maps=True)
            o_ref[rows, sl] = y.astype(o_ref.dtype)


def diff_attention_context(proj, lam_vec, subln_g, lam_init):
    w = HA * LANE

    def blk(c):
        return pl.BlockSpec((CTX_BATCHES_PER_STEP * SEQ, w), lambda b: (b, c))

    cache = pl.BlockSpec((CTX_BATCHES_PER_STEP * SEQ * HA, LANE), lambda b: (b, 0))
    return pl.pallas_call(
        functools.partial(_diff_ctx_kernel, lam_init=lam_init),
        grid=(BATCH // CTX_BATCHES_PER_STEP,),
        in_specs=[blk(0), blk(1), blk(2),
                  pl.BlockSpec((4, DIFF_HD), lambda b: (0, 0)),
                  pl.BlockSpec((1, DIFF_VD), lambda b: (0, 0))],
        out_specs=[blk(0), cache, cache],
        out_shape=[jax.ShapeDtypeStruct((TP, w), BF16), jax.ShapeDtypeStruct((TP * HA, LANE), F32),
                   jax.ShapeDtypeStruct((TP * HA, LANE), F32)],
        compiler_params=_params(("arbitrary",)),
        name="diff_attention_context",
    )(proj, proj, proj, lam_vec, subln_g.reshape(1, DIFF_VD))


def _diff_lat_kernel(q_ref, k_ref, v_ref, ck_ref, cv_ref, kc_ref, kl_ref, kh_ref, qc_ref, ql_ref, qh_ref,
                     lam_ref, sg_ref, o_ref, k_scr, v_scr, *, lam_init):
    quarter = DIFF_HD // 4

    @pl.when(pl.program_id(2) == 0)
    def _():
        head_rows = pl.ds(pl.program_id(1), PAST_LEN, stride=HA)
        k_scr[0:PAST_LEN, :] = ck_ref[head_rows, :].astype(BF16)
        k_scr[PAST_LEN:, :] = _rope(k_ref[...], kc_ref[...], kl_ref[...], kh_ref[...], quarter).astype(BF16)
        v_scr[0:PAST_LEN, :] = cv_ref[head_rows, :].astype(BF16)
        v_scr[PAST_LEN:, :] = v_ref[...].astype(BF16)

    lam = _diff_lambda(lam_ref, lam_init)
    sub = q_ref.shape[0] // DIFF_LAT_CHAINS
    for c in range(DIFF_LAT_CHAINS):
        rs = slice(c * sub, (c + 1) * sub)
        q = _rope(q_ref[rs, :], qc_ref[rs, :], ql_ref[rs, :], qh_ref[rs, :], quarter) * (DIFF_HD ** -0.5)
        y = _diff_head(q, k_scr[...], v_scr[...], lam, sg_ref[...], lam_init, stack_maps=False)
        o_ref[rs, :] = y.astype(o_ref.dtype)


def diff_attention_latent(proj, cache_k, cache_v, tables, lam_vec, subln_g, lam_init, tq=2048):
    nq = DEC_SEQ // tq
    row0 = TP // tq
    kv_row0 = TP // DEC_SEQ
    cos, sin_lo, sin_hi = tables
    full = pl.BlockSpec((DEC_SEQ, LANE), lambda b, h, i: (0, 0))
    qtab = pl.BlockSpec((tq, LANE), lambda b, h, i: (i, 0))
    cache = pl.BlockSpec((None, PAST_LEN * HA, LANE), lambda b, h, i: (b, 0, 0))
    return pl.pallas_call(
        functools.partial(_diff_lat_kernel, lam_init=lam_init),
        grid=(DEC_BATCH, HA, nq),
        in_specs=[pl.BlockSpec((tq, LANE), lambda b, h, i: (row0 + b * nq + i, h)),
                  pl.BlockSpec((DEC_SEQ, LANE), lambda b, h, i: (kv_row0 + b, HA + h)),
                  pl.BlockSpec((DEC_SEQ, LANE), lambda b, h, i: (kv_row0 + b, 2 * HA + h)),
                  cache, cache, full, full, full, qtab, qtab, qtab,
                  pl.BlockSpec((4, DIFF_HD), lambda b, h, i: (0, 0)),
                  pl.BlockSpec((1, DIFF_VD), lambda b, h, i: (0, 0))],
        out_specs=pl.BlockSpec((tq, LANE), lambda b, h, i: (b * nq + i, h)),
        out_shape=jax.ShapeDtypeStruct((TS, HA * LANE), BF16),
        scratch_shapes=[pltpu.VMEM((PAST_LEN + DEC_SEQ, LANE), BF16),
                        pltpu.VMEM((PAST_LEN + DEC_SEQ, LANE), BF16)],
        compiler_params=_params(("arbitrary", "arbitrary", "arbitrary")),
        name="diff_attention_latent",
    )(proj, proj, proj, cache_k, cache_v, cos, sin_lo, sin_hi, cos, sin_lo, sin_hi,
      lam_vec, subln_g.reshape(1, DIFF_VD))


def _gla_kernel(q_ref, k_ref, v_ref, r_ref, g_ref, s0_ref, gg_ref, y_ref, sfin_ref, of_scr, ob_scr, st_scr,
                *, seq, pairs):
    c = GLA_CHUNK
    sb = GLA_SUPER * c
    n_super = seq // sb
    rows = lax.broadcasted_iota(jnp.int32, (sb, sb), 0)
    cols = lax.broadcasted_iota(jnp.int32, (sb, sb), 1)
    same_chunk = (rows // c) == (cols // c)
    keep = (same_chunk & (rows >= cols), same_chunk & (cols >= rows))
    lane = lax.broadcasted_iota(jnp.int32, (sb, LANE), 1)
    own = (lane < GLA_DK, lane >= GLA_DK)
    chunk_of_row = lax.broadcasted_iota(jnp.int32, (sb, LANE), 0) // c
    st_rows = lax.broadcasted_iota(jnp.int32, (2 * GLA_DV, LANE), 0)
    st_lane = lax.broadcasted_iota(jnp.int32, (2 * GLA_DV, LANE), 1)
    st_own = (st_rows < GLA_DV) == (st_lane < GLA_DK)
    zpad = jnp.zeros((GLA_DK, GLA_DV), F32)

    def expand(x):
        return jnp.concatenate([jnp.where(chunk_of_row == ci, x, 0.0) for ci in range(GLA_SUPER)],
                               axis=1).astype(BF16)

    for p in range(pairs):
        for d in range(2):
            s0 = [s0_ref[d, 2 * p + hh] for hh in range(2)]
            st_scr[p, d] = jnp.concatenate([jnp.concatenate([s0[0], zpad], axis=0).T,
                                            jnp.concatenate([zpad, s0[1]], axis=0).T], axis=0)

    def rows_of(n, d):
        r0 = (n if d == 0 else n_super - 1 - n) * sb
        return pl.ds(r0, sb) if n_super == 1 else pl.ds(pl.multiple_of(r0, sb), sb)

    def chunk_totals(prefix):
        ends = [prefix[ci * c + c - 1:ci * c + c, :] for ci in range(GLA_SUPER)]
        return ends, jnp.concatenate([jnp.broadcast_to(e, (c, LANE)) for e in ends], axis=0)

    def log_decays(n, p):
        kl = slice(p * LANE, (p + 1) * LANE)
        g_b = g_ref[1, rows_of(n, 1), kl]
        prefix = jnp.concatenate([g_ref[0, rows_of(n, 0), kl], g_b], axis=1)
        pos = lax.broadcasted_iota(jnp.int32, prefix.shape, 0) % c
        step = 1
        while step < c:
            prefix = prefix + jnp.where(pos >= step, pltpu.roll(prefix, step, 0), 0.0)
            step *= 2
        ends_f, total_f = chunk_totals(prefix[:, :LANE])
        ends_b, total_b = chunk_totals(prefix[:, LANE:])
        return (prefix[:, :LANE], ends_f, total_f), (total_b - prefix[:, LANE:] + g_b, ends_b, total_b)

    def one_pair_direction(n, p, d, decays):
        kl = slice(p * LANE, (p + 1) * LANE)
        vl = slice(2 * p * GLA_DV, 2 * (p + 1) * GLA_DV)
        rs = rows_of(n, d)
        b, ends, total = decays
        q_in = (q_ref[rs, kl] * (GLA_DK ** -0.5)) * jnp.exp(b)
        kk = k_ref[rs, kl]
        k_in = (kk * jnp.exp(-b)).astype(BF16)
        k_end = kk * jnp.exp(total - b)
        v = v_ref[rs, vl]
        v_bf = v.astype(BF16)
        kv_all = _dot(v.T.astype(BF16), expand(k_end))
        q2 = jnp.concatenate([jnp.where(own[0], q_in, 0.0), jnp.where(own[1], q_in, 0.0)], axis=0)
        a2 = _dot_nt(q2.astype(BF16), k_in)
        o = jnp.concatenate(
            [_dot(jnp.where(keep[d], a2[hh * sb:(hh + 1) * sb], 0.0).astype(BF16),
                  v_bf[:, hh * GLA_DV:(hh + 1) * GLA_DV]) for hh in range(2)], axis=1)
        st = st_scr[p, d]
        entering = [None] * GLA_SUPER
        for ci in (range(GLA_SUPER) if d == 0 else range(GLA_SUPER - 1, -1, -1)):
            entering[ci] = st
            st = st * jnp.exp(ends[ci]) + jnp.where(st_own, kv_all[:, ci * LANE:(ci + 1) * LANE], 0.0)
        st_scr[p, d] = st
        o += _dot_nt(expand(q_in), jnp.concatenate(entering, axis=1).astype(BF16))
        if d == 0:
            of_scr[rs, vl] = o
        else:
            ob_scr[rs, vl] = o

    def super_block(n, carry):
        for p in range(pairs):
            decays = log_decays(n, p)
            for d in range(2):
                one_pair_direction(n, p, d, decays[d])
        return carry

    if n_super == 1:
        super_block(0, 0)
    else:
        lax.fori_loop(0, n_super, super_block, 0)

    for h in range(2 * pairs):
        vs = slice(h * GLA_DV, (h + 1) * GLA_DV)
        r = r_ref[:, vs]
        y = _rms(of_scr[:, vs] + ob_scr[:, vs], gg_ref[...]) * (r * jax.nn.sigmoid(r))
        y_ref[:, vs] = y.astype(y_ref.dtype)
        p, hh = divmod(h, 2)
        for d in range(2):
            sfin_ref[d, h] = st_scr[p, d, hh * GLA_DV:(hh + 1) * GLA_DV, :].T[hh * GLA_DK:(hh + 1) * GLA_DK, :]


def bidir_gla(proj, gates, s0, gla_g, n_batch, seq, row_blk0):
    shared_s0 = s0.shape[0] == 1
    pairs = HB // 2 if seq * HB * GLA_DV * 4 <= 1024 * 1024 else 2
    wqk = pairs * LANE
    wv = pairs * 2 * GLA_DV
    col_q = (2 * HA * 2 * DIFF_HD + HA * DIFF_VD) // wqk
    col_k = col_q + HB * GLA_DK // wqk
    col_v = (col_k * wqk + HB * GLA_DK) // wv
    col_r = col_v + HB * GLA_DV // wv
    n_tok = n_batch * seq
    return pl.pallas_call(
        functools.partial(_gla_kernel, seq=seq, pairs=pairs),
        grid=(n_batch, HB // 2 // pairs),
        in_specs=[pl.BlockSpec((seq, wqk), lambda b, p: (row_blk0 + b, col_q + p)),
                  pl.BlockSpec((seq, wqk), lambda b, p: (row_blk0 + b, col_k + p)),
                  pl.BlockSpec((seq, wv), lambda b, p: (row_blk0 + b, col_v + p)),
                  pl.BlockSpec((seq, wv), lambda b, p: (row_blk0 + b, col_r + p)),
                  pl.BlockSpec((2, seq, wqk), lambda b, p: (0, row_blk0 + b, p)),
                  pl.BlockSpec((None, 2, 2 * pairs, GLA_DK, GLA_DV),
                               lambda b, p: (0 if shared_s0 else b, 0, p, 0, 0)),
                  pl.BlockSpec((1, GLA_DV), lambda b, p: (0, 0))],
        out_specs=[pl.BlockSpec((seq, wv), lambda b, p: (b, p)),
                   pl.BlockSpec((None, 2, 2 * pairs, GLA_DK, GLA_DV), lambda b, p: (b, 0, p, 0, 0))],
        out_shape=[jax.ShapeDtypeStruct((n_tok, HB * GLA_DV), BF16),
                   jax.ShapeDtypeStruct((n_batch, 2, HB, GLA_DK, GLA_DV), F32)],
        scratch_shapes=[pltpu.VMEM((seq, wv), F32), pltpu.VMEM((seq, wv), F32),
                        pltpu.VMEM((pairs, 2, 2 * GLA_DV, LANE), F32)],
        compiler_params=_params(("arbitrary", "arbitrary")),
        name="bidir_gla",
    )(proj, proj, proj, proj, gates, s0, gla_g.reshape(1, GLA_DV))


def _head_slice(kh, gq):
    h = kh * Q_PER_KV + gq
    return slice(h * LANE, (h + 1) * LANE)


def _sink_column(sink_ref, kh, rows):
    return jnp.concatenate([jnp.full((rows, 1), sink_ref[kh * Q_PER_KV + gq], F32) for gq in range(Q_PER_KV)],
                           axis=0)


def _gqa_ctx_kernel(sink_ref, q_ref, k_ref, v_ref, o_ref, ko_ref, vo_ref):
    scale = HEAD_DIM ** -0.5
    n_rows = CTX_BATCHES_PER_STEP * SEQ
    for kh in range(KV_HEADS):
        ks = slice(kh * LANE, (kh + 1) * LANE)
        ko_ref[pl.ds(kh, n_rows, stride=KV_HEADS), :] = k_ref[:, ks]
        vo_ref[pl.ds(kh, n_rows, stride=KV_HEADS), :] = v_ref[:, ks]
    for bb in range(CTX_BATCHES_PER_STEP):
        rows = slice(bb * SEQ, (bb + 1) * SEQ)
        for kh in range(KV_HEADS):
            ks = slice(kh * LANE, (kh + 1) * LANE)
            k_bf = k_ref[rows, ks].astype(BF16)
            v1_bf = _with_ones(v_ref[rows, ks].astype(BF16))
            for gq in range(Q_PER_KV):
                hs = _head_slice(kh, gq)
                sink = sink_ref[kh * Q_PER_KV + gq]
                s = _dot_nt((q_ref[rows, hs] * scale).astype(BF16), k_bf)
                m = jnp.maximum(jnp.max(s, axis=-1, keepdims=True), sink)
                ov = _dot(jnp.exp(s - m).astype(BF16), v1_bf)
                o_ref[rows, hs] = (ov[:, :LANE] * (1.0 / (ov[:, LANE:] + jnp.exp(sink - m)))).astype(o_ref.dtype)


def gqa_context(proj, sinks):
    wq = N_HEADS * HEAD_DIM
    wkv = KV_HEADS * HEAD_DIM
    rows = CTX_BATCHES_PER_STEP * SEQ

    def kv(c):
        return pl.BlockSpec((rows, wkv), lambda b: (b, c))

    cache = pl.BlockSpec((rows * KV_HEADS, LANE), lambda b: (b, 0))
    return pl.pallas_call(
        _gqa_ctx_kernel,
        grid=(BATCH // CTX_BATCHES_PER_STEP,),
        in_specs=[pl.BlockSpec(memory_space=pltpu.SMEM),
                  pl.BlockSpec((rows, wq), lambda b: (b, 0)), kv(wq // wkv), kv(wq // wkv + 1)],
        out_specs=[pl.BlockSpec((rows, wq), lambda b: (b, 0)), cache, cache],
        out_shape=[jax.ShapeDtypeStruct((TP, wq), BF16), jax.ShapeDtypeStruct((TP * KV_HEADS, LANE), F32),
                   jax.ShapeDtypeStruct((TP * KV_HEADS, LANE), F32)],
        compiler_params=_params(("arbitrary",)),
        name="gqa_context",
    )(sinks, proj, proj, proj)


def _gqa_lat_kernel(sink_ref, q_ref, k_ref, v_ref, ck_ref, cv_ref, kc_ref, kl_ref, kh_ref,
                    qc_ref, ql_ref, qh_ref, o_ref, kw_scr, vw_scr, kc_scr, vc_scr):
    quarter = HEAD_DIM // 4
    w = WINDOW
    wkv = KV_HEADS * HEAD_DIM

    @pl.when(pl.program_id(1) == 0)
    def _():
        kw_scr[0:w, :] = jnp.zeros((w, wkv), BF16)
        kw_scr[w + DEC_SEQ:, :] = jnp.zeros((w, wkv), BF16)
        vw_scr[0:w, :] = jnp.zeros((w, 2 * wkv), BF16)
        vw_scr[w + DEC_SEQ:, :] = jnp.zeros((w, 2 * wkv), BF16)
        for kh in range(KV_HEADS):
            ks = slice(kh * LANE, (kh + 1) * LANE)
            vs = slice(2 * kh * LANE, 2 * (kh + 1) * LANE)
            kw_scr[w:w + DEC_SEQ, ks] = _rope(k_ref[:, ks], kc_ref[...], kl_ref[...], kh_ref[...],
                                              quarter).astype(BF16)
            vw_scr[w:w + DEC_SEQ, vs] = _with_ones(v_ref[:, ks].astype(BF16))
            head_rows = pl.ds(kh, PAST_LEN, stride=KV_HEADS)
            vc_scr[:, vs] = _with_ones(cv_ref[head_rows, :].astype(BF16))
            kc_scr[:, ks] = ck_ref[head_rows, :].astype(BF16)

    qi = lax.broadcasted_iota(jnp.int32, (Q_PER_KV * w, 3 * w), 0) % w
    kj = lax.broadcasted_iota(jnp.int32, (Q_PER_KV * w, 3 * w), 1)
    scale = HEAD_DIM ** -0.5
    for blk in range(LAT_QBLOCKS_PER_STEP):
        n = pl.program_id(1) * LAT_QBLOCKS_PER_STEP + blk
        rows = slice(blk * w, (blk + 1) * w)
        win = pl.ds(pl.multiple_of(n * w, w), 3 * w)
        kpos = n * w - w + kj
        valid = (kj >= qi) & (kj <= qi + 2 * w) & (kpos >= 0) & (kpos < DEC_SEQ)
        for kh in range(KV_HEADS):
            ks = slice(kh * LANE, (kh + 1) * LANE)
            vs = slice(2 * kh * LANE, 2 * (kh + 1) * LANE)
            q = jnp.concatenate(
                [(_rope(q_ref[rows, _head_slice(kh, gq)], qc_ref[rows, :], ql_ref[rows, :], qh_ref[rows, :],
                        quarter) * scale).astype(BF16) for gq in range(Q_PER_KV)], axis=0)
            sink = _sink_column(sink_ref, kh, w)
            s_c = _dot_nt(q, kc_scr[:, ks])
            s_w = jnp.where(valid, _dot_nt(q, kw_scr[win, ks]), NEG_INF)
            m = jnp.maximum(jnp.maximum(jnp.max(s_c, axis=-1, keepdims=True),
                                        jnp.max(s_w, axis=-1, keepdims=True)), sink)
            ov = (_dot(jnp.exp(s_c - m).astype(BF16), vc_scr[:, vs])
                  + _dot(jnp.exp(s_w - m).astype(BF16), vw_scr[win, vs]))
            o = ov[:, :LANE] * (1.0 / (ov[:, LANE:] + jnp.exp(sink - m)))
            for gq in range(Q_PER_KV):
                o_ref[rows, _head_slice(kh, gq)] = o[gq * w:(gq + 1) * w].astype(o_ref.dtype)


def gqa_latent(proj, cache_k, cache_v, tables, sinks):
    wq = N_HEADS * HEAD_DIM
    wkv = KV_HEADS * HEAD_DIM
    tq = LAT_QBLOCKS_PER_STEP * WINDOW
    nq = DEC_SEQ // tq
    row0 = TP // tq
    kv_row0 = TP // DEC_SEQ
    cos, sin_lo, sin_hi = tables
    full = pl.BlockSpec((DEC_SEQ, LANE), lambda b, i: (0, 0))
    qtab = pl.BlockSpec((tq, LANE), lambda b, i: (i, 0))
    cache = pl.BlockSpec((None, PAST_LEN * KV_HEADS, LANE), lambda b, i: (b, 0, 0))
    return pl.pallas_call(
        _gqa_lat_kernel,
        grid=(DEC_BATCH, nq),
        in_specs=[pl.BlockSpec(memory_space=pltpu.SMEM),
                  pl.BlockSpec((tq, wq), lambda b, i: (row0 + b * nq + i, 0)),
                  pl.BlockSpec((DEC_SEQ, wkv), lambda b, i: (kv_row0 + b, wq // wkv)),
                  pl.BlockSpec((DEC_SEQ, wkv), lambda b, i: (kv_row0 + b, wq // wkv + 1)),
                  cache, cache, full, full, full, qtab, qtab, qtab],
        out_specs=pl.BlockSpec((tq, wq), lambda b, i: (b * nq + i, 0)),
        out_shape=jax.ShapeDtypeStruct((TS, wq), BF16),
        scratch_shapes=[pltpu.VMEM((DEC_SEQ + 2 * WINDOW, wkv), BF16),
                        pltpu.VMEM((DEC_SEQ + 2 * WINDOW, 2 * wkv), BF16),
                        pltpu.VMEM((PAST_LEN, wkv), BF16),
                        pltpu.VMEM((PAST_LEN, 2 * wkv), BF16)],
        compiler_params=_params(("arbitrary", "arbitrary")),
        name="gqa_latent",
    )(sinks, proj, proj, proj, cache_k, cache_v, cos, sin_lo, sin_hi, cos, sin_lo, sin_hi)


def _even_layer(xc, xs, mod, j, layer, norm_g, w_in_even, lam_vec, subln_g, w_gate1, w_gate2, b_gate, gla_norm_g,
                w_out_even, cache_a_k, cache_a_v, state_b, tables):
    lam_init = 0.8 - 0.6 * math.exp(-0.3 * layer)
    ng = HB * GLA_DK
    wg1 = jnp.zeros((D_MODEL, LANE), F32).at[:, :2 * GLA_GATE_RANK].set(
        jnp.concatenate([w_gate1[j, 0], w_gate1[j, 1]], axis=-1)).astype(BF16)
    wg2 = jnp.zeros((2, LANE, ng), F32)
    for e in range(2):
        wg2 = wg2.at[e, e * GLA_GATE_RANK:(e + 1) * GLA_GATE_RANK, :].set(w_gate2[j, e])
    h, gates = norm_mod_gate(xc, xs, norm_g[layer, 0], mod, 0, 1, wg1, wg2.astype(BF16),
                             b_gate[j].reshape(2, 1, ng))
    proj = ws_matmul(h, w_in_even, j)
    ya_c, new_k, new_v = diff_attention_context(proj, lam_vec, subln_g[j], lam_init)
    ya_s = diff_attention_latent(proj, cache_a_k[:, j].reshape(DEC_BATCH, PAST_LEN * HA, 2 * DIFF_HD),
                                 cache_a_v[:, j].reshape(DEC_BATCH, PAST_LEN * HA, DIFF_VD),
                                 tables, lam_vec, subln_g[j], lam_init)
    yb_c, s_fin = bidir_gla(proj, gates, jnp.zeros((1, 2, HB, GLA_DK, GLA_DV), F32), gla_norm_g[j],
                            BATCH, SEQ, 0)
    yb_s, _ = bidir_gla(proj, gates, state_b[:, j], gla_norm_g[j], DEC_BATCH, DEC_SEQ, TP // DEC_SEQ)
    x, h_ffn = ws_matmul_residual([(ya_c, ya_s), (yb_c, yb_s)], w_out_even, j, (xc, xs), mod, 2,
                                  tm=OUT_PROJ_ROWS, next_norm=(norm_g[layer, 1], 3, 4))
    return (x, h_ffn, new_k.reshape(BATCH, SEQ, HA, 2 * DIFF_HD), new_v.reshape(BATCH, SEQ, HA, DIFF_VD), s_fin)


def _odd_layer(x, mod, j, layer, norm_g, w_in_odd, sinks, w_out_odd, cache_c_k, cache_c_v, tables):
    proj = ws_matmul(norm_mod(x, norm_g[layer, 0], mod, 0, 1), w_in_odd, j)
    o_c, new_k, new_v = gqa_context(proj, sinks[j])
    o_s = gqa_latent(proj, cache_c_k[:, j].reshape(DEC_BATCH, PAST_LEN * KV_HEADS, HEAD_DIM),
                     cache_c_v[:, j].reshape(DEC_BATCH, PAST_LEN * KV_HEADS, HEAD_DIM), tables, sinks[j])
    x, h_ffn = ws_matmul_residual([(o_c, o_s)], w_out_odd, j, (x,), mod, 2,
                                  tm=2 * OUT_PROJ_ROWS, next_norm=(norm_g[layer, 1], 3, 4))
    return (x, h_ffn, new_k.reshape(BATCH, SEQ, KV_HEADS, HEAD_DIM), new_v.reshape(BATCH, SEQ, KV_HEADS, HEAD_DIM))


def _ffn(x, h_ffn, mod, layer, w_ffn_in, w_ffn_out):
    act = ws_swiglu(h_ffn, w_ffn_in, layer)
    return ffn_out(act, w_ffn_out, layer, x, mod, 5)


def kernel(x_prompt, x_sample, c, cache_a_k, cache_a_v, state_b, cache_c_k, cache_c_v, c_ctx, w_ada, b_ada,
           norm_g, w_in_even, lam_q1, lam_k1, lam_q2, lam_k2, subln_g, w_gate1, w_gate2, b_gate, gla_norm_g,
           w_out_even, w_in_odd, sinks, w_out_odd, w_ffn_in, w_ffn_out, final_norm_g):
    assert DEPTH == 2, "layer 0 reads the two input streams, layer 1 the fused token matrix"
    cvec = jnp.concatenate([c_ctx[None, :], c, jnp.zeros((MOD_ROWS - 1 - DEC_BATCH, D_MODEL), F32)], axis=0)
    mods = adaln(cvec, w_ada, b_ada).reshape(DEPTH, MOD_ROWS, 1, 6 * D_MODEL)
    tab_d = rope_tables(DEC_SEQ, DIFF_HD)
    tab_h = rope_tables(DEC_SEQ, HEAD_DIM)

    lam_vec = jnp.stack([lam_q1[0], lam_k1[0], lam_q2[0], lam_k2[0]], axis=0)
    x, h_ffn, ak, av, sb = _even_layer(x_prompt.reshape(TP, D_MODEL), x_sample.reshape(TS, D_MODEL), mods[0], 0, 0,
                                       norm_g, w_in_even, lam_vec, subln_g, w_gate1, w_gate2, b_gate, gla_norm_g,
                                       w_out_even, cache_a_k, cache_a_v, state_b, tab_d)
    x = _ffn(x, h_ffn, mods[0], 0, w_ffn_in, w_ffn_out)
    x, h_ffn, ck, cv = _odd_layer(x, mods[1], 0, 1, norm_g, w_in_odd, sinks, w_out_odd, cache_c_k, cache_c_v,
                                  tab_h)
    x = _ffn(x, h_ffn, mods[1], 1, w_ffn_in, w_ffn_out)

    y_prompt = final_norm(x, final_norm_g, 0, TP).reshape(BATCH, SEQ, D_MODEL)
    y_sample = final_norm(x, final_norm_g, TP, TS).reshape(DEC_BATCH, DEC_SEQ, D_MODEL)
    return (y_prompt, y_sample, ak[:, None], av[:, None], sb[:, None], ck[:, None], cv[:, None])
```

```python
import functools
import math

import jax
import jax.numpy as jnp
from jax import lax
from jax.experimental import pallas as pl
from jax.experimental.pallas import tpu as pltpu

D_MODEL = 2048
BATCH = 32
SEQ = 256
DEPTH = 2
DEC_BATCH = 2
DEC_SEQ = 2048
PAST_LEN = 512
GRID_W = 64
HEAD_DIM = 128
N_HEADS = D_MODEL // HEAD_DIM
HA = N_HEADS // 2
DIFF_HD = HEAD_DIM // 2
DIFF_VD = HEAD_DIM
HB = N_HEADS // 2
GLA_DK = HEAD_DIM // 2
GLA_DV = HEAD_DIM
GLA_GATE_RANK = 16
GLA_TAU = 16.0
GLA_CHUNK = 64
GLA_SUPER = 4
CTX_BATCHES_PER_STEP = 4
LAT_QBLOCKS_PER_STEP = 2
DIFF_LAT_CHAINS = 8
KV_HEADS = N_HEADS // 4
Q_PER_KV = N_HEADS // KV_HEADS
WINDOW = 128
D_FF = -(-8 * D_MODEL // (3 * 256)) * 256
ROPE_BASE = 10000.0
EPS = 1e-6
NEG_INF = -1e30
EVEN_IN = 2 * HA * 2 * DIFF_HD + HA * DIFF_VD + 2 * HB * GLA_DK + 2 * HB * GLA_DV
ODD_IN = (N_HEADS + 2 * KV_HEADS) * HEAD_DIM

TP = BATCH * SEQ
TS = DEC_BATCH * DEC_SEQ
T = TP + TS
LANE = 128
MOD_ROWS = 8
NORM_ROWS = 16
NORM_UNROLL = 8
FFN_W_CHUNK = 512
OUT_PROJ_ROWS = 256
VMEM_LIMIT = 56 * 1024 * 1024

F32 = jnp.float32
BF16 = jnp.bfloat16


def _params(semantics, vmem=VMEM_LIMIT):
    return pltpu.CompilerParams(dimension_semantics=semantics, vmem_limit_bytes=vmem)


def _dot(a, b):
    return jnp.dot(a, b, preferred_element_type=F32)


def _dot_nt(a, b):
    return lax.dot_general(a, b, (((1,), (1,)), ((), ())), preferred_element_type=F32)


def _group_of_tile(i, tm):
    r = i * tm
    return jnp.where(r < TP, 0, 1 + (r - TP) // DEC_SEQ)


def _rms(x, g):
    return (x * lax.rsqrt(jnp.mean(x * x, axis=-1, keepdims=True) + EPS)) * g


def _adaln_kernel(c_ref, w_ref, b_ref, o_ref):
    c = c_ref[...]
    s = c * jax.nn.sigmoid(c)
    o_ref[...] = _dot(s.astype(BF16), w_ref[...].astype(BF16)) + b_ref[...]


def adaln(cvec, w_ada, b_ada, tn=2048):
    n = w_ada.shape[-1]
    return pl.pallas_call(
        _adaln_kernel,
        grid=(DEPTH, n // tn),
        in_specs=[
            pl.BlockSpec((MOD_ROWS, D_MODEL), lambda l, j: (0, 0)),
            pl.BlockSpec((None, D_MODEL, tn), lambda l, j: (l, 0, j)),
            pl.BlockSpec((None, 1, tn), lambda l, j: (l, 0, j)),
        ],
        out_specs=pl.BlockSpec((None, MOD_ROWS, tn), lambda l, j: (l, 0, j)),
        out_shape=jax.ShapeDtypeStruct((DEPTH, MOD_ROWS, n), F32),
        compiler_params=_params(("arbitrary", "arbitrary")),
        name="adaln",
    )(cvec, w_ada, b_ada.reshape(DEPTH, 1, n))


def _by_row_groups(x_ref, o_ref, fn):
    def body(r, carry):
        rs = pl.ds(pl.multiple_of(r * NORM_ROWS, NORM_ROWS), NORM_ROWS)
        o_ref[rs, :] = fn(x_ref[rs, :]).astype(o_ref.dtype)
        return carry

    lax.fori_loop(0, x_ref.shape[0] // NORM_ROWS, body, 0, unroll=NORM_UNROLL)


def _norm_mod_into(h_ref, x_ref, g_ref, shift_ref, scale_ref):
    _by_row_groups(x_ref, h_ref, lambda x: _rms(x, g_ref[...]) * (1 + scale_ref[...]) + shift_ref[...])


def _norm_mod_kernel(x_ref, g_ref, shift_ref, scale_ref, h_ref):
    _norm_mod_into(h_ref, x_ref, g_ref, shift_ref, scale_ref)


def _norm_mod_gate_kernel(xc_ref, xs_ref, g_ref, shift_ref, scale_ref, wg1_ref, wg2_ref, bg_ref,
                          h_ref, gate_ref, *, ctx_tiles):
    @pl.when(pl.program_id(0) < ctx_tiles)
    def _():
        _norm_mod_into(h_ref, xc_ref, g_ref, shift_ref, scale_ref)

    @pl.when(pl.program_id(0) >= ctx_tiles)
    def _():
        _norm_mod_into(h_ref, xs_ref, g_ref, shift_ref, scale_ref)

    low = _dot(h_ref[...], wg1_ref[...]).astype(BF16)
    for e in range(2):
        logit = _dot(low, wg2_ref[e]) + bg_ref[e]
        log_sig = jnp.minimum(logit, 0.0) - jnp.log(1.0 + jnp.exp(-jnp.abs(logit)))
        gate_ref[e] = log_sig / GLA_TAU


def _mod_specs(tm, shift_idx, scale_idx):
    def spec(idx):
        return pl.BlockSpec((None, 1, D_MODEL), lambda i: (_group_of_tile(i, tm), 0, idx))
    return [pl.BlockSpec((1, D_MODEL), lambda i: (0, 0)), spec(shift_idx), spec(scale_idx)]


def norm_mod(x, g, mod, shift_idx, scale_idx, tm=1024):
    return pl.pallas_call(
        _norm_mod_kernel,
        grid=(T // tm,),
        in_specs=[pl.BlockSpec((tm, D_MODEL), lambda i: (i, 0))] + _mod_specs(tm, shift_idx, scale_idx),
        out_specs=pl.BlockSpec((tm, D_MODEL), lambda i: (i, 0)),
        out_shape=jax.ShapeDtypeStruct((T, D_MODEL), BF16),
        compiler_params=_params(("arbitrary",)),
        name="norm_mod",
    )(x, g.reshape(1, D_MODEL), mod, mod)


def norm_mod_gate(xc, xs, g, mod, shift_idx, scale_idx, wg1, wg2, bg, tm=512):
    ng = HB * GLA_DK
    ctx_tiles = TP // tm
    return pl.pallas_call(
        functools.partial(_norm_mod_gate_kernel, ctx_tiles=ctx_tiles),
        grid=(T // tm,),
        in_specs=[pl.BlockSpec((tm, D_MODEL), lambda i: (jnp.minimum(i, ctx_tiles - 1), 0)),
                  pl.BlockSpec((tm, D_MODEL), lambda i: (jnp.maximum(i - ctx_tiles, 0), 0))]
        + _mod_specs(tm, shift_idx, scale_idx)
        + [pl.BlockSpec((D_MODEL, LANE), lambda i: (0, 0)),
           pl.BlockSpec((2, LANE, ng), lambda i: (0, 0, 0)),
           pl.BlockSpec((2, 1, ng), lambda i: (0, 0, 0))],
        out_specs=[pl.BlockSpec((tm, D_MODEL), lambda i: (i, 0)),
                   pl.BlockSpec((2, tm, ng), lambda i: (0, i, 0))],
        out_shape=[jax.ShapeDtypeStruct((T, D_MODEL), BF16), jax.ShapeDtypeStruct((2, T, ng), F32)],
        compiler_params=_params(("arbitrary",)),
        name="norm_mod_gate",
    )(xc, xs, g.reshape(1, D_MODEL), mod, mod, wg1, wg2, bg)


def _cast_weights(w_refs, w_scrs):
    @pl.when(pl.program_id(1) == 0)
    def _():
        for w_ref, w_scr in zip(w_refs, w_scrs):
            w_scr[...] = w_ref[...].astype(BF16)


def _ws_plain_kernel(h_ref, w_ref, o_ref, w_scr):
    _cast_weights([w_ref], [w_scr])
    o_ref[...] = _dot(h_ref[...], w_scr[...])


def _ws_swiglu_kernel(h_ref, wg_ref, wu_ref, o_ref, wg_scr, wu_scr):
    _cast_weights([wg_ref, wu_ref], [wg_scr, wu_scr])
    h = h_ref[...]
    gate = _dot(h, wg_scr[...])
    up = _dot(h, wu_scr[...])
    o_ref[...] = (gate * jax.nn.sigmoid(gate) * up).astype(o_ref.dtype)


def _stream_weights(w_hbm, layer, w_scr, stage, sem):
    n_chunks = w_scr.shape[0] // FFN_W_CHUNK

    def chunk_copy(c):
        return pltpu.make_async_copy(w_hbm.at[layer, pl.ds(c * FFN_W_CHUNK, FFN_W_CHUNK), :],
                                     stage.at[c % 2], sem.at[c % 2])

    chunk_copy(0).start()
    chunk_copy(1).start()
    for c in range(n_chunks):
        chunk_copy(c).wait()
        w_scr[c * FFN_W_CHUNK:(c + 1) * FFN_W_CHUNK, :] = stage[c % 2].astype(BF16)
        if c + 2 < n_chunks:
            chunk_copy(c + 2).start()


def _ws_res_kernel(*refs, lhs_arity, x_arity, ctx_tiles, next_norm, layer):
    n_l = sum(lhs_arity)
    w_ref = refs[n_l]
    x_refs = refs[n_l + 1:n_l + 1 + x_arity]
    rest = refs[n_l + 1 + x_arity:]
    groups, at = [], 0
    for a in lhs_arity:
        groups.append(refs[at:at + a])
        at += a
    if next_norm:
        gate_ref, g2_ref, shift2_ref, scale2_ref, o_ref, h_ref, w_scr, stage, sem = rest

        @pl.when(pl.program_id(1) == 0)
        def _():
            _stream_weights(w_ref, layer, w_scr, stage, sem)
    else:
        gate_ref, o_ref, w_scr = rest
        _cast_weights([w_ref], [w_scr])

    def emit(side):
        acc, k0 = None, 0
        for grp in groups:
            ref = grp[side] if len(grp) == 2 else grp[0]
            part = _dot(ref[...], w_scr[k0:k0 + ref.shape[1], :])
            acc = part if acc is None else acc + part
            k0 += ref.shape[1]
        x_ref = x_refs[side] if x_arity == 2 else x_refs[0]
        x_new = x_ref[...] + gate_ref[...] * acc
        o_ref[...] = x_new
        if next_norm:
            h_ref[...] = (_rms(x_new, g2_ref[...]) * (1 + scale2_ref[...]) + shift2_ref[...]).astype(BF16)

    if x_arity == 2 or 2 in lhs_arity:
        @pl.when(pl.program_id(1) < ctx_tiles)
        def _():
            emit(0)

        @pl.when(pl.program_id(1) >= ctx_tiles)
        def _():
            emit(1)
    else:
        emit(0)


def _rows_specs(arrays, tm, width, col):
    if len(arrays) == 1:
        return [pl.BlockSpec((tm, width), lambda j, i: (i, col(j)))]
    ctx_tiles = TP // tm
    return [pl.BlockSpec((tm, width), lambda j, i: (jnp.minimum(i, ctx_tiles - 1), col(j))),
            pl.BlockSpec((tm, width), lambda j, i: (jnp.maximum(i - ctx_tiles, 0), col(j)))]


def ws_matmul(h, w_stack, layer, tm=1024, tn=1024):
    kdim, n = w_stack.shape[1:]
    return pl.pallas_call(
        _ws_plain_kernel,
        grid=(n // tn, T // tm),
        in_specs=[pl.BlockSpec((tm, kdim), lambda j, i: (i, 0)),
                  pl.BlockSpec((None, kdim, tn), lambda j, i: (layer, 0, j))],
        out_specs=pl.BlockSpec((tm, tn), lambda j, i: (i, j)),
        out_shape=jax.ShapeDtypeStruct((T, n), F32),
        scratch_shapes=[pltpu.VMEM((kdim, tn), BF16)],
        compiler_params=_params(("arbitrary", "arbitrary")),
        name="ws_matmul",
    )(h, w_stack)


def ws_swiglu(h, w_stack, layer, tm=1024, tn=512):
    kdim = w_stack.shape[1]
    nf = D_FF // tn
    return pl.pallas_call(
        _ws_swiglu_kernel,
        grid=(nf, T // tm),
        in_specs=[pl.BlockSpec((tm, kdim), lambda j, i: (i, 0)),
                  pl.BlockSpec((None, kdim, tn), lambda j, i: (layer, 0, j)),
                  pl.BlockSpec((None, kdim, tn), lambda j, i: (layer, 0, j + nf))],
        out_specs=pl.BlockSpec((tm, tn), lambda j, i: (i, j)),
        out_shape=jax.ShapeDtypeStruct((T, D_FF), BF16),
        scratch_shapes=[pltpu.VMEM((kdim, tn), BF16), pltpu.VMEM((kdim, tn), BF16)],
        compiler_params=_params(("arbitrary", "arbitrary")),
        name="ws_swiglu",
    )(h, w_stack, w_stack)


def ws_matmul_residual(lhs_groups, w_stack, layer, x, mod, gate_idx, tm=512, tn=1024, next_norm=None):
    kdim, n = w_stack.shape[1:]
    if next_norm is not None:
        tn = n
    gate_blk = n // tn
    specs, args = [], []
    for grp in lhs_groups:
        specs += _rows_specs(grp, tm, grp[0].shape[1], lambda j: 0)
        args += list(grp)
    scratch = [pltpu.VMEM((kdim, tn), BF16)]
    if next_norm is None:
        specs.append(pl.BlockSpec((None, kdim, tn), lambda j, i: (layer, 0, j)))
    else:
        specs.append(pl.BlockSpec(memory_space=pl.ANY))
        scratch += [pltpu.VMEM((2, FFN_W_CHUNK, tn), F32), pltpu.SemaphoreType.DMA((2,))]
    specs += _rows_specs(x, tm, tn, lambda j: j)
    specs.append(pl.BlockSpec((None, 1, tn), lambda j, i: (_group_of_tile(i, tm), 0, gate_idx * gate_blk + j)))
    tail = [mod]
    out_specs = [pl.BlockSpec((tm, tn), lambda j, i: (i, j))]
    out_shape = [jax.ShapeDtypeStruct((T, n), F32)]
    if next_norm is not None:
        gain, shift_idx, scale_idx = next_norm
        specs.append(pl.BlockSpec((1, n), lambda j, i: (0, 0)))
        specs += [pl.BlockSpec((None, 1, n), lambda j, i, idx=idx: (_group_of_tile(i, tm), 0, idx))
                  for idx in (shift_idx, scale_idx)]
        tail += [gain.reshape(1, n), mod, mod]
        out_specs.append(pl.BlockSpec((tm, tn), lambda j, i: (i, j)))
        out_shape.append(jax.ShapeDtypeStruct((T, n), BF16))
    outs = pl.pallas_call(
        functools.partial(_ws_res_kernel, lhs_arity=tuple(len(grp) for grp in lhs_groups), x_arity=len(x),
                          ctx_tiles=TP // tm, next_norm=next_norm is not None, layer=layer),
        grid=(n // tn, T // tm),
        in_specs=specs,
        out_specs=out_specs,
        out_shape=out_shape,
        scratch_shapes=scratch,
        compiler_params=_params(("arbitrary", "arbitrary")),
        name="ws_matmul_residual",
    )(*args, w_stack, *x, *tail)
    return outs[0] if next_norm is None else tuple(outs)


def _ffn_out_kernel(act_ref, w_hbm, x_ref, gate_ref, o_ref, w_scr, stage, sem, *, layer, tn):
    @pl.when(pl.program_id(1) == 0)
    def _():
        n_chunks = w_scr.shape[0] // FFN_W_CHUNK
        col0 = pl.multiple_of(pl.program_id(0) * tn, tn)

        def chunk_copy(c):
            return pltpu.make_async_copy(w_hbm.at[layer, pl.ds(c * FFN_W_CHUNK, FFN_W_CHUNK), pl.ds(col0, tn)],
                                         stage.at[c % 2], sem.at[c % 2])

        chunk_copy(0).start()
        chunk_copy(1).start()
        for c in range(n_chunks):
            chunk_copy(c).wait()
            w_scr[c * FFN_W_CHUNK:(c + 1) * FFN_W_CHUNK, :] = stage[c % 2].astype(BF16)
            if c + 2 < n_chunks:
                chunk_copy(c + 2).start()

    o_ref[...] = x_ref[...] + gate_ref[...] * _dot(act_ref[...], w_scr[...])


def ffn_out(act, w_stack, layer, x, mod, gate_idx, tm=512, tn=1024):
    kdim, n = w_stack.shape[1:]
    gate_blk = n // tn
    return pl.pallas_call(
        functools.partial(_ffn_out_kernel, layer=layer, tn=tn),
        grid=(n // tn, T // tm),
        in_specs=[pl.BlockSpec((tm, kdim), lambda j, i: (i, 0)),
                  pl.BlockSpec(memory_space=pl.ANY),
                  pl.BlockSpec((tm, tn), lambda j, i: (i, j)),
                  pl.BlockSpec((None, 1, tn), lambda j, i: (_group_of_tile(i, tm), 0, gate_idx * gate_blk + j))],
        out_specs=pl.BlockSpec((tm, tn), lambda j, i: (i, j)),
        out_shape=jax.ShapeDtypeStruct((T, n), F32),
        scratch_shapes=[pltpu.VMEM((kdim, tn), BF16), pltpu.VMEM((2, FFN_W_CHUNK, tn), F32),
                        pltpu.SemaphoreType.DMA((2,))],
        compiler_params=_params(("arbitrary", "arbitrary")),
        name="ffn_out",
    )(act, w_stack, x, mod)


def _final_norm_kernel(x_ref, g_ref, o_ref):
    o_ref[...] = _rms(x_ref[...], g_ref[...])


def final_norm(x, g, row0, n_rows, tm=1024):
    blk0 = row0 // tm
    return pl.pallas_call(
        _final_norm_kernel,
        grid=(n_rows // tm,),
        in_specs=[pl.BlockSpec((tm, D_MODEL), lambda i: (blk0 + i, 0)),
                  pl.BlockSpec((1, D_MODEL), lambda i: (0, 0))],
        out_specs=pl.BlockSpec((tm, D_MODEL), lambda i: (i, 0)),
        out_shape=jax.ShapeDtypeStruct((n_rows, D_MODEL), F32),
        compiler_params=_params(("arbitrary",)),
        name="final_norm",
    )(x, g.reshape(1, D_MODEL))


def _grid_angles(n_tok, rot_dim):
    t = jnp.arange((n_tok // GRID_W) * GRID_W)
    row = (t // GRID_W).astype(F32)
    col = (t % GRID_W).astype(F32)
    half = rot_dim // 2
    inv = ROPE_BASE ** (-jnp.arange(0, half, 2, dtype=F32) / half)
    return row[:, None] * inv[None], col[:, None] * inv[None]


def rope_tables(n_tok, rot_dim):
    ang_row, ang_col = _grid_angles(n_tok, rot_dim)
    zeros = jnp.zeros_like(ang_row)
    reps = LANE // rot_dim

    def lanes(r1, r2, c1, c2):
        return jnp.tile(jnp.concatenate([r1, r2, c1, c2], axis=-1), (1, reps))

    cr, sr, cc, sc = jnp.cos(ang_row), jnp.sin(ang_row), jnp.cos(ang_col), jnp.sin(ang_col)
    return lanes(cr, cr, cc, cc), lanes(-sr, zeros, -sc, zeros), lanes(zeros, sr, zeros, sc)


def _rope(x, cos, sin_lo, sin_hi, quarter):
    return (x * cos + pltpu.roll(x, LANE - quarter, 1) * sin_lo + pltpu.roll(x, quarter, 1) * sin_hi)


def _diff_lambda(lam_ref, lam_init):
    e1 = jnp.exp(jnp.sum(lam_ref[0:1, :] * lam_ref[1:2, :], axis=-1, keepdims=True))
    e2 = jnp.exp(jnp.sum(lam_ref[2:3, :] * lam_ref[3:4, :], axis=-1, keepdims=True))
    return e1 - e2 + lam_init


def _with_ones(v_bf):
    return jnp.concatenate([v_bf, jnp.ones(v_bf.shape, BF16)], axis=1)


def _softmax_av(q_bf, k_bf, v1_bf):
    s = _dot_nt(q_bf, k_bf)
    e = jnp.exp(s - jnp.max(s, axis=-1, keepdims=True))
    if v1_bf.shape[1] == LANE:
        return _dot(e.astype(BF16), v1_bf) * (1.0 / jnp.sum(e, axis=-1, keepdims=True))
    ov = _dot(e.astype(BF16), v1_bf)
    return ov[:, :LANE] * (1.0 / ov[:, LANE:])


def _diff_head(q, k_bf, v1_bf, lam, subln_g, lam_init, stack_maps):
    n = q.shape[0]
    first = lax.broadcasted_iota(jnp.int32, q.shape, 1) < DIFF_HD
    q0 = jnp.where(first, q, 0.0).astype(BF16)
    q1 = jnp.where(first, 0.0, q).astype(BF16)
    if stack_maps:
        r = _softmax_av(jnp.concatenate([q0, q1], axis=0), k_bf, v1_bf)
        o = r[:n] - lam * r[n:]
    else:
        o = _softmax_av(q0, k_bf, v1_bf) - lam * _softmax_av(q1, k_bf, v1_bf)
    return _rms(o, subln_g) * (1.0 - lam_init)


def _diff_ctx_kernel(q_ref, k_ref, v_ref, lam_ref, sg_ref, o_ref, ko_ref, vo_ref, *, lam_init):
    lam = _diff_lambda(lam_ref, lam_init)
    scale = DIFF_HD ** -0.5
    n_rows = CTX_BATCHES_PER_STEP * SEQ
    for h in range(HA):
        sl = slice(h * LANE, (h + 1) * LANE)
        ko_ref[pl.ds(h, n_rows, stride=HA), :] = k_ref[:, sl]
        vo_ref[pl.ds(h, n_rows, stride=HA), :] = v_ref[:, sl]
    for bb in range(CTX_BATCHES_PER_STEP):
        rows = slice(bb * SEQ, (bb + 1) * SEQ)
        for h in range(HA):
            sl = slice(h * LANE, (h + 1) * LANE)
            y = _diff_head(q_ref[rows, sl] * scale, k_ref[rows, sl].astype(BF16),
                           _with_ones(v_ref[rows, sl].astype(BF16)), lam, sg_ref[...], lam_init, stack_maps=True)
            o_ref[rows, sl] = y.astype(o_ref.dtype)


def diff_attention_context(proj, lam_vec, subln_g, lam_init):
    w = HA * LANE

    def blk(c):
        return pl.BlockSpec((CTX_BATCHES_PER_STEP * SEQ, w), lambda b: (b, c))

    cache = pl.BlockSpec((CTX_BATCHES_PER_STEP * SEQ * HA, LANE), lambda b: (b, 0))
    return pl.pallas_call(
        functools.partial(_diff_ctx_kernel, lam_init=lam_init),
        grid=(BATCH // CTX_BATCHES_PER_STEP,),
        in_specs=[blk(0), blk(1), blk(2),
                  pl.BlockSpec((4, DIFF_HD), lambda b: (0, 0)),
                  pl.BlockSpec((1, DIFF_VD), lambda b: (0, 0))],
        out_specs=[blk(0), cache, cache],
        out_shape=[jax.ShapeDtypeStruct((TP, w), BF16), jax.ShapeDtypeStruct((TP * HA, LANE), F32),
                   jax.ShapeDtypeStruct((TP * HA, LANE), F32)],
        compiler_params=_params(("arbitrary",)),
        name="diff_attention_context",
    )(proj, proj, proj, lam_vec, subln_g.reshape(1, DIFF_VD))


def _diff_lat_kernel(q_ref, k_ref, v_ref, ck_ref, cv_ref, kc_ref, kl_ref, kh_ref, qc_ref, ql_ref, qh_ref,
                     lam_ref, sg_ref, o_ref, k_scr, v_scr, *, lam_init):
    quarter = DIFF_HD // 4

    @pl.when(pl.program_id(2) == 0)
    def _():
        head_rows = pl.ds(pl.program_id(1), PAST_LEN, stride=HA)
        k_scr[0:PAST_LEN, :] = ck_ref[head_rows, :].astype(BF16)
        k_scr[PAST_LEN:, :] = _rope(k_ref[...], kc_ref[...], kl_ref[...], kh_ref[...], quarter).astype(BF16)
        v_scr[0:PAST_LEN, :] = cv_ref[head_rows, :].astype(BF16)
        v_scr[PAST_LEN:, :] = v_ref[...].astype(BF16)

    lam = _diff_lambda(lam_ref, lam_init)
    sub = q_ref.shape[0] // DIFF_LAT_CHAINS
    for c in range(DIFF_LAT_CHAINS):
        rs = slice(c * sub, (c + 1) * sub)
        q = _rope(q_ref[rs, :], qc_ref[rs, :], ql_ref[rs, :], qh_ref[rs, :], quarter) * (DIFF_HD ** -0.5)
        y = _diff_head(q, k_scr[...], v_scr[...], lam, sg_ref[...], lam_init, stack_maps=False)
        o_ref[rs, :] = y.astype(o_ref.dtype)


def diff_attention_latent(proj, cache_k, cache_v, tables, lam_vec, subln_g, lam_init, tq=2048):
    nq = DEC_SEQ // tq
    row0 = TP // tq
    kv_row0 = TP // DEC_SEQ
    cos, sin_lo, sin_hi = tables
    full = pl.BlockSpec((DEC_SEQ, LANE), lambda b, h, i: (0, 0))
    qtab = pl.BlockSpec((tq, LANE), lambda b, h, i: (i, 0))
    cache = pl.BlockSpec((None, PAST_LEN * HA, LANE), lambda b, h, i: (b, 0, 0))
    return pl.pallas_call(
        functools.partial(_diff_lat_kernel, lam_init=lam_init),
        grid=(DEC_BATCH, HA, nq),
        in_specs=[pl.BlockSpec((tq, LANE), lambda b, h, i: (row0 + b * nq + i, h)),
                  pl.BlockSpec((DEC_SEQ, LANE), lambda b, h, i: (kv_row0 + b, HA + h)),
                  pl.BlockSpec((DEC_SEQ, LANE), lambda b, h, i: (kv_row0 + b, 2 * HA + h)),
                  cache, cache, full, full, full, qtab, qtab, qtab,
                  pl.BlockSpec((4, DIFF_HD), lambda b, h, i: (0, 0)),
                  pl.BlockSpec((1, DIFF_VD), lambda b, h, i: (0, 0))],
        out_specs=pl.BlockSpec((tq, LANE), lambda b, h, i: (b * nq + i, h)),
        out_shape=jax.ShapeDtypeStruct((TS, HA * LANE), BF16),
        scratch_shapes=[pltpu.VMEM((PAST_LEN + DEC_SEQ, LANE), BF16),
                        pltpu.VMEM((PAST_LEN + DEC_SEQ, LANE), BF16)],
        compiler_params=_params(("arbitrary", "arbitrary", "arbitrary")),
        name="diff_attention_latent",
    )(proj, proj, proj, cache_k, cache_v, cos, sin_lo, sin_hi, cos, sin_lo, sin_hi,
      lam_vec, subln_g.reshape(1, DIFF_VD))


def _gla_kernel(q_ref, k_ref, v_ref, r_ref, g_ref, s0_ref, gg_ref, y_ref, sfin_ref, of_scr, ob_scr, st_scr,
                *, seq, pairs):
    c = GLA_CHUNK
    sb = GLA_SUPER * c
    n_super = seq // sb
    rows = lax.broadcasted_iota(jnp.int32, (sb, sb), 0)
    cols = lax.broadcasted_iota(jnp.int32, (sb, sb), 1)
    same_chunk = (rows // c) == (cols // c)
    keep = (same_chunk & (rows >= cols), same_chunk & (cols >= rows))
    lane = lax.broadcasted_iota(jnp.int32, (sb, LANE), 1)
    own = (lane < GLA_DK, lane >= GLA_DK)
    chunk_of_row = lax.broadcasted_iota(jnp.int32, (sb, LANE), 0) // c
    st_rows = lax.broadcasted_iota(jnp.int32, (2 * GLA_DV, LANE), 0)
    st_lane = lax.broadcasted_iota(jnp.int32, (2 * GLA_DV, LANE), 1)
    st_own = (st_rows < GLA_DV) == (st_lane < GLA_DK)
    zpad = jnp.zeros((GLA_DK, GLA_DV), F32)

    def expand(x):
        return jnp.concatenate([jnp.where(chunk_of_row == ci, x, 0.0) for ci in range(GLA_SUPER)],
                               axis=1).astype(BF16)

    for p in range(pairs):
        for d in range(2):
            s0 = [s0_ref[d, 2 * p + hh] for hh in range(2)]
            st_scr[p, d] = jnp.concatenate([jnp.concatenate([s0[0], zpad], axis=0).T,
                                            jnp.concatenate([zpad, s0[1]], axis=0).T], axis=0)

    def rows_of(n, d):
        r0 = (n if d == 0 else n_super - 1 - n) * sb
        return pl.ds(r0, sb) if n_super == 1 else pl.ds(pl.multiple_of(r0, sb), sb)

    def chunk_totals(prefix):
        ends = [prefix[ci * c + c - 1:ci * c + c, :] for ci in range(GLA_SUPER)]
        return ends, jnp.concatenate([jnp.broadcast_to(e, (c, LANE)) for e in ends], axis=0)

    def log_decays(n, p):
        kl = slice(p * LANE, (p + 1) * LANE)
        g_b = g_ref[1, rows_of(n, 1), kl]
        prefix = jnp.concatenate([g_ref[0, rows_of(n, 0), kl], g_b], axis=1)
        pos = lax.broadcasted_iota(jnp.int32, prefix.shape, 0) % c
        step = 1
        while step < c:
            prefix = prefix + jnp.where(pos >= step, pltpu.roll(prefix, step, 0), 0.0)
            step *= 2
        ends_f, total_f = chunk_totals(prefix[:, :LANE])
        ends_b, total_b = chunk_totals(prefix[:, LANE:])
        return (prefix[:, :LANE], ends_f, total_f), (total_b - prefix[:, LANE:] + g_b, ends_b, total_b)

    def one_pair_direction(n, p, d, decays):
        kl = slice(p * LANE, (p + 1) * LANE)
        vl = slice(2 * p * GLA_DV, 2 * (p + 1) * GLA_DV)
        rs = rows_of(n, d)
        b, ends, total = decays
        q_in = (q_ref[rs, kl] * (GLA_DK ** -0.5)) * jnp.exp(b)
        kk = k_ref[rs, kl]
        k_in = (kk * jnp.exp(-b)).astype(BF16)
        k_end = kk * jnp.exp(total - b)
        v = v_ref[rs, vl]
        v_bf = v.astype(BF16)
        kv_all = _dot(v.T.astype(BF16), expand(k_end))
        q2 = jnp.concatenate([jnp.where(own[0], q_in, 0.0), jnp.where(own[1], q_in, 0.0)], axis=0)
        a2 = _dot_nt(q2.astype(BF16), k_in)
        o = jnp.concatenate(
            [_dot(jnp.where(keep[d], a2[hh * sb:(hh + 1) * sb], 0.0).astype(BF16),
                  v_bf[:, hh * GLA_DV:(hh + 1) * GLA_DV]) for hh in range(2)], axis=1)
        st = st_scr[p, d]
        entering = [None] * GLA_SUPER
        for ci in (range(GLA_SUPER) if d == 0 else range(GLA_SUPER - 1, -1, -1)):
            entering[ci] = st
            st = st * jnp.exp(ends[ci]) + jnp.where(st_own, kv_all[:, ci * LANE:(ci + 1) * LANE], 0.0)
        st_scr[p, d] = st
        o += _dot_nt(expand(q_in), jnp.concatenate(entering, axis=1).astype(BF16))
        if d == 0:
            of_scr[rs, vl] = o
        else:
            ob_scr[rs, vl] = o

    def super_block(n, carry):
        for p in range(pairs):
            decays = log_decays(n, p)
            for d in range(2):
                one_pair_direction(n, p, d, decays[d])
        return carry

    if n_super == 1:
        super_block(0, 0)
    else:
        lax.fori_loop(0, n_super, super_block, 0)

    for h in range(2 * pairs):
        vs = slice(h * GLA_DV, (h + 1) * GLA_DV)
        r = r_ref[:, vs]
        y = _rms(of_scr[:, vs] + ob_scr[:, vs], gg_ref[...]) * (r * jax.nn.sigmoid(r))
        y_ref[:, vs] = y.astype(y_ref.dtype)
        p, hh = divmod(h, 2)
        for d in range(2):
            sfin_ref[d, h] = st_scr[p, d, hh * GLA_DV:(hh + 1) * GLA_DV, :].T[hh * GLA_DK:(hh + 1) * GLA_DK, :]


def bidir_gla(proj, gates, s0, gla_g, n_batch, seq, row_blk0):
    shared_s0 = s0.shape[0] == 1
    pairs = HB // 2 if seq * HB * GLA_DV * 4 <= 1024 * 1024 else 2
    wqk = pairs * LANE
    wv = pairs * 2 * GLA_DV
    col_q = (2 * HA * 2 * DIFF_HD + HA * DIFF_VD) // wqk
    col_k = col_q + HB * GLA_DK // wqk
    col_v = (col_k * wqk + HB * GLA_DK) // wv
    col_r = col_v + HB * GLA_DV // wv
    n_tok = n_batch * seq
    return pl.pallas_call(
        functools.partial(_gla_kernel, seq=seq, pairs=pairs),
        grid=(n_batch, HB // 2 // pairs),
        in_specs=[pl.BlockSpec((seq, wqk), lambda b, p: (row_blk0 + b, col_q + p)),
                  pl.BlockSpec((seq, wqk), lambda b, p: (row_blk0 + b, col_k + p)),
                  pl.BlockSpec((seq, wv), lambda b, p: (row_blk0 + b, col_v + p)),
                  pl.BlockSpec((seq, wv), lambda b, p: (row_blk0 + b, col_r + p)),
                  pl.BlockSpec((2, seq, wqk), lambda b, p: (0, row_blk0 + b, p)),
                  pl.BlockSpec((None, 2, 2 * pairs, GLA_DK, GLA_DV),
                               lambda b, p: (0 if shared_s0 else b, 0, p, 0, 0)),
                  pl.BlockSpec((1, GLA_DV), lambda b, p: (0, 0))],
        out_specs=[pl.BlockSpec((seq, wv), lambda b, p: (b, p)),
                   pl.BlockSpec((None, 2, 2 * pairs, GLA_DK, GLA_DV), lambda b, p: (b, 0, p, 0, 0))],
        out_shape=[jax.ShapeDtypeStruct((n_tok, HB * GLA_DV), BF16),
                   jax.ShapeDtypeStruct((n_batch, 2, HB, GLA_DK, GLA_DV), F32)],
        scratch_shapes=[pltpu.VMEM((seq, wv), F32), pltpu.VMEM((seq, wv), F32),
                        pltpu.VMEM((pairs, 2, 2 * GLA_DV, LANE), F32)],
        compiler_params=_params(("arbitrary", "arbitrary")),
        name="bidir_gla",
    )(proj, proj, proj, proj, gates, s0, gla_g.reshape(1, GLA_DV))


def _head_slice(kh, gq):
    h = kh * Q_PER_KV + gq
    return slice(h * LANE, (h + 1) * LANE)


def _sink_column(sink_ref, kh, rows):
    return jnp.concatenate([jnp.full((rows, 1), sink_ref[kh * Q_PER_KV + gq], F32) for gq in range(Q_PER_KV)],
                           axis=0)


def _gqa_ctx_kernel(sink_ref, q_ref, k_ref, v_ref, o_ref, ko_ref, vo_ref):
    scale = HEAD_DIM ** -0.5
    n_rows = CTX_BATCHES_PER_STEP * SEQ
    for kh in range(KV_HEADS):
        ks = slice(kh * LANE, (kh + 1) * LANE)
        ko_ref[pl.ds(kh, n_rows, stride=KV_HEADS), :] = k_ref[:, ks]
        vo_ref[pl.ds(kh, n_rows, stride=KV_HEADS), :] = v_ref[:, ks]
    for bb in range(CTX_BATCHES_PER_STEP):
        rows = slice(bb * SEQ, (bb + 1) * SEQ)
        for kh in range(KV_HEADS):
            ks = slice(kh * LANE, (kh + 1) * LANE)
            k_bf = k_ref[rows, ks].astype(BF16)
            v1_bf = _with_ones(v_ref[rows, ks].astype(BF16))
            for gq in range(Q_PER_KV):
                hs = _head_slice(kh, gq)
                sink = sink_ref[kh * Q_PER_KV + gq]
                s = _dot_nt((q_ref[rows, hs] * scale).astype(BF16), k_bf)
                m = jnp.maximum(jnp.max(s, axis=-1, keepdims=True), sink)
                ov = _dot(jnp.exp(s - m).astype(BF16), v1_bf)
                o_ref[rows, hs] = (ov[:, :LANE] * (1.0 / (ov[:, LANE:] + jnp.exp(sink - m)))).astype(o_ref.dtype)


def gqa_context(proj, sinks):
    wq = N_HEADS * HEAD_DIM
    wkv = KV_HEADS * HEAD_DIM
    rows = CTX_BATCHES_PER_STEP * SEQ

    def kv(c):
        return pl.BlockSpec((rows, wkv), lambda b: (b, c))

    cache = pl.BlockSpec((rows * KV_HEADS, LANE), lambda b: (b, 0))
    return pl.pallas_call(
        _gqa_ctx_kernel,
        grid=(BATCH // CTX_BATCHES_PER_STEP,),
        in_specs=[pl.BlockSpec(memory_space=pltpu.SMEM),
                  pl.BlockSpec((rows, wq), lambda b: (b, 0)), kv(wq // wkv), kv(wq // wkv + 1)],
        out_specs=[pl.BlockSpec((rows, wq), lambda b: (b, 0)), cache, cache],
        out_shape=[jax.ShapeDtypeStruct((TP, wq), BF16), jax.ShapeDtypeStruct((TP * KV_HEADS, LANE), F32),
                   jax.ShapeDtypeStruct((TP * KV_HEADS, LANE), F32)],
        compiler_params=_params(("arbitrary",)),
        name="gqa_context",
    )(sinks, proj, proj, proj)


def _gqa_lat_kernel(sink_ref, q_ref, k_ref, v_ref, ck_ref, cv_ref, kc_ref, kl_ref, kh_ref,
                    qc_ref, ql_ref, qh_ref, o_ref, kw_scr, vw_scr, kc_scr, vc_scr):
    quarter = HEAD_DIM // 4
    w = WINDOW
    wkv = KV_HEADS * HEAD_DIM

    @pl.when(pl.program_id(1) == 0)
    def _():
        kw_scr[0:w, :] = jnp.zeros((w, wkv), BF16)
        kw_scr[w + DEC_SEQ:, :] = jnp.zeros((w, wkv), BF16)
        vw_scr[0:w, :] = jnp.zeros((w, 2 * wkv), BF16)
        vw_scr[w + DEC_SEQ:, :] = jnp.zeros((w, 2 * wkv), BF16)
        for kh in range(KV_HEADS):
            ks = slice(kh * LANE, (kh + 1) * LANE)
            vs = slice(2 * kh * LANE, 2 * (kh + 1) * LANE)
            kw_scr[w:w + DEC_SEQ, ks] = _rope(k_ref[:, ks], kc_ref[...], kl_ref[...], kh_ref[...],
                                              quarter).astype(BF16)
            vw_scr[w:w + DEC_SEQ, vs] = _with_ones(v_ref[:, ks].astype(BF16))
            head_rows = pl.ds(kh, PAST_LEN, stride=KV_HEADS)
            vc_scr[:, vs] = _with_ones(cv_ref[head_rows, :].astype(BF16))
            kc_scr[:, ks] = ck_ref[head_rows, :].astype(BF16)

    qi = lax.broadcasted_iota(jnp.int32, (Q_PER_KV * w, 3 * w), 0) % w
    kj = lax.broadcasted_iota(jnp.int32, (Q_PER_KV * w, 3 * w), 1)
    scale = HEAD_DIM ** -0.5
    for blk in range(LAT_QBLOCKS_PER_STEP):
        n = pl.program_id(1) * LAT_QBLOCKS_PER_STEP + blk
        rows = slice(blk * w, (blk + 1) * w)
        win = pl.ds(pl.multiple_of(n * w, w), 3 * w)
        kpos = n * w - w + kj
        valid = (kj >= qi) & (kj <= qi + 2 * w) & (kpos >= 0) & (kpos < DEC_SEQ)
        for kh in range(KV_HEADS):
            ks = slice(kh * LANE, (kh + 1) * LANE)
            vs = slice(2 * kh * LANE, 2 * (kh + 1) * LANE)
            q = jnp.concatenate(
                [(_rope(q_ref[rows, _head_slice(kh, gq)], qc_ref[rows, :], ql_ref[rows, :], qh_ref[rows, :],
                        quarter) * scale).astype(BF16) for gq in range(Q_PER_KV)], axis=0)
            sink = _sink_column(sink_ref, kh, w)
            s_c = _dot_nt(q, kc_scr[:, ks])
            s_w = jnp.where(valid, _dot_nt(q, kw_scr[win, ks]), NEG_INF)
            m = jnp.maximum(jnp.maximum(jnp.max(s_c, axis=-1, keepdims=True),
                                        jnp.max(s_w, axis=-1, keepdims=True)), sink)
            ov = (_dot(jnp.exp(s_c - m).astype(BF16), vc_scr[:, vs])
                  + _dot(jnp.exp(s_w - m).astype(BF16), vw_scr[win, vs]))
            o = ov[:, :LANE] * (1.0 / (ov[:, LANE:] + jnp.exp(sink - m)))
            for gq in range(Q_PER_KV):
                o_ref[rows, _head_slice(kh, gq)] = o[gq * w:(gq + 1) * w].astype(o_ref.dtype)


def gqa_latent(proj, cache_k, cache_v, tables, sinks):
    wq = N_HEADS * HEAD_DIM
    wkv = KV_HEADS * HEAD_DIM
    tq = LAT_QBLOCKS_PER_STEP * WINDOW
    nq = DEC_SEQ // tq
    row0 = TP // tq
    kv_row0 = TP // DEC_SEQ
    cos, sin_lo, sin_hi = tables
    full = pl.BlockSpec((DEC_SEQ, LANE), lambda b, i: (0, 0))
    qtab = pl.BlockSpec((tq, LANE), lambda b, i: (i, 0))
    cache = pl.BlockSpec((None, PAST_LEN * KV_HEADS, LANE), lambda b, i: (b, 0, 0))
    return pl.pallas_call(
        _gqa_lat_kernel,
        grid=(DEC_BATCH, nq),
        in_specs=[pl.BlockSpec(memory_space=pltpu.SMEM),
                  pl.BlockSpec((tq, wq), lambda b, i: (row0 + b * nq + i, 0)),
                  pl.BlockSpec((DEC_SEQ, wkv), lambda b, i: (kv_row0 + b, wq // wkv)),
                  pl.BlockSpec((DEC_SEQ, wkv), lambda b, i: (kv_row0 + b, wq // wkv + 1)),
                  cache, cache, full, full, full, qtab, qtab, qtab],
        out_specs=pl.BlockSpec((tq, wq), lambda b, i: (b * nq + i, 0)),
        out_shape=jax.ShapeDtypeStruct((TS, wq), BF16),
        scratch_shapes=[pltpu.VMEM((DEC_SEQ + 2 * WINDOW, wkv), BF16),
                        pltpu.VMEM((DEC_SEQ + 2 * WINDOW, 2 * wkv), BF16),
                        pltpu.VMEM((PAST_LEN, wkv), BF16),
                        pltpu.VMEM((PAST_LEN, 2 * wkv), BF16)],
        compiler_params=_params(("arbitrary", "arbitrary")),
        name="gqa_latent",
    )(sinks, proj, proj, proj, cache_k, cache_v, cos, sin_lo, sin_hi, cos, sin_lo, sin_hi)


def _even_layer(xc, xs, mod, j, layer, norm_g, w_in_even, lam_vec, subln_g, w_gate1, w_gate2, b_gate, gla_norm_g,
                w_out_even, cache_a_k, cache_a_v, state_b, tables):
    lam_init = 0.8 - 0.6 * math.exp(-0.3 * layer)
    ng = HB * GLA_DK
    wg1 = jnp.zeros((D_MODEL, LANE), F32).at[:, :2 * GLA_GATE_RANK].set(
        jnp.concatenate([w_gate1[j, 0], w_gate1[j, 1]], axis=-1)).astype(BF16)
    wg2 = jnp.zeros((2, LANE, ng), F32)
    for e in range(2):
        wg2 = wg2.at[e, e * GLA_GATE_RANK:(e + 1) * GLA_GATE_RANK, :].set(w_gate2[j, e])
    h, gates = norm_mod_gate(xc, xs, norm_g[layer, 0], mod, 0, 1, wg1, wg2.astype(BF16),
                             b_gate[j].reshape(2, 1, ng))
    proj = ws_matmul(h, w_in_even, j)
    ya_c, new_k, new_v = diff_attention_context(proj, lam_vec, subln_g[j], lam_init)
    ya_s = diff_attention_latent(proj, cache_a_k[:, j].reshape(DEC_BATCH, PAST_LEN * HA, 2 * DIFF_HD),
                                 cache_a_v[:, j].reshape(DEC_BATCH, PAST_LEN * HA, DIFF_VD),
                                 tables, lam_vec, subln_g[j], lam_init)
    yb_c, s_fin = bidir_gla(proj, gates, jnp.zeros((1, 2, HB, GLA_DK, GLA_DV), F32), gla_norm_g[j],
                            BATCH, SEQ, 0)
    yb_s, _ = bidir_gla(proj, gates, state_b[:, j], gla_norm_g[j], DEC_BATCH, DEC_SEQ, TP // DEC_SEQ)
    x, h_ffn = ws_matmul_residual([(ya_c, ya_s), (yb_c, yb_s)], w_out_even, j, (xc, xs), mod, 2,
                                  tm=2 * OUT_PROJ_ROWS, next_norm=(norm_g[layer, 1], 3, 4))
    return (x, h_ffn, new_k.reshape(BATCH, SEQ, HA, 2 * DIFF_HD), new_v.reshape(BATCH, SEQ, HA, DIFF_VD), s_fin)


def _odd_layer(x, mod, j, layer, norm_g, w_in_odd, sinks, w_out_odd, cache_c_k, cache_c_v, tables):
    proj = ws_matmul(norm_mod(x, norm_g[layer, 0], mod, 0, 1), w_in_odd, j)
    o_c, new_k, new_v = gqa_context(proj, sinks[j])
    o_s = gqa_latent(proj, cache_c_k[:, j].reshape(DEC_BATCH, PAST_LEN * KV_HEADS, HEAD_DIM),
                     cache_c_v[:, j].reshape(DEC_BATCH, PAST_LEN * KV_HEADS, HEAD_DIM), tables, sinks[j])
    x, h_ffn = ws_matmul_residual([(o_c, o_s)], w_out_odd, j, (x,), mod, 2,
                                  tm=2 * OUT_PROJ_ROWS, next_norm=(norm_g[layer, 1], 3, 4))
    return (x, h_ffn, new_k.reshape(BATCH, SEQ, KV_HEADS, HEAD_DIM), new_v.reshape(BATCH, SEQ, KV_HEADS, HEAD_DIM))


def _ffn(x, h_ffn, mod, layer, w_ffn_in, w_ffn_out):
    act = ws_swiglu(h_ffn, w_ffn_in, layer)
    return ffn_out(act, w_ffn_out, layer, x, mod, 5)


def kernel(x_prompt, x_sample, c, cache_a_k, cache_a_v, state_b, cache_c_k, cache_c_v, c_ctx, w_ada, b_ada,
           norm_g, w_in_even, lam_q1, lam_k1, lam_q2, lam_k2, subln_g, w_gate1, w_gate2, b_gate, gla_norm_g,
           w_out_even, w_in_odd, sinks, w_out_odd, w_ffn_in, w_ffn_out, final_norm_g):
    assert DEPTH == 2, "layer 0 reads the two input streams, layer 1 the fused token matrix"
    cvec = jnp.concatenate([c_ctx[None, :], c, jnp.zeros((MOD_ROWS - 1 - DEC_BATCH, D_MODEL), F32)], axis=0)
    mods = adaln(cvec, w_ada, b_ada).reshape(DEPTH, MOD_ROWS, 1, 6 * D_MODEL)
    tab_d = rope_tables(DEC_SEQ, DIFF_HD)
    tab_h = rope_tables(DEC_SEQ, HEAD_DIM)

    lam_vec = jnp.stack([lam_q1[0], lam_k1[0], lam_q2[0], lam_k2[0]], axis=0)
    x, h_ffn, ak, av, sb = _even_layer(x_prompt.reshape(TP, D_MODEL), x_sample.reshape(TS, D_MODEL), mods[0], 0, 0,
                                       norm_g, w_in_even, lam_vec, subln_g, w_gate1, w_gate2, b_gate, gla_norm_g,
                                       w_out_even, cache_a_k, cache_a_v, state_b, tab_d)
    x = _ffn(x, h_ffn, mods[0], 0, w_ffn_in, w_ffn_out)
    x, h_ffn, ck, cv = _odd_layer(x, mods[1], 0, 1, norm_g, w_in_odd, sinks, w_out_odd, cache_c_k, cache_c_v,
                                  tab_h)
    x = _ffn(x, h_ffn, mods[1], 1, w_ffn_in, w_ffn_out)

    y_prompt = final_norm(x, final_norm_g, 0, TP).reshape(BATCH, SEQ, D_MODEL)
    y_sample = final_norm(x, final_norm_g, TP, TS).reshape(DEC_BATCH, DEC_SEQ, D_MODEL)
    return (y_prompt, y_sample, ak[:, None], av[:, None], sb[:, None], ck[:, None], cv[:, None])
```
